```python
import jax, jax.numpy as jnp
from jax import lax
import numpy as np

D_MODEL = 1024
BATCH = 2
SEQ = 8192
DEPTH = 1

HEAD_DIM = 64
GRID_W = 64
NA_HEADS = 8
NA_KH = 8
NA_KW = 16
DIL_CONFIGS = ((128, 1), (512, 4), (2048, 16))
DIL_HEADS_PER_GROUP = 4
DIL_HEADS = DIL_HEADS_PER_GROUP * len(DIL_CONFIGS)
DIL_QBLOCK = 64
ROT_DIM = HEAD_DIM // 4
ROPE_THETA = 500000.0
D_FF = -(-8 * D_MODEL // (3 * 256)) * 256
EPS = 1e-6
NEG_INF = -1e30
WA = NA_HEADS * HEAD_DIM
WB = DIL_HEADS * HEAD_DIM
WB_OUT = DIL_HEADS_PER_GROUP * HEAD_DIM
W_IN = 3 * WA + 3 * WB + 2 * D_MODEL

kernel_name = "hybrid_natten_dilated_gated_encoder"

f32 = jnp.float32


def rms_norm(x, g):
    x32 = x.astype(f32)
    y = x32 * lax.rsqrt(jnp.mean(x32 * x32, axis=-1, keepdims=True) + EPS)
    return (y * g.astype(f32)).astype(x.dtype)


def partial_rotary(t, pos):
    half = ROT_DIM // 2
    inv_freq = ROPE_THETA ** (-(jnp.arange(half, dtype=f32) * 2.0) / ROT_DIM)
    ang = pos.astype(f32)[:, None] * inv_freq[None, :]
    cos = jnp.cos(ang)[None, :, None, :]
    sin = jnp.sin(ang)[None, :, None, :]
    x1 = t[..., :half].astype(f32)
    x2 = t[..., half:ROT_DIM].astype(f32)
    rot = jnp.concatenate([x1 * cos - x2 * sin, x2 * cos + x1 * sin], axis=-1).astype(t.dtype)
    return jnp.concatenate([rot, t[..., ROT_DIM:]], axis=-1)


def neighbourhood_attention(q, k, v, rpb):
    b, s, h, dh = q.shape
    rows = s // GRID_W
    kh = min(NA_KH, rows)
    qg = q.reshape(b, rows, GRID_W, h, dh)
    kg = k.reshape(b, rows, GRID_W, h, dh)
    vg = v.reshape(b, rows, GRID_W, h, dh)
    r = jnp.arange(rows)
    row_start = jnp.clip(r - kh // 2, 0, rows - kh)
    row_idx = row_start[:, None] + jnp.arange(kh)[None, :]
    kn = kg[:, row_idx].reshape(b, rows, kh * GRID_W, h, dh)
    vn = vg[:, row_idx].reshape(b, rows, kh * GRID_W, h, dh)
    col = jnp.arange(GRID_W)
    col_start = jnp.clip(col - NA_KW // 2, 0, GRID_W - NA_KW)
    col_mask = (col[None, :] >= col_start[:, None]) & (col[None, :] < col_start[:, None] + NA_KW)
    mask = jnp.broadcast_to(col_mask[:, None, :], (GRID_W, kh, GRID_W)).reshape(GRID_W, kh * GRID_W)
    row_off = row_idx - r[:, None] + (NA_KH - 1)
    col_off = jnp.clip(col[None, :] - col[:, None] + (NA_KW - 1), 0, 2 * NA_KW - 2)
    bias = rpb[:, row_off[:, None, :, None], col_off[None, :, None, :]]
    bias = bias.reshape(h, rows, GRID_W, kh * GRID_W).astype(f32)
    scores = jnp.einsum('brqhd,brkhd->bhrqk', qg, kn).astype(f32) * (dh ** -0.5) + bias[None]
    scores = jnp.where(mask, scores, NEG_INF)
    p = jax.nn.softmax(scores, axis=-1).astype(v.dtype)
    o = jnp.einsum('bhrqk,brkhd->brqhd', p, vn)
    return o.reshape(b, s, h * dh)


def dilated_window_attention(q, k, v, window, dilation):
    b, s, h, dh = q.shape
    half = (window // 2) // dilation
    seg = s // dilation

    def split(t):
        return t.reshape(b, seg, dilation, h, dh).transpose(0, 2, 1, 3, 4).reshape(b * dilation, seg, h, dh)

    qs, ks, vs = split(q), split(k), split(v)
    nb = -(-seg // DIL_QBLOCK)
    lp = nb * DIL_QBLOCK
    qs = jnp.pad(qs, ((0, 0), (0, lp - seg), (0, 0), (0, 0)))
    pad_k = ((0, 0), (half, lp - seg + half), (0, 0), (0, 0))
    kp, vp = jnp.pad(ks, pad_k), jnp.pad(vs, pad_k)
    span = DIL_QBLOCK + 2 * half
    key_idx = jnp.arange(nb)[:, None] * DIL_QBLOCK + jnp.arange(span)[None, :]
    kb = kp[:, key_idx]
    vb = vp[:, key_idx]
    qb = qs.reshape(-1, nb, DIL_QBLOCK, h, dh)
    qpos = jnp.arange(nb)[:, None] * DIL_QBLOCK + jnp.arange(DIL_QBLOCK)[None, :]
    kpos = key_idx - half
    rel = kpos[:, None, :] - qpos[:, :, None]
    mask = (jnp.abs(rel) <= half) & (kpos[:, None, :] >= 0) & (kpos[:, None, :] < seg)
    scores = jnp.einsum('nbqhd,nbkhd->nhbqk', qb, kb).astype(f32) * (dh ** -0.5)
    scores = jnp.where(mask[None, None], scores, NEG_INF)
    lse = jax.nn.logsumexp(scores, axis=-1)
    p = jnp.exp(scores - lse[..., None]).astype(v.dtype)
    o = jnp.einsum('nhbqk,nbkhd->nbqhd', p, vb).reshape(-1, lp, h, dh)[:, :seg]
    lse = lse.transpose(0, 2, 3, 1).reshape(-1, lp, h)[:, :seg]
    o = o.reshape(b, dilation, seg, h, dh).transpose(0, 2, 1, 3, 4).reshape(b, s, h, dh)
    lse = lse.reshape(b, dilation, seg, h).transpose(0, 2, 1, 3).reshape(b, s, h)
    return o, lse


def setup_inputs(seed: int = 0) -> dict:
    key = jax.random.key(seed)
    ks = jax.random.split(key, 20)
    nrm = lambda k, shape, scale: jax.random.normal(k, shape, f32) * scale
    d = D_MODEL
    return {
        "x": nrm(ks[0], (BATCH, SEQ, d), 1.0),
        "c": nrm(ks[1], (BATCH, d), 1.0),
        "w_ada": nrm(ks[2], (DEPTH, d, 6 * d), 0.5 * d ** -0.5),
        "b_ada": nrm(ks[3], (DEPTH, 6 * d), 0.02),
        "g_norm1": 1.0 + nrm(ks[4], (DEPTH, d), 0.05),
        "g_norm2": 1.0 + nrm(ks[5], (DEPTH, d), 0.05),
        "w_in": nrm(ks[6], (DEPTH, d, W_IN), d ** -0.5),
        "b_gate": nrm(ks[7], (DEPTH, 2 * d), 0.02),
        "g_qa": 1.0 + nrm(ks[8], (DEPTH, HEAD_DIM), 0.05),
        "g_ka": 1.0 + nrm(ks[9], (DEPTH, HEAD_DIM), 0.05),
        "g_qb": 1.0 + nrm(ks[10], (DEPTH, HEAD_DIM), 0.05),
        "g_kb": 1.0 + nrm(ks[11], (DEPTH, HEAD_DIM), 0.05),
        "rpb": nrm(ks[12], (DEPTH, NA_HEADS, 2 * NA_KH - 1, 2 * NA_KW - 1), 0.1),
        "w_proj_a": nrm(ks[13], (DEPTH, WA, d), WA ** -0.5),
        "w_proj_b": nrm(ks[14], (DEPTH, WB_OUT, d), WB_OUT ** -0.5),
        "w_o": nrm(ks[15], (DEPTH, d, d), d ** -0.5),
        "w_ffn_in": nrm(ks[16], (DEPTH, d, 2 * D_FF), d ** -0.5),
        "w_ffn_out": nrm(ks[17], (DEPTH, D_FF, d), D_FF ** -0.5),
    }


def reference(x, c, w_ada, b_ada, g_norm1, g_norm2, w_in, b_gate, g_qa, g_ka, g_qb, g_kb,
              rpb, w_proj_a, w_proj_b, w_o, w_ffn_in, w_ffn_out):
    b, s, _ = x.shape
    pos = jnp.arange(s)
    c_act = jax.nn.silu(c)
    split_at = [WA, 2 * WA, 3 * WA, 3 * WA + WB, 3 * WA + 2 * WB, 3 * WA + 3 * WB]
    for l in range(DEPTH):
        mod = c_act @ w_ada[l] + b_ada[l]
        sh1, sc1, gt1, sh2, sc2, gt2 = [m[:, None, :] for m in jnp.split(mod, 6, axis=-1)]

        h = rms_norm(x, g_norm1[l]) * (1.0 + sc1) + sh1
        proj = h @ w_in[l]
        qa, ka, va, qb, kb, vb, gates = jnp.split(proj, split_at, axis=-1)

        qa = rms_norm(qa.reshape(b, s, NA_HEADS, HEAD_DIM), g_qa[l])
        ka = rms_norm(ka.reshape(b, s, NA_HEADS, HEAD_DIM), g_ka[l])
        va = va.reshape(b, s, NA_HEADS, HEAD_DIM)
        o_a = neighbourhood_attention(qa, ka, va, rpb[l])

        qb = partial_rotary(rms_norm(qb.reshape(b, s, DIL_HEADS, HEAD_DIM), g_qb[l]), pos)
        kb = partial_rotary(rms_norm(kb.reshape(b, s, DIL_HEADS, HEAD_DIM), g_kb[l]), pos)
        vb = vb.reshape(b, s, DIL_HEADS, HEAD_DIM)
        outs, lses = [], []
        for g, (win, dil) in enumerate(DIL_CONFIGS):
            sl = slice(g * DIL_HEADS_PER_GROUP, (g + 1) * DIL_HEADS_PER_GROUP)
            o_g, lse_g = dilated_window_attention(qb[:, :, sl], kb[:, :, sl], vb[:, :, sl], win, dil)
            outs.append(o_g)
            lses.append(lse_g)
        wts = jax.nn.softmax(jnp.stack(lses, axis=0), axis=0)
        o_b = jnp.einsum('gbsh,gbshd->bshd', wts.astype(vb.dtype), jnp.stack(outs, axis=0))
        o_b = o_b.reshape(b, s, WB_OUT)

        gate_a, gate_b = jnp.split(jax.nn.sigmoid(gates + b_gate[l]), 2, axis=-1)
        merged = gate_a * (o_a @ w_proj_a[l]) + gate_b * (o_b @ w_proj_b[l])
        x = x + gt1 * (merged @ w_o[l])

        h2 = rms_norm(x, g_norm2[l]) * (1.0 + sc2) + sh2
        a, up = jnp.split(h2 @ w_ffn_in[l], 2, axis=-1)
        x = x + gt2 * ((jax.nn.silu(a) * up) @ w_ffn_out[l])
    return x
```

```python
import functools

import jax
import jax.numpy as jnp
from jax import lax
from jax.experimental import pallas as pl
from jax.experimental.pallas import tpu as pltpu

f32 = jnp.float32
bf16 = jnp.bfloat16

D_MODEL = 1024
HEAD_DIM = 64
GRID_W = 64
NA_HEADS = 8
NA_KH = 8
NA_KW = 16
DIL_CONFIGS = ((128, 1), (512, 4), (2048, 16))
DIL_HPG = 4
DIL_HEADS = DIL_HPG * len(DIL_CONFIGS)
DIL_QBLOCK = 64
ROT_DIM = HEAD_DIM // 4
ROPE_THETA = 500000.0
D_FF = -(-8 * D_MODEL // (3 * 256)) * 256
EPS = 1e-6
NEG_INF = -1e30
WA = NA_HEADS * HEAD_DIM
WB = DIL_HEADS * HEAD_DIM
WB_OUT = DIL_HPG * HEAD_DIM
W_QKV = 3 * WA + 3 * WB

LANES = 128
COL_TILE = 256
DIL_TILE = 1024
NA_ROWS = 8
TAIL_TM = 512
FF_CHUNK = 256
VMEM_LIMIT = 56 * 1024 * 1024


def _params(sem):
    return pltpu.CompilerParams(dimension_semantics=sem, vmem_limit_bytes=VMEM_LIMIT)


def _const_spec(shape):
    nd = len(shape)
    return pl.BlockSpec(shape, lambda *_: (0,) * nd, pipeline_mode=pl.Buffered(1))


def _mod_kernel(ct_ref, w_ref, b_ref, o_ref):
    ct = ct_ref[...]
    act = ct * jax.nn.sigmoid(ct)
    w = w_ref[...]
    rows = [jnp.sum(act[:, b:b + 1] * w, axis=0, keepdims=True) for b in range(ct.shape[1])]
    o_ref[...] = jnp.concatenate(rows, axis=0) + b_ref[...]


def _mod(c, w_ada, b_ada):
    bsz, d = c.shape
    n = w_ada.shape[1]
    tn = n // 4
    return pl.pallas_call(
        _mod_kernel,
        grid=(n // tn,),
        in_specs=[pl.BlockSpec((d, bsz), lambda j: (0, 0)),
                  pl.BlockSpec((d, tn), lambda j: (0, j)),
                  pl.BlockSpec((1, tn), lambda j: (0, j))],
        out_specs=pl.BlockSpec((bsz, tn), lambda j: (0, j)),
        out_shape=jax.ShapeDtypeStruct((bsz, n), f32),
        compiler_params=_params(("arbitrary",)),
        name="mod",
    )(c.T, w_ada, b_ada.reshape(1, n))


def _modulated_norm(x, g, sc, sh):
    ms = jnp.mean(x * x, axis=-1, keepdims=True)
    y = x * lax.rsqrt(ms + EPS) * g
    return y * (1.0 + sc) + sh


_J_QB = 3 * WA // COL_TILE
_J_KB = _J_QB + WB // COL_TILE
_J_VB = _J_KB + WB // COL_TILE
_N_J = W_QKV // COL_TILE


def _qkv_kernel(x_ref, mod_ref, g1_ref, w_ref, gvec_ref, ones_ref, cos_ref, sina_ref, sinb_ref,
                o_ref, h_scr, y_scr):
    j = pl.program_id(2)
    tm = x_ref.shape[1]

    @pl.when(j == 0)
    def _():
        mod = mod_ref[0]
        h = _modulated_norm(x_ref[0], g1_ref[...], mod[1:2], mod[0:1])
        h_scr[...] = h.astype(bf16)

    acc = jnp.dot(h_scr[...], w_ref[...], preferred_element_type=f32)

    is_v = ((j >= 2 * WA // COL_TILE) & (j < _J_QB)) | (j >= _J_VB)
    is_b = j >= _J_QB
    kind = jnp.where(j < WA // COL_TILE, 0, jnp.where(j < _J_QB, 1, jnp.where(j < _J_KB, 2, 3)))

    def put_y(y):
        for c in range(COL_TILE // LANES):
            y_scr[c] = y[:, c * LANES:(c + 1) * LANES]

    @pl.when(is_v)
    def _():
        put_y(acc)

    @pl.when(jnp.logical_not(is_v))
    def _():
        ss = jnp.dot((acc * acc).astype(bf16), ones_ref[...], preferred_element_type=f32)
        y = acc * lax.rsqrt(ss * (1.0 / HEAD_DIM) + EPS) * gvec_ref[pl.ds(kind, 1), :]

        @pl.when(is_b)
        def _():
            cos = jnp.concatenate([cos_ref[...]] * (COL_TILE // LANES), axis=1)
            sina = jnp.concatenate([sina_ref[...]] * (COL_TILE // LANES), axis=1)
            sinb = jnp.concatenate([sinb_ref[...]] * (COL_TILE // LANES), axis=1)
            half = ROT_DIM // 2
            up = pltpu.roll(y, COL_TILE - half, 1)
            dn = pltpu.roll(y, half, 1)
            put_y(y * cos + up * sina + dn * sinb)

        @pl.when(jnp.logical_not(is_b))
        def _():
            put_y(y)

    grp = jnp.where(is_b, (j - _J_QB) % len(DIL_CONFIGS), 0)

    @pl.when(grp == 0)
    def _():
        for c in range(COL_TILE // LANES):
            o_ref[0, :, c * LANES:(c + 1) * LANES] = y_scr[c].astype(bf16)

    for g, (_, dil) in enumerate(DIL_CONFIGS):
        if dil == 1:
            continue

        @pl.when(grp == g)
        def _(dil=dil):
            seg = DIL_TILE // dil
            for t in range(tm // DIL_TILE):
                for r in range(dil):
                    for c in range(COL_TILE // LANES):
                        rows = y_scr[c, pl.ds(t * DIL_TILE + r, seg, stride=dil), :]
                        o_ref[0, t * DIL_TILE + r * seg:t * DIL_TILE + (r + 1) * seg,
                              c * LANES:(c + 1) * LANES] = rows.astype(bf16)


def _qkv(x, mod6, g1, w_qkv, gvec, ones_bd, cos_t, sina_t, sinb_t, tm=DIL_TILE):
    bsz, s, d = x.shape
    return pl.pallas_call(
        _qkv_kernel,
        grid=(bsz, s // tm, _N_J),
        in_specs=[pl.BlockSpec((1, tm, d), lambda b, i, j: (b, i, 0)),
                  pl.BlockSpec((1, 6, d), lambda b, i, j: (b, 0, 0)),
                  pl.BlockSpec((1, d), lambda b, i, j: (0, 0)),
                  pl.BlockSpec((d, COL_TILE), lambda b, i, j: (0, j)),
                  pl.BlockSpec((4, COL_TILE), lambda b, i, j: (0, 0)),
                  pl.BlockSpec((COL_TILE, COL_TILE), lambda b, i, j: (0, 0)),
                  pl.BlockSpec((tm, LANES), lambda b, i, j: (i, 0)),
                  pl.BlockSpec((tm, LANES), lambda b, i, j: (i, 0)),
                  pl.BlockSpec((tm, LANES), lambda b, i, j: (i, 0))],
        out_specs=pl.BlockSpec((1, tm, COL_TILE), lambda b, i, j: (b, i, j)),
        out_shape=jax.ShapeDtypeStruct((bsz, s, W_QKV), bf16),
        scratch_shapes=[pltpu.VMEM((tm, d), bf16), pltpu.VMEM((COL_TILE // LANES, tm, LANES), f32)],
        compiler_params=_params(("parallel", "parallel", "arbitrary")),
        name="qkv",
    )(x, mod6, g1, w_qkv, gvec, ones_bd, cos_t, sina_t, sinb_t)


def _na_kernel(q_ref, kp_ref, kc_ref, kn_ref, vp_ref, vc_ref, vn_ref, tbl_ref, o_ref, kwin, vwin):
    rb = pl.program_id(1)
    blk = NA_ROWS * GRID_W
    n_rows = pl.num_programs(1) * NA_ROWS
    for t, (kr, vr) in enumerate(((kp_ref, vp_ref), (kc_ref, vc_ref), (kn_ref, vn_ref))):
        kwin[t * blk:(t + 1) * blk, :] = kr[0]
        vwin[t * blk:(t + 1) * blk, :] = vr[0]
    lane = lax.broadcasted_iota(jnp.int32, (GRID_W, LANES), 1)
    first_head = lane < HEAD_DIM
    nkeys = NA_KH * GRID_W

    def row_body(a, carry):
        r = rb * NA_ROWS + a
        row_start = jnp.clip(r - NA_KH // 2, 0, n_rows - NA_KH)
        delta = r - row_start
        off = pl.multiple_of((row_start - (rb - 1) * NA_ROWS) * GRID_W, GRID_W)
        qoff = pl.multiple_of(a * GRID_W, GRID_W)
        for hp in range(NA_HEADS // 2):
            cols = slice(hp * LANES, (hp + 1) * LANES)
            qp = q_ref[0, pl.ds(qoff, GRID_W), cols]
            kk = kwin[pl.ds(off, nkeys), cols]
            vv = vwin[pl.ds(off, nkeys), cols]
            outs = []
            for hh in range(2):
                keep = first_head if hh == 0 else jnp.logical_not(first_head)
                qm = jnp.where(keep, qp, jnp.zeros_like(qp))
                s = lax.dot_general(qm, kk, (((1,), (1,)), ((), ())), preferred_element_type=f32)
                s = s + tbl_ref[delta * NA_HEADS + hp * 2 + hh]
                m = jnp.max(s, axis=-1, keepdims=True)
                p = jnp.exp(s - m)
                l = jnp.sum(p, axis=-1, keepdims=True)
                pv = jnp.dot(p.astype(bf16), vv, preferred_element_type=f32)
                outs.append(pv / l)
            o_ref[0, pl.ds(qoff, GRID_W), cols] = jnp.where(first_head, outs[0], outs[1]).astype(bf16)
        return carry

    lax.fori_loop(0, NA_ROWS, row_body, 0)


def _na(qkv, tbl):
    bsz, s, _ = qkv.shape
    blk = NA_ROWS * GRID_W
    nb = s // blk
    qspec = pl.BlockSpec((1, blk, WA), lambda b, i: (b, i, 0))

    def halo(col):
        return [pl.BlockSpec((1, blk, WA), lambda b, i: (b, jnp.maximum(i - 1, 0), col)),
                pl.BlockSpec((1, blk, WA), lambda b, i: (b, i, col)),
                pl.BlockSpec((1, blk, WA), lambda b, i: (b, jnp.minimum(i + 1, nb - 1), col))]

    return pl.pallas_call(
        _na_kernel,
        grid=(bsz, nb),
        in_specs=[qspec] + halo(1) + halo(2) + [_const_spec(tbl.shape)],
        out_specs=pl.BlockSpec((1, blk, WA), lambda b, i: (b, i, 0)),
        out_shape=jax.ShapeDtypeStruct((bsz, s, WA), bf16),
        scratch_shapes=[pltpu.VMEM((3 * blk, WA), bf16), pltpu.VMEM((3 * blk, WA), bf16)],
        compiler_params=_params(("parallel", "parallel")),
        name="na",
    )(qkv, qkv, qkv, qkv, qkv, qkv, qkv, tbl)


def _na_bias_table(rpb):
    col = jnp.arange(GRID_W)
    col_start = jnp.clip(col - NA_KW // 2, 0, GRID_W - NA_KW)
    cmask = (col[None, :] >= col_start[:, None]) & (col[None, :] < col_start[:, None] + NA_KW)
    col_off = jnp.clip(col[None, :] - col[:, None] + (NA_KW - 1), 0, 2 * NA_KW - 2)
    t = rpb[:, :, col_off]
    ro = jnp.arange(NA_KH)[None, :] - jnp.arange(NA_KH)[:, None] + (NA_KH - 1)
    t = t[:, ro]
    t = jnp.where(cmask[None, None, None], t, NEG_INF)
    t = t.transpose(1, 0, 3, 2, 4).reshape(NA_KH * NA_HEADS, GRID_W, NA_KH * GRID_W)
    return t.astype(f32)


def _dil_kernel(*refs):
    ng = len(DIL_CONFIGS)
    in_refs = refs[:7 * ng]
    o_ref = refs[7 * ng]
    scr = refs[7 * ng + 1:]
    kwins, vwins = scr[0:ng], scr[ng:2 * ng]
    o_scr, m_scr, l_scr = scr[2 * ng], scr[2 * ng + 1], scr[2 * ng + 2]
    ti = pl.program_id(1)
    n_tiles = pl.num_programs(1)
    qb = DIL_QBLOCK
    span = 3 * qb

    lane = lax.broadcasted_iota(jnp.int32, (qb, LANES), 1)
    first_head = lane < HEAD_DIM
    qi = lax.broadcasted_iota(jnp.int32, (qb, span), 0)
    kj = lax.broadcasted_iota(jnp.int32, (qb, span), 1)
    band = jnp.where((kj >= qi) & (kj <= qi + 2 * qb), 0.0, NEG_INF).astype(f32)
    kcol = lax.broadcasted_iota(jnp.int32, (1, span), 1)

    for g, (_, dil) in enumerate(DIL_CONFIGS):
        q_ref, kp, kc, kn, vp, vc, vn = in_refs[7 * g:7 * g + 7]
        kwin, vwin = kwins[g], vwins[g]
        seg = DIL_TILE // dil
        nj = seg // qb
        for r in range(dil):
            for win, (p_ref, c_ref, n_ref) in ((kwin, (kp, kc, kn)), (vwin, (vp, vc, vn))):
                win[r, 0:qb, :] = p_ref[0, (r + 1) * seg - qb:(r + 1) * seg, :]
                win[r, qb:qb + seg, :] = c_ref[0, r * seg:(r + 1) * seg, :]
                win[r, qb + seg:2 * qb + seg, :] = n_ref[0, r * seg:r * seg + qb, :]

        def blk_body(t, carry, g=g, dil=dil, nj=nj, q_ref=q_ref, kwin=kwin, vwin=vwin):
            r = t // nj
            j = t % nj
            pen_lo = jnp.where((ti == 0) & (j == 0), NEG_INF, 0.0).astype(f32)
            pen_hi = jnp.where((ti == n_tiles - 1) & (j == nj - 1), NEG_INF, 0.0).astype(f32)
            pen = jnp.where(kcol < qb, pen_lo, 0.0) + jnp.where(kcol >= 2 * qb, pen_hi, 0.0)
            mask = band + pen
            qoff = pl.multiple_of(t * qb, qb)
            koff = pl.multiple_of(j * qb, qb)
            store_rows = pl.ds(j * (qb * dil) + r, qb, stride=dil) if dil > 1 else pl.ds(qoff, qb)
            for hp in range(DIL_HPG // 2):
                cols = slice(hp * LANES, (hp + 1) * LANES)
                qp = q_ref[0, pl.ds(qoff, qb), cols]
                kk = kwin[r, pl.ds(koff, span), cols]
                vv = vwin[r, pl.ds(koff, span), cols]
                res = []
                for hh in range(2):
                    keep = first_head if hh == 0 else jnp.logical_not(first_head)
                    qm = jnp.where(keep, qp, jnp.zeros_like(qp))
                    s = lax.dot_general(qm, kk, (((1,), (1,)), ((), ())), preferred_element_type=f32)
                    s = s + mask
                    m = jnp.max(s, axis=-1, keepdims=True)
                    p = jnp.exp(s - m)
                    l = jnp.sum(p, axis=-1, keepdims=True)
                    pv = jnp.dot(p.astype(bf16), vv, preferred_element_type=f32)
                    res.append((pv, m, l))
                slot = g * (DIL_HPG // 2) + hp
                o_scr[slot, store_rows, :] = jnp.where(first_head, res[0][0], res[1][0])
                m_scr[slot, store_rows, :] = jnp.where(first_head, res[0][1], res[1][1])
                l_scr[slot, store_rows, :] = jnp.where(first_head, res[0][2], res[1][2])
            return carry

        lax.fori_loop(0, dil * nj, blk_body, 0)

    npair = DIL_HPG // 2
    for hp in range(npair):
        m_all = m_scr[hp]
        for g in range(1, ng):
            m_all = jnp.maximum(m_all, m_scr[g * npair + hp])
        num = jnp.zeros_like(m_all)
        den = jnp.zeros_like(m_all)
        for g in range(ng):
            w = jnp.exp(m_scr[g * npair + hp] - m_all)
            num = num + w * o_scr[g * npair + hp]
            den = den + w * l_scr[g * npair + hp]
        o_ref[0, :, hp * LANES:(hp + 1) * LANES] = (num / den).astype(bf16)


def _dil(qkv):
    bsz, s, _ = qkv.shape
    nt = s // DIL_TILE
    ng = len(DIL_CONFIGS)
    blk = (1, DIL_TILE, WB_OUT)
    in_specs = []
    for g in range(ng):
        in_specs.append(pl.BlockSpec(blk, lambda b, i, c=_J_QB + g: (b, i, c)))
        for base in (_J_KB, _J_VB):
            c = base + g
            in_specs += [pl.BlockSpec(blk, lambda b, i, c=c: (b, jnp.maximum(i - 1, 0), c)),
                         pl.BlockSpec(blk, lambda b, i, c=c: (b, i, c)),
                         pl.BlockSpec(blk, lambda b, i, c=c: (b, jnp.minimum(i + 1, nt - 1), c))]
    win_shapes = [pltpu.VMEM((dil, DIL_TILE // dil + 2 * DIL_QBLOCK, WB_OUT), bf16) for _, dil in DIL_CONFIGS]
    return pl.pallas_call(
        _dil_kernel,
        grid=(bsz, nt),
        in_specs=in_specs,
        out_specs=pl.BlockSpec(blk, lambda b, i: (b, i, 0)),
        out_shape=jax.ShapeDtypeStruct((bsz, s, WB_OUT), bf16),
        scratch_shapes=win_shapes + win_shapes + [pltpu.VMEM((ng * DIL_HPG // 2, DIL_TILE, LANES), f32)] * 3,
        compiler_params=_params(("parallel", "parallel")),
        name="dil",
    )(*([qkv] * (7 * ng)))


def _tail_kernel(x_ref, oa_ref, ob_ref, mod_ref, g1_ref, g2_ref, wg_ref, bg_ref, wpa_ref, wpb_ref,
                 wo_ref, wa_ref, wu_ref, wout_ref, o_ref):
    d = x_ref.shape[2]
    x = x_ref[0]
    mod = mod_ref[0]
    sh1, sc1, gt1, sh2, sc2, gt2 = [mod[k:k + 1] for k in range(6)]
    h = _modulated_norm(x, g1_ref[...], sc1, sh1).astype(bf16)
    gates = jax.nn.sigmoid(jnp.dot(h, wg_ref[...], preferred_element_type=f32) + bg_ref[...])
    pa = jnp.dot(oa_ref[0], wpa_ref[...], preferred_element_type=f32)
    pb = jnp.dot(ob_ref[0], wpb_ref[...], preferred_element_type=f32)
    merged = gates[:, :d] * pa + gates[:, d:] * pb
    x1 = x + gt1 * jnp.dot(merged.astype(bf16), wo_ref[...], preferred_element_type=f32)
    h2 = _modulated_norm(x1, g2_ref[...], sc2, sh2).astype(bf16)

    def ff_body(f, acc):
        a = jnp.dot(h2, wa_ref[f], preferred_element_type=f32)
        u = jnp.dot(h2, wu_ref[f], preferred_element_type=f32)
        act = (a * jax.nn.sigmoid(a)) * u
        return acc + jnp.dot(act.astype(bf16), wout_ref[f], preferred_element_type=f32)

    acc = lax.fori_loop(0, wa_ref.shape[0], ff_body, jnp.zeros_like(x))
    o_ref[0] = x1 + gt2 * acc


def _tail(x, o_a, o_b, mod6, g1, g2, wg, bg, wpa, wpb, wo, wa3, wu3, wout3, tm=TAIL_TM):
    bsz, s, d = x.shape
    tok = lambda w: pl.BlockSpec((1, tm, w), lambda b, i: (b, i, 0))
    consts = [g1, g2, wg, bg, wpa, wpb, wo, wa3, wu3, wout3]
    return pl.pallas_call(
        _tail_kernel,
        grid=(bsz, s // tm),
        in_specs=[tok(d), tok(WA), tok(WB_OUT), pl.BlockSpec((1, 6, d), lambda b, i: (b, 0, 0))]
        + [_const_spec(c.shape) for c in consts],
        out_specs=tok(d),
        out_shape=jax.ShapeDtypeStruct((bsz, s, d), f32),
        compiler_params=_params(("parallel", "parallel")),
        name="tail",
    )(x, o_a, o_b, mod6, *consts)


def _rotary_tables(s):
    half = ROT_DIM // 2
    inv_freq = ROPE_THETA ** (-(jnp.arange(half, dtype=f32) * 2.0) / ROT_DIM)
    ang = jnp.arange(s).astype(f32)[:, None] * inv_freq[None, :]
    cos, sin = jnp.cos(ang), jnp.sin(ang)
    ones = jnp.ones((s, HEAD_DIM - ROT_DIM), f32)
    zeros = jnp.zeros((s, HEAD_DIM - ROT_DIM), f32)
    zh = jnp.zeros((s, half), f32)
    cos_h = jnp.concatenate([cos, cos, ones], axis=1)
    sina_h = jnp.concatenate([-sin, zh, zeros], axis=1)
    sinb_h = jnp.concatenate([zh, sin, zeros], axis=1)
    rep = LANES // HEAD_DIM
    return tuple(jnp.tile(t, (1, rep)) for t in (cos_h, sina_h, sinb_h))


def _layer(x, mod6, g_norm1, g_norm2, w_in, b_gate, g_qa, g_ka, g_qb, g_kb, rpb,
           w_proj_a, w_proj_b, w_o, w_ffn_in, w_ffn_out):
    bsz, s, d = x.shape
    scale = HEAD_DIM ** -0.5
    rep = COL_TILE // HEAD_DIM
    gvec = jnp.stack([jnp.tile(g_qa * scale, rep), jnp.tile(g_ka, rep),
                      jnp.tile(g_qb * scale, rep), jnp.tile(g_kb, rep)]).astype(f32)
    hid = jnp.arange(COL_TILE) // HEAD_DIM
    ones_bd = (hid[:, None] == hid[None, :]).astype(bf16)
    cos_t, sina_t, sinb_t = _rotary_tables(s)
    g1 = g_norm1.reshape(1, d)
    g2 = g_norm2.reshape(1, d)

    qkv = _qkv(x, mod6, g1, w_in[:, :W_QKV].astype(bf16), gvec, ones_bd, cos_t, sina_t, sinb_t)
    o_a = _na(qkv, _na_bias_table(rpb))
    o_b = _dil(qkv)

    nf = D_FF // FF_CHUNK
    wa3 = w_ffn_in[:, :D_FF].astype(bf16).reshape(d, nf, FF_CHUNK).transpose(1, 0, 2)
    wu3 = w_ffn_in[:, D_FF:].astype(bf16).reshape(d, nf, FF_CHUNK).transpose(1, 0, 2)
    wout3 = w_ffn_out.astype(bf16).reshape(nf, FF_CHUNK, d)
    return _tail(x, o_a, o_b, mod6, g1, g2, w_in[:, W_QKV:].astype(bf16), b_gate.reshape(1, 2 * d),
                 w_proj_a.astype(bf16), w_proj_b.astype(bf16), w_o.astype(bf16), wa3, wu3, wout3)


def kernel(x, c, w_ada, b_ada, g_norm1, g_norm2, w_in, b_gate, g_qa, g_ka, g_qb, g_kb, rpb,
           w_proj_a, w_proj_b, w_o, w_ffn_in, w_ffn_out):
    depth = w_ada.shape[0]
    bsz, d = c.shape
    for l in range(depth):
        mod6 = _mod(c, w_ada[l], b_ada[l]).reshape(bsz, 6, d)
        x = _layer(x, mod6, g_norm1[l], g_norm2[l], w_in[l], b_gate[l], g_qa[l], g_ka[l], g_qb[l],
                   g_kb[l], rpb[l], w_proj_a[l], w_proj_b[l], w_o[l], w_ffn_in[l], w_ffn_out[l])
    return x
```

```python
import functools

import jax
import jax.numpy as jnp
from jax import lax
from jax.experimental import pallas as pl
from jax.experimental.pallas import tpu as pltpu

f32 = jnp.float32
bf16 = jnp.bfloat16

D_MODEL = 1024
HEAD_DIM = 64
GRID_W = 64
NA_HEADS = 8
NA_KH = 8
NA_KW = 16
DIL_CONFIGS = ((128, 1), (512, 4), (2048, 16))
DIL_HPG = 4
DIL_HEADS = DIL_HPG * len(DIL_CONFIGS)
DIL_QBLOCK = 64
ROT_DIM = HEAD_DIM // 4
ROPE_THETA = 500000.0
D_FF = -(-8 * D_MODEL // (3 * 256)) * 256
EPS = 1e-6
NEG_INF = -1e30
WA = NA_HEADS * HEAD_DIM
WB = DIL_HEADS * HEAD_DIM
WB_OUT = DIL_HPG * HEAD_DIM
W_QKV = 3 * WA + 3 * WB

LANES = 128
COL_TILE = 256
DIL_TILE = 1024
DIL_UNROLL = 4
NA_ROWS = 8
TAIL_TM = 512
FF_CHUNK = 256
VMEM_LIMIT = 56 * 1024 * 1024


def _params(sem):
    return pltpu.CompilerParams(dimension_semantics=sem, vmem_limit_bytes=VMEM_LIMIT)


def _const_spec(shape):
    nd = len(shape)
    return pl.BlockSpec(shape, lambda *_: (0,) * nd, pipeline_mode=pl.Buffered(1))


def _mod_kernel(ct_ref, w_ref, b_ref, o_ref):
    ct = ct_ref[...]
    act = ct * jax.nn.sigmoid(ct)
    w = w_ref[...]
    rows = [jnp.sum(act[:, b:b + 1] * w, axis=0, keepdims=True) for b in range(ct.shape[1])]
    o_ref[...] = jnp.concatenate(rows, axis=0) + b_ref[...]


def _mod(c, w_ada, b_ada):
    bsz, d = c.shape
    n = w_ada.shape[1]
    tn = n // 4
    return pl.pallas_call(
        _mod_kernel,
        grid=(n // tn,),
        in_specs=[pl.BlockSpec((d, bsz), lambda j: (0, 0)),
                  pl.BlockSpec((d, tn), lambda j: (0, j)),
                  pl.BlockSpec((1, tn), lambda j: (0, j))],
        out_specs=pl.BlockSpec((bsz, tn), lambda j: (0, j)),
        out_shape=jax.ShapeDtypeStruct((bsz, n), f32),
        compiler_params=_params(("arbitrary",)),
        name="mod",
    )(c.T, w_ada, b_ada.reshape(1, n))


def _modulated_norm(x, g, sc, sh):
    ms = jnp.mean(x * x, axis=-1, keepdims=True)
    y = x * lax.rsqrt(ms + EPS) * g
    return y * (1.0 + sc) + sh


_J_QB = 3 * WA // COL_TILE
_J_KB = _J_QB + WB // COL_TILE
_J_VB = _J_KB + WB // COL_TILE
_N_J = W_QKV // COL_TILE


def _qkv_kernel(x_ref, mod_ref, g1_ref, w_ref, gvec_ref, ones_ref, cos_ref, sina_ref, sinb_ref,
                o_ref, h_scr, y_scr):
    j = pl.program_id(2)
    tm = x_ref.shape[1]

    @pl.when(j == 0)
    def _():
        mod = mod_ref[0]
        h = _modulated_norm(x_ref[0], g1_ref[...], mod[1:2], mod[0:1])
        h_scr[...] = h.astype(bf16)

    acc = jnp.dot(h_scr[...], w_ref[...], preferred_element_type=f32)

    is_v = ((j >= 2 * WA // COL_TILE) & (j < _J_QB)) | (j >= _J_VB)
    is_b = j >= _J_QB
    kind = jnp.where(j < WA // COL_TILE, 0, jnp.where(j < _J_QB, 1, jnp.where(j < _J_KB, 2, 3)))

    def put_y(y):
        for c in range(COL_TILE // LANES):
            y_scr[c] = y[:, c * LANES:(c + 1) * LANES]

    @pl.when(is_v)
    def _():
        put_y(acc)

    @pl.when(jnp.logical_not(is_v))
    def _():
        ss = jnp.dot((acc * acc).astype(bf16), ones_ref[...], preferred_element_type=f32)
        y = acc * lax.rsqrt(ss * (1.0 / HEAD_DIM) + EPS) * gvec_ref[pl.ds(kind, 1), :]

        @pl.when(is_b)
        def _():
            cos = jnp.concatenate([cos_ref[...]] * (COL_TILE // LANES), axis=1)
            sina = jnp.concatenate([sina_ref[...]] * (COL_TILE // LANES), axis=1)
            sinb = jnp.concatenate([sinb_ref[...]] * (COL_TILE // LANES), axis=1)
            half = ROT_DIM // 2
            up = pltpu.roll(y, COL_TILE - half, 1)
            dn = pltpu.roll(y, half, 1)
            put_y(y * cos + up * sina + dn * sinb)

        @pl.when(jnp.logical_not(is_b))
        def _():
            put_y(y)

    grp = jnp.where(is_b, (j - _J_QB) % len(DIL_CONFIGS), 0)

    @pl.when(grp == 0)
    def _():
        for c in range(COL_TILE // LANES):
            o_ref[0, :, c * LANES:(c + 1) * LANES] = y_scr[c].astype(bf16)

    for g, (_, dil) in enumerate(DIL_CONFIGS):
        if dil == 1:
            continue

        @pl.when(grp == g)
        def _(dil=dil):
            seg = DIL_TILE // dil
            for t in range(tm // DIL_TILE):
                for r in range(dil):
                    for c in range(COL_TILE // LANES):
                        rows = y_scr[c, pl.ds(t * DIL_TILE + r, seg, stride=dil), :]
                        o_ref[0, t * DIL_TILE + r * seg:t * DIL_TILE + (r + 1) * seg,
                              c * LANES:(c + 1) * LANES] = rows.astype(bf16)


def _qkv(x, mod6, g1, w_qkv, gvec, ones_bd, cos_t, sina_t, sinb_t, tm=DIL_TILE):
    bsz, s, d = x.shape
    return pl.pallas_call(
        _qkv_kernel,
        grid=(bsz, s // tm, _N_J),
        in_specs=[pl.BlockSpec((1, tm, d), lambda b, i, j: (b, i, 0)),
                  pl.BlockSpec((1, 6, d), lambda b, i, j: (b, 0, 0)),
                  pl.BlockSpec((1, d), lambda b, i, j: (0, 0)),
                  pl.BlockSpec((d, COL_TILE), lambda b, i, j: (0, j)),
                  pl.BlockSpec((4, COL_TILE), lambda b, i, j: (0, 0)),
                  pl.BlockSpec((COL_TILE, COL_TILE), lambda b, i, j: (0, 0)),
                  pl.BlockSpec((tm, LANES), lambda b, i, j: (i, 0)),
                  pl.BlockSpec((tm, LANES), lambda b, i, j: (i, 0)),
                  pl.BlockSpec((tm, LANES), lambda b, i, j: (i, 0))],
        out_specs=pl.BlockSpec((1, tm, COL_TILE), lambda b, i, j: (b, i, j)),
        out_shape=jax.ShapeDtypeStruct((bsz, s, W_QKV), bf16),
        scratch_shapes=[pltpu.VMEM((tm, d), bf16), pltpu.VMEM((COL_TILE // LANES, tm, LANES), f32)],
        compiler_params=_params(("parallel", "parallel", "arbitrary")),
        name="qkv",
    )(x, mod6, g1, w_qkv, gvec, ones_bd, cos_t, sina_t, sinb_t)


def _pair_mask(nq):
    row = lax.broadcasted_iota(jnp.int32, (2 * nq, LANES), 0)
    lane = lax.broadcasted_iota(jnp.int32, (2 * nq, LANES), 1)
    return (row < nq) == (lane < HEAD_DIM)


def _stack_pair(qp, own_head):
    q2 = jnp.concatenate([qp, qp], axis=0)
    return jnp.where(own_head, q2, jnp.zeros_like(q2))


def _na_kernel(q_ref, kp_ref, kc_ref, kn_ref, vp_ref, vc_ref, vn_ref, tbl_ref, o_ref, kwin, vwin):
    rb = pl.program_id(1)
    blk = NA_ROWS * GRID_W
    n_rows = pl.num_programs(1) * NA_ROWS
    for t, (kr, vr) in enumerate(((kp_ref, vp_ref), (kc_ref, vc_ref), (kn_ref, vn_ref))):
        kwin[t * blk:(t + 1) * blk, :] = kr[0]
        vwin[t * blk:(t + 1) * blk, :] = vr[0]
    first_head = lax.broadcasted_iota(jnp.int32, (GRID_W, LANES), 1) < HEAD_DIM
    own_head = _pair_mask(GRID_W)
    nkeys = NA_KH * GRID_W
    npair = NA_HEADS // 2
    pair_rows = 2 * GRID_W

    def row_body(a, carry):
        r = rb * NA_ROWS + a
        row_start = jnp.clip(r - NA_KH // 2, 0, n_rows - NA_KH)
        delta = r - row_start
        off = pl.multiple_of((row_start - (rb - 1) * NA_ROWS) * GRID_W, GRID_W)
        qoff = pl.multiple_of(a * GRID_W, GRID_W)
        s_parts = []
        for hp in range(npair):
            cols = slice(hp * LANES, (hp + 1) * LANES)
            q2 = _stack_pair(q_ref[0, pl.ds(qoff, GRID_W), cols], own_head)
            kk = kwin[pl.ds(off, nkeys), cols]
            s_parts.append(lax.dot_general(q2, kk, (((1,), (1,)), ((), ())), preferred_element_type=f32))
        toff = pl.multiple_of(delta * (NA_HEADS * GRID_W), NA_HEADS * GRID_W)
        s = jnp.concatenate(s_parts, axis=0) + tbl_ref[pl.ds(toff, NA_HEADS * GRID_W), :]
        m = jnp.max(s, axis=-1, keepdims=True)
        p = jnp.exp(s - m)
        inv = 1.0 / jnp.sum(p, axis=-1, keepdims=True)
        pb = p.astype(bf16)
        for hp in range(npair):
            cols = slice(hp * LANES, (hp + 1) * LANES)
            rows = slice(hp * pair_rows, (hp + 1) * pair_rows)
            vv = vwin[pl.ds(off, nkeys), cols]
            pv = jnp.dot(pb[rows], vv, preferred_element_type=f32) * inv[rows]
            o_ref[0, pl.ds(qoff, GRID_W), cols] = jnp.where(first_head, pv[:GRID_W], pv[GRID_W:]).astype(bf16)
        return carry

    lax.fori_loop(0, NA_ROWS, row_body, 0)


def _na(qkv, tbl):
    bsz, s, _ = qkv.shape
    blk = NA_ROWS * GRID_W
    nb = s // blk
    qspec = pl.BlockSpec((1, blk, WA), lambda b, i: (b, i, 0))

    def halo(col):
        return [pl.BlockSpec((1, blk, WA), lambda b, i: (b, jnp.maximum(i - 1, 0), col)),
                pl.BlockSpec((1, blk, WA), lambda b, i: (b, i, col)),
                pl.BlockSpec((1, blk, WA), lambda b, i: (b, jnp.minimum(i + 1, nb - 1), col))]

    return pl.pallas_call(
        _na_kernel,
        grid=(bsz, nb),
        in_specs=[qspec] + halo(1) + halo(2) + [_const_spec(tbl.shape)],
        out_specs=pl.BlockSpec((1, blk, WA), lambda b, i: (b, i, 0)),
        out_shape=jax.ShapeDtypeStruct((bsz, s, WA), bf16),
        scratch_shapes=[pltpu.VMEM((3 * blk, WA), bf16), pltpu.VMEM((3 * blk, WA), bf16)],
        compiler_params=_params(("parallel", "parallel")),
        name="na",
    )(qkv, qkv, qkv, qkv, qkv, qkv, qkv, tbl)


def _na_bias_table(rpb):
    col = jnp.arange(GRID_W)
    col_start = jnp.clip(col - NA_KW // 2, 0, GRID_W - NA_KW)
    cmask = (col[None, :] >= col_start[:, None]) & (col[None, :] < col_start[:, None] + NA_KW)
    col_off = jnp.clip(col[None, :] - col[:, None] + (NA_KW - 1), 0, 2 * NA_KW - 2)
    t = rpb[:, :, col_off]
    ro = jnp.arange(NA_KH)[None, :] - jnp.arange(NA_KH)[:, None] + (NA_KH - 1)
    t = t[:, ro]
    t = jnp.where(cmask[None, None, None], t, NEG_INF)
    t = t.transpose(1, 0, 3, 2, 4).reshape(NA_KH * NA_HEADS * GRID_W, NA_KH * GRID_W)
    return t.astype(f32)


def _dil_kernel(*refs):
    ng = len(DIL_CONFIGS)
    in_refs = refs[:7 * ng]
    o_ref = refs[7 * ng]
    scr = refs[7 * ng + 1:]
    kwins, vwins = scr[0:ng], scr[ng:2 * ng]
    o_scr, m_scr, l_scr = scr[2 * ng], scr[2 * ng + 1], scr[2 * ng + 2]
    ti = pl.program_id(1)
    n_tiles = pl.num_programs(1)
    qb = DIL_QBLOCK
    span = 3 * qb

    first_head = lax.broadcasted_iota(jnp.int32, (qb, LANES), 1) < HEAD_DIM
    own_head = _pair_mask(qb)
    npair = DIL_HPG // 2
    qi = lax.broadcasted_iota(jnp.int32, (qb, span), 0)
    kj = lax.broadcasted_iota(jnp.int32, (qb, span), 1)
    band = jnp.where((kj >= qi) & (kj <= qi + 2 * qb), 0.0, NEG_INF).astype(f32)
    kcol = lax.broadcasted_iota(jnp.int32, (1, span), 1)

    for g, (_, dil) in enumerate(DIL_CONFIGS):
        q_ref, kp, kc, kn, vp, vc, vn = in_refs[7 * g:7 * g + 7]
        kwin, vwin = kwins[g], vwins[g]
        seg = DIL_TILE // dil
        nj = seg // qb
        for r in range(dil):
            for win, (p_ref, c_ref, n_ref) in ((kwin, (kp, kc, kn)), (vwin, (vp, vc, vn))):
                win[r, 0:qb, :] = p_ref[0, (r + 1) * seg - qb:(r + 1) * seg, :]
                win[r, qb:qb + seg, :] = c_ref[0, r * seg:(r + 1) * seg, :]
                win[r, qb + seg:2 * qb + seg, :] = n_ref[0, r * seg:r * seg + qb, :]

        def blk_body(u, carry, g=g, dil=dil, nj=nj, q_ref=q_ref, kwin=kwin, vwin=vwin):
            blocks = []
            s_parts = []
            for k in range(DIL_UNROLL):
                t = u * DIL_UNROLL + k
                r = t // nj
                j = t % nj
                pen_lo = jnp.where((ti == 0) & (j == 0), NEG_INF, 0.0).astype(f32)
                pen_hi = jnp.where((ti == n_tiles - 1) & (j == nj - 1), NEG_INF, 0.0).astype(f32)
                mask = band + (jnp.where(kcol < qb, pen_lo, 0.0) + jnp.where(kcol >= 2 * qb, pen_hi, 0.0))
                qoff = pl.multiple_of(t * qb, qb)
                koff = pl.multiple_of(j * qb, qb)
                store_rows = pl.ds(j * (qb * dil) + r, qb, stride=dil) if dil > 1 else pl.ds(qoff, qb)
                blocks.append((r, koff, store_rows))
                for hp in range(npair):
                    cols = slice(hp * LANES, (hp + 1) * LANES)
                    q2 = _stack_pair(q_ref[0, pl.ds(qoff, qb), cols], own_head)
                    kk = kwin[r, pl.ds(koff, span), cols]
                    s2 = lax.dot_general(q2, kk, (((1,), (1,)), ((), ())), preferred_element_type=f32)
                    s_parts += [s2[:qb] + mask, s2[qb:] + mask]
            s = jnp.concatenate(s_parts, axis=0)
            m = jnp.max(s, axis=-1, keepdims=True)
            p = jnp.exp(s - m)
            l = jnp.sum(p, axis=-1, keepdims=True)
            pb = p.astype(bf16)
            for k, (r, koff, store_rows) in enumerate(blocks):
                for hp in range(npair):
                    cols = slice(hp * LANES, (hp + 1) * LANES)
                    r0 = (k * npair + hp) * 2 * qb
                    vv = vwin[r, pl.ds(koff, span), cols]
                    pv = jnp.dot(pb[r0:r0 + 2 * qb], vv, preferred_element_type=f32)
                    slot = g * npair + hp
                    o_scr[slot, store_rows, :] = jnp.where(first_head, pv[:qb], pv[qb:])
                    m_scr[slot, store_rows, :] = jnp.where(first_head, m[r0:r0 + qb], m[r0 + qb:r0 + 2 * qb])
                    l_scr[slot, store_rows, :] = jnp.where(first_head, l[r0:r0 + qb], l[r0 + qb:r0 + 2 * qb])
            return carry

        lax.fori_loop(0, dil * nj // DIL_UNROLL, blk_body, 0)

    for hp in range(npair):
        m_all = m_scr[hp]
        for g in range(1, ng):
            m_all = jnp.maximum(m_all, m_scr[g * npair + hp])
        num = jnp.zeros_like(m_all)
        den = jnp.zeros_like(m_all)
        for g in range(ng):
            w = jnp.exp(m_scr[g * npair + hp] - m_all)
            num = num + w * o_scr[g * npair + hp]
            den = den + w * l_scr[g * npair + hp]
        o_ref[0, :, hp * LANES:(hp + 1) * LANES] = (num / den).astype(bf16)


def _dil(qkv):
    bsz, s, _ = qkv.shape
    nt = s // DIL_TILE
    ng = len(DIL_CONFIGS)
    blk = (1, DIL_TILE, WB_OUT)
    in_specs = []
    for g in range(ng):
        in_specs.append(pl.BlockSpec(blk, lambda b, i, c=_J_QB + g: (b, i, c)))
        for base in (_J_KB, _J_VB):
            c = base + g
            in_specs += [pl.BlockSpec(blk, lambda b, i, c=c: (b, jnp.maximum(i - 1, 0), c)),
                         pl.BlockSpec(blk, lambda b, i, c=c: (b, i, c)),
                         pl.BlockSpec(blk, lambda b, i, c=c: (b, jnp.minimum(i + 1, nt - 1), c))]
    win_shapes = [pltpu.VMEM((dil, DIL_TILE // dil + 2 * DIL_QBLOCK, WB_OUT), bf16) for _, dil in DIL_CONFIGS]
    return pl.pallas_call(
        _dil_kernel,
        grid=(bsz, nt),
        in_specs=in_specs,
        out_specs=pl.BlockSpec(blk, lambda b, i: (b, i, 0)),
        out_shape=jax.ShapeDtypeStruct((bsz, s, WB_OUT), bf16),
        scratch_shapes=win_shapes + win_shapes + [pltpu.VMEM((ng * DIL_HPG // 2, DIL_TILE, LANES), f32)] * 3,
        compiler_params=_params(("parallel", "parallel")),
        name="dil",
    )(*([qkv] * (7 * ng)))


def _tail_kernel(x_ref, oa_ref, ob_ref, mod_ref, g1_ref, g2_ref, wg_ref, bg_ref, wpa_ref, wpb_ref,
                 wo_ref, wa_ref, wu_ref, wout_ref, o_ref):
    d = x_ref.shape[2]
    x = x_ref[0]
    mod = mod_ref[0]
    sh1, sc1, gt1, sh2, sc2, gt2 = [mod[k:k + 1] for k in range(6)]
    h = _modulated_norm(x, g1_ref[...], sc1, sh1).astype(bf16)
    gates = jax.nn.sigmoid(jnp.dot(h, wg_ref[...], preferred_element_type=f32) + bg_ref[...])
    pa = jnp.dot(oa_ref[0], wpa_ref[...], preferred_element_type=f32)
    pb = jnp.dot(ob_ref[0], wpb_ref[...], preferred_element_type=f32)
    merged = gates[:, :d] * pa + gates[:, d:] * pb
    x1 = x + gt1 * jnp.dot(merged.astype(bf16), wo_ref[...], preferred_element_type=f32)
    h2 = _modulated_norm(x1, g2_ref[...], sc2, sh2).astype(bf16)

    def ff_body(f, acc):
        a = jnp.dot(h2, wa_ref[f], preferred_element_type=f32)
        u = jnp.dot(h2, wu_ref[f], preferred_element_type=f32)
        act = (a * jax.nn.sigmoid(a)) * u
        return acc + jnp.dot(act.astype(bf16), wout_ref[f], preferred_element_type=f32)

    acc = lax.fori_loop(0, wa_ref.shape[0], ff_body, jnp.zeros_like(x))
    o_ref[0] = x1 + gt2 * acc


def _tail(x, o_a, o_b, mod6, g1, g2, wg, bg, wpa, wpb, wo, wa3, wu3, wout3, tm=TAIL_TM):
    bsz, s, d = x.shape
    tok = lambda w: pl.BlockSpec((1, tm, w), lambda b, i: (b, i, 0))
    consts = [g1, g2, wg, bg, wpa, wpb, wo, wa3, wu3, wout3]
    return pl.pallas_call(
        _tail_kernel,
        grid=(bsz, s // tm),
        in_specs=[tok(d), tok(WA), tok(WB_OUT), pl.BlockSpec((1, 6, d), lambda b, i: (b, 0, 0))]
        + [_const_spec(c.shape) for c in consts],
        out_specs=tok(d),
        out_shape=jax.ShapeDtypeStruct((bsz, s, d), f32),
        compiler_params=_params(("parallel", "parallel")),
        name="tail",
    )(x, o_a, o_b, mod6, *consts)


def _rotary_tables(s):
    half = ROT_DIM // 2
    inv_freq = ROPE_THETA ** (-(jnp.arange(half, dtype=f32) * 2.0) / ROT_DIM)
    ang = jnp.arange(s).astype(f32)[:, None] * inv_freq[None, :]
    cos, sin = jnp.cos(ang), jnp.sin(ang)
    ones = jnp.ones((s, HEAD_DIM - ROT_DIM), f32)
    zeros = jnp.zeros((s, HEAD_DIM - ROT_DIM), f32)
    zh = jnp.zeros((s, half), f32)
    cos_h = jnp.concatenate([cos, cos, ones], axis=1)
    sina_h = jnp.concatenate([-sin, zh, zeros], axis=1)
    sinb_h = jnp.concatenate([zh, sin, zeros], axis=1)
    rep = LANES // HEAD_DIM
    return tuple(jnp.tile(t, (1, rep)) for t in (cos_h, sina_h, sinb_h))


def _layer(x, mod6, g_norm1, g_norm2, w_in, b_gate, g_qa, g_ka, g_qb, g_kb, rpb,
           w_proj_a, w_proj_b, w_o, w_ffn_in, w_ffn_out):
    bsz, s, d = x.shape
    scale = HEAD_DIM ** -0.5
    rep = COL_TILE // HEAD_DIM
    gvec = jnp.stack([jnp.tile(g_qa * scale, rep), jnp.tile(g_ka, rep),
                      jnp.tile(g_qb * scale, rep), jnp.tile(g_kb, rep)]).astype(f32)
    hid = jnp.arange(COL_TILE) // HEAD_DIM
    ones_bd = (hid[:, None] == hid[None, :]).astype(bf16)
    cos_t, sina_t, sinb_t = _rotary_tables(s)
    g1 = g_norm1.reshape(1, d)
    g2 = g_norm2.reshape(1, d)

    qkv = _qkv(x, mod6, g1, w_in[:, :W_QKV].astype(bf16), gvec, ones_bd, cos_t, sina_t, sinb_t)
    o_a = _na(qkv, _na_bias_table(rpb))
    o_b = _dil(qkv)

    nf = D_FF // FF_CHUNK
    wa3 = w_ffn_in[:, :D_FF].astype(bf16).reshape(d, nf, FF_CHUNK).transpose(1, 0, 2)
    wu3 = w_ffn_in[:, D_FF:].astype(bf16).reshape(d, nf, FF_CHUNK).transpose(1, 0, 2)
    wout3 = w_ffn_out.astype(bf16).reshape(nf, FF_CHUNK, d)
    return _tail(x, o_a, o_b, mod6, g1, g2, w_in[:, W_QKV:].astype(bf16), b_gate.reshape(1, 2 * d),
                 w_proj_a.astype(bf16), w_proj_b.astype(bf16), w_o.astype(bf16), wa3, wu3, wout3)


def kernel(x, c, w_ada, b_ada, g_norm1, g_norm2, w_in, b_gate, g_qa, g_ka, g_qb, g_kb, rpb,
           w_proj_a, w_proj_b, w_o, w_ffn_in, w_ffn_out):
    depth = w_ada.shape[0]
    bsz, d = c.shape
    for l in range(depth):
        mod6 = _mod(c, w_ada[l], b_ada[l]).reshape(bsz, 6, d)
        x = _layer(x, mod6, g_norm1[l], g_norm2[l], w_in[l], b_gate[l], g_qa[l], g_ka[l], g_qb[l],
                   g_kb[l], rpb[l], w_proj_a[l], w_proj_b[l], w_o[l], w_ffn_in[l], w_ffn_out[l])
    return x
```

```python
import functools

import jax
import jax.numpy as jnp
from jax import lax
from jax.experimental import pallas as pl
from jax.experimental.pallas import tpu as pltpu

f32 = jnp.float32
bf16 = jnp.bfloat16

D_MODEL = 1024
HEAD_DIM = 64
GRID_W = 64
NA_HEADS = 8
NA_KH = 8
NA_KW = 16
DIL_CONFIGS = ((128, 1), (512, 4), (2048, 16))
DIL_HPG = 4
DIL_HEADS = DIL_HPG * len(DIL_CONFIGS)
DIL_QBLOCK = 64
ROT_DIM = HEAD_DIM // 4
ROPE_THETA = 500000.0
D_FF = -(-8 * D_MODEL // (3 * 256)) * 256
EPS = 1e-6
NEG_INF = -1e30
WA = NA_HEADS * HEAD_DIM
WB = DIL_HEADS * HEAD_DIM
WB_OUT = DIL_HPG * HEAD_DIM
W_QKV = 3 * WA + 3 * WB

LANES = 128
COL_TILE = 256
DIL_TILE = 1024
DIL_UNROLL = 4
NA_ROWS = 8
TAIL_TM = 512
FF_CHUNK = 256
VMEM_LIMIT = 56 * 1024 * 1024


def _params(sem):
    return pltpu.CompilerParams(dimension_semantics=sem, vmem_limit_bytes=VMEM_LIMIT)


def _const_spec(shape):
    nd = len(shape)
    return pl.BlockSpec(shape, lambda *_: (0,) * nd, pipeline_mode=pl.Buffered(1))


def _mod_kernel(ct_ref, w_ref, b_ref, o_ref):
    ct = ct_ref[...]
    act = ct * jax.nn.sigmoid(ct)
    w = w_ref[...]
    rows = [jnp.sum(act[:, b:b + 1] * w, axis=0, keepdims=True) for b in range(ct.shape[1])]
    o_ref[...] = jnp.concatenate(rows, axis=0) + b_ref[...]


def _mod(c, w_ada, b_ada):
    bsz, d = c.shape
    n = w_ada.shape[1]
    tn = n // 4
    return pl.pallas_call(
        _mod_kernel,
        grid=(n // tn,),
        in_specs=[pl.BlockSpec((d, bsz), lambda j: (0, 0)),
                  pl.BlockSpec((d, tn), lambda j: (0, j)),
                  pl.BlockSpec((1, tn), lambda j: (0, j))],
        out_specs=pl.BlockSpec((bsz, tn), lambda j: (0, j)),
        out_shape=jax.ShapeDtypeStruct((bsz, n), f32),
        compiler_params=_params(("arbitrary",)),
        name="mod",
    )(c.T, w_ada, b_ada.reshape(1, n))


def _modulated_norm(x, g, sc, sh):
    ms = jnp.mean(x * x, axis=-1, keepdims=True)
    y = x * lax.rsqrt(ms + EPS) * g
    return y * (1.0 + sc) + sh


_J_QB = 3 * WA // COL_TILE
_J_KB = _J_QB + WB // COL_TILE
_J_VB = _J_KB + WB // COL_TILE
_N_J = W_QKV // COL_TILE


def _qkv_tile_kind(j):
    if j < WA // COL_TILE:
        return 0, False, 1
    if j < 2 * WA // COL_TILE:
        return 1, False, 1
    if j < _J_QB:
        return None, False, 1
    dil = DIL_CONFIGS[(j - _J_QB) % len(DIL_CONFIGS)][1]
    if j < _J_KB:
        return 2, True, dil
    if j < _J_VB:
        return 3, True, dil
    return None, False, dil


def _qkv_kernel(x_ref, mod_ref, g1_ref, w_ref, gvec_ref, ones_ref, cos_ref, sina_ref, sinb_ref,
                o_ref, h_scr, y_scr):
    tm = x_ref.shape[1]
    nc = COL_TILE // LANES
    mod = mod_ref[0]
    h_scr[...] = _modulated_norm(x_ref[0], g1_ref[...], mod[1:2], mod[0:1]).astype(bf16)

    n_perm = 0
    for j in range(_N_J):
        gain_row, rotary, dil = _qkv_tile_kind(j)
        cols = slice(j * COL_TILE, (j + 1) * COL_TILE)
        y = jnp.dot(h_scr[...], w_ref[:, cols], preferred_element_type=f32)
        if gain_row is not None:
            ss = jnp.dot((y * y).astype(bf16), ones_ref[...], preferred_element_type=f32)
            y = y * lax.rsqrt(ss * (1.0 / HEAD_DIM) + EPS) * gvec_ref[gain_row:gain_row + 1, :]
        if rotary:
            cos = jnp.concatenate([cos_ref[...]] * nc, axis=1)
            sina = jnp.concatenate([sina_ref[...]] * nc, axis=1)
            sinb = jnp.concatenate([sinb_ref[...]] * nc, axis=1)
            half = ROT_DIM // 2
            up = pltpu.roll(y, COL_TILE - half, 1)
            dn = pltpu.roll(y, half, 1)
            y = y * cos + up * sina + dn * sinb
        if dil == 1:
            o_ref[0, :, cols] = y.astype(bf16)
            continue
        slot = n_perm % y_scr.shape[0]
        n_perm += 1
        seg = DIL_TILE // dil
        for c in range(nc):
            y_scr[slot, c] = y[:, c * LANES:(c + 1) * LANES]
        for t in range(tm // DIL_TILE):
            for r in range(dil):
                for c in range(nc):
                    rows = y_scr[slot, c, pl.ds(t * DIL_TILE + r, seg, stride=dil), :]
                    o_ref[0, t * DIL_TILE + r * seg:t * DIL_TILE + (r + 1) * seg,
                          j * COL_TILE + c * LANES:j * COL_TILE + (c + 1) * LANES] = rows.astype(bf16)


def _qkv(x, mod6, g1, w_qkv, gvec, ones_bd, cos_t, sina_t, sinb_t, tm=DIL_TILE):
    bsz, s, d = x.shape
    return pl.pallas_call(
        _qkv_kernel,
        grid=(bsz, s // tm),
        in_specs=[pl.BlockSpec((1, tm, d), lambda b, i: (b, i, 0)),
                  pl.BlockSpec((1, 6, d), lambda b, i: (b, 0, 0)),
                  _const_spec(g1.shape), _const_spec(w_qkv.shape), _const_spec(gvec.shape),
                  _const_spec(ones_bd.shape),
                  pl.BlockSpec((tm, LANES), lambda b, i: (i, 0)),
                  pl.BlockSpec((tm, LANES), lambda b, i: (i, 0)),
                  pl.BlockSpec((tm, LANES), lambda b, i: (i, 0))],
        out_specs=pl.BlockSpec((1, tm, W_QKV), lambda b, i: (b, i, 0)),
        out_shape=jax.ShapeDtypeStruct((bsz, s, W_QKV), bf16),
        scratch_shapes=[pltpu.VMEM((tm, d), bf16),
                        pltpu.VMEM((2, COL_TILE // LANES, tm, LANES), f32)],
        compiler_params=_params(("parallel", "parallel")),
        name="qkv",
    )(x, mod6, g1, w_qkv, gvec, ones_bd, cos_t, sina_t, sinb_t)


def _pair_mask(nq):
    row = lax.broadcasted_iota(jnp.int32, (2 * nq, LANES), 0)
    lane = lax.broadcasted_iota(jnp.int32, (2 * nq, LANES), 1)
    return (row < nq) == (lane < HEAD_DIM)


def _stack_pair(qp, own_head):
    q2 = jnp.concatenate([qp, qp], axis=0)
    return jnp.where(own_head, q2, jnp.zeros_like(q2))


def _na_kernel(q_ref, kp_ref, kc_ref, kn_ref, vp_ref, vc_ref, vn_ref, tbl_ref, o_ref, kwin, vwin):
    rb = pl.program_id(1)
    blk = NA_ROWS * GRID_W
    n_rows = pl.num_programs(1) * NA_ROWS
    for t, (kr, vr) in enumerate(((kp_ref, vp_ref), (kc_ref, vc_ref), (kn_ref, vn_ref))):
        kwin[t * blk:(t + 1) * blk, :] = kr[0]
        vwin[t * blk:(t + 1) * blk, :] = vr[0]
    first_head = lax.broadcasted_iota(jnp.int32, (GRID_W, LANES), 1) < HEAD_DIM
    own_head = _pair_mask(GRID_W)
    nkeys = NA_KH * GRID_W
    npair = NA_HEADS // 2
    pair_rows = 2 * GRID_W

    def row_body(a, carry):
        r = rb * NA_ROWS + a
        row_start = jnp.clip(r - NA_KH // 2, 0, n_rows - NA_KH)
        delta = r - row_start
        off = pl.multiple_of((row_start - (rb - 1) * NA_ROWS) * GRID_W, GRID_W)
        qoff = pl.multiple_of(a * GRID_W, GRID_W)
        s_parts = []
        for hp in range(npair):
            cols = slice(hp * LANES, (hp + 1) * LANES)
            q2 = _stack_pair(q_ref[0, pl.ds(qoff, GRID_W), cols], own_head)
            kk = kwin[pl.ds(off, nkeys), cols]
            s_parts.append(lax.dot_general(q2, kk, (((1,), (1,)), ((), ())), preferred_element_type=f32))
        toff = pl.multiple_of(delta * (NA_HEADS * GRID_W), NA_HEADS * GRID_W)
        s = jnp.concatenate(s_parts, axis=0) + tbl_ref[pl.ds(toff, NA_HEADS * GRID_W), :]
        m = jnp.max(s, axis=-1, keepdims=True)
        p = jnp.exp(s - m)
        inv = 1.0 / jnp.sum(p, axis=-1, keepdims=True)
        pb = p.astype(bf16)
        for hp in range(npair):
            cols = slice(hp * LANES, (hp + 1) * LANES)
            rows = slice(hp * pair_rows, (hp + 1) * pair_rows)
            vv = vwin[pl.ds(off, nkeys), cols]
            pv = jnp.dot(pb[rows], vv, preferred_element_type=f32) * inv[rows]
            o_ref[0, pl.ds(qoff, GRID_W), cols] = jnp.where(first_head, pv[:GRID_W], pv[GRID_W:]).astype(bf16)
        return carry

    lax.fori_loop(0, NA_ROWS, row_body, 0)


def _na(qkv, tbl):
    bsz, s, _ = qkv.shape
    blk = NA_ROWS * GRID_W
    nb = s // blk
    qspec = pl.BlockSpec((1, blk, WA), lambda b, i: (b, i, 0))

    def halo(col):
        return [pl.BlockSpec((1, blk, WA), lambda b, i: (b, jnp.maximum(i - 1, 0), col)),
                pl.BlockSpec((1, blk, WA), lambda b, i: (b, i, col)),
                pl.BlockSpec((1, blk, WA), lambda b, i: (b, jnp.minimum(i + 1, nb - 1), col))]

    return pl.pallas_call(
        _na_kernel,
        grid=(bsz, nb),
        in_specs=[qspec] + halo(1) + halo(2) + [_const_spec(tbl.shape)],
        out_specs=pl.BlockSpec((1, blk, WA), lambda b, i: (b, i, 0)),
        out_shape=jax.ShapeDtypeStruct((bsz, s, WA), bf16),
        scratch_shapes=[pltpu.VMEM((3 * blk, WA), bf16), pltpu.VMEM((3 * blk, WA), bf16)],
        compiler_params=_params(("parallel", "parallel")),
        name="na",
    )(qkv, qkv, qkv, qkv, qkv, qkv, qkv, tbl)


def _na_bias_table(rpb):
    col = jnp.arange(GRID_W)
    col_start = jnp.clip(col - NA_KW // 2, 0, GRID_W - NA_KW)
    cmask = (col[None, :] >= col_start[:, None]) & (col[None, :] < col_start[:, None] + NA_KW)
    col_off = jnp.clip(col[None, :] - col[:, None] + (NA_KW - 1), 0, 2 * NA_KW - 2)
    t = rpb[:, :, col_off]
    ro = jnp.arange(NA_KH)[None, :] - jnp.arange(NA_KH)[:, None] + (NA_KH - 1)
    t = t[:, ro]
    t = jnp.where(cmask[None, None, None], t, NEG_INF)
    t = t.transpose(1, 0, 3, 2, 4).reshape(NA_KH * NA_HEADS * GRID_W, NA_KH * GRID_W)
    return t.astype(f32)


def _dil_kernel(*refs):
    ng = len(DIL_CONFIGS)
    in_refs = refs[:7 * ng]
    o_ref = refs[7 * ng]
    scr = refs[7 * ng + 1:]
    kwins, vwins = scr[0:ng], scr[ng:2 * ng]
    o_scr, m_scr, l_scr = scr[2 * ng], scr[2 * ng + 1], scr[2 * ng + 2]
    ti = pl.program_id(1)
    n_tiles = pl.num_programs(1)
    qb = DIL_QBLOCK
    span = 3 * qb

    first_head = lax.broadcasted_iota(jnp.int32, (qb, LANES), 1) < HEAD_DIM
    own_head = _pair_mask(qb)
    npair = DIL_HPG // 2
    qi = lax.broadcasted_iota(jnp.int32, (qb, span), 0)
    kj = lax.broadcasted_iota(jnp.int32, (qb, span), 1)
    band = jnp.where((kj >= qi) & (kj <= qi + 2 * qb), 0.0, NEG_INF).astype(f32)
    kcol = lax.broadcasted_iota(jnp.int32, (1, span), 1)

    for g, (_, dil) in enumerate(DIL_CONFIGS):
        q_ref, kp, kc, kn, vp, vc, vn = in_refs[7 * g:7 * g + 7]
        kwin, vwin = kwins[g], vwins[g]
        seg = DIL_TILE // dil
        nj = seg // qb
        for r in range(dil):
            for win, (p_ref, c_ref, n_ref) in ((kwin, (kp, kc, kn)), (vwin, (vp, vc, vn))):
                win[r, 0:qb, :] = p_ref[0, (r + 1) * seg - qb:(r + 1) * seg, :]
                win[r, qb:qb + seg, :] = c_ref[0, r * seg:(r + 1) * seg, :]
                win[r, qb + seg:2 * qb + seg, :] = n_ref[0, r * seg:r * seg + qb, :]

        def blk_body(u, carry, g=g, dil=dil, nj=nj, q_ref=q_ref, kwin=kwin, vwin=vwin):
            blocks = []
            s_parts = []
            for k in range(DIL_UNROLL):
                t = u * DIL_UNROLL + k
                r = t // nj
                j = t % nj
                pen_lo = jnp.where((ti == 0) & (j == 0), NEG_INF, 0.0).astype(f32)
                pen_hi = jnp.where((ti == n_tiles - 1) & (j == nj - 1), NEG_INF, 0.0).astype(f32)
                mask = band + (jnp.where(kcol < qb, pen_lo, 0.0) + jnp.where(kcol >= 2 * qb, pen_hi, 0.0))
                qoff = pl.multiple_of(t * qb, qb)
                koff = pl.multiple_of(j * qb, qb)
                store_rows = pl.ds(j * (qb * dil) + r, qb, stride=dil) if dil > 1 else pl.ds(qoff, qb)
                blocks.append((r, koff, store_rows))
                for hp in range(npair):
                    cols = slice(hp * LANES, (hp + 1) * LANES)
                    q2 = _stack_pair(q_ref[0, pl.ds(qoff, qb), cols], own_head)
                    kk = kwin[r, pl.ds(koff, span), cols]
                    s2 = lax.dot_general(q2, kk, (((1,), (1,)), ((), ())), preferred_element_type=f32)
                    s_parts += [s2[:qb] + mask, s2[qb:] + mask]
            s = jnp.concatenate(s_parts, axis=0)
            m = jnp.max(s, axis=-1, keepdims=True)
            p = jnp.exp(s - m)
            l = jnp.sum(p, axis=-1, keepdims=True)
            pb = p.astype(bf16)
            for k, (r, koff, store_rows) in enumerate(blocks):
                for hp in range(npair):
                    cols = slice(hp * LANES, (hp + 1) * LANES)
                    r0 = (k * npair + hp) * 2 * qb
                    vv = vwin[r, pl.ds(koff, span), cols]
                    pv = jnp.dot(pb[r0:r0 + 2 * qb], vv, preferred_element_type=f32)
                    slot = g * npair + hp
                    o_scr[slot, store_rows, :] = jnp.where(first_head, pv[:qb], pv[qb:])
                    m_scr[slot, store_rows, :] = jnp.where(first_head, m[r0:r0 + qb], m[r0 + qb:r0 + 2 * qb])
                    l_scr[slot, store_rows, :] = jnp.where(first_head, l[r0:r0 + qb], l[r0 + qb:r0 + 2 * qb])
            return carry

        lax.fori_loop(0, dil * nj // DIL_UNROLL, blk_body, 0)

    for hp in range(npair):
        m_all = m_scr[hp]
        for g in range(1, ng):
            m_all = jnp.maximum(m_all, m_scr[g * npair + hp])
        num = jnp.zeros_like(m_all)
        den = jnp.zeros_like(m_all)
        for g in range(ng):
            w = jnp.exp(m_scr[g * npair + hp] - m_all)
            num = num + w * o_scr[g * npair + hp]
            den = den + w * l_scr[g * npair + hp]
        o_ref[0, :, hp * LANES:(hp + 1) * LANES] = (num / den).astype(bf16)


def _dil(qkv):
    bsz, s, _ = qkv.shape
    nt = s // DIL_TILE
    ng = len(DIL_CONFIGS)
    blk = (1, DIL_TILE, WB_OUT)
    in_specs = []
    for g in range(ng):
        in_specs.append(pl.BlockSpec(blk, lambda b, i, c=_J_QB + g: (b, i, c)))
        for base in (_J_KB, _J_VB):
            c = base + g
            in_specs += [pl.BlockSpec(blk, lambda b, i, c=c: (b, jnp.maximum(i - 1, 0), c)),
                         pl.BlockSpec(blk, lambda b, i, c=c: (b, i, c)),
                         pl.BlockSpec(blk, lambda b, i, c=c: (b, jnp.minimum(i + 1, nt - 1), c))]
    win_shapes = [pltpu.VMEM((dil, DIL_TILE // dil + 2 * DIL_QBLOCK, WB_OUT), bf16) for _, dil in DIL_CONFIGS]
    return pl.pallas_call(
        _dil_kernel,
        grid=(bsz, nt),
        in_specs=in_specs,
        out_specs=pl.BlockSpec(blk, lambda b, i: (b, i, 0)),
        out_shape=jax.ShapeDtypeStruct((bsz, s, WB_OUT), bf16),
        scratch_shapes=win_shapes + win_shapes + [pltpu.VMEM((ng * DIL_HPG // 2, DIL_TILE, LANES), f32)] * 3,
        compiler_params=_params(("parallel", "parallel")),
        name="dil",
    )(*([qkv] * (7 * ng)))


def _tail_kernel(x_ref, oa_ref, ob_ref, mod_ref, g1_ref, g2_ref, wg_ref, bg_ref, wpa_ref, wpb_ref,
                 wo_ref, wa_ref, wu_ref, wout_ref, o_ref):
    d = x_ref.shape[2]
    x = x_ref[0]
    mod = mod_ref[0]
    sh1, sc1, gt1, sh2, sc2, gt2 = [mod[k:k + 1] for k in range(6)]
    h = _modulated_norm(x, g1_ref[...], sc1, sh1).astype(bf16)
    gates = jax.nn.sigmoid(jnp.dot(h, wg_ref[...], preferred_element_type=f32) + bg_ref[...])
    pa = jnp.dot(oa_ref[0], wpa_ref[...], preferred_element_type=f32)
    pb = jnp.dot(ob_ref[0], wpb_ref[...], preferred_element_type=f32)
    merged = gates[:, :d] * pa + gates[:, d:] * pb
    x1 = x + gt1 * jnp.dot(merged.astype(bf16), wo_ref[...], preferred_element_type=f32)
    h2 = _modulated_norm(x1, g2_ref[...], sc2, sh2).astype(bf16)

    def ff_body(f, acc):
        a = jnp.dot(h2, wa_ref[f], preferred_element_type=f32)
        u = jnp.dot(h2, wu_ref[f], preferred_element_type=f32)
        act = (a * jax.nn.sigmoid(a)) * u
        return acc + jnp.dot(act.astype(bf16), wout_ref[f], preferred_element_type=f32)

    acc = lax.fori_loop(0, wa_ref.shape[0], ff_body, jnp.zeros_like(x))
    o_ref[0] = x1 + gt2 * acc


def _tail(x, o_a, o_b, mod6, g1, g2, wg, bg, wpa, wpb, wo, wa3, wu3, wout3, tm=TAIL_TM):
    bsz, s, d = x.shape
    tok = lambda w: pl.BlockSpec((1, tm, w), lambda b, i: (b, i, 0))
    consts = [g1, g2, wg, bg, wpa, wpb, wo, wa3, wu3, wout3]
    return pl.pallas_call(
        _tail_kernel,
        grid=(bsz, s // tm),
        in_specs=[tok(d), tok(WA), tok(WB_OUT), pl.BlockSpec((1, 6, d), lambda b, i: (b, 0, 0))]
        + [_const_spec(c.shape) for c in consts],
        out_specs=tok(d),
        out_shape=jax.ShapeDtypeStruct((bsz, s, d), f32),
        compiler_params=_params(("parallel", "parallel")),
        name="tail",
    )(x, o_a, o_b, mod6, *consts)


def _rotary_tables(s):
    half = ROT_DIM // 2
    inv_freq = ROPE_THETA ** (-(jnp.arange(half, dtype=f32) * 2.0) / ROT_DIM)
    ang = jnp.arange(s).astype(f32)[:, None] * inv_freq[None, :]
    cos, sin = jnp.cos(ang), jnp.sin(ang)
    ones = jnp.ones((s, HEAD_DIM - ROT_DIM), f32)
    zeros = jnp.zeros((s, HEAD_DIM - ROT_DIM), f32)
    zh = jnp.zeros((s, half), f32)
    cos_h = jnp.concatenate([cos, cos, ones], axis=1)
    sina_h = jnp.concatenate([-sin, zh, zeros], axis=1)
    sinb_h = jnp.concatenate([zh, sin, zeros], axis=1)
    rep = LANES // HEAD_DIM
    return tuple(jnp.tile(t, (1, rep)) for t in (cos_h, sina_h, sinb_h))


def _layer(x, mod6, g_norm1, g_norm2, w_in, b_gate, g_qa, g_ka, g_qb, g_kb, rpb,
           w_proj_a, w_proj_b, w_o, w_ffn_in, w_ffn_out):
    bsz, s, d = x.shape
    scale = HEAD_DIM ** -0.5
    rep = COL_TILE // HEAD_DIM
    gvec = jnp.stack([jnp.tile(g_qa * scale, rep), jnp.tile(g_ka, rep),
                      jnp.tile(g_qb * scale, rep), jnp.tile(g_kb, rep)]).astype(f32)
    hid = jnp.arange(COL_TILE) // HEAD_DIM
    ones_bd = (hid[:, None] == hid[None, :]).astype(bf16)
    cos_t, sina_t, sinb_t = _rotary_tables(s)
    g1 = g_norm1.reshape(1, d)
    g2 = g_norm2.reshape(1, d)

    qkv = _qkv(x, mod6, g1, w_in[:, :W_QKV].astype(bf16), gvec, ones_bd, cos_t, sina_t, sinb_t)
    o_a = _na(qkv, _na_bias_table(rpb))
    o_b = _dil(qkv)

    nf = D_FF // FF_CHUNK
    wa3 = w_ffn_in[:, :D_FF].astype(bf16).reshape(d, nf, FF_CHUNK).transpose(1, 0, 2)
    wu3 = w_ffn_in[:, D_FF:].astype(bf16).reshape(d, nf, FF_CHUNK).transpose(1, 0, 2)
    wout3 = w_ffn_out.astype(bf16).reshape(nf, FF_CHUNK, d)
    return _tail(x, o_a, o_b, mod6, g1, g2, w_in[:, W_QKV:].astype(bf16), b_gate.reshape(1, 2 * d),
                 w_proj_a.astype(bf16), w_proj_b.astype(bf16), w_o.astype(bf16), wa3, wu3, wout3)


def kernel(x, c, w_ada, b_ada, g_norm1, g_norm2, w_in, b_gate, g_qa, g_ka, g_qb, g_kb, rpb,
           w_proj_a, w_proj_b, w_o, w_ffn_in, w_ffn_out):
    depth = w_ada.shape[0]
    bsz, d = c.shape
    for l in range(depth):
        mod6 = _mod(c, w_ada[l], b_ada[l]).reshape(bsz, 6, d)
        x = _layer(x, mod6, g_norm1[l], g_norm2[l], w_in[l], b_gate[l], g_qa[l], g_ka[l], g_qb[l],
                   g_kb[l], rpb[l], w_proj_a[l], w_proj_b[l], w_o[l], w_ffn_in[l], w_ffn_out[l])
    return x
```

```python
import functools

import jax
import jax.numpy as jnp
from jax import lax
from jax.experimental import pallas as pl
from jax.experimental.pallas import tpu as pltpu

f32 = jnp.float32
bf16 = jnp.bfloat16

D_MODEL = 1024
HEAD_DIM = 64
GRID_W = 64
NA_HEADS = 8
NA_KH = 8
NA_KW = 16
DIL_CONFIGS = ((128, 1), (512, 4), (2048, 16))
DIL_HPG = 4
DIL_HEADS = DIL_HPG * len(DIL_CONFIGS)
DIL_QBLOCK = 64
ROT_DIM = HEAD_DIM // 4
ROPE_THETA = 500000.0
D_FF = -(-8 * D_MODEL // (3 * 256)) * 256
EPS = 1e-6
NEG_INF = -1e30
WA = NA_HEADS * HEAD_DIM
WB = DIL_HEADS * HEAD_DIM
WB_OUT = DIL_HPG * HEAD_DIM
W_QKV = 3 * WA + 3 * WB

LANES = 128
COL_TILE = 256
DIL_TILE = 1024
DIL_UNROLL = 4
NA_ROWS = 8
TAIL_TM = 512
FF_CHUNK = 256
VMEM_LIMIT = 56 * 1024 * 1024


def _params(sem):
    return pltpu.CompilerParams(dimension_semantics=sem, vmem_limit_bytes=VMEM_LIMIT)


def _const_spec(shape):
    nd = len(shape)
    return pl.BlockSpec(shape, lambda *_: (0,) * nd, pipeline_mode=pl.Buffered(1))


def _mod_kernel(ct_ref, w_ref, b_ref, o_ref):
    ct = ct_ref[...]
    act = ct * jax.nn.sigmoid(ct)
    w = w_ref[...]
    rows = [jnp.sum(act[:, b:b + 1] * w, axis=0, keepdims=True) for b in range(ct.shape[1])]
    o_ref[...] = jnp.concatenate(rows, axis=0) + b_ref[...]


def _mod(c, w_ada, b_ada):
    bsz, d = c.shape
    n = w_ada.shape[1]
    tn = n // 4
    return pl.pallas_call(
        _mod_kernel,
        grid=(n // tn,),
        in_specs=[pl.BlockSpec((d, bsz), lambda j: (0, 0)),
                  pl.BlockSpec((d, tn), lambda j: (0, j)),
                  pl.BlockSpec((1, tn), lambda j: (0, j))],
        out_specs=pl.BlockSpec((bsz, tn), lambda j: (0, j)),
        out_shape=jax.ShapeDtypeStruct((bsz, n), f32),
        compiler_params=_params(("arbitrary",)),
        name="mod",
    )(c.T, w_ada, b_ada.reshape(1, n))


def _modulated_norm(x, g, sc, sh):
    ms = jnp.mean(x * x, axis=-1, keepdims=True)
    y = x * lax.rsqrt(ms + EPS) * g
    return y * (1.0 + sc) + sh


_J_QB = 3 * WA // COL_TILE
_J_KB = _J_QB + WB // COL_TILE
_J_VB = _J_KB + WB // COL_TILE
_N_J = W_QKV // COL_TILE


def _qkv_tile_kind(j):
    if j < WA // COL_TILE:
        return 0, False, 1
    if j < 2 * WA // COL_TILE:
        return 1, False, 1
    if j < _J_QB:
        return None, False, 1
    dil = DIL_CONFIGS[(j - _J_QB) % len(DIL_CONFIGS)][1]
    if j < _J_KB:
        return 2, True, dil
    if j < _J_VB:
        return 3, True, dil
    return None, False, dil


def _qkv_kernel(x_ref, mod_ref, g1_ref, w_ref, gvec_ref, ones_ref, cos_ref, sina_ref, sinb_ref,
                o_ref, h_scr, y_scr):
    tm = x_ref.shape[1]
    nc = COL_TILE // LANES
    mod = mod_ref[0]
    h_scr[...] = _modulated_norm(x_ref[0], g1_ref[...], mod[1:2], mod[0:1]).astype(bf16)

    n_perm = 0
    for j in range(_N_J):
        gain_row, rotary, dil = _qkv_tile_kind(j)
        cols = slice(j * COL_TILE, (j + 1) * COL_TILE)
        y = jnp.dot(h_scr[...], w_ref[:, cols], preferred_element_type=f32)
        if gain_row is not None:
            ss = jnp.dot((y * y).astype(bf16), ones_ref[...], preferred_element_type=f32)
            y = y * lax.rsqrt(ss * (1.0 / HEAD_DIM) + EPS) * gvec_ref[gain_row:gain_row + 1, :]
        if rotary:
            cos = jnp.concatenate([cos_ref[...]] * nc, axis=1)
            sina = jnp.concatenate([sina_ref[...]] * nc, axis=1)
            sinb = jnp.concatenate([sinb_ref[...]] * nc, axis=1)
            half = ROT_DIM // 2
            up = pltpu.roll(y, COL_TILE - half, 1)
            dn = pltpu.roll(y, half, 1)
            y = y * cos + up * sina + dn * sinb
        if dil == 1:
            o_ref[0, :, cols] = y.astype(bf16)
            continue
        slot = n_perm % y_scr.shape[0]
        n_perm += 1
        seg = DIL_TILE // dil
        for c in range(nc):
            y_scr[slot, c] = y[:, c * LANES:(c + 1) * LANES]
        for t in range(tm // DIL_TILE):
            for r in range(dil):
                for c in range(nc):
                    rows = y_scr[slot, c, pl.ds(t * DIL_TILE + r, seg, stride=dil), :]
                    o_ref[0, t * DIL_TILE + r * seg:t * DIL_TILE + (r + 1) * seg,
                          j * COL_TILE + c * LANES:j * COL_TILE + (c + 1) * LANES] = rows.astype(bf16)


def _qkv(x, mod6, g1, w_qkv, gvec, ones_bd, cos_t, sina_t, sinb_t, tm=DIL_TILE):
    bsz, s, d = x.shape
    return pl.pallas_call(
        _qkv_kernel,
        grid=(bsz, s // tm),
        in_specs=[pl.BlockSpec((1, tm, d), lambda b, i: (b, i, 0)),
                  pl.BlockSpec((1, 6, d), lambda b, i: (b, 0, 0)),
                  _const_spec(g1.shape), _const_spec(w_qkv.shape), _const_spec(gvec.shape),
                  _const_spec(ones_bd.shape),
                  pl.BlockSpec((tm, LANES), lambda b, i: (i, 0)),
                  pl.BlockSpec((tm, LANES), lambda b, i: (i, 0)),
                  pl.BlockSpec((tm, LANES), lambda b, i: (i, 0))],
        out_specs=pl.BlockSpec((1, tm, W_QKV), lambda b, i: (b, i, 0)),
        out_shape=jax.ShapeDtypeStruct((bsz, s, W_QKV), bf16),
        scratch_shapes=[pltpu.VMEM((tm, d), bf16),
                        pltpu.VMEM((2, COL_TILE // LANES, tm, LANES), f32)],
        compiler_params=_params(("parallel", "parallel")),
        name="qkv",
    )(x, mod6, g1, w_qkv, gvec, ones_bd, cos_t, sina_t, sinb_t)


def _pair_mask(nq):
    row = lax.broadcasted_iota(jnp.int32, (2 * nq, LANES), 0)
    lane = lax.broadcasted_iota(jnp.int32, (2 * nq, LANES), 1)
    return (row < nq) == (lane < HEAD_DIM)


def _stack_pair(qp, own_head):
    q2 = jnp.concatenate([qp, qp], axis=0)
    return jnp.where(own_head, q2, jnp.zeros_like(q2))


def _na_kernel(q_ref, kp_ref, kc_ref, kn_ref, vp_ref, vc_ref, vn_ref, tbl_ref, o_ref, kwin, vwin):
    rb = pl.program_id(1)
    blk = NA_ROWS * GRID_W
    n_rows = pl.num_programs(1) * NA_ROWS
    for t, (kr, vr) in enumerate(((kp_ref, vp_ref), (kc_ref, vc_ref), (kn_ref, vn_ref))):
        kwin[t * blk:(t + 1) * blk, :] = kr[0]
        vwin[t * blk:(t + 1) * blk, :] = vr[0]
    first_head = lax.broadcasted_iota(jnp.int32, (GRID_W, LANES), 1) < HEAD_DIM
    own_head = _pair_mask(GRID_W)
    nkeys = NA_KH * GRID_W
    npair = NA_HEADS // 2
    pair_rows = 2 * GRID_W

    def row_body(a, carry):
        r = rb * NA_ROWS + a
        row_start = jnp.clip(r - NA_KH // 2, 0, n_rows - NA_KH)
        delta = r - row_start
        off = pl.multiple_of((row_start - (rb - 1) * NA_ROWS) * GRID_W, GRID_W)
        qoff = pl.multiple_of(a * GRID_W, GRID_W)
        s_parts = []
        for hp in range(npair):
            cols = slice(hp * LANES, (hp + 1) * LANES)
            q2 = _stack_pair(q_ref[0, pl.ds(qoff, GRID_W), cols], own_head)
            kk = kwin[pl.ds(off, nkeys), cols]
            s_parts.append(lax.dot_general(q2, kk, (((1,), (1,)), ((), ())), preferred_element_type=f32))
        toff = pl.multiple_of(delta * (NA_HEADS * GRID_W), NA_HEADS * GRID_W)
        s = jnp.concatenate(s_parts, axis=0) + tbl_ref[pl.ds(toff, NA_HEADS * GRID_W), :]
        m = jnp.max(s, axis=-1, keepdims=True)
        p = jnp.exp(s - m)
        inv = 1.0 / jnp.sum(p, axis=-1, keepdims=True)
        pb = p.astype(bf16)
        for hp in range(npair):
            cols = slice(hp * LANES, (hp + 1) * LANES)
            rows = slice(hp * pair_rows, (hp + 1) * pair_rows)
            vv = vwin[pl.ds(off, nkeys), cols]
            pv = jnp.dot(pb[rows], vv, preferred_element_type=f32) * inv[rows]
            o_ref[0, pl.ds(qoff, GRID_W), cols] = jnp.where(first_head, pv[:GRID_W], pv[GRID_W:]).astype(bf16)
        return carry

    lax.fori_loop(0, NA_ROWS, row_body, 0)


def _na(qkv, tbl):
    bsz, s, _ = qkv.shape
    blk = NA_ROWS * GRID_W
    nb = s // blk
    qspec = pl.BlockSpec((1, blk, WA), lambda b, i: (b, i, 0))

    def halo(col):
        return [pl.BlockSpec((1, blk, WA), lambda b, i: (b, jnp.maximum(i - 1, 0), col)),
                pl.BlockSpec((1, blk, WA), lambda b, i: (b, i, col)),
                pl.BlockSpec((1, blk, WA), lambda b, i: (b, jnp.minimum(i + 1, nb - 1), col))]

    return pl.pallas_call(
        _na_kernel,
        grid=(bsz, nb),
        in_specs=[qspec] + halo(1) + halo(2) + [_const_spec(tbl.shape)],
        out_specs=pl.BlockSpec((1, blk, WA), lambda b, i: (b, i, 0)),
        out_shape=jax.ShapeDtypeStruct((bsz, s, WA), bf16),
        scratch_shapes=[pltpu.VMEM((3 * blk, WA), bf16), pltpu.VMEM((3 * blk, WA), bf16)],
        compiler_params=_params(("parallel", "parallel")),
        name="na",
    )(qkv, qkv, qkv, qkv, qkv, qkv, qkv, tbl)


def _na_bias_table(rpb):
    col = jnp.arange(GRID_W)
    col_start = jnp.clip(col - NA_KW // 2, 0, GRID_W - NA_KW)
    cmask = (col[None, :] >= col_start[:, None]) & (col[None, :] < col_start[:, None] + NA_KW)
    col_off = jnp.clip(col[None, :] - col[:, None] + (NA_KW - 1), 0, 2 * NA_KW - 2)
    t = rpb[:, :, col_off]
    ro = jnp.arange(NA_KH)[None, :] - jnp.arange(NA_KH)[:, None] + (NA_KH - 1)
    t = t[:, ro]
    t = jnp.where(cmask[None, None, None], t, NEG_INF)
    t = t.transpose(1, 0, 3, 2, 4).reshape(NA_KH * NA_HEADS * GRID_W, NA_KH * GRID_W)
    return t.astype(f32)


def _dil_kernel(*refs):
    ng = len(DIL_CONFIGS)
    in_refs = refs[:7 * ng]
    o_ref = refs[7 * ng]
    scr = refs[7 * ng + 1:]
    kwins, vwins = scr[0:ng], scr[ng:2 * ng]
    o_scr, m_scr, l_scr = scr[2 * ng], scr[2 * ng + 1], scr[2 * ng + 2]
    ti = pl.program_id(1)
    n_tiles = pl.num_programs(1)
    qb = DIL_QBLOCK
    span = 3 * qb

    first_head = lax.broadcasted_iota(jnp.int32, (qb, LANES), 1) < HEAD_DIM
    own_head = _pair_mask(qb)
    npair = DIL_HPG // 2
    qi = lax.broadcasted_iota(jnp.int32, (qb, span), 0)
    kj = lax.broadcasted_iota(jnp.int32, (qb, span), 1)
    band = jnp.where((kj >= qi) & (kj <= qi + 2 * qb), 0.0, NEG_INF).astype(f32)
    kcol = lax.broadcasted_iota(jnp.int32, (1, span), 1)

    for g, (_, dil) in enumerate(DIL_CONFIGS):
        q_ref, kp, kc, kn, vp, vc, vn = in_refs[7 * g:7 * g + 7]
        kwin, vwin = kwins[g], vwins[g]
        seg = DIL_TILE // dil
        nj = seg // qb
        for r in range(dil):
            for win, (p_ref, c_ref, n_ref) in ((kwin, (kp, kc, kn)), (vwin, (vp, vc, vn))):
                win[r, 0:qb, :] = p_ref[0, (r + 1) * seg - qb:(r + 1) * seg, :]
                win[r, qb:qb + seg, :] = c_ref[0, r * seg:(r + 1) * seg, :]
                win[r, qb + seg:2 * qb + seg, :] = n_ref[0, r * seg:r * seg + qb, :]

        def blk_body(u, carry, g=g, dil=dil, nj=nj, q_ref=q_ref, kwin=kwin, vwin=vwin):
            blocks = []
            s_parts = []
            for k in range(DIL_UNROLL):
                t = u * DIL_UNROLL + k
                r = t // nj
                j = t % nj
                pen_lo = jnp.where((ti == 0) & (j == 0), NEG_INF, 0.0).astype(f32)
                pen_hi = jnp.where((ti == n_tiles - 1) & (j == nj - 1), NEG_INF, 0.0).astype(f32)
                mask = band + (jnp.where(kcol < qb, pen_lo, 0.0) + jnp.where(kcol >= 2 * qb, pen_hi, 0.0))
                qoff = pl.multiple_of(t * qb, qb)
                koff = pl.multiple_of(j * qb, qb)
                store_rows = pl.ds(j * (qb * dil) + r, qb, stride=dil) if dil > 1 else pl.ds(qoff, qb)
                blocks.append((r, koff, store_rows))
                for hp in range(npair):
                    cols = slice(hp * LANES, (hp + 1) * LANES)
                    q2 = _stack_pair(q_ref[0, pl.ds(qoff, qb), cols], own_head)
                    kk = kwin[r, pl.ds(koff, span), cols]
                    s2 = lax.dot_general(q2, kk, (((1,), (1,)), ((), ())), preferred_element_type=f32)
                    s_parts += [s2[:qb] + mask, s2[qb:] + mask]
            s = jnp.concatenate(s_parts, axis=0)
            m = jnp.max(s, axis=-1, keepdims=True)
            p = jnp.exp(s - m)
            l = jnp.sum(p, axis=-1, keepdims=True)
            pb = p.astype(bf16)
            for k, (r, koff, store_rows) in enumerate(blocks):
                for hp in range(npair):
                    cols = slice(hp * LANES, (hp + 1) * LANES)
                    r0 = (k * npair + hp) * 2 * qb
                    vv = vwin[r, pl.ds(koff, span), cols]
                    pv = jnp.dot(pb[r0:r0 + 2 * qb], vv, preferred_element_type=f32)
                    slot = g * npair + hp
                    o_scr[slot, store_rows, :] = jnp.where(first_head, pv[:qb], pv[qb:])
                    m_scr[slot, store_rows, :] = jnp.where(first_head, m[r0:r0 + qb], m[r0 + qb:r0 + 2 * qb])
                    l_scr[slot, store_rows, :] = jnp.where(first_head, l[r0:r0 + qb], l[r0 + qb:r0 + 2 * qb])
            return carry

        lax.fori_loop(0, dil * nj // DIL_UNROLL, blk_body, 0)

    for hp in range(npair):
        m_all = m_scr[hp]
        for g in range(1, ng):
            m_all = jnp.maximum(m_all, m_scr[g * npair + hp])
        num = jnp.zeros_like(m_all)
        den = jnp.zeros_like(m_all)
        for g in range(ng):
            w = jnp.exp(m_scr[g * npair + hp] - m_all)
            num = num + w * o_scr[g * npair + hp]
            den = den + w * l_scr[g * npair + hp]
        o_ref[0, :, hp * LANES:(hp + 1) * LANES] = (num / den).astype(bf16)


def _dil(qkv):
    bsz, s, _ = qkv.shape
    nt = s // DIL_TILE
    ng = len(DIL_CONFIGS)
    blk = (1, DIL_TILE, WB_OUT)
    in_specs = []
    for g in range(ng):
        in_specs.append(pl.BlockSpec(blk, lambda b, i, c=_J_QB + g: (b, i, c)))
        for base in (_J_KB, _J_VB):
            c = base + g
            in_specs += [pl.BlockSpec(blk, lambda b, i, c=c: (b, jnp.maximum(i - 1, 0), c)),
                         pl.BlockSpec(blk, lambda b, i, c=c: (b, i, c)),
                         pl.BlockSpec(blk, lambda b, i, c=c: (b, jnp.minimum(i + 1, nt - 1), c))]
    win_shapes = [pltpu.VMEM((dil, DIL_TILE // dil + 2 * DIL_QBLOCK, WB_OUT), bf16) for _, dil in DIL_CONFIGS]
    return pl.pallas_call(
        _dil_kernel,
        grid=(bsz, nt),
        in_specs=in_specs,
        out_specs=pl.BlockSpec(blk, lambda b, i: (b, i, 0)),
        out_shape=jax.ShapeDtypeStruct((bsz, s, WB_OUT), bf16),
        scratch_shapes=win_shapes + win_shapes + [pltpu.VMEM((ng * DIL_HPG // 2, DIL_TILE, LANES), f32)] * 3,
        compiler_params=_params(("parallel", "parallel")),
        name="dil",
    )(*([qkv] * (7 * ng)))


def _tail_kernel(x_ref, oa_ref, ob_ref, mod_ref, g1_ref, g2_ref, wg_ref, bg_ref, wpa_ref, wpb_ref,
                 wo_ref, win_ref, wout_ref, o_ref, h_scr, m_scr, act_scr):
    d = x_ref.shape[2]
    x = x_ref[0]
    mod = mod_ref[0]
    sh1, sc1, gt1, sh2, sc2, gt2 = [mod[k:k + 1] for k in range(6)]
    h_scr[...] = _modulated_norm(x, g1_ref[...], sc1, sh1).astype(bf16)
    oa = oa_ref[0]
    ob = ob_ref[0]
    for n in range(d // COL_TILE):
        ca = slice(n * COL_TILE, (n + 1) * COL_TILE)
        cb = slice(d + n * COL_TILE, d + (n + 1) * COL_TILE)
        ga = jax.nn.sigmoid(jnp.dot(h_scr[...], wg_ref[:, ca], preferred_element_type=f32) + bg_ref[:, ca])
        gb = jax.nn.sigmoid(jnp.dot(h_scr[...], wg_ref[:, cb], preferred_element_type=f32) + bg_ref[:, cb])
        pa = jnp.dot(oa, wpa_ref[:, ca], preferred_element_type=f32)
        pb = jnp.dot(ob, wpb_ref[:, ca], preferred_element_type=f32)
        m_scr[:, ca] = (ga * pa + gb * pb).astype(bf16)
    x1 = x + gt1 * jnp.dot(m_scr[...], wo_ref[...], preferred_element_type=f32)
    h_scr[...] = _modulated_norm(x1, g2_ref[...], sc2, sh2).astype(bf16)
    for f in range(D_FF // FF_CHUNK):
        ca = slice(f * FF_CHUNK, (f + 1) * FF_CHUNK)
        cu = slice(D_FF + f * FF_CHUNK, D_FF + (f + 1) * FF_CHUNK)
        a = jnp.dot(h_scr[...], win_ref[:, ca], preferred_element_type=f32)
        u = jnp.dot(h_scr[...], win_ref[:, cu], preferred_element_type=f32)
        act_scr[:, ca] = ((a * jax.nn.sigmoid(a)) * u).astype(bf16)
    o_ref[0] = x1 + gt2 * jnp.dot(act_scr[...], wout_ref[...], preferred_element_type=f32)


def _tail(x, o_a, o_b, mod6, g1, g2, wg, bg, wpa, wpb, wo, w_ffn_in, w_ffn_out, tm=TAIL_TM):
    bsz, s, d = x.shape
    tok = lambda w: pl.BlockSpec((1, tm, w), lambda b, i: (b, i, 0))
    consts = [g1, g2, wg, bg, wpa, wpb, wo, w_ffn_in, w_ffn_out]
    return pl.pallas_call(
        _tail_kernel,
        grid=(bsz, s // tm),
        in_specs=[tok(d), tok(WA), tok(WB_OUT), pl.BlockSpec((1, 6, d), lambda b, i: (b, 0, 0))]
        + [_const_spec(c.shape) for c in consts],
        out_specs=tok(d),
        out_shape=jax.ShapeDtypeStruct((bsz, s, d), f32),
        scratch_shapes=[pltpu.VMEM((tm, d), bf16), pltpu.VMEM((tm, d), bf16), pltpu.VMEM((tm, D_FF), bf16)],
        compiler_params=_params(("parallel", "parallel")),
        name="tail",
    )(x, o_a, o_b, mod6, *consts)


def _rotary_tables(s):
    half = ROT_DIM // 2
    inv_freq = ROPE_THETA ** (-(jnp.arange(half, dtype=f32) * 2.0) / ROT_DIM)
    ang = jnp.arange(s).astype(f32)[:, None] * inv_freq[None, :]
    cos, sin = jnp.cos(ang), jnp.sin(ang)
    ones = jnp.ones((s, HEAD_DIM - ROT_DIM), f32)
    zeros = jnp.zeros((s, HEAD_DIM - ROT_DIM), f32)
    zh = jnp.zeros((s, half), f32)
    cos_h = jnp.concatenate([cos, cos, ones], axis=1)
    sina_h = jnp.concatenate([-sin, zh, zeros], axis=1)
    sinb_h = jnp.concatenate([zh, sin, zeros], axis=1)
    rep = LANES // HEAD_DIM
    return tuple(jnp.tile(t, (1, rep)) for t in (cos_h, sina_h, sinb_h))


def _layer(x, mod6, g_norm1, g_norm2, w_in, b_gate, g_qa, g_ka, g_qb, g_kb, rpb,
           w_proj_a, w_proj_b, w_o, w_ffn_in, w_ffn_out):
    bsz, s, d = x.shape
    scale = HEAD_DIM ** -0.5
    rep = COL_TILE // HEAD_DIM
    gvec = jnp.stack([jnp.tile(g_qa * scale, rep), jnp.tile(g_ka, rep),
                      jnp.tile(g_qb * scale, rep), jnp.tile(g_kb, rep)]).astype(f32)
    hid = jnp.arange(COL_TILE) // HEAD_DIM
    ones_bd = (hid[:, None] == hid[None, :]).astype(bf16)
    cos_t, sina_t, sinb_t = _rotary_tables(s)
    g1 = g_norm1.reshape(1, d)
    g2 = g_norm2.reshape(1, d)

    qkv = _qkv(x, mod6, g1, w_in[:, :W_QKV].astype(bf16), gvec, ones_bd, cos_t, sina_t, sinb_t)
    o_a = _na(qkv, _na_bias_table(rpb))
    o_b = _dil(qkv)

    return _tail(x, o_a, o_b, mod6, g1, g2, w_in[:, W_QKV:].astype(bf16), b_gate.reshape(1, 2 * d),
                 w_proj_a.astype(bf16), w_proj_b.astype(bf16), w_o.astype(bf16),
                 w_ffn_in.astype(bf16), w_ffn_out.astype(bf16))


def kernel(x, c, w_ada, b_ada, g_norm1, g_norm2, w_in, b_gate, g_qa, g_ka, g_qb, g_kb, rpb,
           w_proj_a, w_proj_b, w_o, w_ffn_in, w_ffn_out):
    depth = w_ada.shape[0]
    bsz, d = c.shape
    for l in range(depth):
        mod6 = _mod(c, w_ada[l], b_ada[l]).reshape(bsz, 6, d)
        x = _layer(x, mod6, g_norm1[l], g_norm2[l], w_in[l], b_gate[l], g_qa[l], g_ka[l], g_qb[l],
                   g_kb[l], rpb[l], w_proj_a[l], w_proj_b[l], w_o[l], w_ffn_in[l], w_ffn_out[l])
    return x
```

```python
import functools

import jax
import jax.numpy as jnp
from jax import lax
from jax.experimental import pallas as pl
from jax.experimental.pallas import tpu as pltpu

f32 = jnp.float32
bf16 = jnp.bfloat16

D_MODEL = 1024
HEAD_DIM = 64
GRID_W = 64
NA_HEADS = 8
NA_KH = 8
NA_KW = 16
DIL_CONFIGS = ((128, 1), (512, 4), (2048, 16))
DIL_HPG = 4
DIL_HEADS = DIL_HPG * len(DIL_CONFIGS)
DIL_QBLOCK = 64
ROT_DIM = HEAD_DIM // 4
ROPE_THETA = 500000.0
D_FF = -(-8 * D_MODEL // (3 * 256)) * 256
EPS = 1e-6
NEG_INF = -1e30
WA = NA_HEADS * HEAD_DIM
WB = DIL_HEADS * HEAD_DIM
WB_OUT = DIL_HPG * HEAD_DIM
W_QKV = 3 * WA + 3 * WB

LANES = 128
COL_TILE = 256
DIL_TILE = 1024
QKV_ROW_SPLIT = 2
QKV_DOT_TILES = 2
DIL_UNROLL = 4
NA_ROWS = 8
TAIL_TM = 512
FF_CHUNK = 256
VMEM_LIMIT = 56 * 1024 * 1024


def _params(sem):
    return pltpu.CompilerParams(dimension_semantics=sem, vmem_limit_bytes=VMEM_LIMIT)


def _const_spec(shape):
    nd = len(shape)
    return pl.BlockSpec(shape, lambda *_: (0,) * nd, pipeline_mode=pl.Buffered(1))


def _mod_kernel(ct_ref, w_ref, b_ref, o_ref):
    ct = ct_ref[...]
    act = ct * jax.nn.sigmoid(ct)
    w = w_ref[...]
    rows = [jnp.sum(act[:, b:b + 1] * w, axis=0, keepdims=True) for b in range(ct.shape[1])]
    o_ref[...] = jnp.concatenate(rows, axis=0) + b_ref[...]


def _mod(c, w_ada, b_ada):
    bsz, d = c.shape
    n = w_ada.shape[1]
    tn = n // 4
    return pl.pallas_call(
        _mod_kernel,
        grid=(n // tn,),
        in_specs=[pl.BlockSpec((d, bsz), lambda j: (0, 0)),
                  pl.BlockSpec((d, tn), lambda j: (0, j)),
                  pl.BlockSpec((1, tn), lambda j: (0, j))],
        out_specs=pl.BlockSpec((bsz, tn), lambda j: (0, j)),
        out_shape=jax.ShapeDtypeStruct((bsz, n), f32),
        compiler_params=_params(("arbitrary",)),
        name="mod",
    )(c.T, w_ada, b_ada.reshape(1, n))


def _modulated_norm(x, g, sc, sh):
    ms = jnp.mean(x * x, axis=-1, keepdims=True)
    y = x * lax.rsqrt(ms + EPS) * g
    return y * (1.0 + sc) + sh


_J_QB = 3 * WA // COL_TILE
_J_KB = _J_QB + WB // COL_TILE
_J_VB = _J_KB + WB // COL_TILE
_N_J = W_QKV // COL_TILE


def _qkv_tile_kind(j):
    if j < WA // COL_TILE:
        return 0, False, 1
    if j < 2 * WA // COL_TILE:
        return 1, False, 1
    if j < _J_QB:
        return None, False, 1
    dil = DIL_CONFIGS[(j - _J_QB) % len(DIL_CONFIGS)][1]
    if j < _J_KB:
        return 2, True, dil
    if j < _J_VB:
        return 3, True, dil
    return None, False, dil


def _qkv_kernel(x_ref, mod_ref, g1_ref, w_ref, gvec_ref, ones_ref, cos_ref, sina_ref, sinb_ref,
                o_ref, h_scr, y_scr):
    tm = x_ref.shape[1]
    nc = COL_TILE // LANES
    mod = mod_ref[0]
    h_scr[...] = _modulated_norm(x_ref[0], g1_ref[...], mod[1:2], mod[0:1]).astype(bf16)

    wide = {}
    for j in range(_N_J):
        gain_row, rotary, dil = _qkv_tile_kind(j)
        cols = slice(j * COL_TILE, (j + 1) * COL_TILE)
        slot = j % y_scr.shape[0]
        for hb in range(QKV_ROW_SPLIT):
            rows = slice(hb * tm // QKV_ROW_SPLIT, (hb + 1) * tm // QKV_ROW_SPLIT)
            if j % QKV_DOT_TILES == 0:
                wcols = slice(j * COL_TILE, min(j + QKV_DOT_TILES, _N_J) * COL_TILE)
                wide[hb] = jnp.dot(h_scr[rows, :], w_ref[:, wcols], preferred_element_type=f32)
            sub = j % QKV_DOT_TILES
            y = wide[hb][:, sub * COL_TILE:(sub + 1) * COL_TILE]
            if gain_row is not None:
                ss = jnp.dot((y * y).astype(bf16), ones_ref[...], preferred_element_type=f32)
                y = y * lax.rsqrt(ss * (1.0 / HEAD_DIM) + EPS) * gvec_ref[gain_row:gain_row + 1, :]
            if rotary:
                cos = jnp.concatenate([cos_ref[rows, :]] * nc, axis=1)
                sina = jnp.concatenate([sina_ref[rows, :]] * nc, axis=1)
                sinb = jnp.concatenate([sinb_ref[rows, :]] * nc, axis=1)
                half = ROT_DIM // 2
                up = pltpu.roll(y, COL_TILE - half, 1)
                dn = pltpu.roll(y, half, 1)
                y = y * cos + up * sina + dn * sinb
            if dil == 1:
                o_ref[0, rows, cols] = y.astype(bf16)
            else:
                for c in range(nc):
                    y_scr[slot, c, rows, :] = y[:, c * LANES:(c + 1) * LANES]
        if dil == 1:
            continue
        seg = DIL_TILE // dil
        for t in range(tm // DIL_TILE):
            for r in range(dil):
                for c in range(nc):
                    rows = y_scr[slot, c, pl.ds(t * DIL_TILE + r, seg, stride=dil), :]
                    o_ref[0, t * DIL_TILE + r * seg:t * DIL_TILE + (r + 1) * seg,
                          j * COL_TILE + c * LANES:j * COL_TILE + (c + 1) * LANES] = rows.astype(bf16)


def _qkv(x, mod6, g1, w_qkv, gvec, ones_bd, cos_t, sina_t, sinb_t, tm=DIL_TILE):
    bsz, s, d = x.shape
    return pl.pallas_call(
        _qkv_kernel,
        grid=(bsz, s // tm),
        in_specs=[pl.BlockSpec((1, tm, d), lambda b, i: (b, i, 0)),
                  pl.BlockSpec((1, 6, d), lambda b, i: (b, 0, 0)),
                  _const_spec(g1.shape),
                  pl.BlockSpec((d, W_QKV), lambda b, i: (0, 0), pipeline_mode=pl.Buffered(1)),
                  _const_spec(gvec.shape),
                  _const_spec(ones_bd.shape),
                  pl.BlockSpec((tm, LANES), lambda b, i: (i, 0)),
                  pl.BlockSpec((tm, LANES), lambda b, i: (i, 0)),
                  pl.BlockSpec((tm, LANES), lambda b, i: (i, 0))],
        out_specs=pl.BlockSpec((1, tm, W_QKV), lambda b, i: (b, i, 0)),
        out_shape=jax.ShapeDtypeStruct((bsz, s, W_QKV), bf16),
        scratch_shapes=[pltpu.VMEM((tm, d), bf16),
                        pltpu.VMEM((2, COL_TILE // LANES, tm, LANES), f32)],
        compiler_params=_params(("parallel", "parallel")),
        name="qkv",
    )(x, mod6, g1, w_qkv, gvec, ones_bd, cos_t, sina_t, sinb_t)


def _pair_mask(nq):
    row = lax.broadcasted_iota(jnp.int32, (2 * nq, LANES), 0)
    lane = lax.broadcasted_iota(jnp.int32, (2 * nq, LANES), 1)
    return (row < nq) == (lane < HEAD_DIM)


def _stack_pair(qp, own_head):
    q2 = jnp.concatenate([qp, qp], axis=0)
    return jnp.where(own_head, q2, jnp.zeros_like(q2))


def _na_kernel(q_ref, kp_ref, kc_ref, kn_ref, vp_ref, vc_ref, vn_ref, tbl_ref, o_ref, kwin, vwin):
    rb = pl.program_id(1)
    blk = NA_ROWS * GRID_W
    n_rows = pl.num_programs(1) * NA_ROWS
    for t, (kr, vr) in enumerate(((kp_ref, vp_ref), (kc_ref, vc_ref), (kn_ref, vn_ref))):
        kwin[t * blk:(t + 1) * blk, :] = kr[0]
        vwin[t * blk:(t + 1) * blk, :] = vr[0]
    first_head = lax.broadcasted_iota(jnp.int32, (GRID_W, LANES), 1) < HEAD_DIM
    own_head = _pair_mask(GRID_W)
    nkeys = NA_KH * GRID_W
    npair = NA_HEADS // 2
    pair_rows = 2 * GRID_W

    def row_body(a, carry):
        r = rb * NA_ROWS + a
        row_start = jnp.clip(r - NA_KH // 2, 0, n_rows - NA_KH)
        delta = r - row_start
        off = pl.multiple_of((row_start - (rb - 1) * NA_ROWS) * GRID_W, GRID_W)
        qoff = pl.multiple_of(a * GRID_W, GRID_W)
        s_parts = []
        for hp in range(npair):
            cols = slice(hp * LANES, (hp + 1) * LANES)
            q2 = _stack_pair(q_ref[0, pl.ds(qoff, GRID_W), cols], own_head)
            kk = kwin[pl.ds(off, nkeys), cols]
            s_parts.append(lax.dot_general(q2, kk, (((1,), (1,)), ((), ())), preferred_element_type=f32))
        n_off = 2 * NA_KH - 2
        bias = jnp.concatenate(
            [jnp.concatenate([tbl_ref[h * n_off + 2 * kp - delta + NA_KH - 1] for kp in range(NA_KH // 2)], axis=1)
             for h in range(NA_HEADS)], axis=0)
        s = jnp.concatenate(s_parts, axis=0) + bias
        m = jnp.max(s, axis=-1, keepdims=True)
        p = jnp.exp(s - m)
        inv = 1.0 / jnp.sum(p, axis=-1, keepdims=True)
        pb = p.astype(bf16)
        for hp in range(npair):
            cols = slice(hp * LANES, (hp + 1) * LANES)
            rows = slice(hp * pair_rows, (hp + 1) * pair_rows)
            vv = vwin[pl.ds(off, nkeys), cols]
            pv = jnp.dot(pb[rows], vv, preferred_element_type=f32) * inv[rows]
            o_ref[0, pl.ds(qoff, GRID_W), cols] = jnp.where(first_head, pv[:GRID_W], pv[GRID_W:]).astype(bf16)
        return carry

    lax.fori_loop(0, NA_ROWS, row_body, 0)


def _na(qkv, tbl):
    bsz, s, _ = qkv.shape
    blk = NA_ROWS * GRID_W
    nb = s // blk
    qspec = pl.BlockSpec((1, blk, WA), lambda b, i: (b, i, 0))

    def halo(col):
        return [pl.BlockSpec((1, blk, WA), lambda b, i: (b, jnp.maximum(i - 1, 0), col)),
                pl.BlockSpec((1, blk, WA), lambda b, i: (b, i, col)),
                pl.BlockSpec((1, blk, WA), lambda b, i: (b, jnp.minimum(i + 1, nb - 1), col))]

    return pl.pallas_call(
        _na_kernel,
        grid=(bsz, nb),
        in_specs=[qspec] + halo(1) + halo(2) + [_const_spec(tbl.shape)],
        out_specs=pl.BlockSpec((1, blk, WA), lambda b, i: (b, i, 0)),
        out_shape=jax.ShapeDtypeStruct((bsz, s, WA), bf16),
        scratch_shapes=[pltpu.VMEM((3 * blk, WA), bf16), pltpu.VMEM((3 * blk, WA), bf16)],
        compiler_params=_params(("parallel", "parallel")),
        name="na",
    )(qkv, qkv, qkv, qkv, qkv, qkv, qkv, tbl)


def _na_bias_table(rpb):
    col = jnp.arange(GRID_W)
    col_start = jnp.clip(col - NA_KW // 2, 0, GRID_W - NA_KW)
    cmask = (col[None, :] >= col_start[:, None]) & (col[None, :] < col_start[:, None] + NA_KW)
    col_off = jnp.clip(col[None, :] - col[:, None] + (NA_KW - 1), 0, 2 * NA_KW - 2)
    onehot = (col_off[None] == jnp.arange(2 * NA_KW - 1)[:, None, None]).astype(f32)
    t = jnp.einsum('hrc,cqk->hrqk', rpb.astype(f32), onehot, precision=lax.Precision.HIGHEST)
    t = jnp.where(cmask, t, NEG_INF)
    t = jnp.concatenate([t[:, :-1], t[:, 1:]], axis=-1)
    return t.reshape(NA_HEADS * (2 * NA_KH - 2), GRID_W, 2 * GRID_W)


def _dil_kernel(*refs):
    ng = len(DIL_CONFIGS)
    in_refs = refs[:7 * ng]
    o_ref = refs[7 * ng]
    scr = refs[7 * ng + 1:]
    kwins, vwins = scr[0:ng], scr[ng:2 * ng]
    o_scr, m_scr, l_scr = scr[2 * ng], scr[2 * ng + 1], scr[2 * ng + 2]
    ti = pl.program_id(1)
    n_tiles = pl.num_programs(1)
    qb = DIL_QBLOCK
    span = 3 * qb

    first_head = lax.broadcasted_iota(jnp.int32, (qb, LANES), 1) < HEAD_DIM
    own_head = _pair_mask(qb)
    npair = DIL_HPG // 2
    qi = lax.broadcasted_iota(jnp.int32, (qb, span), 0)
    kj = lax.broadcasted_iota(jnp.int32, (qb, span), 1)
    band = jnp.where((kj >= qi) & (kj <= qi + 2 * qb), 0.0, NEG_INF).astype(f32)
    kcol = lax.broadcasted_iota(jnp.int32, (1, span), 1)

    for g, (_, dil) in enumerate(DIL_CONFIGS):
        q_ref, kp, kc, kn, vp, vc, vn = in_refs[7 * g:7 * g + 7]
        kwin, vwin = kwins[g], vwins[g]
        seg = DIL_TILE // dil
        nj = seg // qb
        for r in range(dil):
            for win, (p_ref, c_ref, n_ref) in ((kwin, (kp, kc, kn)), (vwin, (vp, vc, vn))):
                win[r, 0:qb, :] = p_ref[0, (r + 1) * seg - qb:(r + 1) * seg, :]
                win[r, qb:qb + seg, :] = c_ref[0, r * seg:(r + 1) * seg, :]
                win[r, qb + seg:2 * qb + seg, :] = n_ref[0, r * seg:r * seg + qb, :]

        def blk_body(u, carry, g=g, dil=dil, nj=nj, q_ref=q_ref, kwin=kwin, vwin=vwin):
            blocks = []
            s_parts = []
            for k in range(DIL_UNROLL):
                t = u * DIL_UNROLL + k
                r = t // nj
                j = t % nj
                pen_lo = jnp.where((ti == 0) & (j == 0), NEG_INF, 0.0).astype(f32)
                pen_hi = jnp.where((ti == n_tiles - 1) & (j == nj - 1), NEG_INF, 0.0).astype(f32)
                mask = band + (jnp.where(kcol < qb, pen_lo, 0.0) + jnp.where(kcol >= 2 * qb, pen_hi, 0.0))
                qoff = pl.multiple_of(t * qb, qb)
                koff = pl.multiple_of(j * qb, qb)
                store_rows = pl.ds(j * (qb * dil) + r, qb, stride=dil) if dil > 1 else pl.ds(qoff, qb)
                blocks.append((r, koff, store_rows))
                for hp in range(npair):
                    cols = slice(hp * LANES, (hp + 1) * LANES)
                    q2 = _stack_pair(q_ref[0, pl.ds(qoff, qb), cols], own_head)
                    kk = kwin[r, pl.ds(koff, span), cols]
                    s2 = lax.dot_general(q2, kk, (((1,), (1,)), ((), ())), preferred_element_type=f32)
                    s_parts += [s2[:qb] + mask, s2[qb:] + mask]
            s = jnp.concatenate(s_parts, axis=0)
            m = jnp.max(s, axis=-1, keepdims=True)
            p = jnp.exp(s - m)
            l = jnp.sum(p, axis=-1, keepdims=True)
            pb = p.astype(bf16)
            for k, (r, koff, store_rows) in enumerate(blocks):
                for hp in range(npair):
                    cols = slice(hp * LANES, (hp + 1) * LANES)
                    r0 = (k * npair + hp) * 2 * qb
                    vv = vwin[r, pl.ds(koff, span), cols]
                    pv = jnp.dot(pb[r0:r0 + 2 * qb], vv, preferred_element_type=f32)
                    slot = g * npair + hp
                    o_scr[slot, store_rows, :] = jnp.where(first_head, pv[:qb], pv[qb:])
                    m_scr[slot, store_rows, :] = jnp.where(first_head, m[r0:r0 + qb], m[r0 + qb:r0 + 2 * qb])
                    l_scr[slot, store_rows, :] = jnp.where(first_head, l[r0:r0 + qb], l[r0 + qb:r0 + 2 * qb])
            return carry

        lax.fori_loop(0, dil * nj // DIL_UNROLL, blk_body, 0)

    for hp in range(npair):
        m_all = m_scr[hp]
        for g in range(1, ng):
            m_all = jnp.maximum(m_all, m_scr[g * npair + hp])
        num = jnp.zeros_like(m_all)
        den = jnp.zeros_like(m_all)
        for g in range(ng):
            w = jnp.exp(m_scr[g * npair + hp] - m_all)
            num = num + w * o_scr[g * npair + hp]
            den = den + w * l_scr[g * npair + hp]
        o_ref[0, :, hp * LANES:(hp + 1) * LANES] = (num / den).astype(bf16)


def _dil(qkv):
    bsz, s, _ = qkv.shape
    nt = s // DIL_TILE
    ng = len(DIL_CONFIGS)
    blk = (1, DIL_TILE, WB_OUT)
    in_specs = []
    for g in range(ng):
        in_specs.append(pl.BlockSpec(blk, lambda b, i, c=_J_QB + g: (b, i, c)))
        for base in (_J_KB, _J_VB):
            c = base + g
            in_specs += [pl.BlockSpec(blk, lambda b, i, c=c: (b, jnp.maximum(i - 1, 0), c)),
                         pl.BlockSpec(blk, lambda b, i, c=c: (b, i, c)),
                         pl.BlockSpec(blk, lambda b, i, c=c: (b, jnp.minimum(i + 1, nt - 1), c))]
    win_shapes = [pltpu.VMEM((dil, DIL_TILE // dil + 2 * DIL_QBLOCK, WB_OUT), bf16) for _, dil in DIL_CONFIGS]
    return pl.pallas_call(
        _dil_kernel,
        grid=(bsz, nt),
        in_specs=in_specs,
        out_specs=pl.BlockSpec(blk, lambda b, i: (b, i, 0)),
        out_shape=jax.ShapeDtypeStruct((bsz, s, WB_OUT), bf16),
        scratch_shapes=win_shapes + win_shapes + [pltpu.VMEM((ng * DIL_HPG // 2, DIL_TILE, LANES), f32)] * 3,
        compiler_params=_params(("parallel", "parallel")),
        name="dil",
    )(*([qkv] * (7 * ng)))


def _tail_kernel(x_ref, oa_ref, ob_ref, mod_ref, g1_ref, g2_ref, bg_ref, wpa_ref, wpb_ref,
                 wo_ref, win_ref, wout_ref, *rest):
    d = x_ref.shape[2]
    n_gate = 2 * d // COL_TILE
    wg_refs = rest[:n_gate]
    o_ref, h_scr, m_scr, act_scr = rest[n_gate:]
    x = x_ref[0]
    mod = mod_ref[0]
    sh1, sc1, gt1, sh2, sc2, gt2 = [mod[k:k + 1] for k in range(6)]
    h_scr[...] = _modulated_norm(x, g1_ref[...], sc1, sh1).astype(bf16)
    oa = oa_ref[0]
    ob = ob_ref[0]
    for n in range(d // COL_TILE):
        ca = slice(n * COL_TILE, (n + 1) * COL_TILE)
        cb = slice(d + n * COL_TILE, d + (n + 1) * COL_TILE)
        wga, wgb = wg_refs[n], wg_refs[n_gate // 2 + n]
        ga = jax.nn.sigmoid(jnp.dot(h_scr[...], wga[...], preferred_element_type=f32) + bg_ref[:, ca])
        gb = jax.nn.sigmoid(jnp.dot(h_scr[...], wgb[...], preferred_element_type=f32) + bg_ref[:, cb])
        pa = jnp.dot(oa, wpa_ref[:, ca], preferred_element_type=f32)
        pb = jnp.dot(ob, wpb_ref[:, ca], preferred_element_type=f32)
        m_scr[:, ca] = (ga * pa + gb * pb).astype(bf16)
    x1 = x + gt1 * jnp.dot(m_scr[...], wo_ref[...], preferred_element_type=f32)
    h_scr[...] = _modulated_norm(x1, g2_ref[...], sc2, sh2).astype(bf16)
    for f in range(D_FF // FF_CHUNK):
        ca = slice(f * FF_CHUNK, (f + 1) * FF_CHUNK)
        cu = slice(D_FF + f * FF_CHUNK, D_FF + (f + 1) * FF_CHUNK)
        a = jnp.dot(h_scr[...], win_ref[:, ca], preferred_element_type=f32)
        u = jnp.dot(h_scr[...], win_ref[:, cu], preferred_element_type=f32)
        act_scr[:, ca] = ((a * jax.nn.sigmoid(a)) * u).astype(bf16)
    o_ref[0] = x1 + gt2 * jnp.dot(act_scr[...], wout_ref[...], preferred_element_type=f32)


def _tail(x, o_a, o_b, mod6, g1, g2, w_in, bg, wpa, wpb, wo, w_ffn_in, w_ffn_out, tm=TAIL_TM):
    bsz, s, d = x.shape
    tok = lambda w: pl.BlockSpec((1, tm, w), lambda b, i: (b, i, 0))
    consts = [g1, g2, bg, wpa, wpb, wo, w_ffn_in, w_ffn_out]
    n_gate = 2 * d // COL_TILE
    gate_specs = [pl.BlockSpec((d, COL_TILE), lambda b, i, c=_N_J + n: (0, c), pipeline_mode=pl.Buffered(1))
                  for n in range(n_gate)]
    return pl.pallas_call(
        _tail_kernel,
        grid=(bsz, s // tm),
        in_specs=[tok(d), tok(WA), tok(WB_OUT), pl.BlockSpec((1, 6, d), lambda b, i: (b, 0, 0))]
        + [_const_spec(c.shape) for c in consts] + gate_specs,
        out_specs=tok(d),
        out_shape=jax.ShapeDtypeStruct((bsz, s, d), f32),
        scratch_shapes=[pltpu.VMEM((tm, d), bf16), pltpu.VMEM((tm, d), bf16), pltpu.VMEM((tm, D_FF), bf16)],
        compiler_params=_params(("parallel", "parallel")),
        name="tail",
    )(x, o_a, o_b, mod6, *consts, *([w_in] * n_gate))


def _rotary_tables(s):
    half = ROT_DIM // 2
    inv_freq = ROPE_THETA ** (-(jnp.arange(half, dtype=f32) * 2.0) / ROT_DIM)
    ang = jnp.arange(s).astype(f32)[:, None] * inv_freq[None, :]
    cos, sin = jnp.cos(ang), jnp.sin(ang)
    ones = jnp.ones((s, HEAD_DIM - ROT_DIM), f32)
    zeros = jnp.zeros((s, HEAD_DIM - ROT_DIM), f32)
    zh = jnp.zeros((s, half), f32)
    cos_h = jnp.concatenate([cos, cos, ones], axis=1)
    sina_h = jnp.concatenate([-sin, zh, zeros], axis=1)
    sinb_h = jnp.concatenate([zh, sin, zeros], axis=1)
    rep = LANES // HEAD_DIM
    return tuple(jnp.tile(t, (1, rep)) for t in (cos_h, sina_h, sinb_h))


def _layer(x, mod6, g_norm1, g_norm2, w_in, b_gate, g_qa, g_ka, g_qb, g_kb, rpb,
           w_proj_a, w_proj_b, w_o, w_ffn_in, w_ffn_out):
    bsz, s, d = x.shape
    scale = HEAD_DIM ** -0.5
    rep = COL_TILE // HEAD_DIM
    gvec = jnp.stack([jnp.tile(g_qa * scale, rep), jnp.tile(g_ka, rep),
                      jnp.tile(g_qb * scale, rep), jnp.tile(g_kb, rep)]).astype(f32)
    hid = jnp.arange(COL_TILE) // HEAD_DIM
    ones_bd = (hid[:, None] == hid[None, :]).astype(bf16)
    cos_t, sina_t, sinb_t = _rotary_tables(s)
    g1 = g_norm1.reshape(1, d)
    g2 = g_norm2.reshape(1, d)

    w_in = w_in.astype(bf16)
    qkv = _qkv(x, mod6, g1, w_in, gvec, ones_bd, cos_t, sina_t, sinb_t)
    o_a = _na(qkv, _na_bias_table(rpb))
    o_b = _dil(qkv)

    return _tail(x, o_a, o_b, mod6, g1, g2, w_in, b_gate.reshape(1, 2 * d),
                 w_proj_a.astype(bf16), w_proj_b.astype(bf16), w_o.astype(bf16),
                 w_ffn_in.astype(bf16), w_ffn_out.astype(bf16))


def kernel(x, c, w_ada, b_ada, g_norm1, g_norm2, w_in, b_gate, g_qa, g_ka, g_qb, g_kb, rpb,
           w_proj_a, w_proj_b, w_o, w_ffn_in, w_ffn_out):
    depth = w_ada.shape[0]
    bsz, d = c.shape
    for l in range(depth):
        mod6 = _mod(c, w_ada[l], b_ada[l]).reshape(bsz, 6, d)
        x = _layer(x, mod6, g_norm1[l], g_norm2[l], w_in[l], b_gate[l], g_qa[l], g_ka[l], g_qb[l],
                   g_kb[l], rpb[l], w_proj_a[l], w_proj_b[l], w_o[l], w_ffn_in[l], w_ffn_out[l])
    return x
```

```python
import functools

import jax
import jax.numpy as jnp
from jax import lax
from jax.experimental import pallas as pl
from jax.experimental.pallas import tpu as pltpu

f32 = jnp.float32
bf16 = jnp.bfloat16

D_MODEL = 1024
HEAD_DIM = 64
GRID_W = 64
NA_HEADS = 8
NA_KH = 8
NA_KW = 16
DIL_CONFIGS = ((128, 1), (512, 4), (2048, 16))
DIL_HPG = 4
DIL_HEADS = DIL_HPG * len(DIL_CONFIGS)
DIL_QBLOCK = 64
ROT_DIM = HEAD_DIM // 4
ROPE_THETA = 500000.0
D_FF = -(-8 * D_MODEL // (3 * 256)) * 256
EPS = 1e-6
NEG_INF = -1e30
WA = NA_HEADS * HEAD_DIM
WB = DIL_HEADS * HEAD_DIM
WB_OUT = DIL_HPG * HEAD_DIM
W_QKV = 3 * WA + 3 * WB

LANES = 128
COL_TILE = 256
DIL_TILE = 1024
QKV_ROW_SPLIT = 2
QKV_DOT_TILES = 2
DIL_UNROLL = 8
NA_ROWS = 8
TAIL_TM = 512
FF_CHUNK = 256
VMEM_LIMIT = 56 * 1024 * 1024


def _params(sem):
    return pltpu.CompilerParams(dimension_semantics=sem, vmem_limit_bytes=VMEM_LIMIT)


def _const_spec(shape):
    nd = len(shape)
    return pl.BlockSpec(shape, lambda *_: (0,) * nd, pipeline_mode=pl.Buffered(1))


def _mod_kernel(ct_ref, w_ref, b_ref, o_ref):
    ct = ct_ref[...]
    act = ct * jax.nn.sigmoid(ct)
    w = w_ref[...]
    rows = [jnp.sum(act[:, b:b + 1] * w, axis=0, keepdims=True) for b in range(ct.shape[1])]
    o_ref[...] = jnp.concatenate(rows, axis=0) + b_ref[...]


def _mod(c, w_ada, b_ada):
    bsz, d = c.shape
    n = w_ada.shape[1]
    tn = n // 4
    return pl.pallas_call(
        _mod_kernel,
        grid=(n // tn,),
        in_specs=[pl.BlockSpec((d, bsz), lambda j: (0, 0)),
                  pl.BlockSpec((d, tn), lambda j: (0, j)),
                  pl.BlockSpec((1, tn), lambda j: (0, j))],
        out_specs=pl.BlockSpec((bsz, tn), lambda j: (0, j)),
        out_shape=jax.ShapeDtypeStruct((bsz, n), f32),
        compiler_params=_params(("arbitrary",)),
        name="mod",
    )(c.T, w_ada, b_ada.reshape(1, n))


def _modulated_norm(x, g, sc, sh):
    ms = jnp.mean(x * x, axis=-1, keepdims=True)
    y = x * lax.rsqrt(ms + EPS) * g
    return y * (1.0 + sc) + sh


_J_QB = 3 * WA // COL_TILE
_J_KB = _J_QB + WB // COL_TILE
_J_VB = _J_KB + WB // COL_TILE
_N_J = W_QKV // COL_TILE


def _qkv_tile_kind(j):
    if j < WA // COL_TILE:
        return 0, False, 1
    if j < 2 * WA // COL_TILE:
        return 1, False, 1
    if j < _J_QB:
        return None, False, 1
    dil = DIL_CONFIGS[(j - _J_QB) % len(DIL_CONFIGS)][1]
    if j < _J_KB:
        return 2, True, dil
    if j < _J_VB:
        return 3, True, dil
    return None, False, dil


def _qkv_kernel(x_ref, mod_ref, g1_ref, w_ref, gvec_ref, ones_ref, cos_ref, sina_ref, sinb_ref,
                o_ref, h_scr, y_scr):
    tm = x_ref.shape[1]
    nc = COL_TILE // LANES
    mod = mod_ref[0]
    h_scr[...] = _modulated_norm(x_ref[0], g1_ref[...], mod[1:2], mod[0:1]).astype(bf16)

    wide = {}
    for j in range(_N_J):
        gain_row, rotary, dil = _qkv_tile_kind(j)
        cols = slice(j * COL_TILE, (j + 1) * COL_TILE)
        slot = j % y_scr.shape[0]
        for hb in range(QKV_ROW_SPLIT):
            rows = slice(hb * tm // QKV_ROW_SPLIT, (hb + 1) * tm // QKV_ROW_SPLIT)
            if j % QKV_DOT_TILES == 0:
                wcols = slice(j * COL_TILE, min(j + QKV_DOT_TILES, _N_J) * COL_TILE)
                wide[hb] = jnp.dot(h_scr[rows, :], w_ref[:, wcols], preferred_element_type=f32)
            sub = j % QKV_DOT_TILES
            y = wide[hb][:, sub * COL_TILE:(sub + 1) * COL_TILE]
            if gain_row is not None:
                ss = jnp.dot((y * y).astype(bf16), ones_ref[...], preferred_element_type=f32)
                y = y * lax.rsqrt(ss * (1.0 / HEAD_DIM) + EPS) * gvec_ref[gain_row:gain_row + 1, :]
            if rotary:
                cos = jnp.concatenate([cos_ref[rows, :]] * nc, axis=1)
                sina = jnp.concatenate([sina_ref[rows, :]] * nc, axis=1)
                sinb = jnp.concatenate([sinb_ref[rows, :]] * nc, axis=1)
                half = ROT_DIM // 2
                up = pltpu.roll(y, COL_TILE - half, 1)
                dn = pltpu.roll(y, half, 1)
                y = y * cos + up * sina + dn * sinb
            if dil == 1:
                o_ref[0, rows, cols] = y.astype(bf16)
            else:
                for c in range(nc):
                    y_scr[slot, c, rows, :] = y[:, c * LANES:(c + 1) * LANES]
        if dil == 1:
            continue
        seg = DIL_TILE // dil
        for t in range(tm // DIL_TILE):
            for r in range(dil):
                for c in range(nc):
                    rows = y_scr[slot, c, pl.ds(t * DIL_TILE + r, seg, stride=dil), :]
                    o_ref[0, t * DIL_TILE + r * seg:t * DIL_TILE + (r + 1) * seg,
                          j * COL_TILE + c * LANES:j * COL_TILE + (c + 1) * LANES] = rows.astype(bf16)


def _qkv(x, mod6, g1, w_qkv, gvec, ones_bd, cos_t, sina_t, sinb_t, tm=DIL_TILE):
    bsz, s, d = x.shape
    return pl.pallas_call(
        _qkv_kernel,
        grid=(bsz, s // tm),
        in_specs=[pl.BlockSpec((1, tm, d), lambda b, i: (b, i, 0)),
                  pl.BlockSpec((1, 6, d), lambda b, i: (b, 0, 0)),
                  _const_spec(g1.shape),
                  pl.BlockSpec((d, W_QKV), lambda b, i: (0, 0), pipeline_mode=pl.Buffered(1)),
                  _const_spec(gvec.shape),
                  _const_spec(ones_bd.shape),
                  pl.BlockSpec((tm, LANES), lambda b, i: (i, 0)),
                  pl.BlockSpec((tm, LANES), lambda b, i: (i, 0)),
                  pl.BlockSpec((tm, LANES), lambda b, i: (i, 0))],
        out_specs=pl.BlockSpec((1, tm, W_QKV), lambda b, i: (b, i, 0)),
        out_shape=jax.ShapeDtypeStruct((bsz, s, W_QKV), bf16),
        scratch_shapes=[pltpu.VMEM((tm, d), bf16),
                        pltpu.VMEM((2, COL_TILE // LANES, tm, LANES), f32)],
        compiler_params=_params(("parallel", "parallel")),
        name="qkv",
    )(x, mod6, g1, w_qkv, gvec, ones_bd, cos_t, sina_t, sinb_t)


def _pair_mask(nq):
    row = lax.broadcasted_iota(jnp.int32, (2 * nq, LANES), 0)
    lane = lax.broadcasted_iota(jnp.int32, (2 * nq, LANES), 1)
    return (row < nq) == (lane < HEAD_DIM)


def _stack_pair(qp, own_head):
    q2 = jnp.concatenate([qp, qp], axis=0)
    return jnp.where(own_head, q2, jnp.zeros_like(q2))


def _na_kernel(q_ref, kp_ref, kc_ref, kn_ref, vp_ref, vc_ref, vn_ref, tbl_ref, o_ref, kwin, vwin):
    rb = pl.program_id(1)
    blk = NA_ROWS * GRID_W
    n_rows = pl.num_programs(1) * NA_ROWS
    for t, (kr, vr) in enumerate(((kp_ref, vp_ref), (kc_ref, vc_ref), (kn_ref, vn_ref))):
        kwin[t * blk:(t + 1) * blk, :] = kr[0]
        vwin[t * blk:(t + 1) * blk, :] = vr[0]
    first_head = lax.broadcasted_iota(jnp.int32, (GRID_W, LANES), 1) < HEAD_DIM
    own_head = _pair_mask(GRID_W)
    nkeys = NA_KH * GRID_W
    npair = NA_HEADS // 2
    pair_rows = 2 * GRID_W

    def row_body(a, carry):
        r = rb * NA_ROWS + a
        row_start = jnp.clip(r - NA_KH // 2, 0, n_rows - NA_KH)
        delta = r - row_start
        off = pl.multiple_of((row_start - (rb - 1) * NA_ROWS) * GRID_W, GRID_W)
        qoff = pl.multiple_of(a * GRID_W, GRID_W)
        s_parts = []
        for hp in range(npair):
            cols = slice(hp * LANES, (hp + 1) * LANES)
            q2 = _stack_pair(q_ref[0, pl.ds(qoff, GRID_W), cols], own_head)
            kk = kwin[pl.ds(off, nkeys), cols]
            s_parts.append(lax.dot_general(q2, kk, (((1,), (1,)), ((), ())), preferred_element_type=f32))
        n_off = 2 * NA_KH - 2
        bias = jnp.concatenate(
            [jnp.concatenate([tbl_ref[h * n_off + 2 * kp - delta + NA_KH - 1] for kp in range(NA_KH // 2)], axis=1)
             for h in range(NA_HEADS)], axis=0)
        s = jnp.concatenate(s_parts, axis=0) + bias
        m = jnp.max(s, axis=-1, keepdims=True)
        p = jnp.exp(s - m)
        inv = 1.0 / jnp.sum(p, axis=-1, keepdims=True)
        pb = p.astype(bf16)
        for hp in range(npair):
            cols = slice(hp * LANES, (hp + 1) * LANES)
            rows = slice(hp * pair_rows, (hp + 1) * pair_rows)
            vv = vwin[pl.ds(off, nkeys), cols]
            pv = jnp.dot(pb[rows], vv, preferred_element_type=f32) * inv[rows]
            o_ref[0, pl.ds(qoff, GRID_W), cols] = jnp.where(first_head, pv[:GRID_W], pv[GRID_W:]).astype(bf16)
        return carry

    lax.fori_loop(0, NA_ROWS, row_body, 0, unroll=4)


def _na(qkv, tbl):
    bsz, s, _ = qkv.shape
    blk = NA_ROWS * GRID_W
    nb = s // blk
    qspec = pl.BlockSpec((1, blk, WA), lambda b, i: (b, i, 0))

    def halo(col):
        return [pl.BlockSpec((1, blk, WA), lambda b, i: (b, jnp.maximum(i - 1, 0), col)),
                pl.BlockSpec((1, blk, WA), lambda b, i: (b, i, col)),
                pl.BlockSpec((1, blk, WA), lambda b, i: (b, jnp.minimum(i + 1, nb - 1), col))]

    return pl.pallas_call(
        _na_kernel,
        grid=(bsz, nb),
        in_specs=[qspec] + halo(1) + halo(2) + [_const_spec(tbl.shape)],
        out_specs=pl.BlockSpec((1, blk, WA), lambda b, i: (b, i, 0)),
        out_shape=jax.ShapeDtypeStruct((bsz, s, WA), bf16),
        scratch_shapes=[pltpu.VMEM((3 * blk, WA), bf16), pltpu.VMEM((3 * blk, WA), bf16)],
        compiler_params=_params(("parallel", "parallel")),
        name="na",
    )(qkv, qkv, qkv, qkv, qkv, qkv, qkv, tbl)


def _na_bias_table(rpb):
    col = jnp.arange(GRID_W)
    col_start = jnp.clip(col - NA_KW // 2, 0, GRID_W - NA_KW)
    cmask = (col[None, :] >= col_start[:, None]) & (col[None, :] < col_start[:, None] + NA_KW)
    col_off = jnp.clip(col[None, :] - col[:, None] + (NA_KW - 1), 0, 2 * NA_KW - 2)
    onehot = (col_off[None] == jnp.arange(2 * NA_KW - 1)[:, None, None]).astype(f32)
    t = jnp.einsum('hrc,cqk->hrqk', rpb.astype(f32), onehot, precision=lax.Precision.HIGHEST)
    t = jnp.where(cmask, t, NEG_INF)
    t = jnp.concatenate([t[:, :-1], t[:, 1:]], axis=-1)
    return t.reshape(NA_HEADS * (2 * NA_KH - 2), GRID_W, 2 * GRID_W)


def _dil_kernel(*refs):
    ng = len(DIL_CONFIGS)
    in_refs = refs[:7 * ng]
    o_ref = refs[7 * ng]
    scr = refs[7 * ng + 1:]
    kwins, vwins = scr[0:ng], scr[ng:2 * ng]
    o_scr, m_scr, l_scr = scr[2 * ng], scr[2 * ng + 1], scr[2 * ng + 2]
    ti = pl.program_id(1)
    n_tiles = pl.num_programs(1)
    qb = DIL_QBLOCK
    span = 3 * qb

    first_head = lax.broadcasted_iota(jnp.int32, (qb, LANES), 1) < HEAD_DIM
    own_head = _pair_mask(qb)
    npair = DIL_HPG // 2
    qi = lax.broadcasted_iota(jnp.int32, (qb, span), 0)
    kj = lax.broadcasted_iota(jnp.int32, (qb, span), 1)
    band = jnp.where((kj >= qi) & (kj <= qi + 2 * qb), 0.0, NEG_INF).astype(f32)
    kcol = lax.broadcasted_iota(jnp.int32, (1, span), 1)

    for g, (_, dil) in enumerate(DIL_CONFIGS):
        q_ref, kp, kc, kn, vp, vc, vn = in_refs[7 * g:7 * g + 7]
        kwin, vwin = kwins[g], vwins[g]
        seg = DIL_TILE // dil
        nj = seg // qb
        for r in range(dil):
            for win, (p_ref, c_ref, n_ref) in ((kwin, (kp, kc, kn)), (vwin, (vp, vc, vn))):
                win[r, 0:qb, :] = p_ref[0, (r + 1) * seg - qb:(r + 1) * seg, :]
                win[r, qb:qb + seg, :] = c_ref[0, r * seg:(r + 1) * seg, :]
                win[r, qb + seg:2 * qb + seg, :] = n_ref[0, r * seg:r * seg + qb, :]

        def blk_body(u, carry, g=g, dil=dil, nj=nj, q_ref=q_ref, kwin=kwin, vwin=vwin):
            blocks = []
            s_parts = []
            for k in range(DIL_UNROLL):
                t = u * DIL_UNROLL + k
                r = t // nj
                j = t % nj
                pen_lo = jnp.where((ti == 0) & (j == 0), NEG_INF, 0.0).astype(f32)
                pen_hi = jnp.where((ti == n_tiles - 1) & (j == nj - 1), NEG_INF, 0.0).astype(f32)
                mask = band + (jnp.where(kcol < qb, pen_lo, 0.0) + jnp.where(kcol >= 2 * qb, pen_hi, 0.0))
                qoff = pl.multiple_of(t * qb, qb)
                koff = pl.multiple_of(j * qb, qb)
                store_rows = pl.ds(j * (qb * dil) + r, qb, stride=dil) if dil > 1 else pl.ds(qoff, qb)
                blocks.append((r, koff, store_rows))
                for hp in range(npair):
                    cols = slice(hp * LANES, (hp + 1) * LANES)
                    q2 = _stack_pair(q_ref[0, pl.ds(qoff, qb), cols], own_head)
                    kk = kwin[r, pl.ds(koff, span), cols]
                    s2 = lax.dot_general(q2, kk, (((1,), (1,)), ((), ())), preferred_element_type=f32)
                    s_parts += [s2[:qb] + mask, s2[qb:] + mask]
            s = jnp.concatenate(s_parts, axis=0)
            m = jnp.max(s, axis=-1, keepdims=True)
            p = jnp.exp(s - m)
            l = jnp.sum(p, axis=-1, keepdims=True)
            pb = p.astype(bf16)
            for k, (r, koff, store_rows) in enumerate(blocks):
                for hp in range(npair):
                    cols = slice(hp * LANES, (hp + 1) * LANES)
                    r0 = (k * npair + hp) * 2 * qb
                    vv = vwin[r, pl.ds(koff, span), cols]
                    pv = jnp.dot(pb[r0:r0 + 2 * qb], vv, preferred_element_type=f32)
                    slot = g * npair + hp
                    o_scr[slot, store_rows, :] = jnp.where(first_head, pv[:qb], pv[qb:])
                    m_scr[slot, store_rows, :] = jnp.where(first_head, m[r0:r0 + qb], m[r0 + qb:r0 + 2 * qb])
                    l_scr[slot, store_rows, :] = jnp.where(first_head, l[r0:r0 + qb], l[r0 + qb:r0 + 2 * qb])
            return carry

        lax.fori_loop(0, dil * nj // DIL_UNROLL, blk_body, 0)

    for hp in range(npair):
        m_all = m_scr[hp]
        for g in range(1, ng):
            m_all = jnp.maximum(m_all, m_scr[g * npair + hp])
        num = jnp.zeros_like(m_all)
        den = jnp.zeros_like(m_all)
        for g in range(ng):
            w = jnp.exp(m_scr[g * npair + hp] - m_all)
            num = num + w * o_scr[g * npair + hp]
            den = den + w * l_scr[g * npair + hp]
        o_ref[0, :, hp * LANES:(hp + 1) * LANES] = (num / den).astype(bf16)


def _dil(qkv):
    bsz, s, _ = qkv.shape
    nt = s // DIL_TILE
    ng = len(DIL_CONFIGS)
    blk = (1, DIL_TILE, WB_OUT)
    in_specs = []
    for g in range(ng):
        in_specs.append(pl.BlockSpec(blk, lambda b, i, c=_J_QB + g: (b, i, c)))
        for base in (_J_KB, _J_VB):
            c = base + g
            in_specs += [pl.BlockSpec(blk, lambda b, i, c=c: (b, jnp.maximum(i - 1, 0), c)),
                         pl.BlockSpec(blk, lambda b, i, c=c: (b, i, c)),
                         pl.BlockSpec(blk, lambda b, i, c=c: (b, jnp.minimum(i + 1, nt - 1), c))]
    win_shapes = [pltpu.VMEM((dil, DIL_TILE // dil + 2 * DIL_QBLOCK, WB_OUT), bf16) for _, dil in DIL_CONFIGS]
    return pl.pallas_call(
        _dil_kernel,
        grid=(bsz, nt),
        in_specs=in_specs,
        out_specs=pl.BlockSpec(blk, lambda b, i: (b, i, 0)),
        out_shape=jax.ShapeDtypeStruct((bsz, s, WB_OUT), bf16),
        scratch_shapes=win_shapes + win_shapes + [pltpu.VMEM((ng * DIL_HPG // 2, DIL_TILE, LANES), f32)] * 3,
        compiler_params=_params(("parallel", "parallel")),
        name="dil",
    )(*([qkv] * (7 * ng)))


def _tail_kernel(x_ref, oa_ref, ob_ref, mod_ref, g1_ref, g2_ref, bg_ref, wpa_ref, wpb_ref,
                 wo_ref, win_ref, wout_ref, *rest):
    d = x_ref.shape[2]
    n_gate = 2 * d // COL_TILE
    wg_refs = rest[:n_gate]
    o_ref, h_scr, m_scr, act_scr = rest[n_gate:]
    x = x_ref[0]
    mod = mod_ref[0]
    sh1, sc1, gt1, sh2, sc2, gt2 = [mod[k:k + 1] for k in range(6)]
    h_scr[...] = _modulated_norm(x, g1_ref[...], sc1, sh1).astype(bf16)
    oa = oa_ref[0]
    ob = ob_ref[0]
    for n in range(d // COL_TILE):
        ca = slice(n * COL_TILE, (n + 1) * COL_TILE)
        cb = slice(d + n * COL_TILE, d + (n + 1) * COL_TILE)
        wga, wgb = wg_refs[n], wg_refs[n_gate // 2 + n]
        ga = jax.nn.sigmoid(jnp.dot(h_scr[...], wga[...], preferred_element_type=f32) + bg_ref[:, ca])
        gb = jax.nn.sigmoid(jnp.dot(h_scr[...], wgb[...], preferred_element_type=f32) + bg_ref[:, cb])
        pa = jnp.dot(oa, wpa_ref[:, ca], preferred_element_type=f32)
        pb = jnp.dot(ob, wpb_ref[:, ca], preferred_element_type=f32)
        m_scr[:, ca] = (ga * pa + gb * pb).astype(bf16)
    x1 = x + gt1 * jnp.dot(m_scr[...], wo_ref[...], preferred_element_type=f32)
    h_scr[...] = _modulated_norm(x1, g2_ref[...], sc2, sh2).astype(bf16)
    for f in range(D_FF // FF_CHUNK):
        ca = slice(f * FF_CHUNK, (f + 1) * FF_CHUNK)
        cu = slice(D_FF + f * FF_CHUNK, D_FF + (f + 1) * FF_CHUNK)
        a = jnp.dot(h_scr[...], win_ref[:, ca], preferred_element_type=f32)
        u = jnp.dot(h_scr[...], win_ref[:, cu], preferred_element_type=f32)
        act_scr[:, ca] = ((a * jax.nn.sigmoid(a)) * u).astype(bf16)
    o_ref[0] = x1 + gt2 * jnp.dot(act_scr[...], wout_ref[...], preferred_element_type=f32)


def _tail(x, o_a, o_b, mod6, g1, g2, w_in, bg, wpa, wpb, wo, w_ffn_in, w_ffn_out, tm=TAIL_TM):
    bsz, s, d = x.shape
    tok = lambda w: pl.BlockSpec((1, tm, w), lambda b, i: (b, i, 0))
    consts = [g1, g2, bg, wpa, wpb, wo, w_ffn_in, w_ffn_out]
    n_gate = 2 * d // COL_TILE
    gate_specs = [pl.BlockSpec((d, COL_TILE), lambda b, i, c=_N_J + n: (0, c), pipeline_mode=pl.Buffered(1))
                  for n in range(n_gate)]
    return pl.pallas_call(
        _tail_kernel,
        grid=(bsz, s // tm),
        in_specs=[tok(d), tok(WA), tok(WB_OUT), pl.BlockSpec((1, 6, d), lambda b, i: (b, 0, 0))]
        + [_const_spec(c.shape) for c in consts] + gate_specs,
        out_specs=tok(d),
        out_shape=jax.ShapeDtypeStruct((bsz, s, d), f32),
        scratch_shapes=[pltpu.VMEM((tm, d), bf16), pltpu.VMEM((tm, d), bf16), pltpu.VMEM((tm, D_FF), bf16)],
        compiler_params=_params(("parallel", "parallel")),
        name="tail",
    )(x, o_a, o_b, mod6, *consts, *([w_in] * n_gate))


def _rotary_tables(s):
    half = ROT_DIM // 2
    inv_freq = ROPE_THETA ** (-(jnp.arange(half, dtype=f32) * 2.0) / ROT_DIM)
    ang = jnp.arange(s).astype(f32)[:, None] * inv_freq[None, :]
    cos, sin = jnp.cos(ang), jnp.sin(ang)
    ones = jnp.ones((s, HEAD_DIM - ROT_DIM), f32)
    zeros = jnp.zeros((s, HEAD_DIM - ROT_DIM), f32)
    zh = jnp.zeros((s, half), f32)
    cos_h = jnp.concatenate([cos, cos, ones], axis=1)
    sina_h = jnp.concatenate([-sin, zh, zeros], axis=1)
    sinb_h = jnp.concatenate([zh, sin, zeros], axis=1)
    rep = LANES // HEAD_DIM
    return tuple(jnp.tile(t, (1, rep)) for t in (cos_h, sina_h, sinb_h))


def _layer(x, mod6, g_norm1, g_norm2, w_in, b_gate, g_qa, g_ka, g_qb, g_kb, rpb,
           w_proj_a, w_proj_b, w_o, w_ffn_in, w_ffn_out):
    bsz, s, d = x.shape
    scale = HEAD_DIM ** -0.5
    rep = COL_TILE // HEAD_DIM
    gvec = jnp.stack([jnp.tile(g_qa * scale, rep), jnp.tile(g_ka, rep),
                      jnp.tile(g_qb * scale, rep), jnp.tile(g_kb, rep)]).astype(f32)
    hid = jnp.arange(COL_TILE) // HEAD_DIM
    ones_bd = (hid[:, None] == hid[None, :]).astype(bf16)
    cos_t, sina_t, sinb_t = _rotary_tables(s)
    g1 = g_norm1.reshape(1, d)
    g2 = g_norm2.reshape(1, d)

    w_in = w_in.astype(bf16)
    qkv = _qkv(x, mod6, g1, w_in, gvec, ones_bd, cos_t, sina_t, sinb_t)
    o_a = _na(qkv, _na_bias_table(rpb))
    o_b = _dil(qkv)

    return _tail(x, o_a, o_b, mod6, g1, g2, w_in, b_gate.reshape(1, 2 * d),
                 w_proj_a.astype(bf16), w_proj_b.astype(bf16), w_o.astype(bf16),
                 w_ffn_in.astype(bf16), w_ffn_out.astype(bf16))


def kernel(x, c, w_ada, b_ada, g_norm1, g_norm2, w_in, b_gate, g_qa, g_ka, g_qb, g_kb, rpb,
           w_proj_a, w_proj_b, w_o, w_ffn_in, w_ffn_out):
    depth = w_ada.shape[0]
    bsz, d = c.shape
    for l in range(depth):
        mod6 = _mod(c, w_ada[l], b_ada[l]).reshape(bsz, 6, d)
        x = _layer(x, mod6, g_norm1[l], g_norm2[l], w_in[l], b_gate[l], g_qa[l], g_ka[l], g_qb[l],
                   g_kb[l], rpb[l], w_proj_a[l], w_proj_b[l], w_o[l], w_ffn_in[l], w_ffn_out[l])
    return x
```

```python
import functools

import jax
import jax.numpy as jnp
from jax import lax
from jax.experimental import pallas as pl
from jax.experimental.pallas import tpu as pltpu

f32 = jnp.float32
bf16 = jnp.bfloat16

D_MODEL = 1024
HEAD_DIM = 64
GRID_W = 64
NA_HEADS = 8
NA_KH = 8
NA_KW = 16
DIL_CONFIGS = ((128, 1), (512, 4), (2048, 16))
DIL_HPG = 4
DIL_HEADS = DIL_HPG * len(DIL_CONFIGS)
DIL_QBLOCK = 64
ROT_DIM = HEAD_DIM // 4
ROPE_THETA = 500000.0
D_FF = -(-8 * D_MODEL // (3 * 256)) * 256
EPS = 1e-6
NEG_INF = -1e30
WA = NA_HEADS * HEAD_DIM
WB = DIL_HEADS * HEAD_DIM
WB_OUT = DIL_HPG * HEAD_DIM
W_QKV = 3 * WA + 3 * WB

LANES = 128
COL_TILE = 256
DIL_TILE = 1024
QKV_ROW_SPLIT = 2
QKV_DOT_TILES = 2
DIL_UNROLL = 16
NA_ROWS = 8
TAIL_TM = 512
FF_CHUNK = 256
VMEM_LIMIT = 56 * 1024 * 1024


def _params(sem):
    return pltpu.CompilerParams(dimension_semantics=sem, vmem_limit_bytes=VMEM_LIMIT)


def _const_spec(shape):
    nd = len(shape)
    return pl.BlockSpec(shape, lambda *_: (0,) * nd, pipeline_mode=pl.Buffered(1))


def _mod_kernel(ct_ref, w_ref, b_ref, o_ref):
    ct = ct_ref[...]
    act = ct * jax.nn.sigmoid(ct)
    w = w_ref[...]
    rows = [jnp.sum(act[:, b:b + 1] * w, axis=0, keepdims=True) for b in range(ct.shape[1])]
    o_ref[...] = jnp.concatenate(rows, axis=0) + b_ref[...]


def _mod(c, w_ada, b_ada):
    bsz, d = c.shape
    n = w_ada.shape[1]
    tn = n // 4
    return pl.pallas_call(
        _mod_kernel,
        grid=(n // tn,),
        in_specs=[pl.BlockSpec((d, bsz), lambda j: (0, 0)),
                  pl.BlockSpec((d, tn), lambda j: (0, j)),
                  pl.BlockSpec((1, tn), lambda j: (0, j))],
        out_specs=pl.BlockSpec((bsz, tn), lambda j: (0, j)),
        out_shape=jax.ShapeDtypeStruct((bsz, n), f32),
        compiler_params=_params(("arbitrary",)),
        name="mod",
    )(c.T, w_ada, b_ada.reshape(1, n))


def _modulated_norm(x, g, sc, sh):
    ms = jnp.mean(x * x, axis=-1, keepdims=True)
    y = x * lax.rsqrt(ms + EPS) * g
    return y * (1.0 + sc) + sh


_J_QB = 3 * WA // COL_TILE
_J_KB = _J_QB + WB // COL_TILE
_J_VB = _J_KB + WB // COL_TILE
_N_J = W_QKV // COL_TILE


def _qkv_tile_kind(j):
    if j < WA // COL_TILE:
        return 0, False, 1
    if j < 2 * WA // COL_TILE:
        return 1, False, 1
    if j < _J_QB:
        return None, False, 1
    dil = DIL_CONFIGS[(j - _J_QB) % len(DIL_CONFIGS)][1]
    if j < _J_KB:
        return 2, True, dil
    if j < _J_VB:
        return 3, True, dil
    return None, False, dil


def _qkv_kernel(x_ref, mod_ref, g1_ref, w_ref, gvec_ref, ones_ref, rot_ref, o_ref, h_scr, y_scr):
    tm = x_ref.shape[1]
    nc = COL_TILE // LANES
    mod = mod_ref[0]
    for hb in range(QKV_ROW_SPLIT):
        rows = slice(hb * tm // QKV_ROW_SPLIT, (hb + 1) * tm // QKV_ROW_SPLIT)
        h_scr[rows, :] = _modulated_norm(x_ref[0, rows, :], g1_ref[...], mod[1:2], mod[0:1]).astype(bf16)

    wide = {}
    for j in range(_N_J):
        gain_row, rotary, dil = _qkv_tile_kind(j)
        cols = slice(j * COL_TILE, (j + 1) * COL_TILE)
        slot = j % y_scr.shape[0]
        for hb in range(QKV_ROW_SPLIT):
            rows = slice(hb * tm // QKV_ROW_SPLIT, (hb + 1) * tm // QKV_ROW_SPLIT)
            if j % QKV_DOT_TILES == 0:
                wcols = slice(j * COL_TILE, min(j + QKV_DOT_TILES, _N_J) * COL_TILE)
                wide[hb] = jnp.dot(h_scr[rows, :], w_ref[:, wcols], preferred_element_type=f32)
            sub = j % QKV_DOT_TILES
            y = wide[hb][:, sub * COL_TILE:(sub + 1) * COL_TILE]
            if gain_row is not None:
                ss = jnp.dot((y * y).astype(bf16), ones_ref[...], preferred_element_type=f32)
                y = y * lax.rsqrt(ss * (1.0 / HEAD_DIM) + EPS) * gvec_ref[gain_row:gain_row + 1, :]
            if rotary:
                cos, sina, sinb = [jnp.concatenate([rot_ref[rows, k * LANES:(k + 1) * LANES]] * nc, axis=1)
                                   for k in range(3)]
                half = ROT_DIM // 2
                up = pltpu.roll(y, COL_TILE - half, 1)
                dn = pltpu.roll(y, half, 1)
                y = y * cos + up * sina + dn * sinb
            if dil == 1:
                o_ref[0, rows, cols] = y.astype(bf16)
            else:
                for c in range(nc):
                    y_scr[slot, c, rows, :] = y[:, c * LANES:(c + 1) * LANES]
        if dil == 1:
            continue
        seg = DIL_TILE // dil
        for t in range(tm // DIL_TILE):
            for r in range(dil):
                for c in range(nc):
                    rows = y_scr[slot, c, pl.ds(t * DIL_TILE + r, seg, stride=dil), :]
                    o_ref[0, t * DIL_TILE + r * seg:t * DIL_TILE + (r + 1) * seg,
                          j * COL_TILE + c * LANES:j * COL_TILE + (c + 1) * LANES] = rows.astype(bf16)


def _qkv(x, mod6, g1, w_qkv, gvec, ones_bd, rot_t, tm=DIL_TILE):
    bsz, s, d = x.shape
    return pl.pallas_call(
        _qkv_kernel,
        grid=(bsz, s // tm),
        in_specs=[pl.BlockSpec((1, tm, d), lambda b, i: (b, i, 0)),
                  pl.BlockSpec((1, 6, d), lambda b, i: (b, 0, 0)),
                  _const_spec(g1.shape),
                  pl.BlockSpec((d, W_QKV), lambda b, i: (0, 0), pipeline_mode=pl.Buffered(1)),
                  _const_spec(gvec.shape),
                  _const_spec(ones_bd.shape),
                  pl.BlockSpec((tm, 3 * LANES), lambda b, i: (i, 0))],
        out_specs=pl.BlockSpec((1, tm, W_QKV), lambda b, i: (b, i, 0)),
        out_shape=jax.ShapeDtypeStruct((bsz, s, W_QKV), bf16),
        scratch_shapes=[pltpu.VMEM((tm, d), bf16),
                        pltpu.VMEM((2, COL_TILE // LANES, tm, LANES), f32)],
        compiler_params=_params(("parallel", "parallel")),
        name="qkv",
    )(x, mod6, g1, w_qkv, gvec, ones_bd, rot_t)


def _pair_mask(nq):
    row = lax.broadcasted_iota(jnp.int32, (2 * nq, LANES), 0)
    lane = lax.broadcasted_iota(jnp.int32, (2 * nq, LANES), 1)
    return (row < nq) == (lane < HEAD_DIM)


def _stack_pair(qp, own_head):
    q2 = jnp.concatenate([qp, qp], axis=0)
    return jnp.where(own_head, q2, jnp.zeros_like(q2))


def _na_kernel(q_ref, kp_ref, kc_ref, kn_ref, vp_ref, vc_ref, vn_ref, tbl_ref, o_ref, kwin, vwin):
    rb = pl.program_id(1)
    blk = NA_ROWS * GRID_W
    n_rows = pl.num_programs(1) * NA_ROWS
    for t, (kr, vr) in enumerate(((kp_ref, vp_ref), (kc_ref, vc_ref), (kn_ref, vn_ref))):
        kwin[t * blk:(t + 1) * blk, :] = kr[0]
        vwin[t * blk:(t + 1) * blk, :] = vr[0]
    first_head = lax.broadcasted_iota(jnp.int32, (GRID_W, LANES), 1) < HEAD_DIM
    own_head = _pair_mask(GRID_W)
    nkeys = NA_KH * GRID_W
    npair = NA_HEADS // 2
    pair_rows = 2 * GRID_W

    def row_body(a, carry):
        r = rb * NA_ROWS + a
        row_start = jnp.clip(r - NA_KH // 2, 0, n_rows - NA_KH)
        delta = r - row_start
        off = pl.multiple_of((row_start - (rb - 1) * NA_ROWS) * GRID_W, GRID_W)
        qoff = pl.multiple_of(a * GRID_W, GRID_W)
        s_parts = []
        for hp in range(npair):
            cols = slice(hp * LANES, (hp + 1) * LANES)
            q2 = _stack_pair(q_ref[0, pl.ds(qoff, GRID_W), cols], own_head)
            kk = kwin[pl.ds(off, nkeys), cols]
            s_parts.append(lax.dot_general(q2, kk, (((1,), (1,)), ((), ())), preferred_element_type=f32))
        n_off = 2 * NA_KH - 2
        bias = jnp.concatenate(
            [jnp.concatenate([tbl_ref[h * n_off + 2 * kp - delta + NA_KH - 1] for kp in range(NA_KH // 2)], axis=1)
             for h in range(NA_HEADS)], axis=0)
        s = jnp.concatenate(s_parts, axis=0) + bias
        m = jnp.max(s, axis=-1, keepdims=True)
        p = jnp.exp(s - m)
        inv = 1.0 / jnp.sum(p, axis=-1, keepdims=True)
        pb = p.astype(bf16)
        for hp in range(npair):
            cols = slice(hp * LANES, (hp + 1) * LANES)
            rows = slice(hp * pair_rows, (hp + 1) * pair_rows)
            vv = vwin[pl.ds(off, nkeys), cols]
            pv = jnp.dot(pb[rows], vv, preferred_element_type=f32) * inv[rows]
            o_ref[0, pl.ds(qoff, GRID_W), cols] = jnp.where(first_head, pv[:GRID_W], pv[GRID_W:]).astype(bf16)
        return carry

    lax.fori_loop(0, NA_ROWS, row_body, 0, unroll=NA_ROWS)


def _na(qkv, tbl):
    bsz, s, _ = qkv.shape
    blk = NA_ROWS * GRID_W
    nb = s // blk
    qspec = pl.BlockSpec((1, blk, WA), lambda b, i: (b, i, 0))

    def halo(col):
        return [pl.BlockSpec((1, blk, WA), lambda b, i: (b, jnp.maximum(i - 1, 0), col)),
                pl.BlockSpec((1, blk, WA), lambda b, i: (b, i, col)),
                pl.BlockSpec((1, blk, WA), lambda b, i: (b, jnp.minimum(i + 1, nb - 1), col))]

    return pl.pallas_call(
        _na_kernel,
        grid=(bsz, nb),
        in_specs=[qspec] + halo(1) + halo(2) + [_const_spec(tbl.shape)],
        out_specs=pl.BlockSpec((1, blk, WA), lambda b, i: (b, i, 0)),
        out_shape=jax.ShapeDtypeStruct((bsz, s, WA), bf16),
        scratch_shapes=[pltpu.VMEM((3 * blk, WA), bf16), pltpu.VMEM((3 * blk, WA), bf16)],
        compiler_params=_params(("parallel", "parallel")),
        name="na",
    )(qkv, qkv, qkv, qkv, qkv, qkv, qkv, tbl)


def _na_bias_table(rpb):
    col = jnp.arange(GRID_W)
    col_start = jnp.clip(col - NA_KW // 2, 0, GRID_W - NA_KW)
    cmask = (col[None, :] >= col_start[:, None]) & (col[None, :] < col_start[:, None] + NA_KW)
    col_off = jnp.clip(col[None, :] - col[:, None] + (NA_KW - 1), 0, 2 * NA_KW - 2)
    onehot = (col_off[None] == jnp.arange(2 * NA_KW - 1)[:, None, None]).astype(f32)
    t = jnp.einsum('hrc,cqk->hrqk', rpb.astype(f32), onehot, precision=lax.Precision.HIGHEST)
    t = jnp.where(cmask, t, NEG_INF)
    t = jnp.concatenate([t[:, :-1], t[:, 1:]], axis=-1)
    return t.reshape(NA_HEADS * (2 * NA_KH - 2), GRID_W, 2 * GRID_W)


def _dil_kernel(*refs):
    ng = len(DIL_CONFIGS)
    in_refs = refs[:7 * ng]
    o_ref = refs[7 * ng]
    o_scr, m_scr, l_scr = refs[7 * ng + 1:]
    ti = pl.program_id(1)
    n_tiles = pl.num_programs(1)
    qb = DIL_QBLOCK
    span = 3 * qb

    first_head = lax.broadcasted_iota(jnp.int32, (qb, LANES), 1) < HEAD_DIM
    own_head = _pair_mask(qb)
    npair = DIL_HPG // 2
    qi = lax.broadcasted_iota(jnp.int32, (qb, span), 0)
    kj = lax.broadcasted_iota(jnp.int32, (qb, span), 1)
    band = jnp.where((kj >= qi) & (kj <= qi + 2 * qb), 0.0, NEG_INF).astype(f32)
    kcol = lax.broadcasted_iota(jnp.int32, (1, span), 1)
    pen_lo = jnp.where(kcol < qb, jnp.where(ti == 0, NEG_INF, 0.0), 0.0).astype(f32)
    pen_hi = jnp.where(kcol >= 2 * qb, jnp.where(ti == n_tiles - 1, NEG_INF, 0.0), 0.0).astype(f32)

    for g, (_, dil) in enumerate(DIL_CONFIGS):
        q_ref, kp, kc, kn, vp, vc, vn = in_refs[7 * g:7 * g + 7]
        seg = DIL_TILE // dil
        nj = seg // qb

        def window(refs3, r, j, cols, seg=seg, nj=nj):
            p_ref, c_ref, n_ref = refs3
            base = r * seg
            if 0 < j < nj - 1:
                return c_ref[0, base + (j - 1) * qb:base + (j + 2) * qb, cols]
            lo = (p_ref[0, base + seg - qb:base + seg, cols] if j == 0
                  else c_ref[0, base + (j - 1) * qb:base + j * qb, cols])
            mid = c_ref[0, base + j * qb:base + (j + 1) * qb, cols]
            hi = (n_ref[0, base:base + qb, cols] if j == nj - 1
                  else c_ref[0, base + (j + 1) * qb:base + (j + 2) * qb, cols])
            return jnp.concatenate([lo, mid, hi], axis=0)

        for t0 in range(0, dil * nj, DIL_UNROLL):
            blocks = [divmod(t, nj) for t in range(t0, t0 + DIL_UNROLL)]
            s_parts = []
            for k, (r, j) in enumerate(blocks):
                mask = band
                if j == 0:
                    mask = mask + pen_lo
                if j == nj - 1:
                    mask = mask + pen_hi
                for hp in range(npair):
                    cols = slice(hp * LANES, (hp + 1) * LANES)
                    q2 = _stack_pair(q_ref[0, (t0 + k) * qb:(t0 + k + 1) * qb, cols], own_head)
                    kk = window((kp, kc, kn), r, j, cols)
                    s2 = lax.dot_general(q2, kk, (((1,), (1,)), ((), ())), preferred_element_type=f32)
                    s_parts += [s2[:qb] + mask, s2[qb:] + mask]
            s = jnp.concatenate(s_parts, axis=0)
            m = jnp.max(s, axis=-1, keepdims=True)
            p = jnp.exp(s - m)
            l = jnp.sum(p, axis=-1, keepdims=True)
            pb = p.astype(bf16)
            for k, (r, j) in enumerate(blocks):
                store_rows = pl.ds(j * qb * dil + r, qb, stride=dil) if dil > 1 else pl.ds(j * qb, qb)
                for hp in range(npair):
                    cols = slice(hp * LANES, (hp + 1) * LANES)
                    r0 = (k * npair + hp) * 2 * qb
                    vv = window((vp, vc, vn), r, j, cols)
                    pv = jnp.dot(pb[r0:r0 + 2 * qb], vv, preferred_element_type=f32)
                    slot = g * npair + hp
                    o_scr[slot, store_rows, :] = jnp.where(first_head, pv[:qb], pv[qb:])
                    m_scr[slot, store_rows, :] = jnp.where(first_head, m[r0:r0 + qb], m[r0 + qb:r0 + 2 * qb])
                    l_scr[slot, store_rows, :] = jnp.where(first_head, l[r0:r0 + qb], l[r0 + qb:r0 + 2 * qb])

    for hp in range(npair):
        m_all = m_scr[hp]
        for g in range(1, ng):
            m_all = jnp.maximum(m_all, m_scr[g * npair + hp])
        num = jnp.zeros_like(m_all)
        den = jnp.zeros_like(m_all)
        for g in range(ng):
            w = jnp.exp(m_scr[g * npair + hp] - m_all)
            num = num + w * o_scr[g * npair + hp]
            den = den + w * l_scr[g * npair + hp]
        o_ref[0, :, hp * LANES:(hp + 1) * LANES] = (num / den).astype(bf16)


def _dil(qkv):
    bsz, s, _ = qkv.shape
    nt = s // DIL_TILE
    ng = len(DIL_CONFIGS)
    blk = (1, DIL_TILE, WB_OUT)
    in_specs = []
    for g in range(ng):
        in_specs.append(pl.BlockSpec(blk, lambda b, i, c=_J_QB + g: (b, i, c)))
        for base in (_J_KB, _J_VB):
            c = base + g
            in_specs += [pl.BlockSpec(blk, lambda b, i, c=c: (b, jnp.maximum(i - 1, 0), c)),
                         pl.BlockSpec(blk, lambda b, i, c=c: (b, i, c)),
                         pl.BlockSpec(blk, lambda b, i, c=c: (b, jnp.minimum(i + 1, nt - 1), c))]
    return pl.pallas_call(
        _dil_kernel,
        grid=(bsz, nt),
        in_specs=in_specs,
        out_specs=pl.BlockSpec(blk, lambda b, i: (b, i, 0)),
        out_shape=jax.ShapeDtypeStruct((bsz, s, WB_OUT), bf16),
        scratch_shapes=[pltpu.VMEM((ng * DIL_HPG // 2, DIL_TILE, LANES), f32)] * 3,
        compiler_params=_params(("parallel", "parallel")),
        name="dil",
    )(*([qkv] * (7 * ng)))


def _tail_kernel(x_ref, oa_ref, ob_ref, mod_ref, g1_ref, g2_ref, bg_ref, wpa_ref, wpb_ref,
                 wo_ref, win_ref, wout_ref, *rest):
    d = x_ref.shape[2]
    n_gate = 2 * d // COL_TILE
    wg_refs = rest[:n_gate]
    o_ref, h_scr, m_scr, act_scr = rest[n_gate:]
    x = x_ref[0]
    mod = mod_ref[0]
    sh1, sc1, gt1, sh2, sc2, gt2 = [mod[k:k + 1] for k in range(6)]
    h_scr[...] = _modulated_norm(x, g1_ref[...], sc1, sh1).astype(bf16)
    oa = oa_ref[0]
    ob = ob_ref[0]
    for n in range(d // COL_TILE):
        ca = slice(n * COL_TILE, (n + 1) * COL_TILE)
        cb = slice(d + n * COL_TILE, d + (n + 1) * COL_TILE)
        wga, wgb = wg_refs[n], wg_refs[n_gate // 2 + n]
        ga = jax.nn.sigmoid(jnp.dot(h_scr[...], wga[...], preferred_element_type=f32) + bg_ref[:, ca])
        gb = jax.nn.sigmoid(jnp.dot(h_scr[...], wgb[...], preferred_element_type=f32) + bg_ref[:, cb])
        pa = jnp.dot(oa, wpa_ref[:, ca], preferred_element_type=f32)
        pb = jnp.dot(ob, wpb_ref[:, ca], preferred_element_type=f32)
        m_scr[:, ca] = (ga * pa + gb * pb).astype(bf16)
    x1 = x + gt1 * jnp.dot(m_scr[...], wo_ref[...], preferred_element_type=f32)
    h_scr[...] = _modulated_norm(x1, g2_ref[...], sc2, sh2).astype(bf16)
    for f in range(D_FF // FF_CHUNK):
        ca = slice(f * FF_CHUNK, (f + 1) * FF_CHUNK)
        cu = slice(D_FF + f * FF_CHUNK, D_FF + (f + 1) * FF_CHUNK)
        a = jnp.dot(h_scr[...], win_ref[:, ca], preferred_element_type=f32)
        u = jnp.dot(h_scr[...], win_ref[:, cu], preferred_element_type=f32)
        act_scr[:, ca] = ((a * jax.nn.sigmoid(a)) * u).astype(bf16)
    o_ref[0] = x1 + gt2 * jnp.dot(act_scr[...], wout_ref[...], preferred_element_type=f32)


def _tail(x, o_a, o_b, mod6, g1, g2, w_in, bg, wpa, wpb, wo, w_ffn_in, w_ffn_out, tm=TAIL_TM):
    bsz, s, d = x.shape
    tok = lambda w: pl.BlockSpec((1, tm, w), lambda b, i: (b, i, 0))
    consts = [g1, g2, bg, wpa, wpb, wo, w_ffn_in, w_ffn_out]
    n_gate = 2 * d // COL_TILE
    gate_specs = [pl.BlockSpec((d, COL_TILE), lambda b, i, c=_N_J + n: (0, c), pipeline_mode=pl.Buffered(1))
                  for n in range(n_gate)]
    return pl.pallas_call(
        _tail_kernel,
        grid=(bsz, s // tm),
        in_specs=[tok(d), tok(WA), tok(WB_OUT), pl.BlockSpec((1, 6, d), lambda b, i: (b, 0, 0))]
        + [_const_spec(c.shape) for c in consts] + gate_specs,
        out_specs=tok(d),
        out_shape=jax.ShapeDtypeStruct((bsz, s, d), f32),
        scratch_shapes=[pltpu.VMEM((tm, d), bf16), pltpu.VMEM((tm, d), bf16), pltpu.VMEM((tm, D_FF), bf16)],
        compiler_params=_params(("parallel", "parallel")),
        name="tail",
    )(x, o_a, o_b, mod6, *consts, *([w_in] * n_gate))


def _rotary_tables(s):
    half = ROT_DIM // 2
    inv_freq = ROPE_THETA ** (-(jnp.arange(half, dtype=f32) * 2.0) / ROT_DIM)
    ang = jnp.arange(s).astype(f32)[:, None] * inv_freq[None, :]
    cos, sin = jnp.cos(ang), jnp.sin(ang)
    ones = jnp.ones((s, HEAD_DIM - ROT_DIM), f32)
    zeros = jnp.zeros((s, HEAD_DIM - ROT_DIM), f32)
    zh = jnp.zeros((s, half), f32)
    cos_h = jnp.concatenate([cos, cos, ones], axis=1)
    sina_h = jnp.concatenate([-sin, zh, zeros], axis=1)
    sinb_h = jnp.concatenate([zh, sin, zeros], axis=1)
    rep = LANES // HEAD_DIM
    return jnp.concatenate([t for t in (cos_h, sina_h, sinb_h) for _ in range(rep)], axis=1)


def _layer(x, mod6, g_norm1, g_norm2, w_in, b_gate, g_qa, g_ka, g_qb, g_kb, rpb,
           w_proj_a, w_proj_b, w_o, w_ffn_in, w_ffn_out):
    bsz, s, d = x.shape
    scale = HEAD_DIM ** -0.5
    rep = COL_TILE // HEAD_DIM
    gvec = jnp.stack([jnp.tile(g_qa * scale, rep), jnp.tile(g_ka, rep),
                      jnp.tile(g_qb * scale, rep), jnp.tile(g_kb, rep)]).astype(f32)
    hid = jnp.arange(COL_TILE) // HEAD_DIM
    ones_bd = (hid[:, None] == hid[None, :]).astype(bf16)
    rot_t = _rotary_tables(s)
    g1 = g_norm1.reshape(1, d)
    g2 = g_norm2.reshape(1, d)

    w_in = w_in.astype(bf16)
    qkv = _qkv(x, mod6, g1, w_in, gvec, ones_bd, rot_t)
    o_a = _na(qkv, _na_bias_table(rpb))
    o_b = _dil(qkv)

    return _tail(x, o_a, o_b, mod6, g1, g2, w_in, b_gate.reshape(1, 2 * d),
                 w_proj_a.astype(bf16), w_proj_b.astype(bf16), w_o.astype(bf16),
                 w_ffn_in.astype(bf16), w_ffn_out.astype(bf16))


def kernel(x, c, w_ada, b_ada, g_norm1, g_norm2, w_in, b_gate, g_qa, g_ka, g_qb, g_kb, rpb,
           w_proj_a, w_proj_b, w_o, w_ffn_in, w_ffn_out):
    depth = w_ada.shape[0]
    bsz, d = c.shape
    for l in range(depth):
        mod6 = _mod(c, w_ada[l], b_ada[l]).reshape(bsz, 6, d)
        x = _layer(x, mod6, g_norm1[l], g_norm2[l], w_in[l], b_gate[l], g_qa[l], g_ka[l], g_qb[l],
                   g_kb[l], rpb[l], w_proj_a[l], w_proj_b[l], w_o[l], w_ffn_in[l], w_ffn_out[l])
    return x
```

```python
import functools

import jax
import jax.numpy as jnp
from jax import lax
from jax.experimental import pallas as pl
from jax.experimental.pallas import tpu as pltpu

f32 = jnp.float32
bf16 = jnp.bfloat16

D_MODEL = 1024
HEAD_DIM = 64
GRID_W = 64
NA_HEADS = 8
NA_KH = 8
NA_KW = 16
DIL_CONFIGS = ((128, 1), (512, 4), (2048, 16))
DIL_HPG = 4
DIL_HEADS = DIL_HPG * len(DIL_CONFIGS)
DIL_QBLOCK = 64
ROT_DIM = HEAD_DIM // 4
ROPE_THETA = 500000.0
D_FF = -(-8 * D_MODEL // (3 * 256)) * 256
EPS = 1e-6
NEG_INF = -1e30
LOG2E = 1.4426950408889634
WA =NA_HEADS * HEAD_DIM
WB = DIL_HEADS * HEAD_DIM
WB_OUT = DIL_HPG * HEAD_DIM
W_QKV = 3 * WA + 3 * WB

LANES = 128
COL_TILE = 256
DIL_TILE = 1024
QKV_ROW_SPLIT = 2
QKV_DOT_TILES = 2
DIL_UNROLL = 16
NA_ROWS = 8
TAIL_TM = 512
FF_CHUNK = 256
VMEM_LIMIT = 56 * 1024 * 1024


def _params(sem):
    return pltpu.CompilerParams(dimension_semantics=sem, vmem_limit_bytes=VMEM_LIMIT)


def _const_spec(shape):
    nd = len(shape)
    return pl.BlockSpec(shape, lambda *_: (0,) * nd, pipeline_mode=pl.Buffered(1))


def _mod_kernel(ct_ref, w_ref, b_ref, o_ref):
    ct = ct_ref[...]
    act = ct * jax.nn.sigmoid(ct)
    w = w_ref[...]
    rows = [jnp.sum(act[:, b:b + 1] * w, axis=0, keepdims=True) for b in range(ct.shape[1])]
    o_ref[...] = jnp.concatenate(rows, axis=0) + b_ref[...]


def _mod(c, w_ada, b_ada):
    bsz, d = c.shape
    n = w_ada.shape[1]
    tn = n // 4
    return pl.pallas_call(
        _mod_kernel,
        grid=(n // tn,),
        in_specs=[pl.BlockSpec((d, bsz), lambda j: (0, 0)),
                  pl.BlockSpec((d, tn), lambda j: (0, j)),
                  pl.BlockSpec((1, tn), lambda j: (0, j))],
        out_specs=pl.BlockSpec((bsz, tn), lambda j: (0, j)),
        out_shape=jax.ShapeDtypeStruct((bsz, n), f32),
        compiler_params=_params(("arbitrary",)),
        name="mod",
    )(c.T, w_ada, b_ada.reshape(1, n))


def _modulated_norm(x, g, sc, sh):
    ms = jnp.mean(x * x, axis=-1, keepdims=True)
    y = x * lax.rsqrt(ms + EPS) * g
    return y * (1.0 + sc) + sh


_J_QB = 3 * WA // COL_TILE
_J_KB = _J_QB + WB // COL_TILE
_J_VB = _J_KB + WB // COL_TILE
_N_J = W_QKV // COL_TILE


def _qkv_tile_kind(j):
    if j < WA // COL_TILE:
        return 0, False, 1
    if j < 2 * WA // COL_TILE:
        return 1, False, 1
    if j < _J_QB:
        return None, False, 1
    dil = DIL_CONFIGS[(j - _J_QB) % len(DIL_CONFIGS)][1]
    if j < _J_KB:
        return 2, True, dil
    if j < _J_VB:
        return 3, True, dil
    return None, False, dil


def _qkv_kernel(n_cast, x_ref, mod_ref, g1_ref, w_ref, gvec_ref, ones_ref, cos_ref, sina_ref, sinb_ref,
                *rest):
    cast_in, o_ref, cast_out = rest[:n_cast], rest[n_cast], rest[n_cast + 1:2 * n_cast + 1]
    h_scr, y_scr = rest[2 * n_cast + 1:]
    for src, dst in zip(cast_in, cast_out):
        dst[...] = src[...].astype(bf16)
    tm = x_ref.shape[1]
    nc = COL_TILE // LANES
    mod = mod_ref[0]
    h_scr[...] = _modulated_norm(x_ref[0], g1_ref[...], mod[1:2], mod[0:1]).astype(bf16)

    wide = {}
    for j in range(_N_J):
        gain_row, rotary, dil = _qkv_tile_kind(j)
        cols = slice(j * COL_TILE, (j + 1) * COL_TILE)
        slot = j % y_scr.shape[0]
        for hb in range(QKV_ROW_SPLIT):
            rows = slice(hb * tm // QKV_ROW_SPLIT, (hb + 1) * tm // QKV_ROW_SPLIT)
            if j % QKV_DOT_TILES == 0:
                wcols = slice(j * COL_TILE, min(j + QKV_DOT_TILES, _N_J) * COL_TILE)
                wide[hb] = jnp.dot(h_scr[rows, :], w_ref[:, wcols], preferred_element_type=f32)
            sub = j % QKV_DOT_TILES
            y = wide[hb][:, sub * COL_TILE:(sub + 1) * COL_TILE]
            if gain_row is not None:
                ss = jnp.dot((y * y).astype(bf16), ones_ref[...], preferred_element_type=f32)
                y = y * lax.rsqrt(ss * (1.0 / HEAD_DIM) + EPS) * gvec_ref[gain_row:gain_row + 1, :]
            if rotary:
                cos, sina, sinb = [jnp.concatenate([t_ref[rows, :]] * nc, axis=1)
                                   for t_ref in (cos_ref, sina_ref, sinb_ref)]
                half = ROT_DIM // 2
                up = pltpu.roll(y, COL_TILE - half, 1)
                dn = pltpu.roll(y, half, 1)
                y = y * cos + up * sina + dn * sinb
            if dil == 1:
                o_ref[0, rows, cols] = y.astype(bf16)
            else:
                for c in range(nc):
                    y_scr[slot, c, rows, :] = y[:, c * LANES:(c + 1) * LANES]
        if dil == 1:
            continue
        seg = DIL_TILE // dil
        for t in range(tm // DIL_TILE):
            for r in range(dil):
                for c in range(nc):
                    rows = y_scr[slot, c, pl.ds(t * DIL_TILE + r, seg, stride=dil), :]
                    o_ref[0, t * DIL_TILE + r * seg:t * DIL_TILE + (r + 1) * seg,
                          j * COL_TILE + c * LANES:j * COL_TILE + (c + 1) * LANES] = rows.astype(bf16)


def _qkv(x, mod6, g1, w_qkv, gvec, ones_bd, rot_tables, cast_ws, tm=DIL_TILE):
    bsz, s, d = x.shape
    nt = s // tm
    steps = bsz * nt
    slab_specs = [pl.BlockSpec((w.shape[0] // steps, w.shape[1]), lambda b, i: (b * nt + i, 0)) for w in cast_ws]
    outs = pl.pallas_call(
        functools.partial(_qkv_kernel, len(cast_ws)),
        grid=(bsz, nt),
        in_specs=[pl.BlockSpec((1, tm, d), lambda b, i: (b, i, 0)),
                  pl.BlockSpec((1, 6, d), lambda b, i: (b, 0, 0)),
                  _const_spec(g1.shape),
                  pl.BlockSpec((d, W_QKV), lambda b, i: (0, 0), pipeline_mode=pl.Buffered(1)),
                  _const_spec(gvec.shape),
                  _const_spec(ones_bd.shape),
                  ] + [pl.BlockSpec((tm, LANES), lambda b, i: (i, 0))] * len(rot_tables) + slab_specs,
        out_specs=[pl.BlockSpec((1, tm, W_QKV), lambda b, i: (b, i, 0))] + slab_specs,
        out_shape=[jax.ShapeDtypeStruct((bsz, s, W_QKV), bf16)]
        + [jax.ShapeDtypeStruct(w.shape, bf16) for w in cast_ws],
        scratch_shapes=[pltpu.VMEM((tm, d), bf16),
                        pltpu.VMEM((2, COL_TILE // LANES, tm, LANES), f32)],
        compiler_params=_params(("parallel", "parallel")),
        name="qkv",
    )(x, mod6, g1, w_qkv, gvec, ones_bd, *rot_tables, *cast_ws)
    return outs[0], outs[1:]


def _pair_mask(nq):
    row = lax.broadcasted_iota(jnp.int32, (2 * nq, LANES), 0)
    lane = lax.broadcasted_iota(jnp.int32, (2 * nq, LANES), 1)
    return (row < nq) == (lane < HEAD_DIM)


def _stack_pair(qp, own_head):
    q2 = jnp.concatenate([qp, qp], axis=0)
    return jnp.where(own_head, q2, jnp.zeros_like(q2))


def _na_kernel(q_ref, kp_ref, kc_ref, kn_ref, vp_ref, vc_ref, vn_ref, tbl_ref, o_ref, kwin, vwin):
    rb = pl.program_id(1)
    blk = NA_ROWS * GRID_W
    n_rows = pl.num_programs(1) * NA_ROWS
    for t, (kr, vr) in enumerate(((kp_ref, vp_ref), (kc_ref, vc_ref), (kn_ref, vn_ref))):
        kwin[t * blk:(t + 1) * blk, :] = kr[0]
        vwin[t * blk:(t + 1) * blk, :] = vr[0]
    first_head = lax.broadcasted_iota(jnp.int32, (GRID_W, LANES), 1) < HEAD_DIM
    own_head = _pair_mask(GRID_W)
    nkeys = NA_KH * GRID_W
    npair = NA_HEADS // 2
    pair_rows = 2 * GRID_W

    def row_body(a, carry):
        r = rb * NA_ROWS + a
        row_start = jnp.clip(r - NA_KH // 2, 0, n_rows - NA_KH)
        delta = r - row_start
        off = pl.multiple_of((row_start - (rb - 1) * NA_ROWS) * GRID_W, GRID_W)
        qoff = pl.multiple_of(a * GRID_W, GRID_W)
        s_parts = []
        for hp in range(npair):
            cols = slice(hp * LANES, (hp + 1) * LANES)
            q2 = _stack_pair(q_ref[0, pl.ds(qoff, GRID_W), cols], own_head)
            kk = kwin[pl.ds(off, nkeys), cols]
            s_parts.append(lax.dot_general(q2, kk, (((1,), (1,)), ((), ())), preferred_element_type=f32))
        n_off = 2 * NA_KH - 2
        bias = jnp.concatenate(
            [jnp.concatenate([tbl_ref[h * n_off + 2 * kp - delta + NA_KH - 1] for kp in range(NA_KH // 2)], axis=1)
             for h in range(NA_HEADS)], axis=0)
        s = jnp.concatenate(s_parts, axis=0) + bias
        m = jnp.max(s, axis=-1, keepdims=True)
        p = jnp.exp2(s - m)
        inv = 1.0 / jnp.sum(p, axis=-1, keepdims=True)
        pb = p.astype(bf16)
        for hp in range(npair):
            cols = slice(hp * LANES, (hp + 1) * LANES)
            rows = slice(hp * pair_rows, (hp + 1) * pair_rows)
            vv = vwin[pl.ds(off, nkeys), cols]
            pv = jnp.dot(pb[rows], vv, preferred_element_type=f32) * inv[rows]
            o_ref[0, pl.ds(qoff, GRID_W), cols] = jnp.where(first_head, pv[:GRID_W], pv[GRID_W:]).astype(bf16)
        return carry

    lax.fori_loop(0, NA_ROWS, row_body, 0, unroll=NA_ROWS)


def _na(qkv, tbl):
    bsz, s, _ = qkv.shape
    blk = NA_ROWS * GRID_W
    nb = s // blk
    qspec = pl.BlockSpec((1, blk, WA), lambda b, i: (b, i, 0))

    def halo(col):
        return [pl.BlockSpec((1, blk, WA), lambda b, i: (b, jnp.maximum(i - 1, 0), col)),
                pl.BlockSpec((1, blk, WA), lambda b, i: (b, i, col)),
                pl.BlockSpec((1, blk, WA), lambda b, i: (b, jnp.minimum(i + 1, nb - 1), col))]

    return pl.pallas_call(
        _na_kernel,
        grid=(bsz, nb),
        in_specs=[qspec] + halo(1) + halo(2) + [_const_spec(tbl.shape)],
        out_specs=pl.BlockSpec((1, blk, WA), lambda b, i: (b, i, 0)),
        out_shape=jax.ShapeDtypeStruct((bsz, s, WA), bf16),
        scratch_shapes=[pltpu.VMEM((3 * blk, WA), bf16), pltpu.VMEM((3 * blk, WA), bf16)],
        compiler_params=_params(("parallel", "parallel")),
        name="na",
    )(qkv, qkv, qkv, qkv, qkv, qkv, qkv, tbl)


def _na_bias_table(rpb):
    col = jnp.arange(GRID_W)
    col_start = jnp.clip(col - NA_KW // 2, 0, GRID_W - NA_KW)
    cmask = (col[None, :] >= col_start[:, None]) & (col[None, :] < col_start[:, None] + NA_KW)
    col_off = jnp.clip(col[None, :] - col[:, None] + (NA_KW - 1), 0, 2 * NA_KW - 2)
    onehot = (col_off[None] == jnp.arange(2 * NA_KW - 1)[:, None, None]).astype(f32)
    t = jnp.einsum('hrc,cqk->hrqk', rpb.astype(f32), onehot, precision=lax.Precision.HIGHEST)
    t = jnp.where(cmask, t * LOG2E, NEG_INF)
    t = jnp.concatenate([t[:, :-1], t[:, 1:]], axis=-1)
    return t.reshape(NA_HEADS * (2 * NA_KH - 2), GRID_W, 2 * GRID_W)


def _dil_kernel(*refs):
    ng = len(DIL_CONFIGS)
    in_refs = refs[:7 * ng]
    o_ref = refs[7 * ng]
    o_scr, m_scr, l_scr = refs[7 * ng + 1:]
    ti = pl.program_id(1)
    n_tiles = pl.num_programs(1)
    qb = DIL_QBLOCK
    span = 3 * qb

    first_head = lax.broadcasted_iota(jnp.int32, (qb, LANES), 1) < HEAD_DIM
    own_head = _pair_mask(qb)
    npair = DIL_HPG // 2
    qi = lax.broadcasted_iota(jnp.int32, (qb, span), 0)
    kj = lax.broadcasted_iota(jnp.int32, (qb, span), 1)
    band = jnp.where((kj >= qi) & (kj <= qi + 2 * qb), 0.0, NEG_INF).astype(f32)
    kcol = lax.broadcasted_iota(jnp.int32, (1, span), 1)
    pen_lo = jnp.where(kcol < qb, jnp.where(ti == 0, NEG_INF, 0.0), 0.0).astype(f32)
    pen_hi = jnp.where(kcol >= 2 * qb, jnp.where(ti == n_tiles - 1, NEG_INF, 0.0), 0.0).astype(f32)

    for g, (_, dil) in enumerate(DIL_CONFIGS):
        q_ref, kp, kc, kn, vp, vc, vn = in_refs[7 * g:7 * g + 7]
        seg = DIL_TILE // dil
        nj = seg // qb

        def window(refs3, r, j, cols, seg=seg, nj=nj):
            p_ref, c_ref, n_ref = refs3
            base = r * seg
            if 0 < j < nj - 1:
                return c_ref[0, base + (j - 1) * qb:base + (j + 2) * qb, cols]
            lo = (p_ref[0, base + seg - qb:base + seg, cols] if j == 0
                  else c_ref[0, base + (j - 1) * qb:base + j * qb, cols])
            mid = c_ref[0, base + j * qb:base + (j + 1) * qb, cols]
            hi = (n_ref[0, base:base + qb, cols] if j == nj - 1
                  else c_ref[0, base + (j + 1) * qb:base + (j + 2) * qb, cols])
            return jnp.concatenate([lo, mid, hi], axis=0)

        for t0 in range(0, dil * nj, DIL_UNROLL):
            blocks = [divmod(t, nj) for t in range(t0, t0 + DIL_UNROLL)]
            s_parts = []
            for k, (r, j) in enumerate(blocks):
                mask = band
                if j == 0:
                    mask = mask + pen_lo
                if j == nj - 1:
                    mask = mask + pen_hi
                for hp in range(npair):
                    cols = slice(hp * LANES, (hp + 1) * LANES)
                    q2 = _stack_pair(q_ref[0, (t0 + k) * qb:(t0 + k + 1) * qb, cols], own_head)
                    kk = window((kp, kc, kn), r, j, cols)
                    s2 = lax.dot_general(q2, kk, (((1,), (1,)), ((), ())), preferred_element_type=f32)
                    s_parts += [s2[:qb] + mask, s2[qb:] + mask]
            s = jnp.concatenate(s_parts, axis=0)
            m = jnp.max(s, axis=-1, keepdims=True)
            p = jnp.exp2(s - m)
            l = jnp.sum(p, axis=-1, keepdims=True)
            pb = p.astype(bf16)
            for k, (r, j) in enumerate(blocks):
                store_rows = pl.ds(j * qb * dil + r, qb, stride=dil) if dil > 1 else pl.ds(j * qb, qb)
                for hp in range(npair):
                    cols = slice(hp * LANES, (hp + 1) * LANES)
                    r0 = (k * npair + hp) * 2 * qb
                    vv = window((vp, vc, vn), r, j, cols)
                    pv = jnp.dot(pb[r0:r0 + 2 * qb], vv, preferred_element_type=f32)
                    slot = g * npair + hp
                    o_scr[slot, store_rows, :] = jnp.where(first_head, pv[:qb], pv[qb:])
                    m_scr[slot, store_rows, :] = jnp.where(first_head, m[r0:r0 + qb], m[r0 + qb:r0 + 2 * qb])
                    l_scr[slot, store_rows, :] = jnp.where(first_head, l[r0:r0 + qb], l[r0 + qb:r0 + 2 * qb])

    for hp in range(npair):
        m_all = m_scr[hp]
        for g in range(1, ng):
            m_all = jnp.maximum(m_all, m_scr[g * npair + hp])
        num = jnp.zeros_like(m_all)
        den = jnp.zeros_like(m_all)
        for g in range(ng):
            w = jnp.exp2(m_scr[g * npair + hp] - m_all)
            num = num + w * o_scr[g * npair + hp]
            den = den + w * l_scr[g * npair + hp]
        o_ref[0, :, hp * LANES:(hp + 1) * LANES] = (num / den).astype(bf16)


def _dil(qkv):
    bsz, s, _ = qkv.shape
    nt = s // DIL_TILE
    ng = len(DIL_CONFIGS)
    blk = (1, DIL_TILE, WB_OUT)
    in_specs = []
    for g in range(ng):
        in_specs.append(pl.BlockSpec(blk, lambda b, i, c=_J_QB + g: (b, i, c)))
        for base in (_J_KB, _J_VB):
            c = base + g
            in_specs += [pl.BlockSpec(blk, lambda b, i, c=c: (b, jnp.maximum(i - 1, 0), c)),
                         pl.BlockSpec(blk, lambda b, i, c=c: (b, i, c)),
                         pl.BlockSpec(blk, lambda b, i, c=c: (b, jnp.minimum(i + 1, nt - 1), c))]
    return pl.pallas_call(
        _dil_kernel,
        grid=(bsz, nt),
        in_specs=in_specs,
        out_specs=pl.BlockSpec(blk, lambda b, i: (b, i, 0)),
        out_shape=jax.ShapeDtypeStruct((bsz, s, WB_OUT), bf16),
        scratch_shapes=[pltpu.VMEM((ng * DIL_HPG // 2, DIL_TILE, LANES), f32)] * 3,
        compiler_params=_params(("parallel", "parallel")),
        name="dil",
    )(*([qkv] * (7 * ng)))


def _tail_kernel(x_ref, oa_ref, ob_ref, mod_ref, g1_ref, g2_ref, bg_ref, wpa_ref, wpb_ref,
                 wo_ref, win_ref, wout_ref, *rest):
    d = x_ref.shape[2]
    n_gate = 2 * d // COL_TILE
    wg_refs = rest[:n_gate]
    o_ref, h_scr, m_scr, act_scr = rest[n_gate:]
    x = x_ref[0]
    mod = mod_ref[0]
    sh1, sc1, gt1, sh2, sc2, gt2 = [mod[k:k + 1] for k in range(6)]
    h_scr[...] = _modulated_norm(x, g1_ref[...], sc1, sh1).astype(bf16)
    oa = oa_ref[0]
    ob = ob_ref[0]
    for n in range(d // COL_TILE):
        ca = slice(n * COL_TILE, (n + 1) * COL_TILE)
        cb = slice(d + n * COL_TILE, d + (n + 1) * COL_TILE)
        wga, wgb = wg_refs[n], wg_refs[n_gate // 2 + n]
        ga = jax.nn.sigmoid(jnp.dot(h_scr[...], wga[...], preferred_element_type=f32) + bg_ref[:, ca])
        gb = jax.nn.sigmoid(jnp.dot(h_scr[...], wgb[...], preferred_element_type=f32) + bg_ref[:, cb])
        pa = jnp.dot(oa, wpa_ref[:, ca], preferred_element_type=f32)
        pb = jnp.dot(ob, wpb_ref[:, ca], preferred_element_type=f32)
        m_scr[:, ca] = (ga * pa + gb * pb).astype(bf16)
    x1 = x + gt1 * jnp.dot(m_scr[...], wo_ref[...], preferred_element_type=f32)
    h_scr[...] = _modulated_norm(x1, g2_ref[...], sc2, sh2).astype(bf16)
    for f in range(D_FF // FF_CHUNK):
        ca = slice(f * FF_CHUNK, (f + 1) * FF_CHUNK)
        cu = slice(D_FF + f * FF_CHUNK, D_FF + (f + 1) * FF_CHUNK)
        a = jnp.dot(h_scr[...], win_ref[:, ca], preferred_element_type=f32)
        u = jnp.dot(h_scr[...], win_ref[:, cu], preferred_element_type=f32)
        act_scr[:, ca] = ((a * jax.nn.sigmoid(a)) * u).astype(bf16)
    o_ref[0] = x1 + gt2 * jnp.dot(act_scr[...], wout_ref[...], preferred_element_type=f32)


def _tail(x, o_a, o_b, mod6, g1, g2, w_in, bg, wpa, wpb, wo, w_ffn_in, w_ffn_out, tm=TAIL_TM):
    bsz, s, d = x.shape
    tok = lambda w: pl.BlockSpec((1, tm, w), lambda b, i: (b, i, 0))
    consts = [g1, g2, bg, wpa, wpb, wo, w_ffn_in, w_ffn_out]
    n_gate = 2 * d // COL_TILE
    gate_specs = [pl.BlockSpec((d, COL_TILE), lambda b, i, c=_N_J + n: (0, c), pipeline_mode=pl.Buffered(1))
                  for n in range(n_gate)]
    return pl.pallas_call(
        _tail_kernel,
        grid=(bsz, s // tm),
        in_specs=[tok(d), tok(WA), tok(WB_OUT), pl.BlockSpec((1, 6, d), lambda b, i: (b, 0, 0))]
        + [_const_spec(c.shape) for c in consts] + gate_specs,
        out_specs=tok(d),
        out_shape=jax.ShapeDtypeStruct((bsz, s, d), f32),
        scratch_shapes=[pltpu.VMEM((tm, d), bf16), pltpu.VMEM((tm, d), bf16), pltpu.VMEM((tm, D_FF), bf16)],
        compiler_params=_params(("parallel", "parallel")),
        name="tail",
    )(x, o_a, o_b, mod6, *consts, *([w_in] * n_gate))


def _rotary_tables(s):
    half = ROT_DIM // 2
    inv_freq = ROPE_THETA ** (-(jnp.arange(half, dtype=f32) * 2.0) / ROT_DIM)
    ang = jnp.arange(s).astype(f32)[:, None] * inv_freq[None, :]
    cos, sin = jnp.cos(ang), jnp.sin(ang)
    ones = jnp.ones((s, HEAD_DIM - ROT_DIM), f32)
    zeros = jnp.zeros((s, HEAD_DIM - ROT_DIM), f32)
    zh = jnp.zeros((s, half), f32)
    cos_h = jnp.concatenate([cos, cos, ones], axis=1)
    sina_h = jnp.concatenate([-sin, zh, zeros], axis=1)
    sinb_h = jnp.concatenate([zh, sin, zeros], axis=1)
    rep = LANES // HEAD_DIM
    return tuple(jnp.tile(t, (1, rep)) for t in (cos_h, sina_h, sinb_h))


def _layer(x, mod6, g_norm1, g_norm2, w_in, b_gate, g_qa, g_ka, g_qb, g_kb, rpb,
           w_proj_a, w_proj_b, w_o, w_ffn_in, w_ffn_out):
    bsz, s, d = x.shape
    scale = HEAD_DIM ** -0.5 * LOG2E
    rep = COL_TILE // HEAD_DIM
    gvec = jnp.stack([jnp.tile(g_qa * scale, rep), jnp.tile(g_ka, rep),
                      jnp.tile(g_qb * scale, rep), jnp.tile(g_kb, rep)]).astype(f32)
    hid = jnp.arange(COL_TILE) // HEAD_DIM
    ones_bd = (hid[:, None] == hid[None, :]).astype(bf16)
    rot_tables = _rotary_tables(s)
    g1 = g_norm1.reshape(1, d)
    g2 = g_norm2.reshape(1, d)

    w_in = w_in.astype(bf16)
    qkv, tail_ws = _qkv(x, mod6, g1, w_in, gvec, ones_bd, rot_tables,
                        [w_proj_a, w_proj_b, w_o, w_ffn_in, w_ffn_out])
    o_a = _na(qkv, _na_bias_table(rpb))
    o_b = _dil(qkv)

    return _tail(x, o_a, o_b, mod6, g1, g2, w_in, b_gate.reshape(1, 2 * d), *tail_ws)


def kernel(x, c, w_ada, b_ada, g_norm1, g_norm2, w_in, b_gate, g_qa, g_ka, g_qb, g_kb, rpb,
           w_proj_a, w_proj_b, w_o, w_ffn_in, w_ffn_out):
    depth = w_ada.shape[0]
    bsz, d = c.shape
    for l in range(depth):
        mod6 = _mod(c, w_ada[l], b_ada[l]).reshape(bsz, 6, d)
        x = _layer(x, mod6, g_norm1[l], g_norm2[l], w_in[l], b_gate[l], g_qa[l], g_ka[l], g_qb[l],
                   g_kb[l], rpb[l], w_proj_a[l], w_proj_b[l], w_o[l], w_ffn_in[l], w_ffn_out[l])
    return x
```

```python
import functools

import jax
import jax.numpy as jnp
from jax import lax
from jax.experimental import pallas as pl
from jax.experimental.pallas import tpu as pltpu

f32 = jnp.float32
bf16 = jnp.bfloat16

D_MODEL = 1024
HEAD_DIM = 64
GRID_W = 64
NA_HEADS = 8
NA_KH = 8
NA_KW = 16
DIL_CONFIGS = ((128, 1), (512, 4), (2048, 16))
DIL_HPG = 4
DIL_HEADS = DIL_HPG * len(DIL_CONFIGS)
DIL_QBLOCK = 64
ROT_DIM = HEAD_DIM // 4
ROPE_THETA = 500000.0
D_FF = -(-8 * D_MODEL // (3 * 256)) * 256
EPS = 1e-6
NEG_INF = -1e30
LOG2E = 1.4426950408889634
WA = NA_HEADS * HEAD_DIM
WB = DIL_HEADS * HEAD_DIM
WB_OUT = DIL_HPG * HEAD_DIM
W_QKV = 3 * WA + 3 * WB

LANES = 128
COL_TILE = 256
DIL_TILE = 1024
QKV_ROW_SPLIT = 2
QKV_DOT_TILES = 2
DIL_UNROLL = 16
NA_ROWS = 8
TAIL_TM = 512
FF_CHUNK = 256
VMEM_LIMIT = 56 * 1024 * 1024

assert all((win // 2) // dil == DIL_QBLOCK for win, dil in DIL_CONFIGS)


def _params(sem):
    return pltpu.CompilerParams(dimension_semantics=sem, vmem_limit_bytes=VMEM_LIMIT)


def _const_spec(shape):
    nd = len(shape)
    return pl.BlockSpec(shape, lambda *_: (0,) * nd, pipeline_mode=pl.Buffered(1))


def _mod_kernel(ct_ref, w_ref, b_ref, o_ref):
    ct = ct_ref[...]
    act = ct * jax.nn.sigmoid(ct)
    w = w_ref[...]
    rows = [jnp.sum(act[:, b:b + 1] * w, axis=0, keepdims=True) for b in range(ct.shape[1])]
    o_ref[...] = jnp.concatenate(rows, axis=0) + b_ref[...]


def _mod(c, w_ada, b_ada):
    bsz, d = c.shape
    n = w_ada.shape[1]
    tn = n // 4
    return pl.pallas_call(
        _mod_kernel,
        grid=(n // tn,),
        in_specs=[pl.BlockSpec((d, bsz), lambda j: (0, 0)),
                  pl.BlockSpec((d, tn), lambda j: (0, j)),
                  pl.BlockSpec((1, tn), lambda j: (0, j))],
        out_specs=pl.BlockSpec((bsz, tn), lambda j: (0, j)),
        out_shape=jax.ShapeDtypeStruct((bsz, n), f32),
        compiler_params=_params(("arbitrary",)),
        name="mod",
    )(c.T, w_ada, b_ada.reshape(1, n))


def _modulated_norm(x, g, sc, sh):
    ms = jnp.mean(x * x, axis=-1, keepdims=True)
    y = x * lax.rsqrt(ms + EPS) * g
    return y * (1.0 + sc) + sh


_J_QB = 3 * WA // COL_TILE
_J_KB = _J_QB + WB // COL_TILE
_J_VB = _J_KB + WB // COL_TILE
_N_J = W_QKV // COL_TILE


def _qkv_tile_kind(j):
    if j < WA // COL_TILE:
        return 0, False, 1
    if j < 2 * WA // COL_TILE:
        return 1, False, 1
    if j < _J_QB:
        return None, False, 1
    dil = DIL_CONFIGS[(j - _J_QB) % len(DIL_CONFIGS)][1]
    if j < _J_KB:
        return 2, True, dil
    if j < _J_VB:
        return 3, True, dil
    return None, False, dil


_QKV_OUT_WIDTHS = (WA,) * 3 + (WB_OUT,) * (3 * len(DIL_CONFIGS))
_OUT_QA, _OUT_KA, _OUT_VA, _OUT_QB, _OUT_KB, _OUT_VB = 0, 1, 2, 3, 3 + len(DIL_CONFIGS), 3 + 2 * len(DIL_CONFIGS)


def _qkv_tile_out(j, o_refs):
    if j < _J_QB:
        per = WA // COL_TILE
        return o_refs[j // per], (j % per) * COL_TILE
    return o_refs[_OUT_QB + j - _J_QB], 0


def _qkv_kernel(n_cast,x_ref, mod_ref, g1_ref, w_ref, gvec_ref, ones_ref, cos_ref, sina_ref, sinb_ref,
                *rest):
    n_out = len(_QKV_OUT_WIDTHS)
    cast_in, o_refs = rest[:n_cast], rest[n_cast:n_cast + n_out]
    cast_out = rest[n_cast + n_out:2 * n_cast + n_out]
    h_scr, y_scr = rest[2 * n_cast + n_out:]
    for src, dst in zip(cast_in, cast_out):
        dst[...] = src[...].astype(bf16)
    tm = x_ref.shape[1]
    nc = COL_TILE // LANES
    mod = mod_ref[0]
    h_scr[...] = _modulated_norm(x_ref[0], g1_ref[...], mod[1:2], mod[0:1]).astype(bf16)

    wide = {}
    for j in range(_N_J):
        gain_row, rotary, dil = _qkv_tile_kind(j)
        o_ref, c0 = _qkv_tile_out(j, o_refs)
        slot = j % y_scr.shape[0]
        for hb in range(QKV_ROW_SPLIT):
            rows = slice(hb * tm // QKV_ROW_SPLIT, (hb + 1) * tm // QKV_ROW_SPLIT)
            if j % QKV_DOT_TILES == 0:
                wcols = slice(j * COL_TILE, min(j + QKV_DOT_TILES, _N_J) * COL_TILE)
                wide[hb] = jnp.dot(h_scr[rows, :], w_ref[:, wcols], preferred_element_type=f32)
            sub = j % QKV_DOT_TILES
            y = wide[hb][:, sub * COL_TILE:(sub + 1) * COL_TILE]
            if gain_row is not None:
                ss = jnp.dot((y * y).astype(bf16), ones_ref[...], preferred_element_type=f32)
                y = y * lax.rsqrt(ss * (1.0 / HEAD_DIM) + EPS) * gvec_ref[gain_row:gain_row + 1, :]
            if rotary:
                cos, sina, sinb = [jnp.concatenate([t_ref[rows, :]] * nc, axis=1)
                                   for t_ref in (cos_ref, sina_ref, sinb_ref)]
                half = ROT_DIM // 2
                up = pltpu.roll(y, COL_TILE - half, 1)
                dn = pltpu.roll(y, half, 1)
                y = y * cos + up * sina + dn * sinb
            if dil == 1:
                o_ref[0, rows, c0:c0 + COL_TILE] = y.astype(bf16)
            else:
                for c in range(nc):
                    y_scr[slot, c, rows, :] = y[:, c * LANES:(c + 1) * LANES]
        if dil == 1:
            continue
        seg = DIL_TILE // dil
        for t in range(tm // DIL_TILE):
            for r in range(dil):
                for c in range(nc):
                    rows = y_scr[slot, c, pl.ds(t * DIL_TILE + r, seg, stride=dil), :]
                    o_ref[0, t * DIL_TILE + r * seg:t * DIL_TILE + (r + 1) * seg,
                          c0 + c * LANES:c0 + (c + 1) * LANES] = rows.astype(bf16)


def _qkv(x, mod6, g1, w_qkv, gvec, ones_bd, rot_tables, cast_ws, tm=DIL_TILE):
    bsz, s, d = x.shape
    nt = s // tm
    steps = bsz * nt
    slab_specs = [pl.BlockSpec((w.shape[0] // steps, w.shape[1]), lambda b, i: (b * nt + i, 0)) for w in cast_ws]
    outs = pl.pallas_call(
        functools.partial(_qkv_kernel, len(cast_ws)),
        grid=(bsz, nt),
        in_specs=[pl.BlockSpec((1, tm, d), lambda b, i: (b, i, 0)),
                  pl.BlockSpec((1, 6, d), lambda b, i: (b, 0, 0)),
                  _const_spec(g1.shape),
                  pl.BlockSpec((d, W_QKV), lambda b, i: (0, 0), pipeline_mode=pl.Buffered(1)),
                  _const_spec(gvec.shape),
                  _const_spec(ones_bd.shape),
                  ] + [pl.BlockSpec((tm, LANES), lambda b, i: (i, 0))] * len(rot_tables) + slab_specs,
        out_specs=[pl.BlockSpec((1, tm, w), lambda b, i: (b, i, 0)) for w in _QKV_OUT_WIDTHS] + slab_specs,
        out_shape=[jax.ShapeDtypeStruct((bsz, s, w), bf16) for w in _QKV_OUT_WIDTHS]
        + [jax.ShapeDtypeStruct(w.shape, bf16) for w in cast_ws],
        scratch_shapes=[pltpu.VMEM((tm, d), bf16),
                        pltpu.VMEM((2, COL_TILE // LANES, tm, LANES), f32)],
        compiler_params=_params(("parallel", "parallel")),
        name="qkv",
    )(x, mod6, g1, w_qkv, gvec, ones_bd, *rot_tables, *cast_ws)
    n_out = len(_QKV_OUT_WIDTHS)
    return outs[:n_out], outs[n_out:]


def _pair_mask(nq):
    row = lax.broadcasted_iota(jnp.int32, (2 * nq, LANES), 0)
    lane = lax.broadcasted_iota(jnp.int32, (2 * nq, LANES), 1)
    return (row < nq) == (lane < HEAD_DIM)


def _stack_pair(qp, own_head):
    q2 = jnp.concatenate([qp, qp], axis=0)
    return jnp.where(own_head, q2, jnp.zeros_like(q2))


def _na_kernel(q_ref, kp_ref, kc_ref, kn_ref, vp_ref, vc_ref, vn_ref, tbl_ref, o_ref, kwin, vwin):
    rb = pl.program_id(1)
    blk = NA_ROWS * GRID_W
    n_rows = pl.num_programs(1) * NA_ROWS
    for t, (kr, vr) in enumerate(((kp_ref, vp_ref), (kc_ref, vc_ref), (kn_ref, vn_ref))):
        kwin[t * blk:(t + 1) * blk, :] = kr[0]
        vwin[t * blk:(t + 1) * blk, :] = vr[0]
    first_head = lax.broadcasted_iota(jnp.int32, (GRID_W, LANES), 1) < HEAD_DIM
    own_head = _pair_mask(GRID_W)
    nkeys = NA_KH * GRID_W
    npair = NA_HEADS // 2
    pair_rows = 2 * GRID_W

    def row_body(a, carry):
        r = rb * NA_ROWS + a
        row_start = jnp.clip(r - NA_KH // 2, 0, n_rows - NA_KH)
        delta = r - row_start
        off = pl.multiple_of((row_start - (rb - 1) * NA_ROWS) * GRID_W, GRID_W)
        qoff = pl.multiple_of(a * GRID_W, GRID_W)
        s_parts = []
        for hp in range(npair):
            cols = slice(hp * LANES, (hp + 1) * LANES)
            q2 = _stack_pair(q_ref[0, pl.ds(qoff, GRID_W), cols], own_head)
            kk = kwin[pl.ds(off, nkeys), cols]
            s_parts.append(lax.dot_general(q2, kk, (((1,), (1,)), ((), ())), preferred_element_type=f32))
        n_off = 2 * NA_KH - 2
        bias = jnp.concatenate(
            [jnp.concatenate([tbl_ref[h * n_off + 2 * kp - delta + NA_KH - 1] for kp in range(NA_KH // 2)], axis=1)
             for h in range(NA_HEADS)], axis=0)
        s = jnp.concatenate(s_parts, axis=0) + bias
        m = jnp.max(s, axis=-1, keepdims=True)
        p = jnp.exp2(s - m)
        inv = 1.0 / jnp.sum(p, axis=-1, keepdims=True)
        pb = p.astype(bf16)
        for hp in range(npair):
            cols = slice(hp * LANES, (hp + 1) * LANES)
            rows = slice(hp * pair_rows, (hp + 1) * pair_rows)
            vv = vwin[pl.ds(off, nkeys), cols]
            pv = jnp.dot(pb[rows], vv, preferred_element_type=f32) * inv[rows]
            o_ref[0, pl.ds(qoff, GRID_W), cols] = jnp.where(first_head, pv[:GRID_W], pv[GRID_W:]).astype(bf16)
        return carry

    lax.fori_loop(0, NA_ROWS, row_body, 0, unroll=NA_ROWS)


def _na(q, k, v, tbl):
    bsz, s, _ = q.shape
    blk = NA_ROWS * GRID_W
    nb = s // blk
    qspec = pl.BlockSpec((1, blk, WA), lambda b, i: (b, i, 0))
    halo = [pl.BlockSpec((1, blk, WA), lambda b, i: (b, jnp.maximum(i - 1, 0), 0)),
            pl.BlockSpec((1, blk, WA), lambda b, i: (b, i, 0)),
            pl.BlockSpec((1, blk, WA), lambda b, i: (b, jnp.minimum(i + 1, nb - 1), 0))]

    return pl.pallas_call(
        _na_kernel,
        grid=(bsz, nb),
        in_specs=[qspec] + halo + halo + [_const_spec(tbl.shape)],
        out_specs=pl.BlockSpec((1, blk, WA), lambda b, i: (b, i, 0)),
        out_shape=jax.ShapeDtypeStruct((bsz, s, WA), bf16),
        scratch_shapes=[pltpu.VMEM((3 * blk, WA), bf16), pltpu.VMEM((3 * blk, WA), bf16)],
        compiler_params=_params(("parallel", "parallel")),
        name="na",
    )(q, k, k, k, v, v, v, tbl)


def _na_bias_table(rpb):
    col = jnp.arange(GRID_W)
    col_start = jnp.clip(col - NA_KW // 2, 0, GRID_W - NA_KW)
    cmask = (col[None, :] >= col_start[:, None]) & (col[None, :] < col_start[:, None] + NA_KW)
    col_off = jnp.clip(col[None, :] - col[:, None] + (NA_KW - 1), 0, 2 * NA_KW - 2)
    onehot = (col_off[None] == jnp.arange(2 * NA_KW - 1)[:, None, None]).astype(f32)
    t = jnp.einsum('hrc,cqk->hrqk', rpb.astype(f32), onehot, precision=lax.Precision.HIGHEST)
    t = jnp.where(cmask, t * LOG2E, NEG_INF)
    t = jnp.concatenate([t[:, :-1], t[:, 1:]], axis=-1)
    return t.reshape(NA_HEADS * (2 * NA_KH - 2), GRID_W, 2 * GRID_W)


def _dil_halo_pieces(dil):
    return 1 if DIL_TILE // dil == DIL_QBLOCK else dil


def _dil_kernel(*refs):
    ng = len(DIL_CONFIGS)
    n_in = sum(1 + 2 * (1 + 2 * _dil_halo_pieces(dil)) for _, dil in DIL_CONFIGS)
    in_refs = list(refs[:n_in])
    o_ref = refs[n_in]
    o_scr, m_scr, l_scr = refs[n_in + 1:]
    ti = pl.program_id(1)
    n_tiles = pl.num_programs(1)
    qb = DIL_QBLOCK
    span = 3 * qb

    first_head = lax.broadcasted_iota(jnp.int32, (qb, LANES), 1) < HEAD_DIM
    own_head = _pair_mask(qb)
    npair = DIL_HPG // 2
    qi = lax.broadcasted_iota(jnp.int32, (qb, span), 0)
    kj = lax.broadcasted_iota(jnp.int32, (qb, span), 1)
    band = jnp.where((kj >= qi) & (kj <= qi + 2 * qb), 0.0, NEG_INF).astype(f32)
    kcol = lax.broadcasted_iota(jnp.int32, (1, span), 1)
    pen_lo = jnp.where(kcol < qb, jnp.where(ti == 0, NEG_INF, 0.0), 0.0).astype(f32)
    pen_hi = jnp.where(kcol >= 2 * qb, jnp.where(ti == n_tiles - 1, NEG_INF, 0.0), 0.0).astype(f32)

    for g, (_, dil) in enumerate(DIL_CONFIGS):
        seg = DIL_TILE // dil
        nj = seg // qb
        nh = _dil_halo_pieces(dil)
        q_ref = in_refs.pop(0)
        kc, kp, kn = in_refs.pop(0), [in_refs.pop(0) for _ in range(nh)], [in_refs.pop(0) for _ in range(nh)]
        vc, vp, vn = in_refs.pop(0), [in_refs.pop(0) for _ in range(nh)], [in_refs.pop(0) for _ in range(nh)]

        def halo(pieces, r, cols, nh=nh):
            return pieces[r][0, :, cols] if nh > 1 else pieces[0][0, r * qb:(r + 1) * qb, cols]

        def window(refs3, r, j, cols, seg=seg, nj=nj, halo=halo):
            p_refs, c_ref, n_refs = refs3
            base = r * seg
            if 0 < j < nj - 1:
                return c_ref[0, base + (j - 1) * qb:base + (j + 2) * qb, cols]
            lo = halo(p_refs, r, cols) if j == 0 else c_ref[0, base + (j - 1) * qb:base + j * qb, cols]
            mid = c_ref[0, base + j * qb:base + (j + 1) * qb, cols]
            hi = halo(n_refs, r, cols) if j == nj - 1 else c_ref[0, base + (j + 1) * qb:base + (j + 2) * qb, cols]
            return jnp.concatenate([lo, mid, hi], axis=0)

        for t0 in range(0, dil * nj, DIL_UNROLL):
            blocks = [divmod(t, nj) for t in range(t0, t0 + DIL_UNROLL)]
            s_parts = []
            for k, (r, j) in enumerate(blocks):
                mask = band
                if j == 0:
                    mask = mask + pen_lo
                if j == nj - 1:
                    mask = mask + pen_hi
                for hp in range(npair):
                    cols = slice(hp * LANES, (hp + 1) * LANES)
                    q2 = _stack_pair(q_ref[0, (t0 + k) * qb:(t0 + k + 1) * qb, cols], own_head)
                    kk = window((kp, kc, kn), r, j, cols)
                    s2 = lax.dot_general(q2, kk, (((1,), (1,)), ((), ())), preferred_element_type=f32)
                    s_parts += [s2[:qb] + mask, s2[qb:] + mask]
            s = jnp.concatenate(s_parts, axis=0)
            m = jnp.max(s, axis=-1, keepdims=True)
            p = jnp.exp2(s - m)
            l = jnp.sum(p, axis=-1, keepdims=True)
            pb = p.astype(bf16)
            for k, (r, j) in enumerate(blocks):
                store_rows = pl.ds(j * qb * dil + r, qb, stride=dil) if dil > 1 else pl.ds(j * qb, qb)
                for hp in range(npair):
                    cols = slice(hp * LANES, (hp + 1) * LANES)
                    r0 = (k * npair + hp) * 2 * qb
                    vv = window((vp, vc, vn), r, j, cols)
                    pv = jnp.dot(pb[r0:r0 + 2 * qb], vv, preferred_element_type=f32)
                    slot = g * npair + hp
                    o_scr[slot, store_rows, :] = jnp.where(first_head, pv[:qb], pv[qb:])
                    m_scr[slot, store_rows, :] = jnp.where(first_head, m[r0:r0 + qb], m[r0 + qb:r0 + 2 * qb])
                    l_scr[slot, store_rows, :] = jnp.where(first_head, l[r0:r0 + qb], l[r0 + qb:r0 + 2 * qb])

    for hp in range(npair):
        m_all = m_scr[hp]
        for g in range(1, ng):
            m_all = jnp.maximum(m_all, m_scr[g * npair + hp])
        num = jnp.zeros_like(m_all)
        den = jnp.zeros_like(m_all)
        for g in range(ng):
            w = jnp.exp2(m_scr[g * npair + hp] - m_all)
            num = num + w * o_scr[g * npair + hp]
            den = den + w * l_scr[g * npair + hp]
        o_ref[0, :, hp * LANES:(hp + 1) * LANES] = (num / den).astype(bf16)


def _dil(qs, ks, vs):
    bsz, s, _ = qs[0].shape
    nt = s // DIL_TILE
    ng = len(DIL_CONFIGS)
    qb = DIL_QBLOCK
    per_tile = DIL_TILE // qb
    blk = (1, DIL_TILE, WB_OUT)
    cur = pl.BlockSpec(blk, lambda b, i: (b, i, 0))
    in_specs, args = [], []
    for g, (_, dil) in enumerate(DIL_CONFIGS):
        seg = DIL_TILE // dil
        if _dil_halo_pieces(dil) == 1 and seg == qb:
            prev = [pl.BlockSpec(blk, lambda b, i: (b, jnp.maximum(i - 1, 0), 0))]
            nxt = [pl.BlockSpec(blk, lambda b, i: (b, jnp.minimum(i + 1, nt - 1), 0))]
        else:
            prev = [pl.BlockSpec((1, qb, WB_OUT), lambda b, i, o=(r + 1) * seg // qb - 1:
                                 (b, jnp.maximum((i - 1) * per_tile + o, 0), 0)) for r in range(dil)]
            nxt = [pl.BlockSpec((1, qb, WB_OUT), lambda b, i, o=r * seg // qb:
                                (b, jnp.minimum((i + 1) * per_tile + o, s // qb - 1), 0)) for r in range(dil)]
        in_specs += [cur] + 2 * ([cur] + prev + nxt)
        args += [qs[g]] + [ks[g]] * (1 + len(prev) + len(nxt)) + [vs[g]] * (1 + len(prev) + len(nxt))
    return pl.pallas_call(
        _dil_kernel,
        grid=(bsz, nt),
        in_specs=in_specs,
        out_specs=pl.BlockSpec(blk, lambda b, i: (b, i, 0)),
        out_shape=jax.ShapeDtypeStruct((bsz, s, WB_OUT), bf16),
        scratch_shapes=[pltpu.VMEM((ng * DIL_HPG // 2, DIL_TILE, LANES), f32)] * 3,
        compiler_params=_params(("parallel", "parallel")),
        name="dil",
    )(*args)


def _tail_kernel(x_ref, oa_ref, ob_ref, mod_ref, g1_ref, g2_ref, bg_ref, wpa_ref, wpb_ref,
                 wo_ref, win_ref, wout_ref, *rest):
    d = x_ref.shape[2]
    n_gate = 2 * d // COL_TILE
    wg_refs = rest[:n_gate]
    o_ref, h_scr, m_scr, act_scr = rest[n_gate:]
    x = x_ref[0]
    mod = mod_ref[0]
    sh1, sc1, gt1, sh2, sc2, gt2 = [mod[k:k + 1] for k in range(6)]
    h_scr[...] = _modulated_norm(x, g1_ref[...], sc1, sh1).astype(bf16)
    oa = oa_ref[0]
    ob = ob_ref[0]
    for n in range(d // COL_TILE):
        ca = slice(n * COL_TILE, (n + 1) * COL_TILE)
        cb = slice(d + n * COL_TILE, d + (n + 1) * COL_TILE)
        wga, wgb = wg_refs[n], wg_refs[n_gate // 2 + n]
        ga = jax.nn.sigmoid(jnp.dot(h_scr[...], wga[...], preferred_element_type=f32) + bg_ref[:, ca])
        gb = jax.nn.sigmoid(jnp.dot(h_scr[...], wgb[...], preferred_element_type=f32) + bg_ref[:, cb])
        pa = jnp.dot(oa, wpa_ref[:, ca], preferred_element_type=f32)
        pb = jnp.dot(ob, wpb_ref[:, ca], preferred_element_type=f32)
        m_scr[:, ca] = (ga * pa + gb * pb).astype(bf16)
    x1 = x + gt1 * jnp.dot(m_scr[...], wo_ref[...], preferred_element_type=f32)
    h_scr[...] = _modulated_norm(x1, g2_ref[...], sc2, sh2).astype(bf16)
    for f in range(D_FF // FF_CHUNK):
        ca = slice(f * FF_CHUNK, (f + 1) * FF_CHUNK)
        cu = slice(D_FF + f * FF_CHUNK, D_FF + (f + 1) * FF_CHUNK)
        a = jnp.dot(h_scr[...], win_ref[:, ca], preferred_element_type=f32)
        u = jnp.dot(h_scr[...], win_ref[:, cu], preferred_element_type=f32)
        act_scr[:, ca] = ((a * jax.nn.sigmoid(a)) * u).astype(bf16)
    o_ref[0] = x1 + gt2 * jnp.dot(act_scr[...], wout_ref[...], preferred_element_type=f32)


def _tail(x, o_a, o_b, mod6, g1, g2, w_in, bg, wpa, wpb, wo, w_ffn_in, w_ffn_out, tm=TAIL_TM):
    bsz, s, d = x.shape
    tok = lambda w: pl.BlockSpec((1, tm, w), lambda b, i: (b, i, 0))
    consts = [g1, g2, bg, wpa, wpb, wo, w_ffn_in, w_ffn_out]
    n_gate = 2 * d // COL_TILE
    gate_specs = [pl.BlockSpec((d, COL_TILE), lambda b, i, c=_N_J + n: (0, c), pipeline_mode=pl.Buffered(1))
                  for n in range(n_gate)]
    return pl.pallas_call(
        _tail_kernel,
        grid=(bsz, s // tm),
        in_specs=[tok(d), tok(WA), tok(WB_OUT), pl.BlockSpec((1, 6, d), lambda b, i: (b, 0, 0))]
        + [_const_spec(c.shape) for c in consts] + gate_specs,
        out_specs=tok(d),
        out_shape=jax.ShapeDtypeStruct((bsz, s, d), f32),
        scratch_shapes=[pltpu.VMEM((tm, d), bf16), pltpu.VMEM((tm, d), bf16), pltpu.VMEM((tm, D_FF), bf16)],
        compiler_params=_params(("parallel", "parallel")),
        name="tail",
    )(x, o_a, o_b, mod6, *consts, *([w_in] * n_gate))


def _rotary_tables(s):
    half = ROT_DIM // 2
    inv_freq = ROPE_THETA ** (-(jnp.arange(half, dtype=f32) * 2.0) / ROT_DIM)
    ang = jnp.arange(s).astype(f32)[:, None] * inv_freq[None, :]
    cos, sin = jnp.cos(ang), jnp.sin(ang)
    ones = jnp.ones((s, HEAD_DIM - ROT_DIM), f32)
    zeros = jnp.zeros((s, HEAD_DIM - ROT_DIM), f32)
    zh = jnp.zeros((s, half), f32)
    cos_h = jnp.concatenate([cos, cos, ones], axis=1)
    sina_h = jnp.concatenate([-sin, zh, zeros], axis=1)
    sinb_h = jnp.concatenate([zh, sin, zeros], axis=1)
    rep = LANES // HEAD_DIM
    return tuple(jnp.tile(t, (1, rep)) for t in (cos_h, sina_h, sinb_h))


def _layer(x, mod6, g_norm1, g_norm2, w_in, b_gate, g_qa, g_ka, g_qb, g_kb, rpb,
           w_proj_a, w_proj_b, w_o, w_ffn_in, w_ffn_out):
    bsz, s, d = x.shape
    scale = HEAD_DIM ** -0.5 * LOG2E
    rep = COL_TILE // HEAD_DIM
    gvec = jnp.stack([jnp.tile(g_qa * scale, rep), jnp.tile(g_ka, rep),
                      jnp.tile(g_qb * scale, rep), jnp.tile(g_kb, rep)]).astype(f32)
    hid = jnp.arange(COL_TILE) // HEAD_DIM
    ones_bd = (hid[:, None] == hid[None, :]).astype(bf16)
    rot_tables = _rotary_tables(s)
    g1 = g_norm1.reshape(1, d)
    g2 = g_norm2.reshape(1, d)

    w_in = w_in.astype(bf16)
    qkv, tail_ws = _qkv(x, mod6, g1, w_in, gvec, ones_bd, rot_tables,
                        [w_proj_a, w_proj_b, w_o, w_ffn_in, w_ffn_out])
    ng = len(DIL_CONFIGS)
    o_a = _na(qkv[_OUT_QA], qkv[_OUT_KA], qkv[_OUT_VA], _na_bias_table(rpb))
    o_b = _dil(qkv[_OUT_QB:_OUT_QB + ng], qkv[_OUT_KB:_OUT_KB + ng], qkv[_OUT_VB:_OUT_VB + ng])

    return _tail(x, o_a, o_b, mod6, g1, g2, w_in, b_gate.reshape(1, 2 * d), *tail_ws)


def kernel(x, c, w_ada, b_ada, g_norm1, g_norm2, w_in, b_gate, g_qa, g_ka, g_qb, g_kb, rpb,
           w_proj_a, w_proj_b, w_o, w_ffn_in, w_ffn_out):
    depth = w_ada.shape[0]
    bsz, d = c.shape
    for l in range(depth):
        mod6 = _mod(c, w_ada[l], b_ada[l]).reshape(bsz, 6, d)
        x = _layer(x, mod6, g_norm1[l], g_norm2[l], w_in[l], b_gate[l], g_qa[l], g_ka[l], g_qb[l],
                   g_kb[l], rpb[l], w_proj_a[l], w_proj_b[l], w_o[l], w_ffn_in[l], w_ffn_out[l])
    return x
```

```python
import functools

import jax
import jax.numpy as jnp
from jax import lax
from jax.experimental import pallas as pl
from jax.experimental.pallas import tpu as pltpu

f32 = jnp.float32
bf16 = jnp.bfloat16

D_MODEL = 1024
HEAD_DIM = 64
GRID_W = 64
NA_HEADS = 8
NA_KH = 8
NA_KW = 16
DIL_CONFIGS = ((128, 1), (512, 4), (2048, 16))
DIL_HPG = 4
DIL_HEADS = DIL_HPG * len(DIL_CONFIGS)
DIL_QBLOCK = 64
ROT_DIM = HEAD_DIM // 4
ROPE_THETA = 500000.0
D_FF = -(-8 * D_MODEL // (3 * 256)) * 256
EPS = 1e-6
NEG_INF = -1e30
LOG2E = 1.4426950408889634
WA = NA_HEADS * HEAD_DIM
WB = DIL_HEADS * HEAD_DIM
WB_OUT = DIL_HPG * HEAD_DIM
W_QKV = 3 * WA + 3 * WB

LANES = 128
COL_TILE = 256
DIL_TILE = 1024
QKV_ROW_SPLIT = 2
QKV_DOT_TILES = 2
DIL_UNROLL = 16
NA_ROWS = 16
TAIL_TM = 512
FF_CHUNK = 256
VMEM_LIMIT = 56 * 1024 * 1024

assert all((win // 2) // dil == DIL_QBLOCK for win, dil in DIL_CONFIGS)


def _params(sem):
    return pltpu.CompilerParams(dimension_semantics=sem, vmem_limit_bytes=VMEM_LIMIT)


def _const_spec(shape):
    nd = len(shape)
    return pl.BlockSpec(shape, lambda *_: (0,) * nd, pipeline_mode=pl.Buffered(1))


def _mod_kernel(ct_ref, w_ref, b_ref, win_ref, o_ref, win_o_ref):
    ct = ct_ref[...]
    act = ct * jax.nn.sigmoid(ct)
    w = w_ref[...]
    rows = [jnp.sum(act[:, b:b + 1] * w, axis=0, keepdims=True) for b in range(ct.shape[1])]
    o_ref[...] = jnp.concatenate(rows, axis=0) + b_ref[...]
    win_o_ref[...] = win_ref[...].astype(bf16)


def _mod(c, w_ada, b_ada, w_in):
    bsz, d = c.shape
    n = w_ada.shape[1]
    steps = 4
    tn = n // steps
    slab = pl.BlockSpec((w_in.shape[0] // steps, w_in.shape[1]), lambda j: (j, 0))
    return pl.pallas_call(
        _mod_kernel,
        grid=(steps,),
        in_specs=[pl.BlockSpec((d, bsz), lambda j: (0, 0)),
                  pl.BlockSpec((d, tn), lambda j: (0, j)),
                  pl.BlockSpec((1, tn), lambda j: (0, j)),
                  slab],
        out_specs=[pl.BlockSpec((bsz, tn), lambda j: (0, j)), slab],
        out_shape=[jax.ShapeDtypeStruct((bsz, n), f32), jax.ShapeDtypeStruct(w_in.shape, bf16)],
        compiler_params=_params(("arbitrary",)),
        name="mod",
    )(c.T, w_ada, b_ada.reshape(1, n), w_in)


def _modulated_norm(x, g, sc, sh):
    ms = jnp.mean(x * x, axis=-1, keepdims=True)
    y = x * lax.rsqrt(ms + EPS) * g
    return y * (1.0 + sc) + sh


_J_QB = 3 * WA // COL_TILE
_J_KB = _J_QB + WB // COL_TILE
_J_VB = _J_KB + WB // COL_TILE
_N_J = W_QKV // COL_TILE


def _qkv_tile_kind(j):
    if j < WA // COL_TILE:
        return 0, False, 1
    if j < 2 * WA // COL_TILE:
        return 1, False, 1
    if j < _J_QB:
        return None, False, 1
    dil = DIL_CONFIGS[(j - _J_QB) % len(DIL_CONFIGS)][1]
    if j < _J_KB:
        return 2, True, dil
    if j < _J_VB:
        return 3, True, dil
    return None, False, dil


_QKV_OUT_WIDTHS = (WA,) * 3 + (WB_OUT,) * (3 * len(DIL_CONFIGS))
_OUT_QA, _OUT_KA, _OUT_VA, _OUT_QB, _OUT_KB, _OUT_VB = 0, 1, 2, 3, 3 + len(DIL_CONFIGS), 3 + 2 * len(DIL_CONFIGS)


def _qkv_tile_out(j, o_refs):
    if j < _J_QB:
        per = WA // COL_TILE
        return o_refs[j // per], (j % per) * COL_TILE
    return o_refs[_OUT_QB + j - _J_QB], 0


def _qkv_kernel(n_cast,x_ref, mod_ref, g1_ref, w_ref, gvec_ref, ones_ref, cos_ref, sina_ref, sinb_ref,
                *rest):
    n_out = len(_QKV_OUT_WIDTHS)
    cast_in, o_refs = rest[:n_cast], rest[n_cast:n_cast + n_out]
    cast_out = rest[n_cast + n_out:2 * n_cast + n_out]
    h_scr, y_scr = rest[2 * n_cast + n_out:]
    for src, dst in zip(cast_in, cast_out):
        dst[...] = src[...].astype(bf16)
    tm = x_ref.shape[1]
    nc = COL_TILE // LANES
    mod = mod_ref[0]
    h_scr[...] = _modulated_norm(x_ref[0], g1_ref[...], mod[1:2], mod[0:1]).astype(bf16)

    wide = {}
    for j in range(_N_J):
        gain_row, rotary, dil = _qkv_tile_kind(j)
        o_ref, c0 = _qkv_tile_out(j, o_refs)
        slot = j % y_scr.shape[0]
        for hb in range(QKV_ROW_SPLIT):
            rows = slice(hb * tm // QKV_ROW_SPLIT, (hb + 1) * tm // QKV_ROW_SPLIT)
            if j % QKV_DOT_TILES == 0:
                wcols = slice(j * COL_TILE, min(j + QKV_DOT_TILES, _N_J) * COL_TILE)
                wide[hb] = jnp.dot(h_scr[rows, :], w_ref[:, wcols], preferred_element_type=f32)
            sub = j % QKV_DOT_TILES
            y = wide[hb][:, sub * COL_TILE:(sub + 1) * COL_TILE]
            if gain_row is not None:
                ss = jnp.dot((y * y).astype(bf16), ones_ref[...], preferred_element_type=f32)
                y = y * lax.rsqrt(ss * (1.0 / HEAD_DIM) + EPS) * gvec_ref[gain_row:gain_row + 1, :]
            if rotary:
                cos, sina, sinb = [jnp.concatenate([t_ref[rows, :]] * nc, axis=1)
                                   for t_ref in (cos_ref, sina_ref, sinb_ref)]
                half = ROT_DIM // 2
                up = pltpu.roll(y, COL_TILE - half, 1)
                dn = pltpu.roll(y, half, 1)
                y = y * cos + up * sina + dn * sinb
            if dil == 1:
                o_ref[0, rows, c0:c0 + COL_TILE] = y.astype(bf16)
            else:
                for c in range(nc):
                    y_scr[slot, c, rows, :] = y[:, c * LANES:(c + 1) * LANES]
        if dil == 1:
            continue
        seg = DIL_TILE // dil
        for t in range(tm // DIL_TILE):
            for r in range(dil):
                for c in range(nc):
                    rows = y_scr[slot, c, pl.ds(t * DIL_TILE + r, seg, stride=dil), :]
                    o_ref[0, t * DIL_TILE + r * seg:t * DIL_TILE + (r + 1) * seg,
                          c0 + c * LANES:c0 + (c + 1) * LANES] = rows.astype(bf16)


def _qkv(x, mod6, g1, w_qkv, gvec, ones_bd, rot_tables, cast_ws, tm=DIL_TILE):
    bsz, s, d = x.shape
    nt = s // tm
    steps = bsz * nt
    slab_specs = [pl.BlockSpec((w.shape[0] // steps, w.shape[1]), lambda b, i: (b * nt + i, 0)) for w in cast_ws]
    outs = pl.pallas_call(
        functools.partial(_qkv_kernel, len(cast_ws)),
        grid=(bsz, nt),
        in_specs=[pl.BlockSpec((1, tm, d), lambda b, i: (b, i, 0)),
                  pl.BlockSpec((1, 6, d), lambda b, i: (b, 0, 0)),
                  _const_spec(g1.shape),
                  pl.BlockSpec((d, W_QKV), lambda b, i: (0, 0), pipeline_mode=pl.Buffered(1)),
                  _const_spec(gvec.shape),
                  _const_spec(ones_bd.shape),
                  ] + [pl.BlockSpec((tm, LANES), lambda b, i: (i, 0))] * len(rot_tables) + slab_specs,
        out_specs=[pl.BlockSpec((1, tm, w), lambda b, i: (b, i, 0)) for w in _QKV_OUT_WIDTHS] + slab_specs,
        out_shape=[jax.ShapeDtypeStruct((bsz, s, w), bf16) for w in _QKV_OUT_WIDTHS]
        + [jax.ShapeDtypeStruct(w.shape, bf16) for w in cast_ws],
        scratch_shapes=[pltpu.VMEM((tm, d), bf16),
                        pltpu.VMEM((2, COL_TILE // LANES, tm, LANES), f32)],
        compiler_params=_params(("parallel", "parallel")),
        name="qkv",
    )(x, mod6, g1, w_qkv, gvec, ones_bd, *rot_tables, *cast_ws)
    n_out = len(_QKV_OUT_WIDTHS)
    return outs[:n_out], outs[n_out:]


def _pair_mask(nq):
    row = lax.broadcasted_iota(jnp.int32, (2 * nq, LANES), 0)
    lane = lax.broadcasted_iota(jnp.int32, (2 * nq, LANES), 1)
    return (row < nq) == (lane < HEAD_DIM)


def _stack_pair(qp, own_head):
    q2 = jnp.concatenate([qp, qp], axis=0)
    return jnp.where(own_head, q2, jnp.zeros_like(q2))


def _na_kernel(q_ref, kp_ref, kc_ref, kn_ref, vp_ref, vc_ref, vn_ref, tbl_ref, o_ref, kwin, vwin):
    rb = pl.program_id(1)
    blk = NA_ROWS * GRID_W
    n_rows = pl.num_programs(1) * NA_ROWS
    for t, (kr, vr) in enumerate(((kp_ref, vp_ref), (kc_ref, vc_ref), (kn_ref, vn_ref))):
        kwin[t * blk:(t + 1) * blk, :] = kr[0]
        vwin[t * blk:(t + 1) * blk, :] = vr[0]
    first_head = lax.broadcasted_iota(jnp.int32, (GRID_W, LANES), 1) < HEAD_DIM
    own_head = _pair_mask(GRID_W)
    nkeys = NA_KH * GRID_W
    npair = NA_HEADS // 2
    pair_rows = 2 * GRID_W

    def row_body(a, carry):
        r = rb * NA_ROWS + a
        row_start = jnp.clip(r - NA_KH // 2, 0, n_rows - NA_KH)
        delta = r - row_start
        off = pl.multiple_of((row_start - (rb - 1) * NA_ROWS) * GRID_W, GRID_W)
        qoff = pl.multiple_of(a * GRID_W, GRID_W)
        s_parts = []
        for hp in range(npair):
            cols = slice(hp * LANES, (hp + 1) * LANES)
            q2 = _stack_pair(q_ref[0, pl.ds(qoff, GRID_W), cols], own_head)
            kk = kwin[pl.ds(off, nkeys), cols]
            s_parts.append(lax.dot_general(q2, kk, (((1,), (1,)), ((), ())), preferred_element_type=f32))
        n_off = 2 * NA_KH - 2
        bias = jnp.concatenate(
            [jnp.concatenate([tbl_ref[h * n_off + 2 * kp - delta + NA_KH - 1] for kp in range(NA_KH // 2)], axis=1)
             for h in range(NA_HEADS)], axis=0)
        s = jnp.concatenate(s_parts, axis=0) + bias
        m = jnp.max(s, axis=-1, keepdims=True)
        p = jnp.exp2(s - m)
        inv = 1.0 / jnp.sum(p, axis=-1, keepdims=True)
        pb = p.astype(bf16)
        for hp in range(npair):
            cols = slice(hp * LANES, (hp + 1) * LANES)
            rows = slice(hp * pair_rows, (hp + 1) * pair_rows)
            vv = vwin[pl.ds(off, nkeys), cols]
            pv = jnp.dot(pb[rows], vv, preferred_element_type=f32) * inv[rows]
            o_ref[0, pl.ds(qoff, GRID_W), cols] = jnp.where(first_head, pv[:GRID_W], pv[GRID_W:]).astype(bf16)
        return carry

    lax.fori_loop(0, NA_ROWS, row_body, 0, unroll=NA_ROWS)


def _na(q, k, v, tbl):
    bsz, s, _ = q.shape
    blk = NA_ROWS * GRID_W
    nb = s // blk
    qspec = pl.BlockSpec((1, blk, WA), lambda b, i: (b, i, 0))
    halo = [pl.BlockSpec((1, blk, WA), lambda b, i: (b, jnp.maximum(i - 1, 0), 0)),
            pl.BlockSpec((1, blk, WA), lambda b, i: (b, i, 0)),
            pl.BlockSpec((1, blk, WA), lambda b, i: (b, jnp.minimum(i + 1, nb - 1), 0))]

    return pl.pallas_call(
        _na_kernel,
        grid=(bsz, nb),
        in_specs=[qspec] + halo + halo + [_const_spec(tbl.shape)],
        out_specs=pl.BlockSpec((1, blk, WA), lambda b, i: (b, i, 0)),
        out_shape=jax.ShapeDtypeStruct((bsz, s, WA), bf16),
        scratch_shapes=[pltpu.VMEM((3 * blk, WA), bf16), pltpu.VMEM((3 * blk, WA), bf16)],
        compiler_params=_params(("parallel", "parallel")),
        name="na",
    )(q, k, k, k, v, v, v, tbl)


def _na_bias_table(rpb):
    col = jnp.arange(GRID_W)
    col_start = jnp.clip(col - NA_KW // 2, 0, GRID_W - NA_KW)
    cmask = (col[None, :] >= col_start[:, None]) & (col[None, :] < col_start[:, None] + NA_KW)
    col_off = jnp.clip(col[None, :] - col[:, None] + (NA_KW - 1), 0, 2 * NA_KW - 2)
    onehot = (col_off[None] == jnp.arange(2 * NA_KW - 1)[:, None, None]).astype(f32)
    t = jnp.einsum('hrc,cqk->hrqk', rpb.astype(f32), onehot, precision=lax.Precision.HIGHEST)
    t = jnp.where(cmask, t * LOG2E, NEG_INF)
    t = jnp.concatenate([t[:, :-1], t[:, 1:]], axis=-1)
    return t.reshape(NA_HEADS * (2 * NA_KH - 2), GRID_W, 2 * GRID_W)


def _dil_halo_pieces(dil):
    return 1 if DIL_TILE // dil == DIL_QBLOCK else dil


def _dil_kernel(*refs):
    ng = len(DIL_CONFIGS)
    n_in = sum(1 + 2 * (1 + 2 * _dil_halo_pieces(dil)) for _, dil in DIL_CONFIGS)
    in_refs = list(refs[:n_in])
    o_ref = refs[n_in]
    o_scr, m_scr, l_scr = refs[n_in + 1:]
    ti = pl.program_id(1)
    n_tiles = pl.num_programs(1)
    qb = DIL_QBLOCK
    span = 3 * qb

    first_head = lax.broadcasted_iota(jnp.int32, (qb, LANES), 1) < HEAD_DIM
    own_head = _pair_mask(qb)
    npair = DIL_HPG // 2
    qi = lax.broadcasted_iota(jnp.int32, (qb, span), 0)
    kj = lax.broadcasted_iota(jnp.int32, (qb, span), 1)
    band = jnp.where((kj >= qi) & (kj <= qi + 2 * qb), 0.0, NEG_INF).astype(f32)
    kcol = lax.broadcasted_iota(jnp.int32, (1, span), 1)
    pen_lo = jnp.where(kcol < qb, jnp.where(ti == 0, NEG_INF, 0.0), 0.0).astype(f32)
    pen_hi = jnp.where(kcol >= 2 * qb, jnp.where(ti == n_tiles - 1, NEG_INF, 0.0), 0.0).astype(f32)

    for g, (_, dil) in enumerate(DIL_CONFIGS):
        seg = DIL_TILE // dil
        nj = seg // qb
        nh = _dil_halo_pieces(dil)
        q_ref = in_refs.pop(0)
        kc, kp, kn = in_refs.pop(0), [in_refs.pop(0) for _ in range(nh)], [in_refs.pop(0) for _ in range(nh)]
        vc, vp, vn = in_refs.pop(0), [in_refs.pop(0) for _ in range(nh)], [in_refs.pop(0) for _ in range(nh)]

        def halo(pieces, r, cols, nh=nh):
            return pieces[r][0, :, cols] if nh > 1 else pieces[0][0, r * qb:(r + 1) * qb, cols]

        def window(refs3, r, j, cols, seg=seg, nj=nj, halo=halo):
            p_refs, c_ref, n_refs = refs3
            base = r * seg
            if 0 < j < nj - 1:
                return c_ref[0, base + (j - 1) * qb:base + (j + 2) * qb, cols]
            lo = halo(p_refs, r, cols) if j == 0 else c_ref[0, base + (j - 1) * qb:base + j * qb, cols]
            mid = c_ref[0, base + j * qb:base + (j + 1) * qb, cols]
            hi = halo(n_refs, r, cols) if j == nj - 1 else c_ref[0, base + (j + 1) * qb:base + (j + 2) * qb, cols]
            return jnp.concatenate([lo, mid, hi], axis=0)

        for t0 in range(0, dil * nj, DIL_UNROLL):
            blocks = [divmod(t, nj) for t in range(t0, t0 + DIL_UNROLL)]
            s_parts = []
            for k, (r, j) in enumerate(blocks):
                mask = band
                if j == 0:
                    mask = mask + pen_lo
                if j == nj - 1:
                    mask = mask + pen_hi
                for hp in range(npair):
                    cols = slice(hp * LANES, (hp + 1) * LANES)
                    q2 = _stack_pair(q_ref[0, (t0 + k) * qb:(t0 + k + 1) * qb, cols], own_head)
                    kk = window((kp, kc, kn), r, j, cols)
                    s2 = lax.dot_general(q2, kk, (((1,), (1,)), ((), ())), preferred_element_type=f32)
                    s_parts += [s2[:qb] + mask, s2[qb:] + mask]
            s = jnp.concatenate(s_parts, axis=0)
            m = jnp.max(s, axis=-1, keepdims=True)
            p = jnp.exp2(s - m)
            l = jnp.sum(p, axis=-1, keepdims=True)
            pb = p.astype(bf16)
            for k, (r, j) in enumerate(blocks):
                store_rows = pl.ds(j * qb * dil + r, qb, stride=dil) if dil > 1 else pl.ds(j * qb, qb)
                for hp in range(npair):
                    cols = slice(hp * LANES, (hp + 1) * LANES)
                    r0 = (k * npair + hp) * 2 * qb
                    vv = window((vp, vc, vn), r, j, cols)
                    pv = jnp.dot(pb[r0:r0 + 2 * qb], vv, preferred_element_type=f32)
                    slot = g * npair + hp
                    o_scr[slot, store_rows, :] = jnp.where(first_head, pv[:qb], pv[qb:])
                    m_scr[slot, store_rows, :] = jnp.where(first_head, m[r0:r0 + qb], m[r0 + qb:r0 + 2 * qb])
                    l_scr[slot, store_rows, :] = jnp.where(first_head, l[r0:r0 + qb], l[r0 + qb:r0 + 2 * qb])

    for hp in range(npair):
        m_all = m_scr[hp]
        for g in range(1, ng):
            m_all = jnp.maximum(m_all, m_scr[g * npair + hp])
        num = jnp.zeros_like(m_all)
        den = jnp.zeros_like(m_all)
        for g in range(ng):
            w = jnp.exp2(m_scr[g * npair + hp] - m_all)
            num = num + w * o_scr[g * npair + hp]
            den = den + w * l_scr[g * npair + hp]
        o_ref[0, :, hp * LANES:(hp + 1) * LANES] = (num / den).astype(bf16)


def _dil(qs, ks, vs):
    bsz, s, _ = qs[0].shape
    nt = s // DIL_TILE
    ng = len(DIL_CONFIGS)
    qb = DIL_QBLOCK
    per_tile = DIL_TILE // qb
    blk = (1, DIL_TILE, WB_OUT)
    cur = pl.BlockSpec(blk, lambda b, i: (b, i, 0))
    in_specs, args = [], []
    for g, (_, dil) in enumerate(DIL_CONFIGS):
        seg = DIL_TILE // dil
        if _dil_halo_pieces(dil) == 1 and seg == qb:
            prev = [pl.BlockSpec(blk, lambda b, i: (b, jnp.maximum(i - 1, 0), 0))]
            nxt = [pl.BlockSpec(blk, lambda b, i: (b, jnp.minimum(i + 1, nt - 1), 0))]
        else:
            prev = [pl.BlockSpec((1, qb, WB_OUT), lambda b, i, o=(r + 1) * seg // qb - 1:
                                 (b, jnp.maximum((i - 1) * per_tile + o, 0), 0)) for r in range(dil)]
            nxt = [pl.BlockSpec((1, qb, WB_OUT), lambda b, i, o=r * seg // qb:
                                (b, jnp.minimum((i + 1) * per_tile + o, s // qb - 1), 0)) for r in range(dil)]
        in_specs += [cur] + 2 * ([cur] + prev + nxt)
        args += [qs[g]] + [ks[g]] * (1 + len(prev) + len(nxt)) + [vs[g]] * (1 + len(prev) + len(nxt))
    return pl.pallas_call(
        _dil_kernel,
        grid=(bsz, nt),
        in_specs=in_specs,
        out_specs=pl.BlockSpec(blk, lambda b, i: (b, i, 0)),
        out_shape=jax.ShapeDtypeStruct((bsz, s, WB_OUT), bf16),
        scratch_shapes=[pltpu.VMEM((ng * DIL_HPG // 2, DIL_TILE, LANES), f32)] * 3,
        compiler_params=_params(("parallel", "parallel")),
        name="dil",
    )(*args)


def _tail_kernel(x_ref, oa_ref, ob_ref, mod_ref, g1_ref, g2_ref, bg_ref, wpa_ref, wpb_ref,
                 wo_ref, win_ref, wout_ref, *rest):
    d = x_ref.shape[2]
    n_gate = 2 * d // COL_TILE
    wg_refs = rest[:n_gate]
    o_ref, h_scr, m_scr, act_scr = rest[n_gate:]
    x = x_ref[0]
    mod = mod_ref[0]
    sh1, sc1, gt1, sh2, sc2, gt2 = [mod[k:k + 1] for k in range(6)]
    h_scr[...] = _modulated_norm(x, g1_ref[...], sc1, sh1).astype(bf16)
    oa = oa_ref[0]
    ob = ob_ref[0]
    for n in range(d // COL_TILE):
        ca = slice(n * COL_TILE, (n + 1) * COL_TILE)
        cb = slice(d + n * COL_TILE, d + (n + 1) * COL_TILE)
        wga, wgb = wg_refs[n], wg_refs[n_gate // 2 + n]
        ga = jax.nn.sigmoid(jnp.dot(h_scr[...], wga[...], preferred_element_type=f32) + bg_ref[:, ca])
        gb = jax.nn.sigmoid(jnp.dot(h_scr[...], wgb[...], preferred_element_type=f32) + bg_ref[:, cb])
        pa = jnp.dot(oa, wpa_ref[:, ca], preferred_element_type=f32)
        pb = jnp.dot(ob, wpb_ref[:, ca], preferred_element_type=f32)
        m_scr[:, ca] = (ga * pa + gb * pb).astype(bf16)
    x1 = x + gt1 * jnp.dot(m_scr[...], wo_ref[...], preferred_element_type=f32)
    h_scr[...] = _modulated_norm(x1, g2_ref[...], sc2, sh2).astype(bf16)
    for f in range(D_FF // FF_CHUNK):
        ca = slice(f * FF_CHUNK, (f + 1) * FF_CHUNK)
        cu = slice(D_FF + f * FF_CHUNK, D_FF + (f + 1) * FF_CHUNK)
        a = jnp.dot(h_scr[...], win_ref[:, ca], preferred_element_type=f32)
        u = jnp.dot(h_scr[...], win_ref[:, cu], preferred_element_type=f32)
        act_scr[:, ca] = ((a * jax.nn.sigmoid(a)) * u).astype(bf16)
    o_ref[0] = x1 + gt2 * jnp.dot(act_scr[...], wout_ref[...], preferred_element_type=f32)


def _tail(x, o_a, o_b, mod6, g1, g2, w_in, bg, wpa, wpb, wo, w_ffn_in, w_ffn_out, tm=TAIL_TM):
    bsz, s, d = x.shape
    tok = lambda w: pl.BlockSpec((1, tm, w), lambda b, i: (b, i, 0))
    consts = [g1, g2, bg, wpa, wpb, wo, w_ffn_in, w_ffn_out]
    n_gate = 2 * d // COL_TILE
    gate_specs = [pl.BlockSpec((d, COL_TILE), lambda b, i, c=_N_J + n: (0, c), pipeline_mode=pl.Buffered(1))
                  for n in range(n_gate)]
    return pl.pallas_call(
        _tail_kernel,
        grid=(bsz, s // tm),
        in_specs=[tok(d), tok(WA), tok(WB_OUT), pl.BlockSpec((1, 6, d), lambda b, i: (b, 0, 0))]
        + [_const_spec(c.shape) for c in consts] + gate_specs,
        out_specs=tok(d),
        out_shape=jax.ShapeDtypeStruct((bsz, s, d), f32),
        scratch_shapes=[pltpu.VMEM((tm, d), bf16), pltpu.VMEM((tm, d), bf16), pltpu.VMEM((tm, D_FF), bf16)],
        compiler_params=_params(("parallel", "parallel")),
        name="tail",
    )(x, o_a, o_b, mod6, *consts, *([w_in] * n_gate))


def _rotary_tables(s):
    half = ROT_DIM // 2
    inv_freq = ROPE_THETA ** (-(jnp.arange(half, dtype=f32) * 2.0) / ROT_DIM)
    ang = jnp.arange(s).astype(f32)[:, None] * inv_freq[None, :]
    cos, sin = jnp.cos(ang), jnp.sin(ang)
    ones = jnp.ones((s, HEAD_DIM - ROT_DIM), f32)
    zeros = jnp.zeros((s, HEAD_DIM - ROT_DIM), f32)
    zh = jnp.zeros((s, half), f32)
    cos_h = jnp.concatenate([cos, cos, ones], axis=1)
    sina_h = jnp.concatenate([-sin, zh, zeros], axis=1)
    sinb_h = jnp.concatenate([zh, sin, zeros], axis=1)
    rep = LANES // HEAD_DIM
    return tuple(jnp.tile(t, (1, rep)) for t in (cos_h, sina_h, sinb_h))


def _layer(x, mod6, g_norm1, g_norm2, w_in, b_gate, g_qa, g_ka, g_qb, g_kb, rpb,
           w_proj_a, w_proj_b, w_o, w_ffn_in, w_ffn_out):
    bsz, s, d = x.shape
    scale = HEAD_DIM ** -0.5 * LOG2E
    rep = COL_TILE // HEAD_DIM
    gvec = jnp.stack([jnp.tile(g_qa * scale, rep), jnp.tile(g_ka, rep),
                      jnp.tile(g_qb * scale, rep), jnp.tile(g_kb, rep)]).astype(f32)
    hid = jnp.arange(COL_TILE) // HEAD_DIM
    ones_bd = (hid[:, None] == hid[None, :]).astype(bf16)
    rot_tables = _rotary_tables(s)
    g1 = g_norm1.reshape(1, d)
    g2 = g_norm2.reshape(1, d)

    qkv, tail_ws = _qkv(x, mod6, g1, w_in, gvec, ones_bd, rot_tables,
                        [w_proj_a, w_proj_b, w_o, w_ffn_in, w_ffn_out])
    ng = len(DIL_CONFIGS)
    o_a = _na(qkv[_OUT_QA], qkv[_OUT_KA], qkv[_OUT_VA], _na_bias_table(rpb))
    o_b = _dil(qkv[_OUT_QB:_OUT_QB + ng], qkv[_OUT_KB:_OUT_KB + ng], qkv[_OUT_VB:_OUT_VB + ng])

    return _tail(x, o_a, o_b, mod6, g1, g2, w_in, b_gate.reshape(1, 2 * d), *tail_ws)


def kernel(x, c, w_ada, b_ada, g_norm1, g_norm2, w_in, b_gate, g_qa, g_ka, g_qb, g_kb, rpb,
           w_proj_a, w_proj_b, w_o, w_ffn_in, w_ffn_out):
    depth = w_ada.shape[0]
    bsz, d = c.shape
    for l in range(depth):
        mod, w_in_bf = _mod(c, w_ada[l], b_ada[l], w_in[l])
        x = _layer(x, mod.reshape(bsz, 6, d), g_norm1[l], g_norm2[l], w_in_bf, b_gate[l], g_qa[l], g_ka[l], g_qb[l],
                   g_kb[l], rpb[l], w_proj_a[l], w_proj_b[l], w_o[l], w_ffn_in[l], w_ffn_out[l])
    return x
```

```python
import functools

import jax
import jax.numpy as jnp
from jax import lax
from jax.experimental import pallas as pl
from jax.experimental.pallas import tpu as pltpu

f32 = jnp.float32
bf16 = jnp.bfloat16

D_MODEL = 1024
HEAD_DIM = 64
GRID_W = 64
NA_HEADS = 8
NA_KH = 8
NA_KW = 16
DIL_CONFIGS = ((128, 1), (512, 4), (2048, 16))
DIL_HPG = 4
DIL_HEADS = DIL_HPG * len(DIL_CONFIGS)
DIL_QBLOCK = 64
ROT_DIM = HEAD_DIM // 4
ROPE_THETA = 500000.0
D_FF = -(-8 * D_MODEL // (3 * 256)) * 256
EPS = 1e-6
NEG_INF = -1e30
LOG2E = 1.4426950408889634
WA = NA_HEADS * HEAD_DIM
WB = DIL_HEADS * HEAD_DIM
WB_OUT = DIL_HPG * HEAD_DIM
W_QKV = 3 * WA + 3 * WB

LANES = 128
COL_TILE = 256
DIL_TILE = 1024
QKV_ROW_SPLIT = 2
QKV_DOT_TILES = 2
DIL_UNROLL = 16
NA_ROWS = 16
TAIL_TM = 512
FF_CHUNK = 256
VMEM_LIMIT = 56 * 1024 * 1024

SAFE_STRIDE = 4

assert all((win // 2) // dil == DIL_QBLOCK for win, dil in DIL_CONFIGS)
assert all(dil <= SAFE_STRIDE or (dil % SAFE_STRIDE == 0 and dil // SAFE_STRIDE <= SAFE_STRIDE)
           for _, dil in DIL_CONFIGS)


def _params(sem):
    return pltpu.CompilerParams(dimension_semantics=sem, vmem_limit_bytes=VMEM_LIMIT)


def _const_spec(shape):
    nd = len(shape)
    return pl.BlockSpec(shape, lambda *_: (0,) * nd, pipeline_mode=pl.Buffered(1))


def _mod_kernel(ct_ref, w_ref, b_ref, win_ref, o_ref, win_o_ref):
    ct = ct_ref[...]
    act = ct * jax.nn.sigmoid(ct)
    w = w_ref[...]
    rows = [jnp.sum(act[:, b:b + 1] * w, axis=0, keepdims=True) for b in range(ct.shape[1])]
    o_ref[...] = jnp.concatenate(rows, axis=0) + b_ref[...]
    win_o_ref[...] = win_ref[...].astype(bf16)


def _mod(c, w_ada, b_ada, w_in):
    bsz, d = c.shape
    n = w_ada.shape[1]
    steps = 4
    tn = n // steps
    slab = pl.BlockSpec((w_in.shape[0] // steps, w_in.shape[1]), lambda j: (j, 0))
    return pl.pallas_call(
        _mod_kernel,
        grid=(steps,),
        in_specs=[pl.BlockSpec((d, bsz), lambda j: (0, 0)),
                  pl.BlockSpec((d, tn), lambda j: (0, j)),
                  pl.BlockSpec((1, tn), lambda j: (0, j)),
                  slab],
        out_specs=[pl.BlockSpec((bsz, tn), lambda j: (0, j)), slab],
        out_shape=[jax.ShapeDtypeStruct((bsz, n), f32), jax.ShapeDtypeStruct(w_in.shape, bf16)],
        compiler_params=_params(("arbitrary",)),
        name="mod",
    )(c.T, w_ada, b_ada.reshape(1, n), w_in)


def _modulated_norm(x, g, sc, sh):
    ms = jnp.mean(x * x, axis=-1, keepdims=True)
    y = x * lax.rsqrt(ms + EPS) * g
    return y * (1.0 + sc) + sh


_J_QB = 3 * WA // COL_TILE
_J_KB = _J_QB + WB // COL_TILE
_J_VB = _J_KB + WB // COL_TILE
_N_J = W_QKV // COL_TILE


def _qkv_tile_kind(j):
    if j < WA // COL_TILE:
        return 0, False, 1
    if j < 2 * WA // COL_TILE:
        return 1, False, 1
    if j < _J_QB:
        return None, False, 1
    dil = DIL_CONFIGS[(j - _J_QB) % len(DIL_CONFIGS)][1]
    if j < _J_KB:
        return 2, True, dil
    if j < _J_VB:
        return 3, True, dil
    return None, False, dil


_QKV_OUT_WIDTHS = (WA,) * 3 + (WB_OUT,) * (3 * len(DIL_CONFIGS))
_OUT_QA, _OUT_KA, _OUT_VA, _OUT_QB, _OUT_KB, _OUT_VB = 0, 1, 2, 3, 3 + len(DIL_CONFIGS), 3 + 2 * len(DIL_CONFIGS)


def _qkv_tile_out(j, o_refs):
    if j < _J_QB:
        per = WA // COL_TILE
        return o_refs[j // per], (j % per) * COL_TILE
    return o_refs[_OUT_QB + j - _J_QB], 0


def _qkv_kernel(n_cast,x_ref, mod_ref, g1_ref, w_ref, gvec_ref, ones_ref, cos_ref, sina_ref, sinb_ref,
                *rest):
    n_out = len(_QKV_OUT_WIDTHS)
    cast_in, o_refs = rest[:n_cast], rest[n_cast:n_cast + n_out]
    cast_out = rest[n_cast + n_out:2 * n_cast + n_out]
    h_scr, y_scr, z_scr = rest[2 * n_cast + n_out:]
    for src, dst in zip(cast_in, cast_out):
        dst[...] = src[...].astype(bf16)
    tm = x_ref.shape[1]
    nc = COL_TILE // LANES
    mod = mod_ref[0]
    h_scr[...] = _modulated_norm(x_ref[0], g1_ref[...], mod[1:2], mod[0:1]).astype(bf16)

    wide = {}
    for j in range(_N_J):
        gain_row, rotary, dil = _qkv_tile_kind(j)
        o_ref, c0 = _qkv_tile_out(j, o_refs)
        slot = j % y_scr.shape[0]
        for hb in range(QKV_ROW_SPLIT):
            rows = slice(hb * tm // QKV_ROW_SPLIT, (hb + 1) * tm // QKV_ROW_SPLIT)
            if j % QKV_DOT_TILES == 0:
                wcols = slice(j * COL_TILE, min(j + QKV_DOT_TILES, _N_J) * COL_TILE)
                wide[hb] = jnp.dot(h_scr[rows, :], w_ref[:, wcols], preferred_element_type=f32)
            sub = j % QKV_DOT_TILES
            y = wide[hb][:, sub * COL_TILE:(sub + 1) * COL_TILE]
            if gain_row is not None:
                ss = jnp.dot((y * y).astype(bf16), ones_ref[...], preferred_element_type=f32)
                y = y * lax.rsqrt(ss * (1.0 / HEAD_DIM) + EPS) * gvec_ref[gain_row:gain_row + 1, :]
            if rotary:
                cos, sina, sinb = [jnp.concatenate([t_ref[rows, :]] * nc, axis=1)
                                   for t_ref in (cos_ref, sina_ref, sinb_ref)]
                half = ROT_DIM // 2
                up = pltpu.roll(y, COL_TILE - half, 1)
                dn = pltpu.roll(y, half, 1)
                y = y * cos + up * sina + dn * sinb
            if dil == 1:
                o_ref[0, rows, c0:c0 + COL_TILE] = y.astype(bf16)
            else:
                for c in range(nc):
                    y_scr[slot, c, rows, :] = y[:, c * LANES:(c + 1) * LANES]
        if dil == 1:
            continue
        seg = DIL_TILE // dil
        for c in range(nc):
            ocols = slice(c0 + c * LANES, c0 + (c + 1) * LANES)
            if dil <= SAFE_STRIDE:
                for r in range(dil):
                    rows = y_scr[slot, c, pl.ds(r, seg, stride=dil), :]
                    o_ref[0, r * seg:(r + 1) * seg, ocols] = rows.astype(bf16)
                continue
            hi = dil // SAFE_STRIDE
            sub = DIL_TILE // SAFE_STRIDE
            for r_lo in range(SAFE_STRIDE):
                z_scr[c, r_lo * sub:(r_lo + 1) * sub, :] = y_scr[slot, c, pl.ds(r_lo, sub, stride=SAFE_STRIDE), :]
            for r_lo in range(SAFE_STRIDE):
                for r_hi in range(hi):
                    r = SAFE_STRIDE * r_hi + r_lo
                    rows = z_scr[c, pl.ds(r_lo * sub + r_hi, seg, stride=hi), :]
                    o_ref[0, r * seg:(r + 1) * seg, ocols] = rows.astype(bf16)


def _qkv(x, mod6, g1, w_qkv, gvec, ones_bd, rot_tables, cast_ws, tm=DIL_TILE):
    bsz, s, d = x.shape
    assert tm == DIL_TILE
    nt = s // tm
    steps = bsz * nt
    slab_specs = [pl.BlockSpec((w.shape[0] // steps, w.shape[1]), lambda b, i: (b * nt + i, 0)) for w in cast_ws]
    outs = pl.pallas_call(
        functools.partial(_qkv_kernel, len(cast_ws)),
        grid=(bsz, nt),
        in_specs=[pl.BlockSpec((1, tm, d), lambda b, i: (b, i, 0)),
                  pl.BlockSpec((1, 6, d), lambda b, i: (b, 0, 0)),
                  _const_spec(g1.shape),
                  pl.BlockSpec((d, W_QKV), lambda b, i: (0, 0), pipeline_mode=pl.Buffered(1)),
                  _const_spec(gvec.shape),
                  _const_spec(ones_bd.shape),
                  ] + [pl.BlockSpec((tm, LANES), lambda b, i: (i, 0))] * len(rot_tables) + slab_specs,
        out_specs=[pl.BlockSpec((1, tm, w), lambda b, i: (b, i, 0)) for w in _QKV_OUT_WIDTHS] + slab_specs,
        out_shape=[jax.ShapeDtypeStruct((bsz, s, w), bf16) for w in _QKV_OUT_WIDTHS]
        + [jax.ShapeDtypeStruct(w.shape, bf16) for w in cast_ws],
        scratch_shapes=[pltpu.VMEM((tm, d), bf16),
                        pltpu.VMEM((2, COL_TILE // LANES, tm, LANES), f32),
                        pltpu.VMEM((COL_TILE // LANES, tm, LANES), f32)],
        compiler_params=_params(("parallel", "parallel")),
        name="qkv",
    )(x, mod6, g1, w_qkv, gvec, ones_bd, *rot_tables, *cast_ws)
    n_out = len(_QKV_OUT_WIDTHS)
    return outs[:n_out], outs[n_out:]


def _pair_mask(nq):
    row = lax.broadcasted_iota(jnp.int32, (2 * nq, LANES), 0)
    lane = lax.broadcasted_iota(jnp.int32, (2 * nq, LANES), 1)
    return (row < nq) == (lane < HEAD_DIM)


def _stack_pair(qp, own_head):
    q2 = jnp.concatenate([qp, qp], axis=0)
    return jnp.where(own_head, q2, jnp.zeros_like(q2))


def _na_kernel(q_ref, kp_ref, kc_ref, kn_ref, vp_ref, vc_ref, vn_ref, tbl_ref, o_ref, kwin, vwin):
    rb = pl.program_id(1)
    blk = NA_ROWS * GRID_W
    n_rows = pl.num_programs(1) * NA_ROWS
    for t, (kr, vr) in enumerate(((kp_ref, vp_ref), (kc_ref, vc_ref), (kn_ref, vn_ref))):
        kwin[t * blk:(t + 1) * blk, :] = kr[0]
        vwin[t * blk:(t + 1) * blk, :] = vr[0]
    first_head = lax.broadcasted_iota(jnp.int32, (GRID_W, LANES), 1) < HEAD_DIM
    own_head = _pair_mask(GRID_W)
    nkeys = NA_KH * GRID_W
    npair = NA_HEADS // 2
    pair_rows = 2 * GRID_W

    def row_body(a, carry):
        r = rb * NA_ROWS + a
        row_start = jnp.clip(r - NA_KH // 2, 0, n_rows - NA_KH)
        delta = r - row_start
        off = pl.multiple_of((row_start - (rb - 1) * NA_ROWS) * GRID_W, GRID_W)
        qoff = pl.multiple_of(a * GRID_W, GRID_W)
        s_parts = []
        for hp in range(npair):
            cols = slice(hp * LANES, (hp + 1) * LANES)
            q2 = _stack_pair(q_ref[0, pl.ds(qoff, GRID_W), cols], own_head)
            kk = kwin[pl.ds(off, nkeys), cols]
            s_parts.append(lax.dot_general(q2, kk, (((1,), (1,)), ((), ())), preferred_element_type=f32))
        n_off = 2 * NA_KH - 2
        bias = jnp.concatenate(
            [jnp.concatenate([tbl_ref[h * n_off + 2 * kp - delta + NA_KH - 1] for kp in range(NA_KH // 2)], axis=1)
             for h in range(NA_HEADS)], axis=0)
        s = jnp.concatenate(s_parts, axis=0) + bias
        m = jnp.max(s, axis=-1, keepdims=True)
        p = jnp.exp2(s - m)
        inv = 1.0 / jnp.sum(p, axis=-1, keepdims=True)
        pb = p.astype(bf16)
        for hp in range(npair):
            cols = slice(hp * LANES, (hp + 1) * LANES)
            rows = slice(hp * pair_rows, (hp + 1) * pair_rows)
            vv = vwin[pl.ds(off, nkeys), cols]
            pv = jnp.dot(pb[rows], vv, preferred_element_type=f32) * inv[rows]
            o_ref[0, pl.ds(qoff, GRID_W), cols] = jnp.where(first_head, pv[:GRID_W], pv[GRID_W:]).astype(bf16)
        return carry

    lax.fori_loop(0, NA_ROWS, row_body, 0, unroll=NA_ROWS)


def _na(q, k, v, tbl):
    bsz, s, _ = q.shape
    blk = NA_ROWS * GRID_W
    nb = s // blk
    qspec = pl.BlockSpec((1, blk, WA), lambda b, i: (b, i, 0))
    halo = [pl.BlockSpec((1, blk, WA), lambda b, i: (b, jnp.maximum(i - 1, 0), 0)),
            pl.BlockSpec((1, blk, WA), lambda b, i: (b, i, 0)),
            pl.BlockSpec((1, blk, WA), lambda b, i: (b, jnp.minimum(i + 1, nb - 1), 0))]

    return pl.pallas_call(
        _na_kernel,
        grid=(bsz, nb),
        in_specs=[qspec] + halo + halo + [_const_spec(tbl.shape)],
        out_specs=pl.BlockSpec((1, blk, WA), lambda b, i: (b, i, 0)),
        out_shape=jax.ShapeDtypeStruct((bsz, s, WA), bf16),
        scratch_shapes=[pltpu.VMEM((3 * blk, WA), bf16), pltpu.VMEM((3 * blk, WA), bf16)],
        compiler_params=_params(("parallel", "parallel")),
        name="na",
    )(q, k, k, k, v, v, v, tbl)


def _na_bias_table(rpb):
    col = jnp.arange(GRID_W)
    col_start = jnp.clip(col - NA_KW // 2, 0, GRID_W - NA_KW)
    cmask = (col[None, :] >= col_start[:, None]) & (col[None, :] < col_start[:, None] + NA_KW)
    col_off = jnp.clip(col[None, :] - col[:, None] + (NA_KW - 1), 0, 2 * NA_KW - 2)
    onehot = (col_off[None] == jnp.arange(2 * NA_KW - 1)[:, None, None]).astype(f32)
    t = jnp.einsum('hrc,cqk->hrqk', rpb.astype(f32), onehot, precision=lax.Precision.HIGHEST)
    t = jnp.where(cmask, t * LOG2E, NEG_INF)
    t = jnp.concatenate([t[:, :-1], t[:, 1:]], axis=-1)
    return t.reshape(NA_HEADS * (2 * NA_KH - 2), GRID_W, 2 * GRID_W)


def _dil_halo_pieces(dil):
    return 1 if DIL_TILE // dil == DIL_QBLOCK else dil


def _dil_kernel(*refs):
    ng = len(DIL_CONFIGS)
    n_in = sum(1 + 2 * (1 + 2 * _dil_halo_pieces(dil)) for _, dil in DIL_CONFIGS)
    in_refs = list(refs[:n_in])
    o_ref = refs[n_in]
    o_scr, m_scr, l_scr = refs[n_in + 1:]
    ti = pl.program_id(1)
    n_tiles = pl.num_programs(1)
    qb = DIL_QBLOCK
    span = 3 * qb

    first_head = lax.broadcasted_iota(jnp.int32, (qb, LANES), 1) < HEAD_DIM
    own_head = _pair_mask(qb)
    npair = DIL_HPG // 2
    qi = lax.broadcasted_iota(jnp.int32, (qb, span), 0)
    kj = lax.broadcasted_iota(jnp.int32, (qb, span), 1)
    band = jnp.where((kj >= qi) & (kj <= qi + 2 * qb), 0.0, NEG_INF).astype(f32)
    kcol = lax.broadcasted_iota(jnp.int32, (1, span), 1)
    pen_lo = jnp.where(kcol < qb, jnp.where(ti == 0, NEG_INF, 0.0), 0.0).astype(f32)
    pen_hi = jnp.where(kcol >= 2 * qb, jnp.where(ti == n_tiles - 1, NEG_INF, 0.0), 0.0).astype(f32)

    for g, (_, dil) in enumerate(DIL_CONFIGS):
        seg = DIL_TILE // dil
        nj = seg // qb
        nh = _dil_halo_pieces(dil)
        q_ref = in_refs.pop(0)
        kc, kp, kn = in_refs.pop(0), [in_refs.pop(0) for _ in range(nh)], [in_refs.pop(0) for _ in range(nh)]
        vc, vp, vn = in_refs.pop(0), [in_refs.pop(0) for _ in range(nh)], [in_refs.pop(0) for _ in range(nh)]

        def halo(pieces, r, cols, nh=nh):
            return pieces[r][0, :, cols] if nh > 1 else pieces[0][0, r * qb:(r + 1) * qb, cols]

        def window(refs3, r, j, cols, seg=seg, nj=nj, halo=halo):
            p_refs, c_ref, n_refs = refs3
            base = r * seg
            if 0 < j < nj - 1:
                return c_ref[0, base + (j - 1) * qb:base + (j + 2) * qb, cols]
            lo = halo(p_refs, r, cols) if j == 0 else c_ref[0, base + (j - 1) * qb:base + j * qb, cols]
            mid = c_ref[0, base + j * qb:base + (j + 1) * qb, cols]
            hi = halo(n_refs, r, cols) if j == nj - 1 else c_ref[0, base + (j + 1) * qb:base + (j + 2) * qb, cols]
            return jnp.concatenate([lo, mid, hi], axis=0)

        for t0 in range(0, dil * nj, DIL_UNROLL):
            blocks = [divmod(t, nj) for t in range(t0, t0 + DIL_UNROLL)]
            s_parts = []
            for k, (r, j) in enumerate(blocks):
                mask = band
                if j == 0:
                    mask = mask + pen_lo
                if j == nj - 1:
                    mask = mask + pen_hi
                for hp in range(npair):
                    cols = slice(hp * LANES, (hp + 1) * LANES)
                    q2 = _stack_pair(q_ref[0, (t0 + k) * qb:(t0 + k + 1) * qb, cols], own_head)
                    kk = window((kp, kc, kn), r, j, cols)
                    s2 = lax.dot_general(q2, kk, (((1,), (1,)), ((), ())), preferred_element_type=f32)
                    s_parts += [s2[:qb] + mask, s2[qb:] + mask]
            s = jnp.concatenate(s_parts, axis=0)
            m = jnp.max(s, axis=-1, keepdims=True)
            p = jnp.exp2(s - m)
            l = jnp.sum(p, axis=-1, keepdims=True)
            pb = p.astype(bf16)
            for k, (r, j) in enumerate(blocks):
                two_pass = dil > SAFE_STRIDE
                if two_pass:
                    hi, sub = dil // SAFE_STRIDE, DIL_TILE // SAFE_STRIDE
                    store_rows = pl.ds((r % SAFE_STRIDE) * sub + j * qb * hi + r // SAFE_STRIDE, qb, stride=hi)
                else:
                    store_rows = pl.ds(j * qb * dil + r, qb, stride=dil) if dil > 1 else pl.ds(j * qb, qb)
                for hp in range(npair):
                    cols = slice(hp * LANES, (hp + 1) * LANES)
                    r0 = (k * npair + hp) * 2 * qb
                    vv = window((vp, vc, vn), r, j, cols)
                    pv = jnp.dot(pb[r0:r0 + 2 * qb], vv, preferred_element_type=f32)
                    slot = (ng if two_pass else g) * npair + hp
                    o_scr[slot, store_rows, :] = jnp.where(first_head, pv[:qb], pv[qb:])
                    m_scr[slot, store_rows, :] = jnp.where(first_head, m[r0:r0 + qb], m[r0 + qb:r0 + 2 * qb])
                    l_scr[slot, store_rows, :] = jnp.where(first_head, l[r0:r0 + qb], l[r0 + qb:r0 + 2 * qb])
        if dil > SAFE_STRIDE:
            sub = DIL_TILE // SAFE_STRIDE
            for hp in range(npair):
                for scr in (o_scr, m_scr, l_scr):
                    for r_lo in range(SAFE_STRIDE):
                        scr[g * npair + hp, pl.ds(r_lo, sub, stride=SAFE_STRIDE), :] = (
                            scr[ng * npair + hp, r_lo * sub:(r_lo + 1) * sub, :])

    for hp in range(npair):
        m_all = m_scr[hp]
        for g in range(1, ng):
            m_all = jnp.maximum(m_all, m_scr[g * npair + hp])
        num = jnp.zeros_like(m_all)
        den = jnp.zeros_like(m_all)
        for g in range(ng):
            w = jnp.exp2(m_scr[g * npair + hp] - m_all)
            num = num + w * o_scr[g * npair + hp]
            den = den + w * l_scr[g * npair + hp]
        o_ref[0, :, hp * LANES:(hp + 1) * LANES] = (num / den).astype(bf16)


def _dil(qs, ks, vs):
    bsz, s, _ = qs[0].shape
    nt = s // DIL_TILE
    ng = len(DIL_CONFIGS)
    qb = DIL_QBLOCK
    per_tile = DIL_TILE // qb
    blk = (1, DIL_TILE, WB_OUT)
    cur = pl.BlockSpec(blk, lambda b, i: (b, i, 0))
    in_specs, args = [], []
    for g, (_, dil) in enumerate(DIL_CONFIGS):
        seg = DIL_TILE // dil
        if _dil_halo_pieces(dil) == 1 and seg == qb:
            prev = [pl.BlockSpec(blk, lambda b, i: (b, jnp.maximum(i - 1, 0), 0))]
            nxt = [pl.BlockSpec(blk, lambda b, i: (b, jnp.minimum(i + 1, nt - 1), 0))]
        else:
            prev = [pl.BlockSpec((1, qb, WB_OUT), lambda b, i, o=(r + 1) * seg // qb - 1:
                                 (b, jnp.maximum((i - 1) * per_tile + o, 0), 0)) for r in range(dil)]
            nxt = [pl.BlockSpec((1, qb, WB_OUT), lambda b, i, o=r * seg // qb:
                                (b, jnp.minimum((i + 1) * per_tile + o, s // qb - 1), 0)) for r in range(dil)]
        in_specs += [cur] + 2 * ([cur] + prev + nxt)
        args += [qs[g]] + [ks[g]] * (1 + len(prev) + len(nxt)) + [vs[g]] * (1 + len(prev) + len(nxt))
    return pl.pallas_call(
        _dil_kernel,
        grid=(bsz, nt),
        in_specs=in_specs,
        out_specs=pl.BlockSpec(blk, lambda b, i: (b, i, 0)),
        out_shape=jax.ShapeDtypeStruct((bsz, s, WB_OUT), bf16),
        scratch_shapes=[pltpu.VMEM(((ng + 1) * DIL_HPG // 2, DIL_TILE, LANES), f32)] * 3,
        compiler_params=_params(("parallel", "parallel")),
        name="dil",
    )(*args)


def _tail_kernel(x_ref, oa_ref, ob_ref, mod_ref, g1_ref, g2_ref, bg_ref, wpa_ref, wpb_ref,
                 wo_ref, win_ref, wout_ref, *rest):
    d = x_ref.shape[2]
    n_gate = 2 * d // COL_TILE
    wg_refs = rest[:n_gate]
    o_ref, h_scr, m_scr, act_scr = rest[n_gate:]
    x = x_ref[0]
    mod = mod_ref[0]
    sh1, sc1, gt1, sh2, sc2, gt2 = [mod[k:k + 1] for k in range(6)]
    h_scr[...] = _modulated_norm(x, g1_ref[...], sc1, sh1).astype(bf16)
    oa = oa_ref[0]
    ob = ob_ref[0]
    for n in range(d // COL_TILE):
        ca = slice(n * COL_TILE, (n + 1) * COL_TILE)
        cb = slice(d + n * COL_TILE, d + (n + 1) * COL_TILE)
        wga, wgb = wg_refs[n], wg_refs[n_gate // 2 + n]
        ga = jax.nn.sigmoid(jnp.dot(h_scr[...], wga[...], preferred_element_type=f32) + bg_ref[:, ca])
        gb = jax.nn.sigmoid(jnp.dot(h_scr[...], wgb[...], preferred_element_type=f32) + bg_ref[:, cb])
        pa = jnp.dot(oa, wpa_ref[:, ca], preferred_element_type=f32)
        pb = jnp.dot(ob, wpb_ref[:, ca], preferred_element_type=f32)
        m_scr[:, ca] = (ga * pa + gb * pb).astype(bf16)
    x1 = x + gt1 * jnp.dot(m_scr[...], wo_ref[...], preferred_element_type=f32)
    h_scr[...] = _modulated_norm(x1, g2_ref[...], sc2, sh2).astype(bf16)
    for f in range(D_FF // FF_CHUNK):
        ca = slice(f * FF_CHUNK, (f + 1) * FF_CHUNK)
        cu = slice(D_FF + f * FF_CHUNK, D_FF + (f + 1) * FF_CHUNK)
        a = jnp.dot(h_scr[...], win_ref[:, ca], preferred_element_type=f32)
        u = jnp.dot(h_scr[...], win_ref[:, cu], preferred_element_type=f32)
        act_scr[:, ca] = ((a * jax.nn.sigmoid(a)) * u).astype(bf16)
    o_ref[0] = x1 + gt2 * jnp.dot(act_scr[...], wout_ref[...], preferred_element_type=f32)


def _tail(x, o_a, o_b, mod6, g1, g2, w_in, bg, wpa, wpb, wo, w_ffn_in, w_ffn_out, tm=TAIL_TM):
    bsz, s, d = x.shape
    tok = lambda w: pl.BlockSpec((1, tm, w), lambda b, i: (b, i, 0))
    consts = [g1, g2, bg, wpa, wpb, wo, w_ffn_in, w_ffn_out]
    n_gate = 2 * d // COL_TILE
    gate_specs = [pl.BlockSpec((d, COL_TILE), lambda b, i, c=_N_J + n: (0, c), pipeline_mode=pl.Buffered(1))
                  for n in range(n_gate)]
    return pl.pallas_call(
        _tail_kernel,
        grid=(bsz, s // tm),
        in_specs=[tok(d), tok(WA), tok(WB_OUT), pl.BlockSpec((1, 6, d), lambda b, i: (b, 0, 0))]
        + [_const_spec(c.shape) for c in consts] + gate_specs,
        out_specs=tok(d),
        out_shape=jax.ShapeDtypeStruct((bsz, s, d), f32),
        scratch_shapes=[pltpu.VMEM((tm, d), bf16), pltpu.VMEM((tm, d), bf16), pltpu.VMEM((tm, D_FF), bf16)],
        compiler_params=_params(("parallel", "parallel")),
        name="tail",
    )(x, o_a, o_b, mod6, *consts, *([w_in] * n_gate))


def _rotary_tables(s):
    half = ROT_DIM // 2
    inv_freq = ROPE_THETA ** (-(jnp.arange(half, dtype=f32) * 2.0) / ROT_DIM)
    ang = jnp.arange(s).astype(f32)[:, None] * inv_freq[None, :]
    cos, sin = jnp.cos(ang), jnp.sin(ang)
    ones = jnp.ones((s, HEAD_DIM - ROT_DIM), f32)
    zeros = jnp.zeros((s, HEAD_DIM - ROT_DIM), f32)
    zh = jnp.zeros((s, half), f32)
    cos_h = jnp.concatenate([cos, cos, ones], axis=1)
    sina_h = jnp.concatenate([-sin, zh, zeros], axis=1)
    sinb_h = jnp.concatenate([zh, sin, zeros], axis=1)
    rep = LANES // HEAD_DIM
    return tuple(jnp.tile(t, (1, rep)) for t in (cos_h, sina_h, sinb_h))


def _layer(x, mod6, g_norm1, g_norm2, w_in, b_gate, g_qa, g_ka, g_qb, g_kb, rpb,
           w_proj_a, w_proj_b, w_o, w_ffn_in, w_ffn_out):
    bsz, s, d = x.shape
    scale = HEAD_DIM ** -0.5 * LOG2E
    rep = COL_TILE // HEAD_DIM
    gvec = jnp.stack([jnp.tile(g_qa * scale, rep), jnp.tile(g_ka, rep),
                      jnp.tile(g_qb * scale, rep), jnp.tile(g_kb, rep)]).astype(f32)
    hid = jnp.arange(COL_TILE) // HEAD_DIM
    ones_bd = (hid[:, None] == hid[None, :]).astype(bf16)
    rot_tables = _rotary_tables(s)
    g1 = g_norm1.reshape(1, d)
    g2 = g_norm2.reshape(1, d)

    qkv, tail_ws = _qkv(x, mod6, g1, w_in, gvec, ones_bd, rot_tables,
                        [w_proj_a, w_proj_b, w_o, w_ffn_in, w_ffn_out])
    ng = len(DIL_CONFIGS)
    o_a = _na(qkv[_OUT_QA], qkv[_OUT_KA], qkv[_OUT_VA], _na_bias_table(rpb))
    o_b = _dil(qkv[_OUT_QB:_OUT_QB + ng], qkv[_OUT_KB:_OUT_KB + ng], qkv[_OUT_VB:_OUT_VB + ng])

    return _tail(x, o_a, o_b, mod6, g1, g2, w_in, b_gate.reshape(1, 2 * d), *tail_ws)


def kernel(x, c, w_ada, b_ada, g_norm1, g_norm2, w_in, b_gate, g_qa, g_ka, g_qb, g_kb, rpb,
           w_proj_a, w_proj_b, w_o, w_ffn_in, w_ffn_out):
    depth = w_ada.shape[0]
    bsz, d = c.shape
    for l in range(depth):
        mod, w_in_bf = _mod(c, w_ada[l], b_ada[l], w_in[l])
        x = _layer(x, mod.reshape(bsz, 6, d), g_norm1[l], g_norm2[l], w_in_bf, b_gate[l], g_qa[l], g_ka[l], g_qb[l],
                   g_kb[l], rpb[l], w_proj_a[l], w_proj_b[l], w_o[l], w_ffn_in[l], w_ffn_out[l])
    return x
```

```python
import functools

import jax
import jax.numpy as jnp
from jax import lax
from jax.experimental import pallas as pl
from jax.experimental.pallas import tpu as pltpu

f32 = jnp.float32
bf16 = jnp.bfloat16

D_MODEL = 1024
HEAD_DIM = 64
GRID_W = 64
NA_HEADS = 8
NA_KH = 8
NA_KW = 16
DIL_CONFIGS = ((128, 1), (512, 4), (2048, 16))
DIL_HPG = 4
DIL_HEADS = DIL_HPG * len(DIL_CONFIGS)
DIL_QBLOCK = 64
ROT_DIM = HEAD_DIM // 4
ROPE_THETA = 500000.0
D_FF = -(-8 * D_MODEL // (3 * 256)) * 256
EPS = 1e-6
NEG_INF = -1e30
LOG2E = 1.4426950408889634
WA = NA_HEADS * HEAD_DIM
WB = DIL_HEADS * HEAD_DIM
WB_OUT = DIL_HPG * HEAD_DIM
W_QKV = 3 * WA + 3 * WB

LANES = 128
COL_TILE = 256
DIL_TILE = 1024
QKV_ROW_SPLIT = 2
QKV_DOT_TILES = 2
DIL_UNROLL = 16
NA_ROWS = 16
TAIL_TM = 512
FF_CHUNK = 256
VMEM_LIMIT = 56 * 1024 * 1024

SAFE_STRIDE = 4

assert all((win // 2) // dil == DIL_QBLOCK for win, dil in DIL_CONFIGS)
assert all(dil <= SAFE_STRIDE or (dil % SAFE_STRIDE == 0 and dil // SAFE_STRIDE <= SAFE_STRIDE)
           for _, dil in DIL_CONFIGS)


def _params(sem):
    return pltpu.CompilerParams(dimension_semantics=sem, vmem_limit_bytes=VMEM_LIMIT)


def _const_spec(shape):
    nd = len(shape)
    return pl.BlockSpec(shape, lambda *_: (0,) * nd, pipeline_mode=pl.Buffered(1))


def _mod_kernel(ct_ref, w_ref, b_ref, win_ref, o_ref, win_o_ref):
    ct = ct_ref[...]
    act = ct * jax.nn.sigmoid(ct)
    w = w_ref[...]
    rows = [jnp.sum(act[:, b:b + 1] * w, axis=0, keepdims=True) for b in range(ct.shape[1])]
    o_ref[...] = jnp.concatenate(rows, axis=0) + b_ref[...]
    win_o_ref[...] = win_ref[...].astype(bf16)


def _mod(c, w_ada, b_ada, w_in):
    bsz, d = c.shape
    n = w_ada.shape[1]
    steps = 4
    tn = n // steps
    slab = pl.BlockSpec((w_in.shape[0] // steps, w_in.shape[1]), lambda j: (j, 0))
    return pl.pallas_call(
        _mod_kernel,
        grid=(steps,),
        in_specs=[pl.BlockSpec((d, bsz), lambda j: (0, 0)),
                  pl.BlockSpec((d, tn), lambda j: (0, j)),
                  pl.BlockSpec((1, tn), lambda j: (0, j)),
                  slab],
        out_specs=[pl.BlockSpec((bsz, tn), lambda j: (0, j)), slab],
        out_shape=[jax.ShapeDtypeStruct((bsz, n), f32), jax.ShapeDtypeStruct(w_in.shape, bf16)],
        compiler_params=_params(("arbitrary",)),
        name="mod",
    )(c.T, w_ada, b_ada.reshape(1, n), w_in)


def _modulated_norm(x, g, sc, sh):
    ms = jnp.mean(x * x, axis=-1, keepdims=True)
    y = x * lax.rsqrt(ms + EPS) * g
    return y * (1.0 + sc) + sh


_J_QB = 3 * WA // COL_TILE
_J_KB = _J_QB + WB // COL_TILE
_J_VB = _J_KB + WB // COL_TILE
_N_J = W_QKV // COL_TILE


def _qkv_tile_kind(j):
    if j < WA // COL_TILE:
        return 0, False, 1
    if j < 2 * WA // COL_TILE:
        return 1, False, 1
    if j < _J_QB:
        return None, False, 1
    dil = DIL_CONFIGS[(j - _J_QB) % len(DIL_CONFIGS)][1]
    if j < _J_KB:
        return 2, True, dil
    if j < _J_VB:
        return 3, True, dil
    return None, False, dil


_QKV_OUT_WIDTHS = (WA,) * 3 + (WB_OUT,) * (3 * len(DIL_CONFIGS))
_OUT_QA, _OUT_KA, _OUT_VA, _OUT_QB, _OUT_KB, _OUT_VB = 0, 1, 2, 3, 3 + len(DIL_CONFIGS), 3 + 2 * len(DIL_CONFIGS)


def _qkv_tile_out(j, o_refs):
    if j < _J_QB:
        per = WA // COL_TILE
        return o_refs[j // per], (j % per) * COL_TILE
    return o_refs[_OUT_QB + j - _J_QB], 0


def _qkv_kernel(n_cast, x_ref, mod_ref, g1_ref, w_ref, gvec_ref, ones_ref, cos_ref, ssin_ref, *rest):
    n_out = len(_QKV_OUT_WIDTHS)
    cast_in, o_refs = rest[:n_cast], rest[n_cast:n_cast + n_out]
    cast_out = rest[n_cast + n_out:2 * n_cast + n_out]
    h_scr, y_scr, z_scr = rest[2 * n_cast + n_out:]
    for src, dst in zip(cast_in, cast_out):
        dst[...] = src[...].astype(bf16)
    tm = x_ref.shape[1]
    nc = COL_TILE // LANES
    mod = mod_ref[0]
    h_scr[...] = _modulated_norm(x_ref[0], g1_ref[...], mod[1:2], mod[0:1]).astype(bf16)
    head_dim = lax.broadcasted_iota(jnp.int32, (tm // QKV_ROW_SPLIT, COL_TILE), 1) % HEAD_DIM
    first_half = head_dim < ROT_DIM // 2

    wide = {}
    for j in range(_N_J):
        gain_row, rotary, dil = _qkv_tile_kind(j)
        o_ref, c0 = _qkv_tile_out(j, o_refs)
        slot = j % y_scr.shape[0]
        for hb in range(QKV_ROW_SPLIT):
            rows = slice(hb * tm // QKV_ROW_SPLIT, (hb + 1) * tm // QKV_ROW_SPLIT)
            if j % QKV_DOT_TILES == 0:
                wcols = slice(j * COL_TILE, min(j + QKV_DOT_TILES, _N_J) * COL_TILE)
                wide[hb] = jnp.dot(h_scr[rows, :], w_ref[:, wcols], preferred_element_type=f32)
            sub = j % QKV_DOT_TILES
            y = wide[hb][:, sub * COL_TILE:(sub + 1) * COL_TILE]
            if gain_row is not None:
                ss = jnp.dot((y * y).astype(bf16), ones_ref[...], preferred_element_type=f32)
                y = y * lax.rsqrt(ss * (1.0 / HEAD_DIM) + EPS) * gvec_ref[gain_row:gain_row + 1, :]
            if rotary:
                cos, ssin = [jnp.concatenate([t_ref[rows, :]] * nc, axis=1) for t_ref in (cos_ref, ssin_ref)]
                half = ROT_DIM // 2
                up = pltpu.roll(y, COL_TILE - half, 1)
                dn = pltpu.roll(y, half, 1)
                y = y * cos + jnp.where(first_half, up, dn) * ssin
            if dil == 1:
                o_ref[0, rows, c0:c0 + COL_TILE] = y.astype(bf16)
            else:
                for c in range(nc):
                    y_scr[slot, c, rows, :] = y[:, c * LANES:(c + 1) * LANES]
        if dil == 1:
            continue
        seg = DIL_TILE // dil
        for c in range(nc):
            ocols = slice(c0 + c * LANES, c0 + (c + 1) * LANES)
            if dil <= SAFE_STRIDE:
                for r in range(dil):
                    rows = y_scr[slot, c, pl.ds(r, seg, stride=dil), :]
                    o_ref[0, r * seg:(r + 1) * seg, ocols] = rows.astype(bf16)
                continue
            hi = dil // SAFE_STRIDE
            sub = DIL_TILE // SAFE_STRIDE
            for r_lo in range(SAFE_STRIDE):
                z_scr[c, r_lo * sub:(r_lo + 1) * sub, :] = y_scr[slot, c, pl.ds(r_lo, sub, stride=SAFE_STRIDE), :]
            for r_lo in range(SAFE_STRIDE):
                for r_hi in range(hi):
                    r = SAFE_STRIDE * r_hi + r_lo
                    rows = z_scr[c, pl.ds(r_lo * sub + r_hi, seg, stride=hi), :]
                    o_ref[0, r * seg:(r + 1) * seg, ocols] = rows.astype(bf16)


def _qkv(x, mod6, g1, w_qkv, gvec, ones_bd, rot_tables, cast_ws, tm=DIL_TILE):
    bsz, s, d = x.shape
    assert tm == DIL_TILE
    nt = s // tm
    steps = bsz * nt
    slab_specs = [pl.BlockSpec((w.shape[0] // steps, w.shape[1]), lambda b, i: (b * nt + i, 0)) for w in cast_ws]
    outs = pl.pallas_call(
        functools.partial(_qkv_kernel, len(cast_ws)),
        grid=(bsz, nt),
        in_specs=[pl.BlockSpec((1, tm, d), lambda b, i: (b, i, 0)),
                  pl.BlockSpec((1, 6, d), lambda b, i: (b, 0, 0)),
                  _const_spec(g1.shape),
                  pl.BlockSpec((d, W_QKV), lambda b, i: (0, 0), pipeline_mode=pl.Buffered(1)),
                  _const_spec(gvec.shape),
                  _const_spec(ones_bd.shape),
                  ] + [pl.BlockSpec((tm, LANES), lambda b, i: (i, 0))] * len(rot_tables) + slab_specs,
        out_specs=[pl.BlockSpec((1, tm, w), lambda b, i: (b, i, 0)) for w in _QKV_OUT_WIDTHS] + slab_specs,
        out_shape=[jax.ShapeDtypeStruct((bsz, s, w), bf16) for w in _QKV_OUT_WIDTHS]
        + [jax.ShapeDtypeStruct(w.shape, bf16) for w in cast_ws],
        scratch_shapes=[pltpu.VMEM((tm, d), bf16),
                        pltpu.VMEM((2, COL_TILE // LANES, tm, LANES), f32),
                        pltpu.VMEM((COL_TILE // LANES, tm, LANES), f32)],
        compiler_params=_params(("parallel", "parallel")),
        name="qkv",
    )(x, mod6, g1, w_qkv, gvec, ones_bd, *rot_tables, *cast_ws)
    n_out = len(_QKV_OUT_WIDTHS)
    return outs[:n_out], outs[n_out:]


def _pair_mask(nq):
    row = lax.broadcasted_iota(jnp.int32, (2 * nq, LANES), 0)
    lane = lax.broadcasted_iota(jnp.int32, (2 * nq, LANES), 1)
    return (row < nq) == (lane < HEAD_DIM)


def _stack_pair(qp, own_head):
    q2 = jnp.concatenate([qp, qp], axis=0)
    return jnp.where(own_head, q2, jnp.zeros_like(q2))


def _na_kernel(q_ref, kp_ref, kc_ref, kn_ref, vp_ref, vc_ref, vn_ref, tbl_ref, o_ref, kwin, vwin):
    rb = pl.program_id(1)
    blk = NA_ROWS * GRID_W
    n_rows = pl.num_programs(1) * NA_ROWS
    for t, (kr, vr) in enumerate(((kp_ref, vp_ref), (kc_ref, vc_ref), (kn_ref, vn_ref))):
        kwin[t * blk:(t + 1) * blk, :] = kr[0]
        vwin[t * blk:(t + 1) * blk, :] = vr[0]
    first_head = lax.broadcasted_iota(jnp.int32, (GRID_W, LANES), 1) < HEAD_DIM
    own_head = _pair_mask(GRID_W)
    nkeys = NA_KH * GRID_W
    npair = NA_HEADS // 2
    pair_rows = 2 * GRID_W

    def row_body(a, carry):
        r = rb * NA_ROWS + a
        row_start = jnp.clip(r - NA_KH // 2, 0, n_rows - NA_KH)
        delta = r - row_start
        off = pl.multiple_of((row_start - (rb - 1) * NA_ROWS) * GRID_W, GRID_W)
        qoff = pl.multiple_of(a * GRID_W, GRID_W)
        s_parts = []
        for hp in range(npair):
            cols = slice(hp * LANES, (hp + 1) * LANES)
            q2 = _stack_pair(q_ref[0, pl.ds(qoff, GRID_W), cols], own_head)
            kk = kwin[pl.ds(off, nkeys), cols]
            s_parts.append(lax.dot_general(q2, kk, (((1,), (1,)), ((), ())), preferred_element_type=f32))
        n_off = 2 * NA_KH - 2
        bias = jnp.concatenate(
            [jnp.concatenate([tbl_ref[h * n_off + 2 * kp - delta + NA_KH - 1] for kp in range(NA_KH // 2)], axis=1)
             for h in range(NA_HEADS)], axis=0)
        s = jnp.concatenate(s_parts, axis=0) + bias
        m = jnp.max(s, axis=-1, keepdims=True)
        p = jnp.exp2(s - m)
        inv = 1.0 / jnp.sum(p, axis=-1, keepdims=True)
        pb = p.astype(bf16)
        for hp in range(npair):
            cols = slice(hp * LANES, (hp + 1) * LANES)
            rows = slice(hp * pair_rows, (hp + 1) * pair_rows)
            vv = vwin[pl.ds(off, nkeys), cols]
            pv = jnp.dot(pb[rows], vv, preferred_element_type=f32) * inv[rows]
            o_ref[0, pl.ds(qoff, GRID_W), cols] = jnp.where(first_head, pv[:GRID_W], pv[GRID_W:]).astype(bf16)
        return carry

    lax.fori_loop(0, NA_ROWS, row_body, 0, unroll=NA_ROWS)


def _na(q, k, v, tbl):
    bsz, s, _ = q.shape
    blk = NA_ROWS * GRID_W
    nb = s // blk
    qspec = pl.BlockSpec((1, blk, WA), lambda b, i: (b, i, 0))
    halo = [pl.BlockSpec((1, blk, WA), lambda b, i: (b, jnp.maximum(i - 1, 0), 0)),
            pl.BlockSpec((1, blk, WA), lambda b, i: (b, i, 0)),
            pl.BlockSpec((1, blk, WA), lambda b, i: (b, jnp.minimum(i + 1, nb - 1), 0))]

    return pl.pallas_call(
        _na_kernel,
        grid=(bsz, nb),
        in_specs=[qspec] + halo + halo + [_const_spec(tbl.shape)],
        out_specs=pl.BlockSpec((1, blk, WA), lambda b, i: (b, i, 0)),
        out_shape=jax.ShapeDtypeStruct((bsz, s, WA), bf16),
        scratch_shapes=[pltpu.VMEM((3 * blk, WA), bf16), pltpu.VMEM((3 * blk, WA), bf16)],
        compiler_params=_params(("parallel", "parallel")),
        name="na",
    )(q, k, k, k, v, v, v, tbl)


def _na_bias_table(rpb):
    col = jnp.arange(GRID_W)
    col_start = jnp.clip(col - NA_KW // 2, 0, GRID_W - NA_KW)
    cmask = (col[None, :] >= col_start[:, None]) & (col[None, :] < col_start[:, None] + NA_KW)
    col_off = jnp.clip(col[None, :] - col[:, None] + (NA_KW - 1), 0, 2 * NA_KW - 2)
    onehot = (col_off[None] == jnp.arange(2 * NA_KW - 1)[:, None, None]).astype(f32)
    t = jnp.einsum('hrc,cqk->hrqk', rpb.astype(f32), onehot, precision=lax.Precision.HIGHEST)
    t = jnp.where(cmask, t * LOG2E, NEG_INF)
    t = jnp.concatenate([t[:, :-1], t[:, 1:]], axis=-1)
    return t.reshape(NA_HEADS * (2 * NA_KH - 2), GRID_W, 2 * GRID_W)


def _dil_halo_pieces(dil):
    return 1 if DIL_TILE // dil == DIL_QBLOCK else dil


def _dil_kernel(*refs):
    ng = len(DIL_CONFIGS)
    n_in = sum(1 + 2 * (1 + 2 * _dil_halo_pieces(dil)) for _, dil in DIL_CONFIGS)
    in_refs = list(refs[:n_in])
    o_ref = refs[n_in]
    o_scr, m_scr, l_scr = refs[n_in + 1:]
    ti = pl.program_id(1)
    n_tiles = pl.num_programs(1)
    qb = DIL_QBLOCK
    span = 3 * qb

    first_head = lax.broadcasted_iota(jnp.int32, (qb, LANES), 1) < HEAD_DIM
    own_head = _pair_mask(qb)
    npair = DIL_HPG // 2
    qi = lax.broadcasted_iota(jnp.int32, (qb, span), 0)
    kj = lax.broadcasted_iota(jnp.int32, (qb, span), 1)
    band = jnp.where((kj >= qi) & (kj <= qi + 2 * qb), 0.0, NEG_INF).astype(f32)
    kcol = lax.broadcasted_iota(jnp.int32, (1, span), 1)
    pen_lo = jnp.where(kcol < qb, jnp.where(ti == 0, NEG_INF, 0.0), 0.0).astype(f32)
    pen_hi = jnp.where(kcol >= 2 * qb, jnp.where(ti == n_tiles - 1, NEG_INF, 0.0), 0.0).astype(f32)

    for g, (_, dil) in enumerate(DIL_CONFIGS):
        seg = DIL_TILE // dil
        nj = seg // qb
        nh = _dil_halo_pieces(dil)
        q_ref = in_refs.pop(0)
        kc, kp, kn = in_refs.pop(0), [in_refs.pop(0) for _ in range(nh)], [in_refs.pop(0) for _ in range(nh)]
        vc, vp, vn = in_refs.pop(0), [in_refs.pop(0) for _ in range(nh)], [in_refs.pop(0) for _ in range(nh)]

        def halo(pieces, r, cols, nh=nh):
            return pieces[r][0, :, cols] if nh > 1 else pieces[0][0, r * qb:(r + 1) * qb, cols]

        def window(refs3, r, j, cols, seg=seg, nj=nj, halo=halo):
            p_refs, c_ref, n_refs = refs3
            base = r * seg
            if 0 < j < nj - 1:
                return c_ref[0, base + (j - 1) * qb:base + (j + 2) * qb, cols]
            lo = halo(p_refs, r, cols) if j == 0 else c_ref[0, base + (j - 1) * qb:base + j * qb, cols]
            mid = c_ref[0, base + j * qb:base + (j + 1) * qb, cols]
            hi = halo(n_refs, r, cols) if j == nj - 1 else c_ref[0, base + (j + 1) * qb:base + (j + 2) * qb, cols]
            return jnp.concatenate([lo, mid, hi], axis=0)

        for t0 in range(0, dil * nj, DIL_UNROLL):
            blocks = [divmod(t, nj) for t in range(t0, t0 + DIL_UNROLL)]
            s_parts = []
            for k, (r, j) in enumerate(blocks):
                mask = band
                if j == 0:
                    mask = mask + pen_lo
                if j == nj - 1:
                    mask = mask + pen_hi
                for hp in range(npair):
                    cols = slice(hp * LANES, (hp + 1) * LANES)
                    q2 = _stack_pair(q_ref[0, (t0 + k) * qb:(t0 + k + 1) * qb, cols], own_head)
                    kk = window((kp, kc, kn), r, j, cols)
                    s2 = lax.dot_general(q2, kk, (((1,), (1,)), ((), ())), preferred_element_type=f32)
                    s_parts += [s2[:qb] + mask, s2[qb:] + mask]
            s = jnp.concatenate(s_parts, axis=0)
            m = jnp.max(s, axis=-1, keepdims=True)
            p = jnp.exp2(s - m)
            l = jnp.sum(p, axis=-1, keepdims=True)
            pb = p.astype(bf16)
            for k, (r, j) in enumerate(blocks):
                two_pass = dil > SAFE_STRIDE
                if two_pass:
                    hi, sub = dil // SAFE_STRIDE, DIL_TILE // SAFE_STRIDE
                    store_rows = pl.ds((r % SAFE_STRIDE) * sub + j * qb * hi + r // SAFE_STRIDE, qb, stride=hi)
                else:
                    store_rows = pl.ds(j * qb * dil + r, qb, stride=dil) if dil > 1 else pl.ds(j * qb, qb)
                for hp in range(npair):
                    cols = slice(hp * LANES, (hp + 1) * LANES)
                    r0 = (k * npair + hp) * 2 * qb
                    vv = window((vp, vc, vn), r, j, cols)
                    pv = jnp.dot(pb[r0:r0 + 2 * qb], vv, preferred_element_type=f32)
                    slot = (ng if two_pass else g) * npair + hp
                    o_scr[slot, store_rows, :] = jnp.where(first_head, pv[:qb], pv[qb:])
                    m_scr[slot, store_rows, :] = jnp.where(first_head, m[r0:r0 + qb], m[r0 + qb:r0 + 2 * qb])
                    l_scr[slot, store_rows, :] = jnp.where(first_head, l[r0:r0 + qb], l[r0 + qb:r0 + 2 * qb])
        if dil > SAFE_STRIDE:
            sub = DIL_TILE // SAFE_STRIDE
            for hp in range(npair):
                for scr in (o_scr, m_scr, l_scr):
                    for r_lo in range(SAFE_STRIDE):
                        scr[g * npair + hp, pl.ds(r_lo, sub, stride=SAFE_STRIDE), :] = (
                            scr[ng * npair + hp, r_lo * sub:(r_lo + 1) * sub, :])

    for hp in range(npair):
        m_all = m_scr[hp]
        for g in range(1, ng):
            m_all = jnp.maximum(m_all, m_scr[g * npair + hp])
        num = jnp.zeros_like(m_all)
        den = jnp.zeros_like(m_all)
        for g in range(ng):
            w = jnp.exp2(m_scr[g * npair + hp] - m_all)
            num = num + w * o_scr[g * npair + hp]
            den = den + w * l_scr[g * npair + hp]
        o_ref[0, :, hp * LANES:(hp + 1) * LANES] = (num / den).astype(bf16)


def _dil(qs, ks, vs):
    bsz, s, _ = qs[0].shape
    nt = s // DIL_TILE
    ng = len(DIL_CONFIGS)
    qb = DIL_QBLOCK
    per_tile = DIL_TILE // qb
    blk = (1, DIL_TILE, WB_OUT)
    cur = pl.BlockSpec(blk, lambda b, i: (b, i, 0))
    in_specs, args = [], []
    for g, (_, dil) in enumerate(DIL_CONFIGS):
        seg = DIL_TILE // dil
        if _dil_halo_pieces(dil) == 1 and seg == qb:
            prev = [pl.BlockSpec(blk, lambda b, i: (b, jnp.maximum(i - 1, 0), 0))]
            nxt = [pl.BlockSpec(blk, lambda b, i: (b, jnp.minimum(i + 1, nt - 1), 0))]
        else:
            prev = [pl.BlockSpec((1, qb, WB_OUT), lambda b, i, o=(r + 1) * seg // qb - 1:
                                 (b, jnp.maximum((i - 1) * per_tile + o, 0), 0)) for r in range(dil)]
            nxt = [pl.BlockSpec((1, qb, WB_OUT), lambda b, i, o=r * seg // qb:
                                (b, jnp.minimum((i + 1) * per_tile + o, s // qb - 1), 0)) for r in range(dil)]
        in_specs += [cur] + 2 * ([cur] + prev + nxt)
        args += [qs[g]] + [ks[g]] * (1 + len(prev) + len(nxt)) + [vs[g]] * (1 + len(prev) + len(nxt))
    return pl.pallas_call(
        _dil_kernel,
        grid=(bsz, nt),
        in_specs=in_specs,
        out_specs=pl.BlockSpec(blk, lambda b, i: (b, i, 0)),
        out_shape=jax.ShapeDtypeStruct((bsz, s, WB_OUT), bf16),
        scratch_shapes=[pltpu.VMEM(((ng + 1) * DIL_HPG // 2, DIL_TILE, LANES), f32)] * 3,
        compiler_params=_params(("parallel", "parallel")),
        name="dil",
    )(*args)


def _tail_kernel(x_ref, oa_ref, ob_ref, mod_ref, g1_ref, g2_ref, bg_ref, wpa_ref, wpb_ref,
                 wo_ref, win_ref, wout_ref, *rest):
    d = x_ref.shape[2]
    n_gate = 2 * d // COL_TILE
    wg_refs = rest[:n_gate]
    o_ref, h_scr, m_scr, act_scr = rest[n_gate:]
    x = x_ref[0]
    mod = mod_ref[0]
    sh1, sc1, gt1, sh2, sc2, gt2 = [mod[k:k + 1] for k in range(6)]
    h_scr[...] = _modulated_norm(x, g1_ref[...], sc1, sh1).astype(bf16)
    oa = oa_ref[0]
    ob = ob_ref[0]
    for n in range(d // COL_TILE):
        ca = slice(n * COL_TILE, (n + 1) * COL_TILE)
        cb = slice(d + n * COL_TILE, d + (n + 1) * COL_TILE)
        wga, wgb = wg_refs[n], wg_refs[n_gate // 2 + n]
        ga = jax.nn.sigmoid(jnp.dot(h_scr[...], wga[...], preferred_element_type=f32) + bg_ref[:, ca])
        gb = jax.nn.sigmoid(jnp.dot(h_scr[...], wgb[...], preferred_element_type=f32) + bg_ref[:, cb])
        pa = jnp.dot(oa, wpa_ref[:, ca], preferred_element_type=f32)
        pb = jnp.dot(ob, wpb_ref[:, ca], preferred_element_type=f32)
        m_scr[:, ca] = (ga * pa + gb * pb).astype(bf16)
    x1 = x + gt1 * jnp.dot(m_scr[...], wo_ref[...], preferred_element_type=f32)
    h_scr[...] = _modulated_norm(x1, g2_ref[...], sc2, sh2).astype(bf16)
    for f in range(D_FF // FF_CHUNK):
        ca = slice(f * FF_CHUNK, (f + 1) * FF_CHUNK)
        cu = slice(D_FF + f * FF_CHUNK, D_FF + (f + 1) * FF_CHUNK)
        a = jnp.dot(h_scr[...], win_ref[:, ca], preferred_element_type=f32)
        u = jnp.dot(h_scr[...], win_ref[:, cu], preferred_element_type=f32)
        act_scr[:, ca] = ((a * jax.nn.sigmoid(a)) * u).astype(bf16)
    o_ref[0] = x1 + gt2 * jnp.dot(act_scr[...], wout_ref[...], preferred_element_type=f32)


def _tail(x, o_a, o_b, mod6, g1, g2, w_in, bg, wpa, wpb, wo, w_ffn_in, w_ffn_out, tm=TAIL_TM):
    bsz, s, d = x.shape
    tok = lambda w: pl.BlockSpec((1, tm, w), lambda b, i: (b, i, 0))
    consts = [g1, g2, bg, wpa, wpb, wo, w_ffn_in, w_ffn_out]
    n_gate = 2 * d // COL_TILE
    gate_specs = [pl.BlockSpec((d, COL_TILE), lambda b, i, c=_N_J + n: (0, c), pipeline_mode=pl.Buffered(1))
                  for n in range(n_gate)]
    return pl.pallas_call(
        _tail_kernel,
        grid=(bsz, s // tm),
        in_specs=[tok(d), tok(WA), tok(WB_OUT), pl.BlockSpec((1, 6, d), lambda b, i: (b, 0, 0))]
        + [_const_spec(c.shape) for c in consts] + gate_specs,
        out_specs=tok(d),
        out_shape=jax.ShapeDtypeStruct((bsz, s, d), f32),
        scratch_shapes=[pltpu.VMEM((tm, d), bf16), pltpu.VMEM((tm, d), bf16), pltpu.VMEM((tm, D_FF), bf16)],
        compiler_params=_params(("parallel", "parallel")),
        name="tail",
    )(x, o_a, o_b, mod6, *consts, *([w_in] * n_gate))


def _rotary_tables(s):
    half = ROT_DIM // 2
    inv_freq = ROPE_THETA ** (-(jnp.arange(half, dtype=f32) * 2.0) / ROT_DIM)
    dim = jnp.arange(LANES) % HEAD_DIM
    freq = jnp.where(dim < ROT_DIM, inv_freq[dim % half], 0.0)
    ang = jnp.arange(s).astype(f32)[:, None] * freq[None, :]
    cos, sin = jnp.cos(ang), jnp.sin(ang)
    return cos, jnp.where(dim < half, -sin, sin)


def _layer(x, mod6, g_norm1, g_norm2, w_in, b_gate, g_qa, g_ka, g_qb, g_kb, rpb,
           w_proj_a, w_proj_b, w_o, w_ffn_in, w_ffn_out):
    bsz, s, d = x.shape
    scale = HEAD_DIM ** -0.5 * LOG2E
    rep = COL_TILE // HEAD_DIM
    gvec = jnp.stack([jnp.tile(g_qa * scale, rep), jnp.tile(g_ka, rep),
                      jnp.tile(g_qb * scale, rep), jnp.tile(g_kb, rep)]).astype(f32)
    hid = jnp.arange(COL_TILE) // HEAD_DIM
    ones_bd = (hid[:, None] == hid[None, :]).astype(bf16)
    rot_tables = _rotary_tables(s)
    g1 = g_norm1.reshape(1, d)
    g2 = g_norm2.reshape(1, d)

    qkv, tail_ws = _qkv(x, mod6, g1, w_in, gvec, ones_bd, rot_tables,
                        [w_proj_a, w_proj_b, w_o, w_ffn_in, w_ffn_out])
    ng = len(DIL_CONFIGS)
    o_a = _na(qkv[_OUT_QA], qkv[_OUT_KA], qkv[_OUT_VA], _na_bias_table(rpb))
    o_b = _dil(qkv[_OUT_QB:_OUT_QB + ng], qkv[_OUT_KB:_OUT_KB + ng], qkv[_OUT_VB:_OUT_VB + ng])

    return _tail(x, o_a, o_b, mod6, g1, g2, w_in, b_gate.reshape(1, 2 * d), *tail_ws)


def kernel(x, c, w_ada, b_ada, g_norm1, g_norm2, w_in, b_gate, g_qa, g_ka, g_qb, g_kb, rpb,
           w_proj_a, w_proj_b, w_o, w_ffn_in, w_ffn_out):
    depth = w_ada.shape[0]
    bsz, d = c.shape
    for l in range(depth):
        mod, w_in_bf = _mod(c, w_ada[l], b_ada[l], w_in[l])
        x = _layer(x, mod.reshape(bsz, 6, d), g_norm1[l], g_norm2[l], w_in_bf, b_gate[l], g_qa[l], g_ka[l], g_qb[l],
                   g_kb[l], rpb[l], w_proj_a[l], w_proj_b[l], w_o[l], w_ffn_in[l], w_ffn_out[l])
    return x
```

```python
import functools

import jax
import jax.numpy as jnp
from jax import lax
from jax.experimental import pallas as pl
from jax.experimental.pallas import tpu as pltpu

f32 = jnp.float32
bf16 = jnp.bfloat16

D_MODEL = 1024
HEAD_DIM = 64
GRID_W = 64
NA_HEADS = 8
NA_KH = 8
NA_KW = 16
DIL_CONFIGS = ((128, 1), (512, 4), (2048, 16))
DIL_HPG = 4
DIL_HEADS = DIL_HPG * len(DIL_CONFIGS)
DIL_QBLOCK = 64
ROT_DIM = HEAD_DIM // 4
ROPE_THETA = 500000.0
D_FF = -(-8 * D_MODEL // (3 * 256)) * 256
EPS = 1e-6
NEG_INF = -1e30
LOG2E = 1.4426950408889634
WA = NA_HEADS * HEAD_DIM
WB = DIL_HEADS * HEAD_DIM
WB_OUT = DIL_HPG * HEAD_DIM
W_QKV = 3 * WA + 3 * WB

LANES = 128
COL_TILE = 256
DIL_TILE = 1024
QKV_ROW_SPLIT = 2
QKV_DOT_TILES = 2
DIL_UNROLL = 16
NA_ROWS = 16
TAIL_TM = 512
FF_CHUNK = 256
VMEM_LIMIT = 56 * 1024 * 1024

SAFE_STRIDE = 4

assert all((win // 2) // dil == DIL_QBLOCK for win, dil in DIL_CONFIGS)
assert all(dil <= SAFE_STRIDE or (dil % SAFE_STRIDE == 0 and dil // SAFE_STRIDE <= SAFE_STRIDE)
           for _, dil in DIL_CONFIGS)


def _params(sem):
    return pltpu.CompilerParams(dimension_semantics=sem, vmem_limit_bytes=VMEM_LIMIT)


def _const_spec(shape):
    nd = len(shape)
    return pl.BlockSpec(shape, lambda *_: (0,) * nd, pipeline_mode=pl.Buffered(1))


def _mod_kernel(ct_ref, w_ref, b_ref, win_ref, o_ref, win_o_ref):
    ct = ct_ref[...]
    act = ct * jax.nn.sigmoid(ct)
    w = w_ref[...]
    rows = [jnp.sum(act[:, b:b + 1] * w, axis=0, keepdims=True) for b in range(ct.shape[1])]
    o_ref[...] = jnp.concatenate(rows, axis=0) + b_ref[...]
    win_o_ref[...] = win_ref[...].astype(bf16)


def _mod(c, w_ada, b_ada, w_in):
    bsz, d = c.shape
    n = w_ada.shape[1]
    steps = 4
    tn = n // steps
    slab = pl.BlockSpec((w_in.shape[0] // steps, w_in.shape[1]), lambda j: (j, 0))
    return pl.pallas_call(
        _mod_kernel,
        grid=(steps,),
        in_specs=[pl.BlockSpec((d, bsz), lambda j: (0, 0)),
                  pl.BlockSpec((d, tn), lambda j: (0, j)),
                  pl.BlockSpec((1, tn), lambda j: (0, j)),
                  slab],
        out_specs=[pl.BlockSpec((bsz, tn), lambda j: (0, j)), slab],
        out_shape=[jax.ShapeDtypeStruct((bsz, n), f32), jax.ShapeDtypeStruct(w_in.shape, bf16)],
        compiler_params=_params(("arbitrary",)),
        name="mod",
    )(c.T, w_ada, b_ada.reshape(1, n), w_in)


def _modulated_norm(x, g, sc, sh):
    ms = jnp.mean(x * x, axis=-1, keepdims=True)
    y = x * lax.rsqrt(ms + EPS) * g
    return y * (1.0 + sc) + sh


_J_QB = 3 * WA // COL_TILE
_J_KB = _J_QB + WB // COL_TILE
_J_VB = _J_KB + WB // COL_TILE
_N_J = W_QKV // COL_TILE


def _qkv_tile_kind(j):
    if j < WA // COL_TILE:
        return 0, False, 1
    if j < 2 * WA // COL_TILE:
        return 1, False, 1
    if j < _J_QB:
        return None, False, 1
    dil = DIL_CONFIGS[(j - _J_QB) % len(DIL_CONFIGS)][1]
    if j < _J_KB:
        return 2, True, dil
    if j < _J_VB:
        return 3, True, dil
    return None, False, dil


_QKV_OUT_WIDTHS = (WA,) * 3 + (WB_OUT,) * (3 * len(DIL_CONFIGS))
_OUT_QA, _OUT_KA, _OUT_VA, _OUT_QB, _OUT_KB, _OUT_VB = 0, 1, 2, 3, 3 + len(DIL_CONFIGS), 3 + 2 * len(DIL_CONFIGS)


def _qkv_tile_out(j, o_refs):
    if j < _J_QB:
        per = WA // COL_TILE
        return o_refs[j // per], (j % per) * COL_TILE
    return o_refs[_OUT_QB + j - _J_QB], 0


def _qkv_kernel(n_cast, x_ref, mod_ref, g1_ref, w_ref, gvec_ref, ones_ref, cos_ref, ssin_ref, *rest):
    n_out = len(_QKV_OUT_WIDTHS)
    cast_in, o_refs = rest[:n_cast], rest[n_cast:n_cast + n_out]
    cast_out = rest[n_cast + n_out:2 * n_cast + n_out]
    h_scr, y_scr, z_scr = rest[2 * n_cast + n_out:]
    for src, dst in zip(cast_in, cast_out):
        dst[...] = src[...].astype(bf16)
    tm = x_ref.shape[1]
    nc = COL_TILE // LANES
    mod = mod_ref[0]
    h_scr[...] = _modulated_norm(x_ref[0], g1_ref[...], mod[1:2], mod[0:1]).astype(bf16)
    head_dim = lax.broadcasted_iota(jnp.int32, (tm // QKV_ROW_SPLIT, COL_TILE), 1) % HEAD_DIM
    first_half = head_dim < ROT_DIM // 2

    wide = {}
    for j in range(_N_J):
        gain_row, rotary, dil = _qkv_tile_kind(j)
        o_ref, c0 = _qkv_tile_out(j, o_refs)
        slot = j % y_scr.shape[0]
        for hb in range(QKV_ROW_SPLIT):
            rows = slice(hb * tm // QKV_ROW_SPLIT, (hb + 1) * tm // QKV_ROW_SPLIT)
            if j % QKV_DOT_TILES == 0:
                wcols = slice(j * COL_TILE, min(j + QKV_DOT_TILES, _N_J) * COL_TILE)
                wide[hb] = jnp.dot(h_scr[rows, :], w_ref[:, wcols], preferred_element_type=f32)
            sub = j % QKV_DOT_TILES
            y = wide[hb][:, sub * COL_TILE:(sub + 1) * COL_TILE]
            if gain_row is not None:
                ss = jnp.dot((y * y).astype(bf16), ones_ref[...], preferred_element_type=f32)
                y = y * lax.rsqrt(ss * (1.0 / HEAD_DIM) + EPS) * gvec_ref[gain_row:gain_row + 1, :]
            if rotary:
                cos, ssin = [jnp.concatenate([t_ref[rows, :]] * nc, axis=1) for t_ref in (cos_ref, ssin_ref)]
                half = ROT_DIM // 2
                up = pltpu.roll(y, COL_TILE - half, 1)
                dn = pltpu.roll(y, half, 1)
                y = y * cos + jnp.where(first_half, up, dn) * ssin
            if dil == 1:
                o_ref[0, rows, c0:c0 + COL_TILE] = y.astype(bf16)
            else:
                for c in range(nc):
                    y_scr[slot, c, rows, :] = y[:, c * LANES:(c + 1) * LANES]
        if dil == 1:
            continue
        seg = DIL_TILE // dil
        for c in range(nc):
            ocols = slice(c0 + c * LANES, c0 + (c + 1) * LANES)
            if dil <= SAFE_STRIDE:
                for r in range(dil):
                    rows = y_scr[slot, c, pl.ds(r, seg, stride=dil), :]
                    o_ref[0, r * seg:(r + 1) * seg, ocols] = rows.astype(bf16)
                continue
            hi = dil // SAFE_STRIDE
            sub = DIL_TILE // SAFE_STRIDE
            for r_lo in range(SAFE_STRIDE):
                z_scr[c, r_lo * sub:(r_lo + 1) * sub, :] = y_scr[slot, c, pl.ds(r_lo, sub, stride=SAFE_STRIDE), :]
            for r_lo in range(SAFE_STRIDE):
                for r_hi in range(hi):
                    r = SAFE_STRIDE * r_hi + r_lo
                    rows = z_scr[c, pl.ds(r_lo * sub + r_hi, seg, stride=hi), :]
                    o_ref[0, r * seg:(r + 1) * seg, ocols] = rows.astype(bf16)


def _qkv(x, mod6, g1, w_qkv, gvec, ones_bd, rot_tables, cast_ws, tm=DIL_TILE):
    bsz, s, d = x.shape
    assert tm == DIL_TILE
    nt = s // tm
    steps = bsz * nt
    slab_specs = [pl.BlockSpec((w.shape[0] // steps, w.shape[1]), lambda b, i: (b * nt + i, 0)) for w in cast_ws]
    outs = pl.pallas_call(
        functools.partial(_qkv_kernel, len(cast_ws)),
        grid=(bsz, nt),
        in_specs=[pl.BlockSpec((1, tm, d), lambda b, i: (b, i, 0)),
                  pl.BlockSpec((1, 6, d), lambda b, i: (b, 0, 0)),
                  _const_spec(g1.shape),
                  pl.BlockSpec((d, W_QKV), lambda b, i: (0, 0), pipeline_mode=pl.Buffered(1)),
                  _const_spec(gvec.shape),
                  _const_spec(ones_bd.shape),
                  ] + [pl.BlockSpec((tm, LANES), lambda b, i: (i, 0))] * len(rot_tables) + slab_specs,
        out_specs=[pl.BlockSpec((1, tm, w), lambda b, i: (b, i, 0)) for w in _QKV_OUT_WIDTHS] + slab_specs,
        out_shape=[jax.ShapeDtypeStruct((bsz, s, w), bf16) for w in _QKV_OUT_WIDTHS]
        + [jax.ShapeDtypeStruct(w.shape, bf16) for w in cast_ws],
        scratch_shapes=[pltpu.VMEM((tm, d), bf16),
                        pltpu.VMEM((2, COL_TILE // LANES, tm, LANES), f32),
                        pltpu.VMEM((COL_TILE // LANES, tm, LANES), f32)],
        compiler_params=_params(("parallel", "parallel")),
        name="qkv",
    )(x, mod6, g1, w_qkv, gvec, ones_bd, *rot_tables, *cast_ws)
    n_out = len(_QKV_OUT_WIDTHS)
    return outs[:n_out], outs[n_out:]


def _pair_mask(nq):
    row = lax.broadcasted_iota(jnp.int32, (2 * nq, LANES), 0)
    lane = lax.broadcasted_iota(jnp.int32, (2 * nq, LANES), 1)
    return (row < nq) == (lane < HEAD_DIM)


def _stack_pair(qp, own_head):
    q2 = jnp.concatenate([qp, qp], axis=0)
    return jnp.where(own_head, q2, jnp.zeros_like(q2))


def _na_kernel(q_ref, kp_ref, kc_ref, kn_ref, vp_ref, vc_ref, vn_ref, tbl_ref, o_ref, kwin, vwin):
    rb = pl.program_id(1)
    blk = NA_ROWS * GRID_W
    n_rows = pl.num_programs(1) * NA_ROWS
    for t, (kr, vr) in enumerate(((kp_ref, vp_ref), (kc_ref, vc_ref), (kn_ref, vn_ref))):
        kwin[t * blk:(t + 1) * blk, :] = kr[0]
        vwin[t * blk:(t + 1) * blk, :] = vr[0]
    first_head = lax.broadcasted_iota(jnp.int32, (GRID_W, LANES), 1) < HEAD_DIM
    own_head = _pair_mask(GRID_W)
    nkeys = NA_KH * GRID_W
    npair = NA_HEADS // 2
    pair_rows = 2 * GRID_W

    def row_body(a, carry):
        r = rb * NA_ROWS + a
        row_start = jnp.clip(r - NA_KH // 2, 0, n_rows - NA_KH)
        delta = r - row_start
        off = pl.multiple_of((row_start - (rb - 1) * NA_ROWS) * GRID_W, GRID_W)
        qoff = pl.multiple_of(a * GRID_W, GRID_W)
        s_parts = []
        for hp in range(npair):
            cols = slice(hp * LANES, (hp + 1) * LANES)
            q2 = _stack_pair(q_ref[0, pl.ds(qoff, GRID_W), cols], own_head)
            kk = kwin[pl.ds(off, nkeys), cols]
            s_parts.append(lax.dot_general(q2, kk, (((1,), (1,)), ((), ())), preferred_element_type=f32))
        n_off = 2 * NA_KH - 2
        bias = jnp.concatenate(
            [jnp.concatenate([tbl_ref[h * n_off + 2 * kp - delta + NA_KH - 1] for kp in range(NA_KH // 2)], axis=1)
             for h in range(NA_HEADS)], axis=0)
        s = jnp.concatenate(s_parts, axis=0) + bias
        m = jnp.max(s, axis=-1, keepdims=True)
        p = jnp.exp2(s - m)
        inv = 1.0 / jnp.sum(p, axis=-1, keepdims=True)
        pb = p.astype(bf16)
        for hp in range(npair):
            cols = slice(hp * LANES, (hp + 1) * LANES)
            rows = slice(hp * pair_rows, (hp + 1) * pair_rows)
            vv = vwin[pl.ds(off, nkeys), cols]
            pv = jnp.dot(pb[rows], vv, preferred_element_type=f32) * inv[rows]
            o_ref[0, pl.ds(qoff, GRID_W), cols] = jnp.where(first_head, pv[:GRID_W], pv[GRID_W:]).astype(bf16)
        return carry

    lax.fori_loop(0, NA_ROWS, row_body, 0, unroll=NA_ROWS)


def _na(q, k, v, tbl):
    bsz, s, _ = q.shape
    blk = NA_ROWS * GRID_W
    nb = s // blk
    qspec = pl.BlockSpec((1, blk, WA), lambda b, i: (b, i, 0))
    halo = [pl.BlockSpec((1, blk, WA), lambda b, i: (b, jnp.maximum(i - 1, 0), 0)),
            pl.BlockSpec((1, blk, WA), lambda b, i: (b, i, 0)),
            pl.BlockSpec((1, blk, WA), lambda b, i: (b, jnp.minimum(i + 1, nb - 1), 0))]

    return pl.pallas_call(
        _na_kernel,
        grid=(bsz, nb),
        in_specs=[qspec] + halo + halo + [_const_spec(tbl.shape)],
        out_specs=pl.BlockSpec((1, blk, WA), lambda b, i: (b, i, 0)),
        out_shape=jax.ShapeDtypeStruct((bsz, s, WA), bf16),
        scratch_shapes=[pltpu.VMEM((3 * blk, WA), bf16), pltpu.VMEM((3 * blk, WA), bf16)],
        compiler_params=_params(("parallel", "parallel")),
        name="na",
    )(q, k, k, k, v, v, v, tbl)


def _na_bias_table(rpb):
    col = jnp.arange(GRID_W)
    col_start = jnp.clip(col - NA_KW // 2, 0, GRID_W - NA_KW)
    cmask = (col[None, :] >= col_start[:, None]) & (col[None, :] < col_start[:, None] + NA_KW)
    col_off = jnp.clip(col[None, :] - col[:, None] + (NA_KW - 1), 0, 2 * NA_KW - 2)
    onehot = (col_off[None] == jnp.arange(2 * NA_KW - 1)[:, None, None]).astype(f32)
    t = jnp.einsum('hrc,cqk->hrqk', rpb.astype(f32), onehot, precision=lax.Precision.HIGHEST)
    t = jnp.where(cmask, t * LOG2E, NEG_INF)
    t = jnp.concatenate([t[:, :-1], t[:, 1:]], axis=-1)
    return t.reshape(NA_HEADS * (2 * NA_KH - 2), GRID_W, 2 * GRID_W)


def _dil_halo_pieces(dil):
    return 1 if DIL_TILE // dil == DIL_QBLOCK else dil


def _dil_kernel(*refs):
    ng = len(DIL_CONFIGS)
    n_in = sum(1 + 2 * (1 + 2 * _dil_halo_pieces(dil)) for _, dil in DIL_CONFIGS)
    in_refs = list(refs[:n_in])
    o_ref = refs[n_in]
    o_scr, m_scr, l_scr = refs[n_in + 1:]
    ti = pl.program_id(1)
    n_tiles = pl.num_programs(1)
    qb = DIL_QBLOCK
    span = 3 * qb

    first_head = lax.broadcasted_iota(jnp.int32, (qb, LANES), 1) < HEAD_DIM
    own_head = _pair_mask(qb)
    npair = DIL_HPG // 2
    qi = lax.broadcasted_iota(jnp.int32, (qb, span), 0)
    kj = lax.broadcasted_iota(jnp.int32, (qb, span), 1)
    band = jnp.where((kj >= qi) & (kj <= qi + 2 * qb), 0.0, NEG_INF).astype(f32)
    kcol = lax.broadcasted_iota(jnp.int32, (1, span), 1)
    pen_lo = jnp.where(kcol < qb, jnp.where(ti == 0, NEG_INF, 0.0), 0.0).astype(f32)
    pen_hi = jnp.where(kcol >= 2 * qb, jnp.where(ti == n_tiles - 1, NEG_INF, 0.0), 0.0).astype(f32)

    for g, (_, dil) in enumerate(DIL_CONFIGS):
        seg = DIL_TILE // dil
        nj = seg // qb
        nh = _dil_halo_pieces(dil)
        q_ref = in_refs.pop(0)
        kc, kp, kn = in_refs.pop(0), [in_refs.pop(0) for _ in range(nh)], [in_refs.pop(0) for _ in range(nh)]
        vc, vp, vn = in_refs.pop(0), [in_refs.pop(0) for _ in range(nh)], [in_refs.pop(0) for _ in range(nh)]

        def halo(pieces, r, cols, nh=nh):
            return pieces[r][0, :, cols] if nh > 1 else pieces[0][0, r * qb:(r + 1) * qb, cols]

        def window(refs3, r, j, cols, seg=seg, nj=nj, halo=halo):
            p_refs, c_ref, n_refs = refs3
            base = r * seg
            if 0 < j < nj - 1:
                return c_ref[0, base + (j - 1) * qb:base + (j + 2) * qb, cols]
            lo = halo(p_refs, r, cols) if j == 0 else c_ref[0, base + (j - 1) * qb:base + j * qb, cols]
            mid = c_ref[0, base + j * qb:base + (j + 1) * qb, cols]
            hi = halo(n_refs, r, cols) if j == nj - 1 else c_ref[0, base + (j + 1) * qb:base + (j + 2) * qb, cols]
            return jnp.concatenate([lo, mid, hi], axis=0)

        for t0 in range(0, dil * nj, DIL_UNROLL):
            blocks = [divmod(t, nj) for t in range(t0, t0 + DIL_UNROLL)]
            s_parts = []
            for k, (r, j) in enumerate(blocks):
                mask = band
                if j == 0:
                    mask = mask + pen_lo
                if j == nj - 1:
                    mask = mask + pen_hi
                for hp in range(npair):
                    cols = slice(hp * LANES, (hp + 1) * LANES)
                    q2 = _stack_pair(q_ref[0, (t0 + k) * qb:(t0 + k + 1) * qb, cols], own_head)
                    kk = window((kp, kc, kn), r, j, cols)
                    s2 = lax.dot_general(q2, kk, (((1,), (1,)), ((), ())), preferred_element_type=f32)
                    s_parts += [s2[:qb] + mask, s2[qb:] + mask]
            s = jnp.concatenate(s_parts, axis=0)
            m = jnp.max(s, axis=-1, keepdims=True)
            p = jnp.exp2(s - m)
            l = jnp.sum(p, axis=-1, keepdims=True)
            pb = p.astype(bf16)
            for k, (r, j) in enumerate(blocks):
                two_pass = dil > SAFE_STRIDE
                if two_pass:
                    hi, sub = dil // SAFE_STRIDE, DIL_TILE // SAFE_STRIDE
                    store_rows = pl.ds((r % SAFE_STRIDE) * sub + j * qb * hi + r // SAFE_STRIDE, qb, stride=hi)
                else:
                    store_rows = pl.ds(j * qb * dil + r, qb, stride=dil) if dil > 1 else pl.ds(j * qb, qb)
                for hp in range(npair):
                    cols = slice(hp * LANES, (hp + 1) * LANES)
                    r0 = (k * npair + hp) * 2 * qb
                    vv = window((vp, vc, vn), r, j, cols)
                    pv = jnp.dot(pb[r0:r0 + 2 * qb], vv, preferred_element_type=f32)
                    slot = (ng if two_pass else g) * npair + hp
                    o_scr[slot, store_rows, :] = jnp.where(first_head, pv[:qb], pv[qb:])
                    m_scr[slot, store_rows, :] = jnp.where(first_head, m[r0:r0 + qb], m[r0 + qb:r0 + 2 * qb])
                    l_scr[slot, store_rows, :] = jnp.where(first_head, l[r0:r0 + qb], l[r0 + qb:r0 + 2 * qb])
        if dil > SAFE_STRIDE:
            sub = DIL_TILE // SAFE_STRIDE
            for hp in range(npair):
                for scr in (o_scr, m_scr, l_scr):
                    for r_lo in range(SAFE_STRIDE):
                        scr[g * npair + hp, pl.ds(r_lo, sub, stride=SAFE_STRIDE), :] = (
                            scr[ng * npair + hp, r_lo * sub:(r_lo + 1) * sub, :])

    for hp in range(npair):
        m_all = m_scr[hp]
        for g in range(1, ng):
            m_all = jnp.maximum(m_all, m_scr[g * npair + hp])
        num = jnp.zeros_like(m_all)
        den = jnp.zeros_like(m_all)
        for g in range(ng):
            w = jnp.exp2(m_scr[g * npair + hp] - m_all)
            num = num + w * o_scr[g * npair + hp]
            den = den + w * l_scr[g * npair + hp]
        o_ref[0, :, hp * LANES:(hp + 1) * LANES] = (num / den).astype(bf16)


def _dil(qs, ks, vs):
    bsz, s, _ = qs[0].shape
    nt = s // DIL_TILE
    ng = len(DIL_CONFIGS)
    qb = DIL_QBLOCK
    per_tile = DIL_TILE // qb
    blk = (1, DIL_TILE, WB_OUT)
    cur = pl.BlockSpec(blk, lambda b, i: (b, i, 0))
    in_specs, args = [], []
    for g, (_, dil) in enumerate(DIL_CONFIGS):
        seg = DIL_TILE // dil
        if _dil_halo_pieces(dil) == 1 and seg == qb:
            prev = [pl.BlockSpec(blk, lambda b, i: (b, jnp.maximum(i - 1, 0), 0))]
            nxt = [pl.BlockSpec(blk, lambda b, i: (b, jnp.minimum(i + 1, nt - 1), 0))]
        else:
            prev = [pl.BlockSpec((1, qb, WB_OUT), lambda b, i, o=(r + 1) * seg // qb - 1:
                                 (b, jnp.maximum((i - 1) * per_tile + o, 0), 0)) for r in range(dil)]
            nxt = [pl.BlockSpec((1, qb, WB_OUT), lambda b, i, o=r * seg // qb:
                                (b, jnp.minimum((i + 1) * per_tile + o, s // qb - 1), 0)) for r in range(dil)]
        in_specs += [cur] + 2 * ([cur] + prev + nxt)
        args += [qs[g]] + [ks[g]] * (1 + len(prev) + len(nxt)) + [vs[g]] * (1 + len(prev) + len(nxt))
    return pl.pallas_call(
        _dil_kernel,
        grid=(bsz, nt),
        in_specs=in_specs,
        out_specs=pl.BlockSpec(blk, lambda b, i: (b, i, 0)),
        out_shape=jax.ShapeDtypeStruct((bsz, s, WB_OUT), bf16),
        scratch_shapes=[pltpu.VMEM(((ng + 1) * DIL_HPG // 2, DIL_TILE, LANES), f32)] * 3,
        compiler_params=_params(("parallel", "parallel")),
        name="dil",
    )(*args)


def _tail_kernel(x_ref, oa_ref, ob_ref, mod_ref, g1_ref, g2_ref, bg_ref, wpa_ref, wpb_ref,
                 wo_ref, win_ref, wout_ref, *rest):
    d = x_ref.shape[2]
    n_gate = 2 * d // COL_TILE
    wg_refs = rest[:n_gate]
    o_ref, h_scr, m_scr, act_scr = rest[n_gate:]
    x = x_ref[0]
    mod = mod_ref[0]
    sh1, sc1, gt1, sh2, sc2, gt2 = [mod[k:k + 1] for k in range(6)]
    h_scr[...] = _modulated_norm(x, g1_ref[...], sc1, sh1).astype(bf16)
    oa = oa_ref[0]
    ob = ob_ref[0]
    for n in range(d // COL_TILE):
        ca = slice(n * COL_TILE, (n + 1) * COL_TILE)
        cb = slice(d + n * COL_TILE, d + (n + 1) * COL_TILE)
        wga, wgb = wg_refs[n], wg_refs[n_gate // 2 + n]
        ga = jax.nn.sigmoid(jnp.dot(h_scr[...], wga[...], preferred_element_type=f32) + bg_ref[:, ca])
        gb = jax.nn.sigmoid(jnp.dot(h_scr[...], wgb[...], preferred_element_type=f32) + bg_ref[:, cb])
        pa = jnp.dot(oa, wpa_ref[:, ca], preferred_element_type=f32)
        pb = jnp.dot(ob, wpb_ref[:, ca], preferred_element_type=f32)
        m_scr[:, ca] = (ga * pa + gb * pb).astype(bf16)
    x1 = x + gt1 * jnp.dot(m_scr[...], wo_ref[...], preferred_element_type=f32)
    h_scr[...] = _modulated_norm(x1, g2_ref[...], sc2, sh2).astype(bf16)
    for f in range(D_FF // FF_CHUNK):
        ca = slice(f * FF_CHUNK, (f + 1) * FF_CHUNK)
        cu = slice(D_FF + f * FF_CHUNK, D_FF + (f + 1) * FF_CHUNK)
        a = jnp.dot(h_scr[...], win_ref[:, ca], preferred_element_type=f32)
        u = jnp.dot(h_scr[...], win_ref[:, cu], preferred_element_type=f32)
        act_scr[:, ca] = ((a * jax.nn.sigmoid(a)) * u).astype(bf16)
    o_ref[0] = x1 + gt2 * jnp.dot(act_scr[...], wout_ref[...], preferred_element_type=f32)


def _tail(x, o_a, o_b, mod6, g1, g2, w_in, bg, wpa, wpb, wo, w_ffn_in, w_ffn_out, tm=TAIL_TM):
    bsz, s, d = x.shape
    tok = lambda w: pl.BlockSpec((1, tm, w), lambda b, i: (b, i, 0))
    consts = [g1, g2, bg, wpa, wpb, wo, w_ffn_in, w_ffn_out]
    n_gate = 2 * d // COL_TILE
    gate_specs = [pl.BlockSpec((d, COL_TILE), lambda b, i, c=_N_J + n: (0, c), pipeline_mode=pl.Buffered(1))
                  for n in range(n_gate)]
    return pl.pallas_call(
        _tail_kernel,
        grid=(bsz, s // tm),
        in_specs=[tok(d), tok(WA), tok(WB_OUT), pl.BlockSpec((1, 6, d), lambda b, i: (b, 0, 0))]
        + [_const_spec(c.shape) for c in consts] + gate_specs,
        out_specs=tok(d),
        out_shape=jax.ShapeDtypeStruct((bsz, s, d), f32),
        scratch_shapes=[pltpu.VMEM((tm, d), bf16), pltpu.VMEM((tm, d), bf16), pltpu.VMEM((tm, D_FF), bf16)],
        compiler_params=_params(("parallel", "parallel")),
        name="tail",
    )(x, o_a, o_b, mod6, *consts, *([w_in] * n_gate))


def _rotary_tables(s):
    half = ROT_DIM // 2
    inv_freq = ROPE_THETA ** (-(jnp.arange(half, dtype=f32) * 2.0) / ROT_DIM)
    ang = jnp.arange(s).astype(f32)[:, None] * inv_freq[None, :]
    cos, sin = jnp.cos(ang), jnp.sin(ang)
    rest = HEAD_DIM - ROT_DIM
    cos_h = jnp.concatenate([cos, cos, jnp.ones((s, rest), f32)], axis=1)
    ssin_h = jnp.concatenate([-sin, sin, jnp.zeros((s, rest), f32)], axis=1)
    rep = LANES // HEAD_DIM
    return jnp.tile(cos_h, (1, rep)), jnp.tile(ssin_h, (1, rep))


def _layer(x, mod6, g_norm1, g_norm2, w_in, b_gate, g_qa, g_ka, g_qb, g_kb, rpb,
           w_proj_a, w_proj_b, w_o, w_ffn_in, w_ffn_out):
    bsz, s, d = x.shape
    scale = HEAD_DIM ** -0.5 * LOG2E
    rep = COL_TILE // HEAD_DIM
    gvec = jnp.stack([jnp.tile(g_qa * scale, rep), jnp.tile(g_ka, rep),
                      jnp.tile(g_qb * scale, rep), jnp.tile(g_kb, rep)]).astype(f32)
    hid = jnp.arange(COL_TILE) // HEAD_DIM
    ones_bd = (hid[:, None] == hid[None, :]).astype(bf16)
    rot_tables = _rotary_tables(s)
    g1 = g_norm1.reshape(1, d)
    g2 = g_norm2.reshape(1, d)

    qkv, tail_ws = _qkv(x, mod6, g1, w_in, gvec, ones_bd, rot_tables,
                        [w_proj_a, w_proj_b, w_o, w_ffn_in, w_ffn_out])
    ng = len(DIL_CONFIGS)
    o_a = _na(qkv[_OUT_QA], qkv[_OUT_KA], qkv[_OUT_VA], _na_bias_table(rpb))
    o_b = _dil(qkv[_OUT_QB:_OUT_QB + ng], qkv[_OUT_KB:_OUT_KB + ng], qkv[_OUT_VB:_OUT_VB + ng])

    return _tail(x, o_a, o_b, mod6, g1, g2, w_in, b_gate.reshape(1, 2 * d), *tail_ws)


def kernel(x, c, w_ada, b_ada, g_norm1, g_norm2, w_in, b_gate, g_qa, g_ka, g_qb, g_kb, rpb,
           w_proj_a, w_proj_b, w_o, w_ffn_in, w_ffn_out):
    depth = w_ada.shape[0]
    bsz, d = c.shape
    for l in range(depth):
        mod, w_in_bf = _mod(c, w_ada[l], b_ada[l], w_in[l])
        x = _layer(x, mod.reshape(bsz, 6, d), g_norm1[l], g_norm2[l], w_in_bf, b_gate[l], g_qa[l], g_ka[l], g_qb[l],
                   g_kb[l], rpb[l], w_proj_a[l], w_proj_b[l], w_o[l], w_ffn_in[l], w_ffn_out[l])
    return x
```

```python
import functools

import jax
import jax.numpy as jnp
from jax import lax
from jax.experimental import pallas as pl
from jax.experimental.pallas import tpu as pltpu

f32 = jnp.float32
bf16 = jnp.bfloat16

D_MODEL = 1024
HEAD_DIM = 64
GRID_W = 64
NA_HEADS = 8
NA_KH = 8
NA_KW = 16
DIL_CONFIGS = ((128, 1), (512, 4), (2048, 16))
DIL_HPG = 4
DIL_HEADS = DIL_HPG * len(DIL_CONFIGS)
DIL_QBLOCK = 64
ROT_DIM = HEAD_DIM // 4
ROPE_THETA = 500000.0
D_FF = -(-8 * D_MODEL // (3 * 256)) * 256
EPS = 1e-6
NEG_INF = -1e30
LOG2E = 1.4426950408889634
WA = NA_HEADS * HEAD_DIM
WB = DIL_HEADS * HEAD_DIM
WB_OUT = DIL_HPG * HEAD_DIM
W_QKV = 3 * WA + 3 * WB

LANES = 128
BF16_ROWS = 16
COL_TILE = 256
DIL_TILE = 1024
QKV_ROW_SPLIT = 2
QKV_DOT_TILES = 2
DIL_UNROLL = 16
NA_ROWS = 16
MOD_STEPS = 8
TAIL_TM = 512
FF_CHUNK = 256
VMEM_LIMIT = 56 * 1024 * 1024

SAFE_STRIDE = 4

assert all((win // 2) // dil == DIL_QBLOCK for win, dil in DIL_CONFIGS)
assert all(dil <= SAFE_STRIDE or (dil % SAFE_STRIDE == 0 and dil // SAFE_STRIDE <= SAFE_STRIDE)
           for _, dil in DIL_CONFIGS)


def _params(sem):
    return pltpu.CompilerParams(dimension_semantics=sem, vmem_limit_bytes=VMEM_LIMIT)


def _const_spec(shape):
    nd = len(shape)
    return pl.BlockSpec(shape, lambda *_: (0,) * nd, pipeline_mode=pl.Buffered(1))


def _mod_kernel(ct_ref, w_ref, b_ref, win_ref, o_ref, win_o_ref):
    ct = ct_ref[...]
    act = ct * jax.nn.sigmoid(ct)
    w = w_ref[...]
    rows = [jnp.sum(act[:, b:b + 1] * w, axis=0, keepdims=True) for b in range(ct.shape[1])]
    o_ref[...] = jnp.concatenate(rows, axis=0) + b_ref[...]
    win_o_ref[...] = win_ref[...].astype(bf16)


def _mod(c, w_ada, b_ada, w_in):
    bsz, d = c.shape
    n = w_ada.shape[1]
    steps = MOD_STEPS
    tn = n // steps
    slab = pl.BlockSpec((w_in.shape[0] // steps, w_in.shape[1]), lambda j: (j, 0))
    return pl.pallas_call(
        _mod_kernel,
        grid=(steps,),
        in_specs=[pl.BlockSpec((d, bsz), lambda j: (0, 0)),
                  pl.BlockSpec((d, tn), lambda j: (0, j)),
                  pl.BlockSpec((1, tn), lambda j: (0, j)),
                  slab],
        out_specs=[pl.BlockSpec((bsz, tn), lambda j: (0, j)), slab],
        out_shape=[jax.ShapeDtypeStruct((bsz, n), f32), jax.ShapeDtypeStruct(w_in.shape, bf16)],
        compiler_params=_params(("arbitrary",)),
        name="mod",
    )(c.T, w_ada, b_ada.reshape(1, n), w_in)


def _modulated_norm(x, g, sc, sh):
    ms = jnp.mean(x * x, axis=-1, keepdims=True)
    y = x * lax.rsqrt(ms + EPS) * g
    return y * (1.0 + sc) + sh


_J_QB = 3 * WA // COL_TILE
_J_KB = _J_QB + WB // COL_TILE
_J_VB = _J_KB + WB // COL_TILE
_N_J = W_QKV // COL_TILE


def _qkv_tile_kind(j):
    if j < WA // COL_TILE:
        return 0, False, 1
    if j < 2 * WA // COL_TILE:
        return 1, False, 1
    if j < _J_QB:
        return None, False, 1
    dil = DIL_CONFIGS[(j - _J_QB) % len(DIL_CONFIGS)][1]
    if j < _J_KB:
        return 2, True, dil
    if j < _J_VB:
        return 3, True, dil
    return None, False, dil


_QKV_OUT_WIDTHS = (WA,) * 3 + (WB_OUT,) * (3 * len(DIL_CONFIGS))
_OUT_QA, _OUT_KA, _OUT_VA, _OUT_QB, _OUT_KB, _OUT_VB = 0, 1, 2, 3, 3 + len(DIL_CONFIGS), 3 + 2 * len(DIL_CONFIGS)


def _qkv_tile_out(j, o_refs):
    if j < _J_QB:
        per = WA // COL_TILE
        return o_refs[j // per], (j % per) * COL_TILE
    return o_refs[_OUT_QB + j - _J_QB], 0


def _qkv_kernel(n_cast, x_ref, mod_ref, g1_ref, w_ref, gvec_ref, ones_ref, cos_ref, ssin_ref, *rest):
    n_out = len(_QKV_OUT_WIDTHS)
    cast_in, o_refs = rest[:n_cast], rest[n_cast:n_cast + n_out]
    cast_out = rest[n_cast + n_out:2 * n_cast + n_out]
    h_scr, y_scr, z_scr = rest[2 * n_cast + n_out:]
    for src, dst in zip(cast_in, cast_out):
        dst[...] = src[...].astype(bf16)
    tm = x_ref.shape[1]
    nc = COL_TILE // LANES
    mod = mod_ref[0]
    h_scr[...] = _modulated_norm(x_ref[0], g1_ref[...], mod[1:2], mod[0:1]).astype(bf16)
    head_dim = lax.broadcasted_iota(jnp.int32, (tm // QKV_ROW_SPLIT, COL_TILE), 1) % HEAD_DIM
    first_half = head_dim < ROT_DIM // 2

    wide = {}
    for j in range(_N_J):
        gain_row, rotary, dil = _qkv_tile_kind(j)
        o_ref, c0 = _qkv_tile_out(j, o_refs)
        slot = j % y_scr.shape[0]
        for hb in range(QKV_ROW_SPLIT):
            rows = slice(hb * tm // QKV_ROW_SPLIT, (hb + 1) * tm // QKV_ROW_SPLIT)
            if j % QKV_DOT_TILES == 0:
                wcols = slice(j * COL_TILE, min(j + QKV_DOT_TILES, _N_J) * COL_TILE)
                wide[hb] = jnp.dot(h_scr[rows, :], w_ref[:, wcols], preferred_element_type=f32)
            sub = j % QKV_DOT_TILES
            y = wide[hb][:, sub * COL_TILE:(sub + 1) * COL_TILE]
            if gain_row is not None:
                ss = jnp.dot((y * y).astype(bf16), ones_ref[...], preferred_element_type=f32)
                y = y * lax.rsqrt(ss * (1.0 / HEAD_DIM) + EPS) * gvec_ref[gain_row:gain_row + 1, :]
            if rotary:
                cos, ssin = [jnp.concatenate([t_ref[rows, :]] * nc, axis=1) for t_ref in (cos_ref, ssin_ref)]
                half = ROT_DIM // 2
                up = pltpu.roll(y, COL_TILE - half, 1)
                dn = pltpu.roll(y, half, 1)
                y = y * cos + jnp.where(first_half, up, dn) * ssin
            if dil == 1:
                o_ref[0, rows, c0:c0 + COL_TILE] = y.astype(bf16)
            else:
                for c in range(nc):
                    y_scr[slot, c, rows, :] = y[:, c * LANES:(c + 1) * LANES]
        if dil == 1:
            continue
        seg = DIL_TILE // dil
        for c in range(nc):
            ocols = slice(c0 + c * LANES, c0 + (c + 1) * LANES)
            if dil <= SAFE_STRIDE:
                for r in range(dil):
                    rows = y_scr[slot, c, pl.ds(r, seg, stride=dil), :]
                    o_ref[0, r * seg:(r + 1) * seg, ocols] = rows.astype(bf16)
                continue
            hi = dil // SAFE_STRIDE
            sub = DIL_TILE // SAFE_STRIDE
            for r_lo in range(SAFE_STRIDE):
                z_scr[c, r_lo * sub:(r_lo + 1) * sub, :] = y_scr[slot, c, pl.ds(r_lo, sub, stride=SAFE_STRIDE), :]
            for r_lo in range(SAFE_STRIDE):
                for r_hi in range(hi):
                    r = SAFE_STRIDE * r_hi + r_lo
                    rows = z_scr[c, pl.ds(r_lo * sub + r_hi, seg, stride=hi), :]
                    o_ref[0, r * seg:(r + 1) * seg, ocols] = rows.astype(bf16)


def _qkv(x, mod6, g1, w_qkv, gvec, ones_bd, rot_tables, cast_ws, tm=DIL_TILE):
    bsz, s, d = x.shape
    assert tm == DIL_TILE
    nt = s // tm
    steps = bsz * nt
    assert all(w.shape[0] % (BF16_ROWS * steps) == 0 for w in cast_ws)
    slab_specs = [pl.BlockSpec((w.shape[0] // steps, w.shape[1]), lambda b, i: (b * nt + i, 0)) for w in cast_ws]
    outs = pl.pallas_call(
        functools.partial(_qkv_kernel, len(cast_ws)),
        grid=(bsz, nt),
        in_specs=[pl.BlockSpec((1, tm, d), lambda b, i: (b, i, 0)),
                  pl.BlockSpec((1, 6, d), lambda b, i: (b, 0, 0)),
                  _const_spec(g1.shape),
                  pl.BlockSpec((d, W_QKV), lambda b, i: (0, 0), pipeline_mode=pl.Buffered(1)),
                  _const_spec(gvec.shape),
                  _const_spec(ones_bd.shape),
                  ] + [pl.BlockSpec((tm, LANES), lambda b, i: (i, 0))] * len(rot_tables) + slab_specs,
        out_specs=[pl.BlockSpec((1, tm, w), lambda b, i: (b, i, 0)) for w in _QKV_OUT_WIDTHS] + slab_specs,
        out_shape=[jax.ShapeDtypeStruct((bsz, s, w), bf16) for w in _QKV_OUT_WIDTHS]
        + [jax.ShapeDtypeStruct(w.shape, bf16) for w in cast_ws],
        scratch_shapes=[pltpu.VMEM((tm, d), bf16),
                        pltpu.VMEM((2, COL_TILE // LANES, tm, LANES), f32),
                        pltpu.VMEM((COL_TILE // LANES, tm, LANES), f32)],
        compiler_params=_params(("parallel", "parallel")),
        name="qkv",
    )(x, mod6, g1, w_qkv, gvec, ones_bd, *rot_tables, *cast_ws)
    n_out = len(_QKV_OUT_WIDTHS)
    return outs[:n_out], outs[n_out:]


def _pair_mask(nq):
    row = lax.broadcasted_iota(jnp.int32, (2 * nq, LANES), 0)
    lane = lax.broadcasted_iota(jnp.int32, (2 * nq, LANES), 1)
    return (row < nq) == (lane < HEAD_DIM)


def _stack_pair(qp, own_head):
    q2 = jnp.concatenate([qp, qp], axis=0)
    return jnp.where(own_head, q2, jnp.zeros_like(q2))


def _na_kernel(q_ref, kp_ref, kc_ref, kn_ref, vp_ref, vc_ref, vn_ref, tbl_ref, o_ref, kwin, vwin):
    rb = pl.program_id(1)
    blk = NA_ROWS * GRID_W
    n_rows = pl.num_programs(1) * NA_ROWS
    for t, (kr, vr) in enumerate(((kp_ref, vp_ref), (kc_ref, vc_ref), (kn_ref, vn_ref))):
        kwin[t * blk:(t + 1) * blk, :] = kr[0]
        vwin[t * blk:(t + 1) * blk, :] = vr[0]
    first_head = lax.broadcasted_iota(jnp.int32, (GRID_W, LANES), 1) < HEAD_DIM
    own_head = _pair_mask(GRID_W)
    nkeys = NA_KH * GRID_W
    npair = NA_HEADS // 2
    pair_rows = 2 * GRID_W

    def row_body(a, carry):
        r = rb * NA_ROWS + a
        row_start = jnp.clip(r - NA_KH // 2, 0, n_rows - NA_KH)
        delta = r - row_start
        off = pl.multiple_of((row_start - (rb - 1) * NA_ROWS) * GRID_W, GRID_W)
        qoff = pl.multiple_of(a * GRID_W, GRID_W)
        s_parts = []
        for hp in range(npair):
            cols = slice(hp * LANES, (hp + 1) * LANES)
            q2 = _stack_pair(q_ref[0, pl.ds(qoff, GRID_W), cols], own_head)
            kk = kwin[pl.ds(off, nkeys), cols]
            s_parts.append(lax.dot_general(q2, kk, (((1,), (1,)), ((), ())), preferred_element_type=f32))
        n_off = 2 * NA_KH - 2
        bias = jnp.concatenate(
            [jnp.concatenate([tbl_ref[h * n_off + 2 * kp - delta + NA_KH - 1] for kp in range(NA_KH // 2)], axis=1)
             for h in range(NA_HEADS)], axis=0)
        s = jnp.concatenate(s_parts, axis=0) + bias
        m = jnp.max(s, axis=-1, keepdims=True)
        p = jnp.exp2(s - m)
        inv = 1.0 / jnp.sum(p, axis=-1, keepdims=True)
        pb = p.astype(bf16)
        for hp in range(npair):
            cols = slice(hp * LANES, (hp + 1) * LANES)
            rows = slice(hp * pair_rows, (hp + 1) * pair_rows)
            vv = vwin[pl.ds(off, nkeys), cols]
            pv = jnp.dot(pb[rows], vv, preferred_element_type=f32) * inv[rows]
            o_ref[0, pl.ds(qoff, GRID_W), cols] = jnp.where(first_head, pv[:GRID_W], pv[GRID_W:]).astype(bf16)
        return carry

    lax.fori_loop(0, NA_ROWS, row_body, 0, unroll=NA_ROWS)


def _na(q, k, v, tbl):
    bsz, s, _ = q.shape
    blk = NA_ROWS * GRID_W
    nb = s // blk
    qspec = pl.BlockSpec((1, blk, WA), lambda b, i: (b, i, 0))
    halo = [pl.BlockSpec((1, blk, WA), lambda b, i: (b, jnp.maximum(i - 1, 0), 0)),
            pl.BlockSpec((1, blk, WA), lambda b, i: (b, i, 0)),
            pl.BlockSpec((1, blk, WA), lambda b, i: (b, jnp.minimum(i + 1, nb - 1), 0))]

    return pl.pallas_call(
        _na_kernel,
        grid=(bsz, nb),
        in_specs=[qspec] + halo + halo + [_const_spec(tbl.shape)],
        out_specs=pl.BlockSpec((1, blk, WA), lambda b, i: (b, i, 0)),
        out_shape=jax.ShapeDtypeStruct((bsz, s, WA), bf16),
        scratch_shapes=[pltpu.VMEM((3 * blk, WA), bf16), pltpu.VMEM((3 * blk, WA), bf16)],
        compiler_params=_params(("parallel", "parallel")),
        name="na",
    )(q, k, k, k, v, v, v, tbl)


def _na_bias_table(rpb):
    col = jnp.arange(GRID_W)
    col_start = jnp.clip(col - NA_KW // 2, 0, GRID_W - NA_KW)
    cmask = (col[None, :] >= col_start[:, None]) & (col[None, :] < col_start[:, None] + NA_KW)
    col_off = jnp.clip(col[None, :] - col[:, None] + (NA_KW - 1), 0, 2 * NA_KW - 2)
    onehot = (col_off[None] == jnp.arange(2 * NA_KW - 1)[:, None, None]).astype(f32)
    t = jnp.einsum('hrc,cqk->hrqk', rpb.astype(f32), onehot, precision=lax.Precision.HIGHEST)
    t = jnp.where(cmask, t * LOG2E, NEG_INF)
    t = jnp.concatenate([t[:, :-1], t[:, 1:]], axis=-1)
    return t.reshape(NA_HEADS * (2 * NA_KH - 2), GRID_W, 2 * GRID_W)


def _dil_halo_pieces(dil):
    return 1 if DIL_TILE // dil == DIL_QBLOCK else dil


def _dil_kernel(*refs):
    ng = len(DIL_CONFIGS)
    n_in = sum(1 + 2 * (1 + 2 * _dil_halo_pieces(dil)) for _, dil in DIL_CONFIGS)
    in_refs = list(refs[:n_in])
    o_ref = refs[n_in]
    o_scr, m_scr, l_scr = refs[n_in + 1:]
    ti = pl.program_id(1)
    n_tiles = pl.num_programs(1)
    qb = DIL_QBLOCK
    span = 3 * qb

    first_head = lax.broadcasted_iota(jnp.int32, (qb, LANES), 1) < HEAD_DIM
    own_head = _pair_mask(qb)
    npair = DIL_HPG // 2
    qi = lax.broadcasted_iota(jnp.int32, (qb, span), 0)
    kj = lax.broadcasted_iota(jnp.int32, (qb, span), 1)
    band = jnp.where((kj >= qi) & (kj <= qi + 2 * qb), 0.0, NEG_INF).astype(f32)
    kcol = lax.broadcasted_iota(jnp.int32, (1, span), 1)
    pen_lo = jnp.where(kcol < qb, jnp.where(ti == 0, NEG_INF, 0.0), 0.0).astype(f32)
    pen_hi = jnp.where(kcol >= 2 * qb, jnp.where(ti == n_tiles - 1, NEG_INF, 0.0), 0.0).astype(f32)

    for g, (_, dil) in enumerate(DIL_CONFIGS):
        seg = DIL_TILE // dil
        nj = seg // qb
        nh = _dil_halo_pieces(dil)
        q_ref = in_refs.pop(0)
        kc, kp, kn = in_refs.pop(0), [in_refs.pop(0) for _ in range(nh)], [in_refs.pop(0) for _ in range(nh)]
        vc, vp, vn = in_refs.pop(0), [in_refs.pop(0) for _ in range(nh)], [in_refs.pop(0) for _ in range(nh)]

        def halo(pieces, r, cols, nh=nh):
            return pieces[r][0, :, cols] if nh > 1 else pieces[0][0, r * qb:(r + 1) * qb, cols]

        def window(refs3, r, j, cols, seg=seg, nj=nj, halo=halo):
            p_refs, c_ref, n_refs = refs3
            base = r * seg
            if 0 < j < nj - 1:
                return c_ref[0, base + (j - 1) * qb:base + (j + 2) * qb, cols]
            lo = halo(p_refs, r, cols) if j == 0 else c_ref[0, base + (j - 1) * qb:base + j * qb, cols]
            mid = c_ref[0, base + j * qb:base + (j + 1) * qb, cols]
            hi = halo(n_refs, r, cols) if j == nj - 1 else c_ref[0, base + (j + 1) * qb:base + (j + 2) * qb, cols]
            return jnp.concatenate([lo, mid, hi], axis=0)

        for t0 in range(0, dil * nj, DIL_UNROLL):
            blocks = [divmod(t, nj) for t in range(t0, t0 + DIL_UNROLL)]
            s_parts = []
            for k, (r, j) in enumerate(blocks):
                mask = band
                if j == 0:
                    mask = mask + pen_lo
                if j == nj - 1:
                    mask = mask + pen_hi
                for hp in range(npair):
                    cols = slice(hp * LANES, (hp + 1) * LANES)
                    q2 = _stack_pair(q_ref[0, (t0 + k) * qb:(t0 + k + 1) * qb, cols], own_head)
                    kk = window((kp, kc, kn), r, j, cols)
                    s2 = lax.dot_general(q2, kk, (((1,), (1,)), ((), ())), preferred_element_type=f32)
                    s_parts += [s2[:qb] + mask, s2[qb:] + mask]
            s = jnp.concatenate(s_parts, axis=0)
            m = jnp.max(s, axis=-1, keepdims=True)
            p = jnp.exp2(s - m)
            l = jnp.sum(p, axis=-1, keepdims=True)
            pb = p.astype(bf16)
            for k, (r, j) in enumerate(blocks):
                two_pass = dil > SAFE_STRIDE
                if two_pass:
                    hi, sub = dil // SAFE_STRIDE, DIL_TILE // SAFE_STRIDE
                    store_rows = pl.ds((r % SAFE_STRIDE) * sub + j * qb * hi + r // SAFE_STRIDE, qb, stride=hi)
                else:
                    store_rows = pl.ds(j * qb * dil + r, qb, stride=dil) if dil > 1 else pl.ds(j * qb, qb)
                for hp in range(npair):
                    cols = slice(hp * LANES, (hp + 1) * LANES)
                    r0 = (k * npair + hp) * 2 * qb
                    vv = window((vp, vc, vn), r, j, cols)
                    pv = jnp.dot(pb[r0:r0 + 2 * qb], vv, preferred_element_type=f32)
                    slot = (ng if two_pass else g) * npair + hp
                    o_scr[slot, store_rows, :] = jnp.where(first_head, pv[:qb], pv[qb:])
                    m_scr[slot, store_rows, :] = jnp.where(first_head, m[r0:r0 + qb], m[r0 + qb:r0 + 2 * qb])
                    l_scr[slot, store_rows, :] = jnp.where(first_head, l[r0:r0 + qb], l[r0 + qb:r0 + 2 * qb])
        if dil > SAFE_STRIDE:
            sub = DIL_TILE // SAFE_STRIDE
            for hp in range(npair):
                for scr in (o_scr, m_scr, l_scr):
                    for r_lo in range(SAFE_STRIDE):
                        scr[g * npair + hp, pl.ds(r_lo, sub, stride=SAFE_STRIDE), :] = (
                            scr[ng * npair + hp, r_lo * sub:(r_lo + 1) * sub, :])

    for hp in range(npair):
        m_all = m_scr[hp]
        for g in range(1, ng):
            m_all = jnp.maximum(m_all, m_scr[g * npair + hp])
        num = jnp.zeros_like(m_all)
        den = jnp.zeros_like(m_all)
        for g in range(ng):
            w = jnp.exp2(m_scr[g * npair + hp] - m_all)
            num = num + w * o_scr[g * npair + hp]
            den = den + w * l_scr[g * npair + hp]
        o_ref[0, :, hp * LANES:(hp + 1) * LANES] = (num / den).astype(bf16)


def _dil(qs, ks, vs):
    bsz, s, _ = qs[0].shape
    nt = s // DIL_TILE
    ng = len(DIL_CONFIGS)
    qb = DIL_QBLOCK
    per_tile = DIL_TILE // qb
    blk = (1, DIL_TILE, WB_OUT)
    cur = pl.BlockSpec(blk, lambda b, i: (b, i, 0))
    in_specs, args = [], []
    for g, (_, dil) in enumerate(DIL_CONFIGS):
        seg = DIL_TILE // dil
        if _dil_halo_pieces(dil) == 1 and seg == qb:
            prev = [pl.BlockSpec(blk, lambda b, i: (b, jnp.maximum(i - 1, 0), 0))]
            nxt = [pl.BlockSpec(blk, lambda b, i: (b, jnp.minimum(i + 1, nt - 1), 0))]
        else:
            prev = [pl.BlockSpec((1, qb, WB_OUT), lambda b, i, o=(r + 1) * seg // qb - 1:
                                 (b, jnp.maximum((i - 1) * per_tile + o, 0), 0)) for r in range(dil)]
            nxt = [pl.BlockSpec((1, qb, WB_OUT), lambda b, i, o=r * seg // qb:
                                (b, jnp.minimum((i + 1) * per_tile + o, s // qb - 1), 0)) for r in range(dil)]
        in_specs += [cur] + 2 * ([cur] + prev + nxt)
        args += [qs[g]] + [ks[g]] * (1 + len(prev) + len(nxt)) + [vs[g]] * (1 + len(prev) + len(nxt))
    return pl.pallas_call(
        _dil_kernel,
        grid=(bsz, nt),
        in_specs=in_specs,
        out_specs=pl.BlockSpec(blk, lambda b, i: (b, i, 0)),
        out_shape=jax.ShapeDtypeStruct((bsz, s, WB_OUT), bf16),
        scratch_shapes=[pltpu.VMEM(((ng + 1) * DIL_HPG // 2, DIL_TILE, LANES), f32)] * 3,
        compiler_params=_params(("parallel", "parallel")),
        name="dil",
    )(*args)


def _tail_kernel(x_ref, oa_ref, ob_ref, mod_ref, g1_ref, g2_ref, bg_ref, wpa_ref, wpb_ref,
                 wo_ref, win_ref, wout_ref, *rest):
    d = x_ref.shape[2]
    n_gate = 2 * d // COL_TILE
    wg_refs = rest[:n_gate]
    o_ref, h_scr, m_scr, act_scr = rest[n_gate:]
    mod = mod_ref[0]
    sh1, sc1, gt1, sh2, sc2, gt2 = [mod[k:k + 1] for k in range(6)]
    x = x_ref[0]
    h_scr[...] = _modulated_norm(x, g1_ref[...], sc1, sh1).astype(bf16)
    oa = oa_ref[0]
    ob = ob_ref[0]
    for n in range(d // COL_TILE):
        ca = slice(n * COL_TILE, (n + 1) * COL_TILE)
        cb = slice(d + n * COL_TILE, d + (n + 1) * COL_TILE)
        wga, wgb = wg_refs[n], wg_refs[n_gate // 2 + n]
        ga = jax.nn.sigmoid(jnp.dot(h_scr[...], wga[...], preferred_element_type=f32) + bg_ref[:, ca])
        gb = jax.nn.sigmoid(jnp.dot(h_scr[...], wgb[...], preferred_element_type=f32) + bg_ref[:, cb])
        pa = jnp.dot(oa, wpa_ref[:, ca], preferred_element_type=f32)
        pb = jnp.dot(ob, wpb_ref[:, ca], preferred_element_type=f32)
        m_scr[:, ca] = (ga * pa + gb * pb).astype(bf16)
    x1 = x + gt1 * jnp.dot(m_scr[...], wo_ref[...], preferred_element_type=f32)
    h_scr[...] = _modulated_norm(x1, g2_ref[...], sc2, sh2).astype(bf16)
    for f in range(D_FF // FF_CHUNK):
        ca = slice(f * FF_CHUNK, (f + 1) * FF_CHUNK)
        cu = slice(D_FF + f * FF_CHUNK, D_FF + (f + 1) * FF_CHUNK)
        a = jnp.dot(h_scr[...], win_ref[:, ca], preferred_element_type=f32)
        u = jnp.dot(h_scr[...], win_ref[:, cu], preferred_element_type=f32)
        act_scr[:, ca] = ((a * jax.nn.sigmoid(a)) * u).astype(bf16)
    o_ref[0] = x1 + gt2 * jnp.dot(act_scr[...], wout_ref[...], preferred_element_type=f32)


def _tail(x, o_a, o_b, mod6, g1, g2, w_in, bg, wpa, wpb, wo, w_ffn_in, w_ffn_out, tm=TAIL_TM):
    bsz, s, d = x.shape
    tok = lambda w: pl.BlockSpec((1, tm, w), lambda b, i: (b, i, 0))
    consts = [g1, g2, bg, wpa, wpb, wo, w_ffn_in, w_ffn_out]
    n_gate = 2 * d // COL_TILE
    gate_specs = [pl.BlockSpec((d, COL_TILE), lambda b, i, c=_N_J + n: (0, c), pipeline_mode=pl.Buffered(1))
                  for n in range(n_gate)]
    return pl.pallas_call(
        _tail_kernel,
        grid=(bsz, s // tm),
        in_specs=[tok(d), tok(WA), tok(WB_OUT), pl.BlockSpec((1, 6, d), lambda b, i: (b, 0, 0))]
        + [_const_spec(c.shape) for c in consts] + gate_specs,
        out_specs=tok(d),
        out_shape=jax.ShapeDtypeStruct((bsz, s, d), f32),
        scratch_shapes=[pltpu.VMEM((tm, d), bf16), pltpu.VMEM((tm, d), bf16), pltpu.VMEM((tm, D_FF), bf16)],
        compiler_params=_params(("parallel", "parallel")),
        name="tail",
    )(x, o_a, o_b, mod6, *consts, *([w_in] * n_gate))


def _rotary_tables(s):
    half = ROT_DIM // 2
    inv_freq = ROPE_THETA ** (-(jnp.arange(half, dtype=f32) * 2.0) / ROT_DIM)
    ang = jnp.arange(s).astype(f32)[:, None] * inv_freq[None, :]
    cos, sin = jnp.cos(ang), jnp.sin(ang)
    rest = HEAD_DIM - ROT_DIM
    cos_h = jnp.concatenate([cos, cos, jnp.ones((s, rest), f32)], axis=1)
    ssin_h = jnp.concatenate([-sin, sin, jnp.zeros((s, rest), f32)], axis=1)
    rep = LANES // HEAD_DIM
    return jnp.tile(cos_h, (1, rep)), jnp.tile(ssin_h, (1, rep))


def _layer(x, mod6, g_norm1, g_norm2, w_in, b_gate, g_qa, g_ka, g_qb, g_kb, rpb,
           w_proj_a, w_proj_b, w_o, w_ffn_in, w_ffn_out):
    bsz, s, d = x.shape
    assert d == D_MODEL and w_in.shape == (d, W_QKV + 2 * d) and w_ffn_in.shape == (d, 2 * D_FF)
    assert s % DIL_TILE == 0 and s % (NA_ROWS * GRID_W) == 0 and s % TAIL_TM == 0
    assert s // GRID_W >= NA_KH
    scale = HEAD_DIM ** -0.5 * LOG2E
    rep = COL_TILE // HEAD_DIM
    gvec = jnp.stack([jnp.tile(g_qa * scale, rep), jnp.tile(g_ka, rep),
                      jnp.tile(g_qb * scale, rep), jnp.tile(g_kb, rep)]).astype(f32)
    hid = jnp.arange(COL_TILE) // HEAD_DIM
    ones_bd = (hid[:, None] == hid[None, :]).astype(bf16)
    rot_tables = _rotary_tables(s)
    g1 = g_norm1.reshape(1, d)
    g2 = g_norm2.reshape(1, d)

    qkv, tail_ws = _qkv(x, mod6, g1, w_in, gvec, ones_bd, rot_tables,
                        [w_proj_a, w_proj_b, w_o, w_ffn_in, w_ffn_out])
    ng = len(DIL_CONFIGS)
    o_a = _na(qkv[_OUT_QA], qkv[_OUT_KA], qkv[_OUT_VA], _na_bias_table(rpb))
    o_b = _dil(qkv[_OUT_QB:_OUT_QB + ng], qkv[_OUT_KB:_OUT_KB + ng], qkv[_OUT_VB:_OUT_VB + ng])

    return _tail(x, o_a, o_b, mod6, g1, g2, w_in, b_gate.reshape(1, 2 * d), *tail_ws)


def kernel(x, c, w_ada, b_ada, g_norm1, g_norm2, w_in, b_gate, g_qa, g_ka, g_qb, g_kb, rpb,
           w_proj_a, w_proj_b, w_o, w_ffn_in, w_ffn_out):
    depth = w_ada.shape[0]
    bsz, d = c.shape
    for l in range(depth):
        mod, w_in_bf = _mod(c, w_ada[l], b_ada[l], w_in[l])
        x = _layer(x, mod.reshape(bsz, 6, d), g_norm1[l], g_norm2[l], w_in_bf, b_gate[l], g_qa[l], g_ka[l], g_qb[l],
                   g_kb[l], rpb[l], w_proj_a[l], w_proj_b[l], w_o[l], w_ffn_in[l], w_ffn_out[l])
    return x
```

```python
import functools

import jax
import jax.numpy as jnp
from jax import lax
from jax.experimental import pallas as pl
from jax.experimental.pallas import tpu as pltpu

f32 = jnp.float32
bf16 = jnp.bfloat16

D_MODEL = 1024
HEAD_DIM = 64
GRID_W = 64
NA_HEADS = 8
NA_KH = 8
NA_KW = 16
DIL_CONFIGS = ((128, 1), (512, 4), (2048, 16))
DIL_HPG = 4
DIL_HEADS = DIL_HPG * len(DIL_CONFIGS)
DIL_QBLOCK = 64
ROT_DIM = HEAD_DIM // 4
ROPE_THETA = 500000.0
D_FF = -(-8 * D_MODEL // (3 * 256)) * 256
EPS = 1e-6
NEG_INF = -1e30
LOG2E = 1.4426950408889634
WA = NA_HEADS * HEAD_DIM
WB = DIL_HEADS * HEAD_DIM
WB_OUT = DIL_HPG * HEAD_DIM
W_QKV = 3 * WA + 3 * WB

LANES = 128
BF16_ROWS = 16
COL_TILE = 256
DIL_TILE = 1024
QKV_ROW_SPLIT = 2
QKV_DOT_TILES = 2
DIL_UNROLL = 16
NA_ROWS = 16
MOD_STEPS = 8
TAIL_TM = 1024
FF_CHUNK = 256
VMEM_LIMIT = 56 * 1024 * 1024

SAFE_STRIDE = 4

assert all((win // 2) // dil == DIL_QBLOCK for win, dil in DIL_CONFIGS)
assert all(dil <= SAFE_STRIDE or (dil % SAFE_STRIDE == 0 and dil // SAFE_STRIDE <= SAFE_STRIDE)
           for _, dil in DIL_CONFIGS)


def _params(sem):
    return pltpu.CompilerParams(dimension_semantics=sem, vmem_limit_bytes=VMEM_LIMIT)


def _const_spec(shape):
    nd = len(shape)
    return pl.BlockSpec(shape, lambda *_: (0,) * nd, pipeline_mode=pl.Buffered(1))


def _mod_kernel(ct_ref, w_ref, b_ref, win_ref, o_ref, win_o_ref):
    ct = ct_ref[...]
    act = ct * jax.nn.sigmoid(ct)
    w = w_ref[...]
    rows = [jnp.sum(act[:, b:b + 1] * w, axis=0, keepdims=True) for b in range(ct.shape[1])]
    o_ref[...] = jnp.concatenate(rows, axis=0) + b_ref[...]
    win_o_ref[...] = win_ref[...].astype(bf16)


def _mod(c, w_ada, b_ada, w_in):
    bsz, d = c.shape
    n = w_ada.shape[1]
    steps = MOD_STEPS
    tn = n // steps
    slab = pl.BlockSpec((w_in.shape[0] // steps, w_in.shape[1]), lambda j: (j, 0))
    return pl.pallas_call(
        _mod_kernel,
        grid=(steps,),
        in_specs=[pl.BlockSpec((d, bsz), lambda j: (0, 0)),
                  pl.BlockSpec((d, tn), lambda j: (0, j)),
                  pl.BlockSpec((1, tn), lambda j: (0, j)),
                  slab],
        out_specs=[pl.BlockSpec((bsz, tn), lambda j: (0, j)), slab],
        out_shape=[jax.ShapeDtypeStruct((bsz, n), f32), jax.ShapeDtypeStruct(w_in.shape, bf16)],
        compiler_params=_params(("arbitrary",)),
        name="mod",
    )(c.T, w_ada, b_ada.reshape(1, n), w_in)


def _modulated_norm(x, g, sc, sh):
    ms = jnp.mean(x * x, axis=-1, keepdims=True)
    y = x * lax.rsqrt(ms + EPS) * g
    return y * (1.0 + sc) + sh


_J_QB = 3 * WA // COL_TILE
_J_KB = _J_QB + WB // COL_TILE
_J_VB = _J_KB + WB // COL_TILE
_N_J = W_QKV // COL_TILE


def _qkv_tile_kind(j):
    if j < WA // COL_TILE:
        return 0, False, 1
    if j < 2 * WA // COL_TILE:
        return 1, False, 1
    if j < _J_QB:
        return None, False, 1
    dil = DIL_CONFIGS[(j - _J_QB) % len(DIL_CONFIGS)][1]
    if j < _J_KB:
        return 2, True, dil
    if j < _J_VB:
        return 3, True, dil
    return None, False, dil


_QKV_OUT_WIDTHS = (WA,) * 3 + (WB_OUT,) * (3 * len(DIL_CONFIGS))
_OUT_QA, _OUT_KA, _OUT_VA, _OUT_QB, _OUT_KB, _OUT_VB = 0, 1, 2, 3, 3 + len(DIL_CONFIGS), 3 + 2 * len(DIL_CONFIGS)


def _qkv_tile_out(j, o_refs):
    if j < _J_QB:
        per = WA // COL_TILE
        return o_refs[j // per], (j % per) * COL_TILE
    return o_refs[_OUT_QB + j - _J_QB], 0


def _qkv_kernel(n_cast, x_ref, mod_ref, g1_ref, w_ref, gvec_ref, ones_ref, cos_ref, ssin_ref, *rest):
    n_out = len(_QKV_OUT_WIDTHS)
    cast_in, o_refs = rest[:n_cast], rest[n_cast:n_cast + n_out]
    cast_out = rest[n_cast + n_out:2 * n_cast + n_out]
    h_scr, y_scr, z_scr = rest[2 * n_cast + n_out:]
    for src, dst in zip(cast_in, cast_out):
        dst[...] = src[...].astype(bf16)
    tm = x_ref.shape[1]
    nc = COL_TILE // LANES
    mod = mod_ref[0]
    h_scr[...] = _modulated_norm(x_ref[0], g1_ref[...], mod[1:2], mod[0:1]).astype(bf16)
    head_dim = lax.broadcasted_iota(jnp.int32, (tm // QKV_ROW_SPLIT, COL_TILE), 1) % HEAD_DIM
    first_half = head_dim < ROT_DIM // 2

    wide = {}
    for j in range(_N_J):
        gain_row, rotary, dil = _qkv_tile_kind(j)
        o_ref, c0 = _qkv_tile_out(j, o_refs)
        slot = j % y_scr.shape[0]
        for hb in range(QKV_ROW_SPLIT):
            rows = slice(hb * tm // QKV_ROW_SPLIT, (hb + 1) * tm // QKV_ROW_SPLIT)
            if j % QKV_DOT_TILES == 0:
                wcols = slice(j * COL_TILE, min(j + QKV_DOT_TILES, _N_J) * COL_TILE)
                wide[hb] = jnp.dot(h_scr[rows, :], w_ref[:, wcols], preferred_element_type=f32)
            sub = j % QKV_DOT_TILES
            y = wide[hb][:, sub * COL_TILE:(sub + 1) * COL_TILE]
            if gain_row is not None:
                ss = jnp.dot((y * y).astype(bf16), ones_ref[...], preferred_element_type=f32)
                y = y * lax.rsqrt(ss * (1.0 / HEAD_DIM) + EPS) * gvec_ref[gain_row:gain_row + 1, :]
            if rotary:
                cos, ssin = [jnp.concatenate([t_ref[rows, :]] * nc, axis=1) for t_ref in (cos_ref, ssin_ref)]
                half = ROT_DIM // 2
                up = pltpu.roll(y, COL_TILE - half, 1)
                dn = pltpu.roll(y, half, 1)
                y = y * cos + jnp.where(first_half, up, dn) * ssin
            if dil == 1:
                o_ref[0, rows, c0:c0 + COL_TILE] = y.astype(bf16)
            else:
                for c in range(nc):
                    y_scr[slot, c, rows, :] = y[:, c * LANES:(c + 1) * LANES]
        if dil == 1:
            continue
        seg = DIL_TILE // dil
        for c in range(nc):
            ocols = slice(c0 + c * LANES, c0 + (c + 1) * LANES)
            if dil <= SAFE_STRIDE:
                for r in range(dil):
                    rows = y_scr[slot, c, pl.ds(r, seg, stride=dil), :]
                    o_ref[0, r * seg:(r + 1) * seg, ocols] = rows.astype(bf16)
                continue
            hi = dil // SAFE_STRIDE
            sub = DIL_TILE // SAFE_STRIDE
            for r_lo in range(SAFE_STRIDE):
                z_scr[c, r_lo * sub:(r_lo + 1) * sub, :] = y_scr[slot, c, pl.ds(r_lo, sub, stride=SAFE_STRIDE), :]
            for r_lo in range(SAFE_STRIDE):
                for r_hi in range(hi):
                    r = SAFE_STRIDE * r_hi + r_lo
                    rows = z_scr[c, pl.ds(r_lo * sub + r_hi, seg, stride=hi), :]
                    o_ref[0, r * seg:(r + 1) * seg, ocols] = rows.astype(bf16)


def _qkv(x, mod6, g1, w_qkv, gvec, ones_bd, rot_tables, cast_ws, tm=DIL_TILE):
    bsz, s, d = x.shape
    assert tm == DIL_TILE
    nt = s // tm
    steps = bsz * nt
    assert all(w.shape[0] % (BF16_ROWS * steps) == 0 for w in cast_ws)
    slab_specs = [pl.BlockSpec((w.shape[0] // steps, w.shape[1]), lambda b, i: (b * nt + i, 0)) for w in cast_ws]
    outs = pl.pallas_call(
        functools.partial(_qkv_kernel, len(cast_ws)),
        grid=(bsz, nt),
        in_specs=[pl.BlockSpec((1, tm, d), lambda b, i: (b, i, 0)),
                  pl.BlockSpec((1, 6, d), lambda b, i: (b, 0, 0)),
                  _const_spec(g1.shape),
                  pl.BlockSpec((d, W_QKV), lambda b, i: (0, 0), pipeline_mode=pl.Buffered(1)),
                  _const_spec(gvec.shape),
                  _const_spec(ones_bd.shape),
                  ] + [pl.BlockSpec((tm, LANES), lambda b, i: (i, 0))] * len(rot_tables) + slab_specs,
        out_specs=[pl.BlockSpec((1, tm, w), lambda b, i: (b, i, 0)) for w in _QKV_OUT_WIDTHS] + slab_specs,
        out_shape=[jax.ShapeDtypeStruct((bsz, s, w), bf16) for w in _QKV_OUT_WIDTHS]
        + [jax.ShapeDtypeStruct(w.shape, bf16) for w in cast_ws],
        scratch_shapes=[pltpu.VMEM((tm, d), bf16),
                        pltpu.VMEM((2, COL_TILE // LANES, tm, LANES), f32),
                        pltpu.VMEM((COL_TILE // LANES, tm, LANES), f32)],
        compiler_params=_params(("parallel", "parallel")),
        name="qkv",
    )(x, mod6, g1, w_qkv, gvec, ones_bd, *rot_tables, *cast_ws)
    n_out = len(_QKV_OUT_WIDTHS)
    return outs[:n_out], outs[n_out:]


def _pair_mask(nq):
    row = lax.broadcasted_iota(jnp.int32, (2 * nq, LANES), 0)
    lane = lax.broadcasted_iota(jnp.int32, (2 * nq, LANES), 1)
    return (row < nq) == (lane < HEAD_DIM)


def _stack_pair(qp, own_head):
    q2 = jnp.concatenate([qp, qp], axis=0)
    return jnp.where(own_head, q2, jnp.zeros_like(q2))


def _na_kernel(q_ref, kp_ref, kc_ref, kn_ref, vp_ref, vc_ref, vn_ref, tbl_ref, o_ref, kwin, vwin):
    rb = pl.program_id(1)
    blk = NA_ROWS * GRID_W
    n_rows = pl.num_programs(1) * NA_ROWS
    for t, (kr, vr) in enumerate(((kp_ref, vp_ref), (kc_ref, vc_ref), (kn_ref, vn_ref))):
        kwin[t * blk:(t + 1) * blk, :] = kr[0]
        vwin[t * blk:(t + 1) * blk, :] = vr[0]
    first_head = lax.broadcasted_iota(jnp.int32, (GRID_W, LANES), 1) < HEAD_DIM
    own_head = _pair_mask(GRID_W)
    nkeys = NA_KH * GRID_W
    npair = NA_HEADS // 2
    pair_rows = 2 * GRID_W

    def row_body(a, carry):
        r = rb * NA_ROWS + a
        row_start = jnp.clip(r - NA_KH // 2, 0, n_rows - NA_KH)
        delta = r - row_start
        off = pl.multiple_of((row_start - (rb - 1) * NA_ROWS) * GRID_W, GRID_W)
        qoff = pl.multiple_of(a * GRID_W, GRID_W)
        s_parts = []
        for hp in range(npair):
            cols = slice(hp * LANES, (hp + 1) * LANES)
            q2 = _stack_pair(q_ref[0, pl.ds(qoff, GRID_W), cols], own_head)
            kk = kwin[pl.ds(off, nkeys), cols]
            s_parts.append(lax.dot_general(q2, kk, (((1,), (1,)), ((), ())), preferred_element_type=f32))
        n_off = 2 * NA_KH - 2
        bias = jnp.concatenate(
            [jnp.concatenate([tbl_ref[h * n_off + 2 * kp - delta + NA_KH - 1] for kp in range(NA_KH // 2)], axis=1)
             for h in range(NA_HEADS)], axis=0)
        s = jnp.concatenate(s_parts, axis=0) + bias
        m = jnp.max(s, axis=-1, keepdims=True)
        p = jnp.exp2(s - m)
        inv = 1.0 / jnp.sum(p, axis=-1, keepdims=True)
        pb = p.astype(bf16)
        for hp in range(npair):
            cols = slice(hp * LANES, (hp + 1) * LANES)
            rows = slice(hp * pair_rows, (hp + 1) * pair_rows)
            vv = vwin[pl.ds(off, nkeys), cols]
            pv = jnp.dot(pb[rows], vv, preferred_element_type=f32) * inv[rows]
            o_ref[0, pl.ds(qoff, GRID_W), cols] = jnp.where(first_head, pv[:GRID_W], pv[GRID_W:]).astype(bf16)
        return carry

    lax.fori_loop(0, NA_ROWS, row_body, 0, unroll=NA_ROWS)


def _na(q, k, v, tbl):
    bsz, s, _ = q.shape
    blk = NA_ROWS * GRID_W
    nb = s // blk
    qspec = pl.BlockSpec((1, blk, WA), lambda b, i: (b, i, 0))
    halo = [pl.BlockSpec((1, blk, WA), lambda b, i: (b, jnp.maximum(i - 1, 0), 0)),
            pl.BlockSpec((1, blk, WA), lambda b, i: (b, i, 0)),
            pl.BlockSpec((1, blk, WA), lambda b, i: (b, jnp.minimum(i + 1, nb - 1), 0))]

    return pl.pallas_call(
        _na_kernel,
        grid=(bsz, nb),
        in_specs=[qspec] + halo + halo + [_const_spec(tbl.shape)],
        out_specs=pl.BlockSpec((1, blk, WA), lambda b, i: (b, i, 0)),
        out_shape=jax.ShapeDtypeStruct((bsz, s, WA), bf16),
        scratch_shapes=[pltpu.VMEM((3 * blk, WA), bf16), pltpu.VMEM((3 * blk, WA), bf16)],
        compiler_params=_params(("parallel", "parallel")),
        name="na",
    )(q, k, k, k, v, v, v, tbl)


def _na_bias_table(rpb):
    col = jnp.arange(GRID_W)
    col_start = jnp.clip(col - NA_KW // 2, 0, GRID_W - NA_KW)
    cmask = (col[None, :] >= col_start[:, None]) & (col[None, :] < col_start[:, None] + NA_KW)
    col_off = jnp.clip(col[None, :] - col[:, None] + (NA_KW - 1), 0, 2 * NA_KW - 2)
    onehot = (col_off[None] == jnp.arange(2 * NA_KW - 1)[:, None, None]).astype(f32)
    t = jnp.einsum('hrc,cqk->hrqk', rpb.astype(f32), onehot, precision=lax.Precision.HIGHEST)
    t = jnp.where(cmask, t * LOG2E, NEG_INF)
    t = jnp.concatenate([t[:, :-1], t[:, 1:]], axis=-1)
    return t.reshape(NA_HEADS * (2 * NA_KH - 2), GRID_W, 2 * GRID_W)


def _dil_halo_pieces(dil):
    return 1 if DIL_TILE // dil == DIL_QBLOCK else dil


def _dil_kernel(*refs):
    ng = len(DIL_CONFIGS)
    n_in = sum(1 + 2 * (1 + 2 * _dil_halo_pieces(dil)) for _, dil in DIL_CONFIGS)
    in_refs = list(refs[:n_in])
    o_ref = refs[n_in]
    o_scr, m_scr, l_scr = refs[n_in + 1:]
    ti = pl.program_id(1)
    n_tiles = pl.num_programs(1)
    qb = DIL_QBLOCK
    span = 3 * qb

    first_head = lax.broadcasted_iota(jnp.int32, (qb, LANES), 1) < HEAD_DIM
    own_head = _pair_mask(qb)
    npair = DIL_HPG // 2
    qi = lax.broadcasted_iota(jnp.int32, (qb, span), 0)
    kj = lax.broadcasted_iota(jnp.int32, (qb, span), 1)
    band = jnp.where((kj >= qi) & (kj <= qi + 2 * qb), 0.0, NEG_INF).astype(f32)
    kcol = lax.broadcasted_iota(jnp.int32, (1, span), 1)
    pen_lo = jnp.where(kcol < qb, jnp.where(ti == 0, NEG_INF, 0.0), 0.0).astype(f32)
    pen_hi = jnp.where(kcol >= 2 * qb, jnp.where(ti == n_tiles - 1, NEG_INF, 0.0), 0.0).astype(f32)

    for g, (_, dil) in enumerate(DIL_CONFIGS):
        seg = DIL_TILE // dil
        nj = seg // qb
        nh = _dil_halo_pieces(dil)
        q_ref = in_refs.pop(0)
        kc, kp, kn = in_refs.pop(0), [in_refs.pop(0) for _ in range(nh)], [in_refs.pop(0) for _ in range(nh)]
        vc, vp, vn = in_refs.pop(0), [in_refs.pop(0) for _ in range(nh)], [in_refs.pop(0) for _ in range(nh)]

        def halo(pieces, r, cols, nh=nh):
            return pieces[r][0, :, cols] if nh > 1 else pieces[0][0, r * qb:(r + 1) * qb, cols]

        def window(refs3, r, j, cols, seg=seg, nj=nj, halo=halo):
            p_refs, c_ref, n_refs = refs3
            base = r * seg
            if 0 < j < nj - 1:
                return c_ref[0, base + (j - 1) * qb:base + (j + 2) * qb, cols]
            lo = halo(p_refs, r, cols) if j == 0 else c_ref[0, base + (j - 1) * qb:base + j * qb, cols]
            mid = c_ref[0, base + j * qb:base + (j + 1) * qb, cols]
            hi = halo(n_refs, r, cols) if j == nj - 1 else c_ref[0, base + (j + 1) * qb:base + (j + 2) * qb, cols]
            return jnp.concatenate([lo, mid, hi], axis=0)

        for t0 in range(0, dil * nj, DIL_UNROLL):
            blocks = [divmod(t, nj) for t in range(t0, t0 + DIL_UNROLL)]
            s_parts = []
            for k, (r, j) in enumerate(blocks):
                mask = band
                if j == 0:
                    mask = mask + pen_lo
                if j == nj - 1:
                    mask = mask + pen_hi
                for hp in range(npair):
                    cols = slice(hp * LANES, (hp + 1) * LANES)
                    q2 = _stack_pair(q_ref[0, (t0 + k) * qb:(t0 + k + 1) * qb, cols], own_head)
                    kk = window((kp, kc, kn), r, j, cols)
                    s2 = lax.dot_general(q2, kk, (((1,), (1,)), ((), ())), preferred_element_type=f32)
                    s_parts += [s2[:qb] + mask, s2[qb:] + mask]
            s = jnp.concatenate(s_parts, axis=0)
            m = jnp.max(s, axis=-1, keepdims=True)
            p = jnp.exp2(s - m)
            l = jnp.sum(p, axis=-1, keepdims=True)
            pb = p.astype(bf16)
            for k, (r, j) in enumerate(blocks):
                two_pass = dil > SAFE_STRIDE
                if two_pass:
                    hi, sub = dil // SAFE_STRIDE, DIL_TILE // SAFE_STRIDE
                    store_rows = pl.ds((r % SAFE_STRIDE) * sub + j * qb * hi + r // SAFE_STRIDE, qb, stride=hi)
                else:
                    store_rows = pl.ds(j * qb * dil + r, qb, stride=dil) if dil > 1 else pl.ds(j * qb, qb)
                for hp in range(npair):
                    cols = slice(hp * LANES, (hp + 1) * LANES)
                    r0 = (k * npair + hp) * 2 * qb
                    vv = window((vp, vc, vn), r, j, cols)
                    pv = jnp.dot(pb[r0:r0 + 2 * qb], vv, preferred_element_type=f32)
                    slot = (ng if two_pass else g) * npair + hp
                    o_scr[slot, store_rows, :] = jnp.where(first_head, pv[:qb], pv[qb:])
                    m_scr[slot, store_rows, :] = jnp.where(first_head, m[r0:r0 + qb], m[r0 + qb:r0 + 2 * qb])
                    l_scr[slot, store_rows, :] = jnp.where(first_head, l[r0:r0 + qb], l[r0 + qb:r0 + 2 * qb])
        if dil > SAFE_STRIDE:
            sub = DIL_TILE // SAFE_STRIDE
            for hp in range(npair):
                for scr in (o_scr, m_scr, l_scr):
                    for r_lo in range(SAFE_STRIDE):
                        scr[g * npair + hp, pl.ds(r_lo, sub, stride=SAFE_STRIDE), :] = (
                            scr[ng * npair + hp, r_lo * sub:(r_lo + 1) * sub, :])

    for hp in range(npair):
        m_all = m_scr[hp]
        for g in range(1, ng):
            m_all = jnp.maximum(m_all, m_scr[g * npair + hp])
        num = jnp.zeros_like(m_all)
        den = jnp.zeros_like(m_all)
        for g in range(ng):
            w = jnp.exp2(m_scr[g * npair + hp] - m_all)
            num = num + w * o_scr[g * npair + hp]
            den = den + w * l_scr[g * npair + hp]
        o_ref[0, :, hp * LANES:(hp + 1) * LANES] = (num / den).astype(bf16)


def _dil(qs, ks, vs):
    bsz, s, _ = qs[0].shape
    nt = s // DIL_TILE
    ng = len(DIL_CONFIGS)
    qb = DIL_QBLOCK
    per_tile = DIL_TILE // qb
    blk = (1, DIL_TILE, WB_OUT)
    cur = pl.BlockSpec(blk, lambda b, i: (b, i, 0))
    in_specs, args = [], []
    for g, (_, dil) in enumerate(DIL_CONFIGS):
        seg = DIL_TILE // dil
        if _dil_halo_pieces(dil) == 1 and seg == qb:
            prev = [pl.BlockSpec(blk, lambda b, i: (b, jnp.maximum(i - 1, 0), 0))]
            nxt = [pl.BlockSpec(blk, lambda b, i: (b, jnp.minimum(i + 1, nt - 1), 0))]
        else:
            prev = [pl.BlockSpec((1, qb, WB_OUT), lambda b, i, o=(r + 1) * seg // qb - 1:
                                 (b, jnp.maximum((i - 1) * per_tile + o, 0), 0)) for r in range(dil)]
            nxt = [pl.BlockSpec((1, qb, WB_OUT), lambda b, i, o=r * seg // qb:
                                (b, jnp.minimum((i + 1) * per_tile + o, s // qb - 1), 0)) for r in range(dil)]
        in_specs += [cur] + 2 * ([cur] + prev + nxt)
        args += [qs[g]] + [ks[g]] * (1 + len(prev) + len(nxt)) + [vs[g]] * (1 + len(prev) + len(nxt))
    return pl.pallas_call(
        _dil_kernel,
        grid=(bsz, nt),
        in_specs=in_specs,
        out_specs=pl.BlockSpec(blk, lambda b, i: (b, i, 0)),
        out_shape=jax.ShapeDtypeStruct((bsz, s, WB_OUT), bf16),
        scratch_shapes=[pltpu.VMEM(((ng + 1) * DIL_HPG // 2, DIL_TILE, LANES), f32)] * 3,
        compiler_params=_params(("parallel", "parallel")),
        name="dil",
    )(*args)


def _tail_kernel(x_ref, oa_ref, ob_ref, mod_ref, g1_ref, g2_ref, bg_ref, wpa_ref, wpb_ref,
                 wo_ref, win_ref, wout_ref, *rest):
    d = x_ref.shape[2]
    n_gate = 2 * d // COL_TILE
    wg_refs = rest[:n_gate]
    o_ref, h_scr, m_scr, act_scr = rest[n_gate:]
    mod = mod_ref[0]
    sh1, sc1, gt1, sh2, sc2, gt2 = [mod[k:k + 1] for k in range(6)]
    x = x_ref[0]
    h_scr[...] = _modulated_norm(x, g1_ref[...], sc1, sh1).astype(bf16)
    oa = oa_ref[0]
    ob = ob_ref[0]
    for n in range(d // COL_TILE):
        ca = slice(n * COL_TILE, (n + 1) * COL_TILE)
        cb = slice(d + n * COL_TILE, d + (n + 1) * COL_TILE)
        wga, wgb = wg_refs[n], wg_refs[n_gate // 2 + n]
        ga = jax.nn.sigmoid(jnp.dot(h_scr[...], wga[...], preferred_element_type=f32) + bg_ref[:, ca])
        gb = jax.nn.sigmoid(jnp.dot(h_scr[...], wgb[...], preferred_element_type=f32) + bg_ref[:, cb])
        pa = jnp.dot(oa, wpa_ref[:, ca], preferred_element_type=f32)
        pb = jnp.dot(ob, wpb_ref[:, ca], preferred_element_type=f32)
        m_scr[:, ca] = (ga * pa + gb * pb).astype(bf16)
    x1 = x + gt1 * jnp.dot(m_scr[...], wo_ref[...], preferred_element_type=f32)
    h_scr[...] = _modulated_norm(x1, g2_ref[...], sc2, sh2).astype(bf16)
    for f in range(D_FF // FF_CHUNK):
        ca = slice(f * FF_CHUNK, (f + 1) * FF_CHUNK)
        cu = slice(D_FF + f * FF_CHUNK, D_FF + (f + 1) * FF_CHUNK)
        a = jnp.dot(h_scr[...], win_ref[:, ca], preferred_element_type=f32)
        u = jnp.dot(h_scr[...], win_ref[:, cu], preferred_element_type=f32)
        act_scr[:, ca] = ((a * jax.nn.sigmoid(a)) * u).astype(bf16)
    o_ref[0] = x1 + gt2 * jnp.dot(act_scr[...], wout_ref[...], preferred_element_type=f32)


def _tail(x, o_a, o_b, mod6, g1, g2, w_in, bg, wpa, wpb, wo, w_ffn_in, w_ffn_out, tm=TAIL_TM):
    bsz, s, d = x.shape
    tok = lambda w: pl.BlockSpec((1, tm, w), lambda b, i: (b, i, 0))
    consts = [g1, g2, bg, wpa, wpb, wo, w_ffn_in, w_ffn_out]
    n_gate = 2 * d // COL_TILE
    gate_specs = [pl.BlockSpec((d, COL_TILE), lambda b, i, c=_N_J + n: (0, c), pipeline_mode=pl.Buffered(1))
                  for n in range(n_gate)]
    return pl.pallas_call(
        _tail_kernel,
        grid=(bsz, s // tm),
        in_specs=[tok(d), tok(WA), tok(WB_OUT), pl.BlockSpec((1, 6, d), lambda b, i: (b, 0, 0))]
        + [_const_spec(c.shape) for c in consts] + gate_specs,
        out_specs=tok(d),
        out_shape=jax.ShapeDtypeStruct((bsz, s, d), f32),
        scratch_shapes=[pltpu.VMEM((tm, d), bf16), pltpu.VMEM((tm, d), bf16), pltpu.VMEM((tm, D_FF), bf16)],
        compiler_params=_params(("parallel", "parallel")),
        name="tail",
    )(x, o_a, o_b, mod6, *consts, *([w_in] * n_gate))


def _rotary_tables(s):
    half = ROT_DIM // 2
    inv_freq = ROPE_THETA ** (-(jnp.arange(half, dtype=f32) * 2.0) / ROT_DIM)
    ang = jnp.arange(s).astype(f32)[:, None] * inv_freq[None, :]
    cos, sin = jnp.cos(ang), jnp.sin(ang)
    rest = HEAD_DIM - ROT_DIM
    cos_h = jnp.concatenate([cos, cos, jnp.ones((s, rest), f32)], axis=1)
    ssin_h = jnp.concatenate([-sin, sin, jnp.zeros((s, rest), f32)], axis=1)
    rep = LANES // HEAD_DIM
    return jnp.tile(cos_h, (1, rep)), jnp.tile(ssin_h, (1, rep))


def _layer(x, mod6, g_norm1, g_norm2, w_in, b_gate, g_qa, g_ka, g_qb, g_kb, rpb,
           w_proj_a, w_proj_b, w_o, w_ffn_in, w_ffn_out):
    bsz, s, d = x.shape
    assert d == D_MODEL and w_in.shape == (d, W_QKV + 2 * d) and w_ffn_in.shape == (d, 2 * D_FF)
    assert s % DIL_TILE == 0 and s % (NA_ROWS * GRID_W) == 0 and s % TAIL_TM == 0
    assert s // GRID_W >= NA_KH
    scale = HEAD_DIM ** -0.5 * LOG2E
    rep = COL_TILE // HEAD_DIM
    gvec = jnp.stack([jnp.tile(g_qa * scale, rep), jnp.tile(g_ka, rep),
                      jnp.tile(g_qb * scale, rep), jnp.tile(g_kb, rep)]).astype(f32)
    hid = jnp.arange(COL_TILE) // HEAD_DIM
    ones_bd = (hid[:, None] == hid[None, :]).astype(bf16)
    rot_tables = _rotary_tables(s)
    g1 = g_norm1.reshape(1, d)
    g2 = g_norm2.reshape(1, d)

    qkv, tail_ws = _qkv(x, mod6, g1, w_in, gvec, ones_bd, rot_tables,
                        [w_proj_a, w_proj_b, w_o, w_ffn_in, w_ffn_out])
    ng = len(DIL_CONFIGS)
    o_a = _na(qkv[_OUT_QA], qkv[_OUT_KA], qkv[_OUT_VA], _na_bias_table(rpb))
    o_b = _dil(qkv[_OUT_QB:_OUT_QB + ng], qkv[_OUT_KB:_OUT_KB + ng], qkv[_OUT_VB:_OUT_VB + ng])

    return _tail(x, o_a, o_b, mod6, g1, g2, w_in, b_gate.reshape(1, 2 * d), *tail_ws)


def kernel(x, c, w_ada, b_ada, g_norm1, g_norm2, w_in, b_gate, g_qa, g_ka, g_qb, g_kb, rpb,
           w_proj_a, w_proj_b, w_o, w_ffn_in, w_ffn_out):
    depth = w_ada.shape[0]
    bsz, d = c.shape
    for l in range(depth):
        mod, w_in_bf = _mod(c, w_ada[l], b_ada[l], w_in[l])
        x = _layer(x, mod.reshape(bsz, 6, d), g_norm1[l], g_norm2[l], w_in_bf, b_gate[l], g_qa[l], g_ka[l], g_qb[l],
                   g_kb[l], rpb[l], w_proj_a[l], w_proj_b[l], w_o[l], w_ffn_in[l], w_ffn_out[l])
    return x
```

```python
import functools

import jax
import jax.numpy as jnp
from jax import lax
from jax.experimental import pallas as pl
from jax.experimental.pallas import tpu as pltpu

f32 = jnp.float32
bf16 = jnp.bfloat16

D_MODEL = 1024
HEAD_DIM = 64
GRID_W = 64
NA_HEADS = 8
NA_KH = 8
NA_KW = 16
DIL_CONFIGS = ((128, 1), (512, 4), (2048, 16))
DIL_HPG = 4
DIL_HEADS = DIL_HPG * len(DIL_CONFIGS)
DIL_QBLOCK = 64
ROT_DIM = HEAD_DIM // 4
ROPE_THETA = 500000.0
D_FF = -(-8 * D_MODEL // (3 * 256)) * 256
EPS = 1e-6
NEG_INF = -1e30
LOG2E = 1.4426950408889634
WA = NA_HEADS * HEAD_DIM
WB = DIL_HEADS * HEAD_DIM
WB_OUT = DIL_HPG * HEAD_DIM
W_QKV = 3 * WA + 3 * WB

LANES = 128
BF16_ROWS = 16
COL_TILE = 256
DIL_TILE = 1024
QKV_ROW_SPLIT = 2
QKV_DOT_TILES = 2
DIL_UNROLL = 16
NA_ROWS = 16
MOD_STEPS = 8
TAIL_TM = 1024
FF_CHUNK = 256
VMEM_LIMIT = 56 * 1024 * 1024

SAFE_STRIDE = 4

assert all((win // 2) // dil == DIL_QBLOCK for win, dil in DIL_CONFIGS)
assert all(dil == 1 or (dil % SAFE_STRIDE == 0 and dil // SAFE_STRIDE <= SAFE_STRIDE)
           for _, dil in DIL_CONFIGS)


def _params(sem):
    return pltpu.CompilerParams(dimension_semantics=sem, vmem_limit_bytes=VMEM_LIMIT)


def _const_spec(shape):
    nd = len(shape)
    return pl.BlockSpec(shape, lambda *_: (0,) * nd, pipeline_mode=pl.Buffered(1))


def _mod_kernel(ct_ref, w_ref, b_ref, win_ref, o_ref, win_o_ref):
    ct = ct_ref[...]
    act = ct * jax.nn.sigmoid(ct)
    w = w_ref[...]
    rows = [jnp.sum(act[:, b:b + 1] * w, axis=0, keepdims=True) for b in range(ct.shape[1])]
    o_ref[...] = jnp.concatenate(rows, axis=0) + b_ref[...]
    win_o_ref[...] = win_ref[...].astype(bf16)


def _mod(c, w_ada, b_ada, w_in):
    bsz, d = c.shape
    n = w_ada.shape[1]
    steps = MOD_STEPS
    tn = n // steps
    slab = pl.BlockSpec((w_in.shape[0] // steps, w_in.shape[1]), lambda j: (j, 0))
    return pl.pallas_call(
        _mod_kernel,
        grid=(steps,),
        in_specs=[pl.BlockSpec((d, bsz), lambda j: (0, 0)),
                  pl.BlockSpec((d, tn), lambda j: (0, j)),
                  pl.BlockSpec((1, tn), lambda j: (0, j)),
                  slab],
        out_specs=[pl.BlockSpec((bsz, tn), lambda j: (0, j)), slab],
        out_shape=[jax.ShapeDtypeStruct((bsz, n), f32), jax.ShapeDtypeStruct(w_in.shape, bf16)],
        compiler_params=_params(("arbitrary",)),
        name="mod",
    )(c.T, w_ada, b_ada.reshape(1, n), w_in)


def _modulated_norm(x, g, sc, sh):
    ms = jnp.mean(x * x, axis=-1, keepdims=True)
    y = x * lax.rsqrt(ms + EPS) * g
    return y * (1.0 + sc) + sh


_J_QB = 3 * WA // COL_TILE
_J_KB = _J_QB + WB // COL_TILE
_J_VB = _J_KB + WB // COL_TILE
_N_J = W_QKV // COL_TILE


def _qkv_tile_kind(j):
    if j < WA // COL_TILE:
        return 0, False, 1
    if j < 2 * WA // COL_TILE:
        return 1, False, 1
    if j < _J_QB:
        return None, False, 1
    dil = DIL_CONFIGS[(j - _J_QB) % len(DIL_CONFIGS)][1]
    if j < _J_KB:
        return 2, True, dil
    if j < _J_VB:
        return 3, True, dil
    return None, False, dil


_QKV_OUT_WIDTHS = (WA,) * 3 + (WB_OUT,) * (3 * len(DIL_CONFIGS))
_OUT_QA, _OUT_KA, _OUT_VA, _OUT_QB, _OUT_KB, _OUT_VB = 0, 1, 2, 3, 3 + len(DIL_CONFIGS), 3 + 2 * len(DIL_CONFIGS)


def _qkv_tile_out(j, o_refs):
    if j < _J_QB:
        per = WA // COL_TILE
        return o_refs[j // per], (j % per) * COL_TILE
    return o_refs[_OUT_QB + j - _J_QB], 0


def _qkv_kernel(n_cast, x_ref, mod_ref, g1_ref, w_ref, gvec_ref, ones_ref, cos_ref, ssin_ref, *rest):
    n_out = len(_QKV_OUT_WIDTHS)
    cast_in, o_refs = rest[:n_cast], rest[n_cast:n_cast + n_out]
    cast_out = rest[n_cast + n_out:2 * n_cast + n_out]
    h_scr, y_scr, z_scr = rest[2 * n_cast + n_out:]
    for src, dst in zip(cast_in, cast_out):
        dst[...] = src[...].astype(bf16)
    tm = x_ref.shape[1]
    nc = COL_TILE // LANES
    mod = mod_ref[0]
    h_scr[...] = _modulated_norm(x_ref[0], g1_ref[...], mod[1:2], mod[0:1]).astype(bf16)
    head_dim = lax.broadcasted_iota(jnp.int32, (tm // QKV_ROW_SPLIT, COL_TILE), 1) % HEAD_DIM
    first_half = head_dim < ROT_DIM // 2

    wide = {}
    for j in range(_N_J):
        gain_row, rotary, dil = _qkv_tile_kind(j)
        o_ref, c0 = _qkv_tile_out(j, o_refs)
        slot = j % y_scr.shape[0]
        for hb in range(QKV_ROW_SPLIT):
            rows = slice(hb * tm // QKV_ROW_SPLIT, (hb + 1) * tm // QKV_ROW_SPLIT)
            if j % QKV_DOT_TILES == 0:
                wcols = slice(j * COL_TILE, min(j + QKV_DOT_TILES, _N_J) * COL_TILE)
                wide[hb] = jnp.dot(h_scr[rows, :], w_ref[:, wcols], preferred_element_type=f32)
            sub = j % QKV_DOT_TILES
            y = wide[hb][:, sub * COL_TILE:(sub + 1) * COL_TILE]
            if gain_row is not None:
                ss = jnp.dot((y * y).astype(bf16), ones_ref[...], preferred_element_type=f32)
                y = y * lax.rsqrt(ss * (1.0 / HEAD_DIM) + EPS) * gvec_ref[gain_row:gain_row + 1, :]
            if rotary:
                cos, ssin = [jnp.concatenate([t_ref[rows, :]] * nc, axis=1) for t_ref in (cos_ref, ssin_ref)]
                half = ROT_DIM // 2
                up = pltpu.roll(y, COL_TILE - half, 1)
                dn = pltpu.roll(y, half, 1)
                y = y * cos + jnp.where(first_half, up, dn) * ssin
            if dil == 1:
                o_ref[0, rows, c0:c0 + COL_TILE] = y.astype(bf16)
            else:
                for c in range(nc):
                    y_scr[slot, c, rows, :] = y[:, c * LANES:(c + 1) * LANES]
        if dil == 1:
            continue
        seg = DIL_TILE // dil
        for c in range(nc):
            ocols = slice(c0 + c * LANES, c0 + (c + 1) * LANES)
            if dil <= SAFE_STRIDE:
                for r in range(dil):
                    rows = y_scr[slot, c, pl.ds(r, seg, stride=dil), :]
                    o_ref[0, r * seg:(r + 1) * seg, ocols] = rows.astype(bf16)
                continue
            hi = dil // SAFE_STRIDE
            sub = DIL_TILE // SAFE_STRIDE
            for r_lo in range(SAFE_STRIDE):
                z_scr[c, r_lo * sub:(r_lo + 1) * sub, :] = y_scr[slot, c, pl.ds(r_lo, sub, stride=SAFE_STRIDE), :]
            for r_lo in range(SAFE_STRIDE):
                for r_hi in range(hi):
                    r = SAFE_STRIDE * r_hi + r_lo
                    rows = z_scr[c, pl.ds(r_lo * sub + r_hi, seg, stride=hi), :]
                    o_ref[0, r * seg:(r + 1) * seg, ocols] = rows.astype(bf16)


def _qkv(x, mod6, g1, w_qkv, gvec, ones_bd, rot_tables, cast_ws, tm=DIL_TILE):
    bsz, s, d = x.shape
    assert tm == DIL_TILE
    nt = s // tm
    steps = bsz * nt
    assert all(w.shape[0] % (BF16_ROWS * steps) == 0 for w in cast_ws)
    slab_specs = [pl.BlockSpec((w.shape[0] // steps, w.shape[1]), lambda b, i: (b * nt + i, 0)) for w in cast_ws]
    outs = pl.pallas_call(
        functools.partial(_qkv_kernel, len(cast_ws)),
        grid=(bsz, nt),
        in_specs=[pl.BlockSpec((1, tm, d), lambda b, i: (b, i, 0)),
                  pl.BlockSpec((1, 6, d), lambda b, i: (b, 0, 0)),
                  _const_spec(g1.shape),
                  pl.BlockSpec((d, W_QKV), lambda b, i: (0, 0), pipeline_mode=pl.Buffered(1)),
                  _const_spec(gvec.shape),
                  _const_spec(ones_bd.shape),
                  ] + [pl.BlockSpec((tm, LANES), lambda b, i: (i, 0))] * len(rot_tables) + slab_specs,
        out_specs=[pl.BlockSpec((1, tm, w), lambda b, i: (b, i, 0)) for w in _QKV_OUT_WIDTHS] + slab_specs,
        out_shape=[jax.ShapeDtypeStruct((bsz, s, w), bf16) for w in _QKV_OUT_WIDTHS]
        + [jax.ShapeDtypeStruct(w.shape, bf16) for w in cast_ws],
        scratch_shapes=[pltpu.VMEM((tm, d), bf16),
                        pltpu.VMEM((2, COL_TILE // LANES, tm, LANES), f32),
                        pltpu.VMEM((COL_TILE // LANES, tm, LANES), f32)],
        compiler_params=_params(("parallel", "parallel")),
        name="qkv",
    )(x, mod6, g1, w_qkv, gvec, ones_bd, *rot_tables, *cast_ws)
    n_out = len(_QKV_OUT_WIDTHS)
    return outs[:n_out], outs[n_out:]


def _pair_mask(nq):
    row = lax.broadcasted_iota(jnp.int32, (2 * nq, LANES), 0)
    lane = lax.broadcasted_iota(jnp.int32, (2 * nq, LANES), 1)
    return (row < nq) == (lane < HEAD_DIM)


def _stack_pair(qp, own_head):
    q2 = jnp.concatenate([qp, qp], axis=0)
    return jnp.where(own_head, q2, jnp.zeros_like(q2))


def _na_kernel(q_ref, kp_ref, kc_ref, kn_ref, vp_ref, vc_ref, vn_ref, tbl_ref, o_ref, kwin, vwin):
    rb = pl.program_id(1)
    blk = NA_ROWS * GRID_W
    n_rows = pl.num_programs(1) * NA_ROWS
    for t, (kr, vr) in enumerate(((kp_ref, vp_ref), (kc_ref, vc_ref), (kn_ref, vn_ref))):
        kwin[t * blk:(t + 1) * blk, :] = kr[0]
        vwin[t * blk:(t + 1) * blk, :] = vr[0]
    first_head = lax.broadcasted_iota(jnp.int32, (GRID_W, LANES), 1) < HEAD_DIM
    own_head = _pair_mask(GRID_W)
    nkeys = NA_KH * GRID_W
    npair = NA_HEADS // 2
    pair_rows = 2 * GRID_W

    def row_body(a, carry):
        r = rb * NA_ROWS + a
        row_start = jnp.clip(r - NA_KH // 2, 0, n_rows - NA_KH)
        delta = r - row_start
        off = pl.multiple_of((row_start - (rb - 1) * NA_ROWS) * GRID_W, GRID_W)
        qoff = pl.multiple_of(a * GRID_W, GRID_W)
        s_parts = []
        for hp in range(npair):
            cols = slice(hp * LANES, (hp + 1) * LANES)
            q2 = _stack_pair(q_ref[0, pl.ds(qoff, GRID_W), cols], own_head)
            kk = kwin[pl.ds(off, nkeys), cols]
            s_parts.append(lax.dot_general(q2, kk, (((1,), (1,)), ((), ())), preferred_element_type=f32))
        n_off = 2 * NA_KH - 2
        bias = jnp.concatenate(
            [jnp.concatenate([tbl_ref[h * n_off + 2 * kp - delta + NA_KH - 1] for kp in range(NA_KH // 2)], axis=1)
             for h in range(NA_HEADS)], axis=0)
        s = jnp.concatenate(s_parts, axis=0) + bias
        m = jnp.max(s, axis=-1, keepdims=True)
        p = jnp.exp2(s - m)
        inv = 1.0 / jnp.sum(p, axis=-1, keepdims=True)
        pb = p.astype(bf16)
        for hp in range(npair):
            cols = slice(hp * LANES, (hp + 1) * LANES)
            rows = slice(hp * pair_rows, (hp + 1) * pair_rows)
            vv = vwin[pl.ds(off, nkeys), cols]
            pv = jnp.dot(pb[rows], vv, preferred_element_type=f32) * inv[rows]
            o_ref[0, pl.ds(qoff, GRID_W), cols] = jnp.where(first_head, pv[:GRID_W], pv[GRID_W:]).astype(bf16)
        return carry

    lax.fori_loop(0, NA_ROWS, row_body, 0, unroll=NA_ROWS)


def _na(q, k, v, tbl):
    bsz, s, _ = q.shape
    blk = NA_ROWS * GRID_W
    nb = s // blk
    qspec = pl.BlockSpec((1, blk, WA), lambda b, i: (b, i, 0))
    halo = [pl.BlockSpec((1, blk, WA), lambda b, i: (b, jnp.maximum(i - 1, 0), 0)),
            pl.BlockSpec((1, blk, WA), lambda b, i: (b, i, 0)),
            pl.BlockSpec((1, blk, WA), lambda b, i: (b, jnp.minimum(i + 1, nb - 1), 0))]

    return pl.pallas_call(
        _na_kernel,
        grid=(bsz, nb),
        in_specs=[qspec] + halo + halo + [_const_spec(tbl.shape)],
        out_specs=pl.BlockSpec((1, blk, WA), lambda b, i: (b, i, 0)),
        out_shape=jax.ShapeDtypeStruct((bsz, s, WA), bf16),
        scratch_shapes=[pltpu.VMEM((3 * blk, WA), bf16), pltpu.VMEM((3 * blk, WA), bf16)],
        compiler_params=_params(("parallel", "parallel")),
        name="na",
    )(q, k, k, k, v, v, v, tbl)


def _na_bias_table(rpb):
    col = jnp.arange(GRID_W)
    col_start = jnp.clip(col - NA_KW // 2, 0, GRID_W - NA_KW)
    cmask = (col[None, :] >= col_start[:, None]) & (col[None, :] < col_start[:, None] + NA_KW)
    col_off = jnp.clip(col[None, :] - col[:, None] + (NA_KW - 1), 0, 2 * NA_KW - 2)
    onehot = (col_off[None] == jnp.arange(2 * NA_KW - 1)[:, None, None]).astype(f32)
    t = jnp.einsum('hrc,cqk->hrqk', rpb.astype(f32), onehot, precision=lax.Precision.HIGHEST)
    t = jnp.where(cmask, t * LOG2E, NEG_INF)
    t = jnp.concatenate([t[:, :-1], t[:, 1:]], axis=-1)
    return t.reshape(NA_HEADS * (2 * NA_KH - 2), GRID_W, 2 * GRID_W)


def _dil_halo_pieces(dil):
    return 1 if DIL_TILE // dil == DIL_QBLOCK else dil


def _dil_kernel(*refs):
    ng = len(DIL_CONFIGS)
    n_in = sum(1 + 2 * (1 + 2 * _dil_halo_pieces(dil)) for _, dil in DIL_CONFIGS)
    in_refs = list(refs[:n_in])
    o_ref = refs[n_in]
    o_scr, m_scr, l_scr = refs[n_in + 1:]
    ti = pl.program_id(1)
    n_tiles = pl.num_programs(1)
    qb = DIL_QBLOCK
    span = 3 * qb
    sub = DIL_TILE // SAFE_STRIDE

    first_head = lax.broadcasted_iota(jnp.int32, (qb, LANES), 1) < HEAD_DIM
    own_head = _pair_mask(qb)
    npair = DIL_HPG // 2
    qi = lax.broadcasted_iota(jnp.int32, (qb, span), 0)
    kj = lax.broadcasted_iota(jnp.int32, (qb, span), 1)
    band = jnp.where((kj >= qi) & (kj <= qi + 2 * qb), 0.0, NEG_INF).astype(f32)
    kcol = lax.broadcasted_iota(jnp.int32, (1, span), 1)
    pen_lo = jnp.where(kcol < qb, jnp.where(ti == 0, NEG_INF, 0.0), 0.0).astype(f32)
    pen_hi = jnp.where(kcol >= 2 * qb, jnp.where(ti == n_tiles - 1, NEG_INF, 0.0), 0.0).astype(f32)

    for g, (_, dil) in enumerate(DIL_CONFIGS):
        seg = DIL_TILE // dil
        nj = seg // qb
        nh = _dil_halo_pieces(dil)
        q_ref = in_refs.pop(0)
        kc, kp, kn = in_refs.pop(0), [in_refs.pop(0) for _ in range(nh)], [in_refs.pop(0) for _ in range(nh)]
        vc, vp, vn = in_refs.pop(0), [in_refs.pop(0) for _ in range(nh)], [in_refs.pop(0) for _ in range(nh)]

        def halo(pieces, r, cols, nh=nh):
            return pieces[r][0, :, cols] if nh > 1 else pieces[0][0, r * qb:(r + 1) * qb, cols]

        def window(refs3, r, j, cols, seg=seg, nj=nj, halo=halo):
            p_refs, c_ref, n_refs = refs3
            base = r * seg
            if 0 < j < nj - 1:
                return c_ref[0, base + (j - 1) * qb:base + (j + 2) * qb, cols]
            lo = halo(p_refs, r, cols) if j == 0 else c_ref[0, base + (j - 1) * qb:base + j * qb, cols]
            mid = c_ref[0, base + j * qb:base + (j + 1) * qb, cols]
            hi = halo(n_refs, r, cols) if j == nj - 1 else c_ref[0, base + (j + 1) * qb:base + (j + 2) * qb, cols]
            return jnp.concatenate([lo, mid, hi], axis=0)

        for t0 in range(0, dil * nj, DIL_UNROLL):
            blocks = [divmod(t, nj) for t in range(t0, t0 + DIL_UNROLL)]
            s_parts = []
            for k, (r, j) in enumerate(blocks):
                mask = band
                if j == 0:
                    mask = mask + pen_lo
                if j == nj - 1:
                    mask = mask + pen_hi
                for hp in range(npair):
                    cols = slice(hp * LANES, (hp + 1) * LANES)
                    q2 = _stack_pair(q_ref[0, (t0 + k) * qb:(t0 + k + 1) * qb, cols], own_head)
                    kk = window((kp, kc, kn), r, j, cols)
                    s2 = lax.dot_general(q2, kk, (((1,), (1,)), ((), ())), preferred_element_type=f32)
                    s_parts += [s2[:qb] + mask, s2[qb:] + mask]
            s = jnp.concatenate(s_parts, axis=0)
            m = jnp.max(s, axis=-1, keepdims=True)
            p = jnp.exp2(s - m)
            l = jnp.sum(p, axis=-1, keepdims=True)
            pb = p.astype(bf16)
            for k, (r, j) in enumerate(blocks):
                if dil == 1:
                    store_rows = pl.ds(j * qb, qb)
                else:
                    hi = dil // SAFE_STRIDE
                    start = (r % SAFE_STRIDE) * sub + j * qb * hi + r // SAFE_STRIDE
                    store_rows = pl.ds(start, qb, stride=hi) if hi > 1 else pl.ds(start, qb)
                for hp in range(npair):
                    cols = slice(hp * LANES, (hp + 1) * LANES)
                    r0 = (k * npair + hp) * 2 * qb
                    vv = window((vp, vc, vn), r, j, cols)
                    pv = jnp.dot(pb[r0:r0 + 2 * qb], vv, preferred_element_type=f32)
                    slot = g * npair + hp
                    o_scr[slot, store_rows, :] = jnp.where(first_head, pv[:qb], pv[qb:])
                    m_scr[slot, store_rows, :] = jnp.where(first_head, m[r0:r0 + qb], m[r0 + qb:r0 + 2 * qb])
                    l_scr[slot, store_rows, :] = jnp.where(first_head, l[r0:r0 + qb], l[r0 + qb:r0 + 2 * qb])

    for hp in range(npair):
        for c in range(SAFE_STRIDE):
            def stream(scr, g):
                if DIL_CONFIGS[g][1] == 1:
                    return scr[g * npair + hp, pl.ds(c, sub, stride=SAFE_STRIDE), :]
                return scr[g * npair + hp, c * sub:(c + 1) * sub, :]

            ms = [stream(m_scr, g) for g in range(ng)]
            m_all = functools.reduce(jnp.maximum, ms)
            num = jnp.zeros_like(m_all)
            den = jnp.zeros_like(m_all)
            for g in range(ng):
                w = jnp.exp2(ms[g] - m_all)
                num = num + w * stream(o_scr, g)
                den = den + w * stream(l_scr, g)
            o_scr[ng * npair + hp, pl.ds(c, sub, stride=SAFE_STRIDE), :] = num / den
        o_ref[0, :, hp * LANES:(hp + 1) * LANES] = o_scr[ng * npair + hp].astype(bf16)


def _dil(qs, ks, vs):
    bsz, s, _ = qs[0].shape
    nt = s // DIL_TILE
    ng = len(DIL_CONFIGS)
    qb = DIL_QBLOCK
    per_tile = DIL_TILE // qb
    blk = (1, DIL_TILE, WB_OUT)
    cur = pl.BlockSpec(blk, lambda b, i: (b, i, 0))
    in_specs, args = [], []
    for g, (_, dil) in enumerate(DIL_CONFIGS):
        seg = DIL_TILE // dil
        if _dil_halo_pieces(dil) == 1 and seg == qb:
            prev = [pl.BlockSpec(blk, lambda b, i: (b, jnp.maximum(i - 1, 0), 0))]
            nxt = [pl.BlockSpec(blk, lambda b, i: (b, jnp.minimum(i + 1, nt - 1), 0))]
        else:
            prev = [pl.BlockSpec((1, qb, WB_OUT), lambda b, i, o=(r + 1) * seg // qb - 1:
                                 (b, jnp.maximum((i - 1) * per_tile + o, 0), 0)) for r in range(dil)]
            nxt = [pl.BlockSpec((1, qb, WB_OUT), lambda b, i, o=r * seg // qb:
                                (b, jnp.minimum((i + 1) * per_tile + o, s // qb - 1), 0)) for r in range(dil)]
        in_specs += [cur] + 2 * ([cur] + prev + nxt)
        args += [qs[g]] + [ks[g]] * (1 + len(prev) + len(nxt)) + [vs[g]] * (1 + len(prev) + len(nxt))
    return pl.pallas_call(
        _dil_kernel,
        grid=(bsz, nt),
        in_specs=in_specs,
        out_specs=pl.BlockSpec(blk, lambda b, i: (b, i, 0)),
        out_shape=jax.ShapeDtypeStruct((bsz, s, WB_OUT), bf16),
        scratch_shapes=[pltpu.VMEM(((ng + 1) * DIL_HPG // 2, DIL_TILE, LANES), f32)]
        + [pltpu.VMEM((ng * DIL_HPG // 2, DIL_TILE, LANES), f32)] * 2,
        compiler_params=_params(("parallel", "parallel")),
        name="dil",
    )(*args)


def _tail_kernel(x_ref, oa_ref, ob_ref, mod_ref, g1_ref, g2_ref, bg_ref, wpa_ref, wpb_ref,
                 wo_ref, win_ref, wout_ref, *rest):
    d = x_ref.shape[2]
    n_gate = 2 * d // COL_TILE
    wg_refs = rest[:n_gate]
    o_ref, h_scr, m_scr, act_scr = rest[n_gate:]
    mod = mod_ref[0]
    sh1, sc1, gt1, sh2, sc2, gt2 = [mod[k:k + 1] for k in range(6)]
    x = x_ref[0]
    h_scr[...] = _modulated_norm(x, g1_ref[...], sc1, sh1).astype(bf16)
    oa = oa_ref[0]
    ob = ob_ref[0]
    for n in range(d // COL_TILE):
        ca = slice(n * COL_TILE, (n + 1) * COL_TILE)
        cb = slice(d + n * COL_TILE, d + (n + 1) * COL_TILE)
        wga, wgb = wg_refs[n], wg_refs[n_gate // 2 + n]
        ga = jax.nn.sigmoid(jnp.dot(h_scr[...], wga[...], preferred_element_type=f32) + bg_ref[:, ca])
        gb = jax.nn.sigmoid(jnp.dot(h_scr[...], wgb[...], preferred_element_type=f32) + bg_ref[:, cb])
        pa = jnp.dot(oa, wpa_ref[:, ca], preferred_element_type=f32)
        pb = jnp.dot(ob, wpb_ref[:, ca], preferred_element_type=f32)
        m_scr[:, ca] = (ga * pa + gb * pb).astype(bf16)
    x1 = x + gt1 * jnp.dot(m_scr[...], wo_ref[...], preferred_element_type=f32)
    h_scr[...] = _modulated_norm(x1, g2_ref[...], sc2, sh2).astype(bf16)
    for f in range(D_FF // FF_CHUNK):
        ca = slice(f * FF_CHUNK, (f + 1) * FF_CHUNK)
        cu = slice(D_FF + f * FF_CHUNK, D_FF + (f + 1) * FF_CHUNK)
        a = jnp.dot(h_scr[...], win_ref[:, ca], preferred_element_type=f32)
        u = jnp.dot(h_scr[...], win_ref[:, cu], preferred_element_type=f32)
        act_scr[:, ca] = ((a * jax.nn.sigmoid(a)) * u).astype(bf16)
    o_ref[0] = x1 + gt2 * jnp.dot(act_scr[...], wout_ref[...], preferred_element_type=f32)


def _tail(x, o_a, o_b, mod6, g1, g2, w_in, bg, wpa, wpb, wo, w_ffn_in, w_ffn_out, tm=TAIL_TM):
    bsz, s, d = x.shape
    tok = lambda w: pl.BlockSpec((1, tm, w), lambda b, i: (b, i, 0))
    consts = [g1, g2, bg, wpa, wpb, wo, w_ffn_in, w_ffn_out]
    n_gate = 2 * d // COL_TILE
    gate_specs = [pl.BlockSpec((d, COL_TILE), lambda b, i, c=_N_J + n: (0, c), pipeline_mode=pl.Buffered(1))
                  for n in range(n_gate)]
    return pl.pallas_call(
        _tail_kernel,
        grid=(bsz, s // tm),
        in_specs=[tok(d), tok(WA), tok(WB_OUT), pl.BlockSpec((1, 6, d), lambda b, i: (b, 0, 0))]
        + [_const_spec(c.shape) for c in consts] + gate_specs,
        out_specs=tok(d),
        out_shape=jax.ShapeDtypeStruct((bsz, s, d), f32),
        scratch_shapes=[pltpu.VMEM((tm, d), bf16), pltpu.VMEM((tm, d), bf16), pltpu.VMEM((tm, D_FF), bf16)],
        compiler_params=_params(("parallel", "parallel")),
        name="tail",
    )(x, o_a, o_b, mod6, *consts, *([w_in] * n_gate))


def _rotary_tables(s):
    half = ROT_DIM // 2
    inv_freq = ROPE_THETA ** (-(jnp.arange(half, dtype=f32) * 2.0) / ROT_DIM)
    ang = jnp.arange(s).astype(f32)[:, None] * inv_freq[None, :]
    cos, sin = jnp.cos(ang), jnp.sin(ang)
    rest = HEAD_DIM - ROT_DIM
    cos_h = jnp.concatenate([cos, cos, jnp.ones((s, rest), f32)], axis=1)
    ssin_h = jnp.concatenate([-sin, sin, jnp.zeros((s, rest), f32)], axis=1)
    rep = LANES // HEAD_DIM
    return jnp.tile(cos_h, (1, rep)), jnp.tile(ssin_h, (1, rep))


def _layer(x, mod6, g_norm1, g_norm2, w_in, b_gate, g_qa, g_ka, g_qb, g_kb, rpb,
           w_proj_a, w_proj_b, w_o, w_ffn_in, w_ffn_out):
    bsz, s, d = x.shape
    assert d == D_MODEL and w_in.shape == (d, W_QKV + 2 * d) and w_ffn_in.shape == (d, 2 * D_FF)
    assert s % DIL_TILE == 0 and s % (NA_ROWS * GRID_W) == 0 and s % TAIL_TM == 0
    assert s // GRID_W >= NA_KH
    scale = HEAD_DIM ** -0.5 * LOG2E
    rep = COL_TILE // HEAD_DIM
    gvec = jnp.stack([jnp.tile(g_qa * scale, rep), jnp.tile(g_ka, rep),
                      jnp.tile(g_qb * scale, rep), jnp.tile(g_kb, rep)]).astype(f32)
    hid = jnp.arange(COL_TILE) // HEAD_DIM
    ones_bd = (hid[:, None] == hid[None, :]).astype(bf16)
    rot_tables = _rotary_tables(s)
    g1 = g_norm1.reshape(1, d)
    g2 = g_norm2.reshape(1, d)

    qkv, tail_ws = _qkv(x, mod6, g1, w_in, gvec, ones_bd, rot_tables,
                        [w_proj_a, w_proj_b, w_o, w_ffn_in, w_ffn_out])
    ng = len(DIL_CONFIGS)
    o_a = _na(qkv[_OUT_QA], qkv[_OUT_KA], qkv[_OUT_VA], _na_bias_table(rpb))
    o_b = _dil(qkv[_OUT_QB:_OUT_QB + ng], qkv[_OUT_KB:_OUT_KB + ng], qkv[_OUT_VB:_OUT_VB + ng])

    return _tail(x, o_a, o_b, mod6, g1, g2, w_in, b_gate.reshape(1, 2 * d), *tail_ws)


def kernel(x, c, w_ada, b_ada, g_norm1, g_norm2, w_in, b_gate, g_qa, g_ka, g_qb, g_kb, rpb,
           w_proj_a, w_proj_b, w_o, w_ffn_in, w_ffn_out):
    depth = w_ada.shape[0]
    bsz, d = c.shape
    for l in range(depth):
        mod, w_in_bf = _mod(c, w_ada[l], b_ada[l], w_in[l])
        x = _layer(x, mod.reshape(bsz, 6, d), g_norm1[l], g_norm2[l], w_in_bf, b_gate[l], g_qa[l], g_ka[l], g_qb[l],
                   g_kb[l], rpb[l], w_proj_a[l], w_proj_b[l], w_o[l], w_ffn_in[l], w_ffn_out[l])
    return x
```

```python
import functools

import jax
import jax.numpy as jnp
from jax import lax
from jax.experimental import pallas as pl
from jax.experimental.pallas import tpu as pltpu

f32 = jnp.float32
bf16 = jnp.bfloat16

D_MODEL = 1024
HEAD_DIM = 64
GRID_W = 64
NA_HEADS = 8
NA_KH = 8
NA_KW = 16
DIL_CONFIGS = ((128, 1), (512, 4), (2048, 16))
DIL_HPG = 4
DIL_HEADS = DIL_HPG * len(DIL_CONFIGS)
DIL_QBLOCK = 64
ROT_DIM = HEAD_DIM // 4
ROPE_THETA = 500000.0
D_FF = -(-8 * D_MODEL // (3 * 256)) * 256
EPS = 1e-6
NEG_INF = -1e30
LOG2E = 1.4426950408889634
WA = NA_HEADS * HEAD_DIM
WB = DIL_HEADS * HEAD_DIM
WB_OUT = DIL_HPG * HEAD_DIM
W_QKV = 3 * WA + 3 * WB

LANES = 128
BF16_ROWS = 16
COL_TILE = 256
DIL_TILE = 1024
QKV_ROW_SPLIT = 2
QKV_DOT_TILES = 2
DIL_UNROLL = 16
NA_ROWS = 16
MOD_STEPS = 8
TAIL_TM = 1024
FF_CHUNK = 256
VMEM_LIMIT = 56 * 1024 * 1024

SAFE_STRIDE = 4

assert all((win // 2) // dil == DIL_QBLOCK for win, dil in DIL_CONFIGS)
assert all(dil == 1 or (dil % SAFE_STRIDE == 0 and dil // SAFE_STRIDE <= SAFE_STRIDE)
           for _, dil in DIL_CONFIGS)


def _params(sem):
    return pltpu.CompilerParams(dimension_semantics=sem, vmem_limit_bytes=VMEM_LIMIT)


def _const_spec(shape):
    nd = len(shape)
    return pl.BlockSpec(shape, lambda *_: (0,) * nd, pipeline_mode=pl.Buffered(1))


def _mod_kernel(ct_ref, w_ref, b_ref, win_ref, o_ref, win_o_ref):
    ct = ct_ref[...]
    act = ct * jax.nn.sigmoid(ct)
    w = w_ref[...]
    rows = [jnp.sum(act[:, b:b + 1] * w, axis=0, keepdims=True) for b in range(ct.shape[1])]
    o_ref[...] = jnp.concatenate(rows, axis=0) + b_ref[...]
    win_o_ref[...] = win_ref[...].astype(bf16)


def _mod(c, w_ada, b_ada, w_in):
    bsz, d = c.shape
    n = w_ada.shape[1]
    steps = MOD_STEPS
    tn = n // steps
    slab = pl.BlockSpec((w_in.shape[0] // steps, w_in.shape[1]), lambda j: (j, 0))
    return pl.pallas_call(
        _mod_kernel,
        grid=(steps,),
        in_specs=[pl.BlockSpec((d, bsz), lambda j: (0, 0)),
                  pl.BlockSpec((d, tn), lambda j: (0, j)),
                  pl.BlockSpec((1, tn), lambda j: (0, j)),
                  slab],
        out_specs=[pl.BlockSpec((bsz, tn), lambda j: (0, j)), slab],
        out_shape=[jax.ShapeDtypeStruct((bsz, n), f32), jax.ShapeDtypeStruct(w_in.shape, bf16)],
        compiler_params=_params(("arbitrary",)),
        name="mod",
    )(c.T, w_ada, b_ada.reshape(1, n), w_in)


def _modulated_norm(x, g, sc, sh):
    ms = jnp.mean(x * x, axis=-1, keepdims=True)
    y = x * lax.rsqrt(ms + EPS) * g
    return y * (1.0 + sc) + sh


_J_QB = 3 * WA // COL_TILE
_J_KB = _J_QB + WB // COL_TILE
_J_VB = _J_KB + WB // COL_TILE
_N_J = W_QKV // COL_TILE


def _qkv_tile_kind(j):
    if j < WA // COL_TILE:
        return 0, False, 1
    if j < 2 * WA // COL_TILE:
        return 1, False, 1
    if j < _J_QB:
        return None, False, 1
    dil = DIL_CONFIGS[(j - _J_QB) % len(DIL_CONFIGS)][1]
    if j < _J_KB:
        return 2, True, dil
    if j < _J_VB:
        return 3, True, dil
    return None, False, dil


_QKV_OUT_WIDTHS = (WA,) * 3 + (WB_OUT,) * (3 * len(DIL_CONFIGS))
_OUT_QA, _OUT_KA, _OUT_VA, _OUT_QB, _OUT_KB, _OUT_VB = 0, 1, 2, 3, 3 + len(DIL_CONFIGS), 3 + 2 * len(DIL_CONFIGS)


def _qkv_tile_out(j, o_refs):
    if j < _J_QB:
        per = WA // COL_TILE
        return o_refs[j // per], (j % per) * COL_TILE
    return o_refs[_OUT_QB + j - _J_QB], 0


def _qkv_kernel(n_cast, x_ref, mod_ref, g1_ref, w_ref, gvec_ref, ones_ref, cos_ref, ssin_ref, *rest):
    n_out = len(_QKV_OUT_WIDTHS)
    cast_in, o_refs = rest[:n_cast], rest[n_cast:n_cast + n_out]
    cast_out = rest[n_cast + n_out:2 * n_cast + n_out]
    h_scr, y_scr, z_scr = rest[2 * n_cast + n_out:]
    for src, dst in zip(cast_in, cast_out):
        dst[...] = src[...].astype(bf16)
    tm = x_ref.shape[1]
    nc = COL_TILE // LANES
    mod = mod_ref[0]
    h_scr[...] = _modulated_norm(x_ref[0], g1_ref[...], mod[1:2], mod[0:1]).astype(bf16)
    head_dim = lax.broadcasted_iota(jnp.int32, (tm // QKV_ROW_SPLIT, COL_TILE), 1) % HEAD_DIM
    first_half = head_dim < ROT_DIM // 2

    wide = {}
    for j in range(_N_J):
        gain_row, rotary, dil = _qkv_tile_kind(j)
        o_ref, c0 = _qkv_tile_out(j, o_refs)
        slot = j % y_scr.shape[0]
        for hb in range(QKV_ROW_SPLIT):
            rows = slice(hb * tm // QKV_ROW_SPLIT, (hb + 1) * tm // QKV_ROW_SPLIT)
            if j % QKV_DOT_TILES == 0:
                wcols = slice(j * COL_TILE, min(j + QKV_DOT_TILES, _N_J) * COL_TILE)
                wide[hb] = jnp.dot(h_scr[rows, :], w_ref[:, wcols], preferred_element_type=f32)
            sub = j % QKV_DOT_TILES
            y = wide[hb][:, sub * COL_TILE:(sub + 1) * COL_TILE]
            if gain_row is not None:
                ss = jnp.dot((y * y).astype(bf16), ones_ref[...], preferred_element_type=f32)
                y = y * lax.rsqrt(ss * (1.0 / HEAD_DIM) + EPS) * gvec_ref[gain_row:gain_row + 1, :]
            if rotary:
                cos, ssin = [jnp.concatenate([t_ref[rows, :]] * nc, axis=1) for t_ref in (cos_ref, ssin_ref)]
                half = ROT_DIM // 2
                up = pltpu.roll(y, COL_TILE - half, 1)
                dn = pltpu.roll(y, half, 1)
                y = y * cos + jnp.where(first_half, up, dn) * ssin
            if dil == 1:
                o_ref[0, rows, c0:c0 + COL_TILE] = y.astype(bf16)
            else:
                for c in range(nc):
                    y_scr[slot, c, rows, :] = y[:, c * LANES:(c + 1) * LANES]
        if dil == 1:
            continue
        seg = DIL_TILE // dil
        for c in range(nc):
            ocols = slice(c0 + c * LANES, c0 + (c + 1) * LANES)
            if dil <= SAFE_STRIDE:
                for r in range(dil):
                    rows = y_scr[slot, c, pl.ds(r, seg, stride=dil), :]
                    o_ref[0, r * seg:(r + 1) * seg, ocols] = rows.astype(bf16)
                continue
            hi = dil // SAFE_STRIDE
            sub = DIL_TILE // SAFE_STRIDE
            for r_lo in range(SAFE_STRIDE):
                z_scr[c, r_lo * sub:(r_lo + 1) * sub, :] = y_scr[slot, c, pl.ds(r_lo, sub, stride=SAFE_STRIDE), :]
            for r_lo in range(SAFE_STRIDE):
                for r_hi in range(hi):
                    r = SAFE_STRIDE * r_hi + r_lo
                    rows = z_scr[c, pl.ds(r_lo * sub + r_hi, seg, stride=hi), :]
                    o_ref[0, r * seg:(r + 1) * seg, ocols] = rows.astype(bf16)


def _qkv(x, mod6, g1, w_qkv, gvec, ones_bd, rot_tables, cast_ws, tm=DIL_TILE):
    bsz, s, d = x.shape
    assert tm == DIL_TILE
    nt = s // tm
    steps = bsz * nt
    assert all(w.shape[0] % (BF16_ROWS * steps) == 0 for w in cast_ws)
    slab_specs = [pl.BlockSpec((w.shape[0] // steps, w.shape[1]), lambda b, i: (b * nt + i, 0)) for w in cast_ws]
    outs = pl.pallas_call(
        functools.partial(_qkv_kernel, len(cast_ws)),
        grid=(bsz, nt),
        in_specs=[pl.BlockSpec((1, tm, d), lambda b, i: (b, i, 0)),
                  pl.BlockSpec((1, 6, d), lambda b, i: (b, 0, 0)),
                  _const_spec(g1.shape),
                  pl.BlockSpec((d, W_QKV), lambda b, i: (0, 0), pipeline_mode=pl.Buffered(1)),
                  _const_spec(gvec.shape),
                  _const_spec(ones_bd.shape),
                  ] + [pl.BlockSpec((tm, LANES), lambda b, i: (i, 0))] * len(rot_tables) + slab_specs,
        out_specs=[pl.BlockSpec((1, tm, w), lambda b, i: (b, i, 0)) for w in _QKV_OUT_WIDTHS] + slab_specs,
        out_shape=[jax.ShapeDtypeStruct((bsz, s, w), bf16) for w in _QKV_OUT_WIDTHS]
        + [jax.ShapeDtypeStruct(w.shape, bf16) for w in cast_ws],
        scratch_shapes=[pltpu.VMEM((tm, d), bf16),
                        pltpu.VMEM((2, COL_TILE // LANES, tm, LANES), f32),
                        pltpu.VMEM((COL_TILE // LANES, tm, LANES), f32)],
        compiler_params=_params(("parallel", "parallel")),
        name="qkv",
    )(x, mod6, g1, w_qkv, gvec, ones_bd, *rot_tables, *cast_ws)
    n_out = len(_QKV_OUT_WIDTHS)
    return outs[:n_out], outs[n_out:]


def _pair_mask(nq):
    row = lax.broadcasted_iota(jnp.int32, (2 * nq, LANES), 0)
    lane = lax.broadcasted_iota(jnp.int32, (2 * nq, LANES), 1)
    return (row < nq) == (lane < HEAD_DIM)


def _stack_pair(qp, own_head):
    q2 = jnp.concatenate([qp, qp], axis=0)
    return jnp.where(own_head, q2, jnp.zeros_like(q2))


def _na_window_start(rb, n_rows):
    return jnp.clip(rb * NA_ROWS - NA_KH // 2, 0, n_rows - (NA_ROWS + NA_KH))


def _na_kernel(q_ref, kwin, vwin, tbl_ref, o_ref):
    rb = pl.program_id(1)
    n_rows = pl.num_programs(1) * NA_ROWS
    win_start = _na_window_start(rb, n_rows)
    first_head = lax.broadcasted_iota(jnp.int32, (GRID_W, LANES), 1) < HEAD_DIM
    own_head = _pair_mask(GRID_W)
    nkeys = NA_KH * GRID_W
    npair = NA_HEADS // 2
    pair_rows = 2 * GRID_W

    def row_body(a, carry):
        r = rb * NA_ROWS + a
        row_start = jnp.clip(r - NA_KH // 2, 0, n_rows - NA_KH)
        delta = r - row_start
        off = pl.multiple_of((row_start - win_start) * GRID_W, GRID_W)
        qoff = pl.multiple_of(a * GRID_W, GRID_W)
        s_parts = []
        for hp in range(npair):
            cols = slice(hp * LANES, (hp + 1) * LANES)
            q2 = _stack_pair(q_ref[0, pl.ds(qoff, GRID_W), cols], own_head)
            kk = kwin[0, pl.ds(off, nkeys), cols]
            s_parts.append(lax.dot_general(q2, kk, (((1,), (1,)), ((), ())), preferred_element_type=f32))
        n_off = 2 * NA_KH - 2
        bias = jnp.concatenate(
            [jnp.concatenate([tbl_ref[h * n_off + 2 * kp - delta + NA_KH - 1] for kp in range(NA_KH // 2)], axis=1)
             for h in range(NA_HEADS)], axis=0)
        s = jnp.concatenate(s_parts, axis=0) + bias
        m = jnp.max(s, axis=-1, keepdims=True)
        p = jnp.exp2(s - m)
        inv = 1.0 / jnp.sum(p, axis=-1, keepdims=True)
        pb = p.astype(bf16)
        for hp in range(npair):
            cols = slice(hp * LANES, (hp + 1) * LANES)
            rows = slice(hp * pair_rows, (hp + 1) * pair_rows)
            vv = vwin[0, pl.ds(off, nkeys), cols]
            pv = jnp.dot(pb[rows], vv, preferred_element_type=f32) * inv[rows]
            o_ref[0, pl.ds(qoff, GRID_W), cols] = jnp.where(first_head, pv[:GRID_W], pv[GRID_W:]).astype(bf16)
        return carry

    lax.fori_loop(0, NA_ROWS, row_body, 0, unroll=NA_ROWS)


def _na(q, k, v, tbl):
    bsz, s, _ = q.shape
    blk = NA_ROWS * GRID_W
    nb = s // blk
    qspec = pl.BlockSpec((1, blk, WA), lambda b, i: (b, i, 0))
    win_tokens = (NA_ROWS + NA_KH) * GRID_W
    n_rows = s // GRID_W
    window = pl.BlockSpec(
        (pl.Element(1), pl.Element(win_tokens), pl.Element(WA)),
        lambda b, i: (b, pl.multiple_of(_na_window_start(i, n_rows) * GRID_W, GRID_W), 0))

    return pl.pallas_call(
        _na_kernel,
        grid=(bsz, nb),
        in_specs=[qspec, window, window, _const_spec(tbl.shape)],
        out_specs=pl.BlockSpec((1, blk, WA), lambda b, i: (b, i, 0)),
        out_shape=jax.ShapeDtypeStruct((bsz, s, WA), bf16),
        compiler_params=_params(("parallel", "parallel")),
        name="na",
    )(q, k, v, tbl)


def _na_bias_table(rpb):
    col = jnp.arange(GRID_W)
    col_start = jnp.clip(col - NA_KW // 2, 0, GRID_W - NA_KW)
    cmask = (col[None, :] >= col_start[:, None]) & (col[None, :] < col_start[:, None] + NA_KW)
    col_off = jnp.clip(col[None, :] - col[:, None] + (NA_KW - 1), 0, 2 * NA_KW - 2)
    onehot = (col_off[None] == jnp.arange(2 * NA_KW - 1)[:, None, None]).astype(f32)
    t = jnp.einsum('hrc,cqk->hrqk', rpb.astype(f32), onehot, precision=lax.Precision.HIGHEST)
    t = jnp.where(cmask, t * LOG2E, NEG_INF)
    t = jnp.concatenate([t[:, :-1], t[:, 1:]], axis=-1)
    return t.reshape(NA_HEADS * (2 * NA_KH - 2), GRID_W, 2 * GRID_W)


def _dil_halo_pieces(dil):
    return 1 if DIL_TILE // dil == DIL_QBLOCK else dil


def _dil_kernel(*refs):
    ng = len(DIL_CONFIGS)
    n_in = sum(1 + 2 * (1 + 2 * _dil_halo_pieces(dil)) for _, dil in DIL_CONFIGS)
    in_refs = list(refs[:n_in])
    o_ref = refs[n_in]
    o_scr, m_scr, l_scr = refs[n_in + 1:]
    ti = pl.program_id(1)
    n_tiles = pl.num_programs(1)
    qb = DIL_QBLOCK
    span = 3 * qb
    sub = DIL_TILE // SAFE_STRIDE

    first_head = lax.broadcasted_iota(jnp.int32, (qb, LANES), 1) < HEAD_DIM
    own_head = _pair_mask(qb)
    npair = DIL_HPG // 2
    qi = lax.broadcasted_iota(jnp.int32, (qb, span), 0)
    kj = lax.broadcasted_iota(jnp.int32, (qb, span), 1)
    band = jnp.where((kj >= qi) & (kj <= qi + 2 * qb), 0.0, NEG_INF).astype(f32)
    kcol = lax.broadcasted_iota(jnp.int32, (1, span), 1)
    pen_lo = jnp.where(kcol < qb, jnp.where(ti == 0, NEG_INF, 0.0), 0.0).astype(f32)
    pen_hi = jnp.where(kcol >= 2 * qb, jnp.where(ti == n_tiles - 1, NEG_INF, 0.0), 0.0).astype(f32)

    for g, (_, dil) in enumerate(DIL_CONFIGS):
        seg = DIL_TILE // dil
        nj = seg // qb
        nh = _dil_halo_pieces(dil)
        q_ref = in_refs.pop(0)
        kc, kp, kn = in_refs.pop(0), [in_refs.pop(0) for _ in range(nh)], [in_refs.pop(0) for _ in range(nh)]
        vc, vp, vn = in_refs.pop(0), [in_refs.pop(0) for _ in range(nh)], [in_refs.pop(0) for _ in range(nh)]

        def halo(pieces, r, cols, nh=nh):
            return pieces[r][0, :, cols] if nh > 1 else pieces[0][0, r * qb:(r + 1) * qb, cols]

        def window(refs3, r, j, cols, seg=seg, nj=nj, halo=halo):
            p_refs, c_ref, n_refs = refs3
            base = r * seg
            if 0 < j < nj - 1:
                return c_ref[0, base + (j - 1) * qb:base + (j + 2) * qb, cols]
            lo = halo(p_refs, r, cols) if j == 0 else c_ref[0, base + (j - 1) * qb:base + j * qb, cols]
            mid = c_ref[0, base + j * qb:base + (j + 1) * qb, cols]
            hi = halo(n_refs, r, cols) if j == nj - 1 else c_ref[0, base + (j + 1) * qb:base + (j + 2) * qb, cols]
            return jnp.concatenate([lo, mid, hi], axis=0)

        for t0 in range(0, dil * nj, DIL_UNROLL):
            blocks = [divmod(t, nj) for t in range(t0, t0 + DIL_UNROLL)]
            s_parts = []
            for k, (r, j) in enumerate(blocks):
                mask = band
                if j == 0:
                    mask = mask + pen_lo
                if j == nj - 1:
                    mask = mask + pen_hi
                for hp in range(npair):
                    cols = slice(hp * LANES, (hp + 1) * LANES)
                    q2 = _stack_pair(q_ref[0, (t0 + k) * qb:(t0 + k + 1) * qb, cols], own_head)
                    kk = window((kp, kc, kn), r, j, cols)
                    s2 = lax.dot_general(q2, kk, (((1,), (1,)), ((), ())), preferred_element_type=f32)
                    s_parts += [s2[:qb] + mask, s2[qb:] + mask]
            s = jnp.concatenate(s_parts, axis=0)
            m = jnp.max(s, axis=-1, keepdims=True)
            p = jnp.exp2(s - m)
            l = jnp.sum(p, axis=-1, keepdims=True)
            pb = p.astype(bf16)
            for k, (r, j) in enumerate(blocks):
                if dil == 1:
                    store_rows = pl.ds(j * qb, qb)
                else:
                    hi = dil // SAFE_STRIDE
                    start = (r % SAFE_STRIDE) * sub + j * qb * hi + r // SAFE_STRIDE
                    store_rows = pl.ds(start, qb, stride=hi) if hi > 1 else pl.ds(start, qb)
                for hp in range(npair):
                    cols = slice(hp * LANES, (hp + 1) * LANES)
                    r0 = (k * npair + hp) * 2 * qb
                    vv = window((vp, vc, vn), r, j, cols)
                    pv = jnp.dot(pb[r0:r0 + 2 * qb], vv, preferred_element_type=f32)
                    slot = g * npair + hp
                    o_scr[slot, store_rows, :] = jnp.where(first_head, pv[:qb], pv[qb:])
                    m_scr[slot, store_rows, :] = jnp.where(first_head, m[r0:r0 + qb], m[r0 + qb:r0 + 2 * qb])
                    l_scr[slot, store_rows, :] = jnp.where(first_head, l[r0:r0 + qb], l[r0 + qb:r0 + 2 * qb])

    for hp in range(npair):
        for c in range(SAFE_STRIDE):
            def stream(scr, g):
                if DIL_CONFIGS[g][1] == 1:
                    return scr[g * npair + hp, pl.ds(c, sub, stride=SAFE_STRIDE), :]
                return scr[g * npair + hp, c * sub:(c + 1) * sub, :]

            ms = [stream(m_scr, g) for g in range(ng)]
            m_all = functools.reduce(jnp.maximum, ms)
            num = jnp.zeros_like(m_all)
            den = jnp.zeros_like(m_all)
            for g in range(ng):
                w = jnp.exp2(ms[g] - m_all)
                num = num + w * stream(o_scr, g)
                den = den + w * stream(l_scr, g)
            o_scr[ng * npair + hp, pl.ds(c, sub, stride=SAFE_STRIDE), :] = num / den
        o_ref[0, :, hp * LANES:(hp + 1) * LANES] = o_scr[ng * npair + hp].astype(bf16)


def _dil(qs, ks, vs):
    bsz, s, _ = qs[0].shape
    nt = s // DIL_TILE
    ng = len(DIL_CONFIGS)
    qb = DIL_QBLOCK
    per_tile = DIL_TILE // qb
    blk = (1, DIL_TILE, WB_OUT)
    cur = pl.BlockSpec(blk, lambda b, i: (b, i, 0))
    in_specs, args = [], []
    for g, (_, dil) in enumerate(DIL_CONFIGS):
        seg = DIL_TILE // dil
        if _dil_halo_pieces(dil) == 1 and seg == qb:
            prev = [pl.BlockSpec(blk, lambda b, i: (b, jnp.maximum(i - 1, 0), 0))]
            nxt = [pl.BlockSpec(blk, lambda b, i: (b, jnp.minimum(i + 1, nt - 1), 0))]
        else:
            prev = [pl.BlockSpec((1, qb, WB_OUT), lambda b, i, o=(r + 1) * seg // qb - 1:
                                 (b, jnp.maximum((i - 1) * per_tile + o, 0), 0)) for r in range(dil)]
            nxt = [pl.BlockSpec((1, qb, WB_OUT), lambda b, i, o=r * seg // qb:
                                (b, jnp.minimum((i + 1) * per_tile + o, s // qb - 1), 0)) for r in range(dil)]
        in_specs += [cur] + 2 * ([cur] + prev + nxt)
        args += [qs[g]] + [ks[g]] * (1 + len(prev) + len(nxt)) + [vs[g]] * (1 + len(prev) + len(nxt))
    return pl.pallas_call(
        _dil_kernel,
        grid=(bsz, nt),
        in_specs=in_specs,
        out_specs=pl.BlockSpec(blk, lambda b, i: (b, i, 0)),
        out_shape=jax.ShapeDtypeStruct((bsz, s, WB_OUT), bf16),
        scratch_shapes=[pltpu.VMEM(((ng + 1) * DIL_HPG // 2, DIL_TILE, LANES), f32)]
        + [pltpu.VMEM((ng * DIL_HPG // 2, DIL_TILE, LANES), f32)] * 2,
        compiler_params=_params(("parallel", "parallel")),
        name="dil",
    )(*args)


def _tail_kernel(x_ref, oa_ref, ob_ref, mod_ref, g1_ref, g2_ref, bg_ref, wpa_ref, wpb_ref,
                 wo_ref, win_ref, wout_ref, *rest):
    d = x_ref.shape[2]
    n_gate = 2 * d // COL_TILE
    wg_refs = rest[:n_gate]
    o_ref, h_scr, m_scr, act_scr = rest[n_gate:]
    mod = mod_ref[0]
    sh1, sc1, gt1, sh2, sc2, gt2 = [mod[k:k + 1] for k in range(6)]
    x = x_ref[0]
    h_scr[...] = _modulated_norm(x, g1_ref[...], sc1, sh1).astype(bf16)
    oa = oa_ref[0]
    ob = ob_ref[0]
    for n in range(d // COL_TILE):
        ca = slice(n * COL_TILE, (n + 1) * COL_TILE)
        cb = slice(d + n * COL_TILE, d + (n + 1) * COL_TILE)
        wga, wgb = wg_refs[n], wg_refs[n_gate // 2 + n]
        ga = jax.nn.sigmoid(jnp.dot(h_scr[...], wga[...], preferred_element_type=f32) + bg_ref[:, ca])
        gb = jax.nn.sigmoid(jnp.dot(h_scr[...], wgb[...], preferred_element_type=f32) + bg_ref[:, cb])
        pa = jnp.dot(oa, wpa_ref[:, ca], preferred_element_type=f32)
        pb = jnp.dot(ob, wpb_ref[:, ca], preferred_element_type=f32)
        m_scr[:, ca] = (ga * pa + gb * pb).astype(bf16)
    x1 = x + gt1 * jnp.dot(m_scr[...], wo_ref[...], preferred_element_type=f32)
    h_scr[...] = _modulated_norm(x1, g2_ref[...], sc2, sh2).astype(bf16)
    for f in range(D_FF // FF_CHUNK):
        ca = slice(f * FF_CHUNK, (f + 1) * FF_CHUNK)
        cu = slice(D_FF + f * FF_CHUNK, D_FF + (f + 1) * FF_CHUNK)
        a = jnp.dot(h_scr[...], win_ref[:, ca], preferred_element_type=f32)
        u = jnp.dot(h_scr[...], win_ref[:, cu], preferred_element_type=f32)
        act_scr[:, ca] = ((a * jax.nn.sigmoid(a)) * u).astype(bf16)
    o_ref[0] = x1 + gt2 * jnp.dot(act_scr[...], wout_ref[...], preferred_element_type=f32)


def _tail(x, o_a, o_b, mod6, g1, g2, w_in, bg, wpa, wpb, wo, w_ffn_in, w_ffn_out, tm=TAIL_TM):
    bsz, s, d = x.shape
    tok = lambda w: pl.BlockSpec((1, tm, w), lambda b, i: (b, i, 0))
    consts = [g1, g2, bg, wpa, wpb, wo, w_ffn_in, w_ffn_out]
    n_gate = 2 * d // COL_TILE
    gate_specs = [pl.BlockSpec((d, COL_TILE), lambda b, i, c=_N_J + n: (0, c), pipeline_mode=pl.Buffered(1))
                  for n in range(n_gate)]
    return pl.pallas_call(
        _tail_kernel,
        grid=(bsz, s // tm),
        in_specs=[tok(d), tok(WA), tok(WB_OUT), pl.BlockSpec((1, 6, d), lambda b, i: (b, 0, 0))]
        + [_const_spec(c.shape) for c in consts] + gate_specs,
        out_specs=tok(d),
        out_shape=jax.ShapeDtypeStruct((bsz, s, d), f32),
        scratch_shapes=[pltpu.VMEM((tm, d), bf16), pltpu.VMEM((tm, d), bf16), pltpu.VMEM((tm, D_FF), bf16)],
        compiler_params=_params(("parallel", "parallel")),
        name="tail",
    )(x, o_a, o_b, mod6, *consts, *([w_in] * n_gate))


def _rotary_tables(s):
    half = ROT_DIM // 2
    inv_freq = ROPE_THETA ** (-(jnp.arange(half, dtype=f32) * 2.0) / ROT_DIM)
    ang = jnp.arange(s).astype(f32)[:, None] * inv_freq[None, :]
    cos, sin = jnp.cos(ang), jnp.sin(ang)
    rest = HEAD_DIM - ROT_DIM
    cos_h = jnp.concatenate([cos, cos, jnp.ones((s, rest), f32)], axis=1)
    ssin_h = jnp.concatenate([-sin, sin, jnp.zeros((s, rest), f32)], axis=1)
    rep = LANES // HEAD_DIM
    return jnp.tile(cos_h, (1, rep)), jnp.tile(ssin_h, (1, rep))


def _layer(x, mod6, g_norm1, g_norm2, w_in, b_gate, g_qa, g_ka, g_qb, g_kb, rpb,
           w_proj_a, w_proj_b, w_o, w_ffn_in, w_ffn_out):
    bsz, s, d = x.shape
    assert d == D_MODEL and w_in.shape == (d, W_QKV + 2 * d) and w_ffn_in.shape == (d, 2 * D_FF)
    assert s % DIL_TILE == 0 and s % (NA_ROWS * GRID_W) == 0 and s % TAIL_TM == 0
    assert s // GRID_W >= NA_ROWS + NA_KH
    scale = HEAD_DIM ** -0.5 * LOG2E
    rep = COL_TILE // HEAD_DIM
    gvec = jnp.stack([jnp.tile(g_qa * scale, rep), jnp.tile(g_ka, rep),
                      jnp.tile(g_qb * scale, rep), jnp.tile(g_kb, rep)]).astype(f32)
    hid = jnp.arange(COL_TILE) // HEAD_DIM
    ones_bd = (hid[:, None] == hid[None, :]).astype(bf16)
    rot_tables = _rotary_tables(s)
    g1 = g_norm1.reshape(1, d)
    g2 = g_norm2.reshape(1, d)

    qkv, tail_ws = _qkv(x, mod6, g1, w_in, gvec, ones_bd, rot_tables,
                        [w_proj_a, w_proj_b, w_o, w_ffn_in, w_ffn_out])
    ng = len(DIL_CONFIGS)
    o_a = _na(qkv[_OUT_QA], qkv[_OUT_KA], qkv[_OUT_VA], _na_bias_table(rpb))
    o_b = _dil(qkv[_OUT_QB:_OUT_QB + ng], qkv[_OUT_KB:_OUT_KB + ng], qkv[_OUT_VB:_OUT_VB + ng])

    return _tail(x, o_a, o_b, mod6, g1, g2, w_in, b_gate.reshape(1, 2 * d), *tail_ws)


def kernel(x, c, w_ada, b_ada, g_norm1, g_norm2, w_in, b_gate, g_qa, g_ka, g_qb, g_kb, rpb,
           w_proj_a, w_proj_b, w_o, w_ffn_in, w_ffn_out):
    depth = w_ada.shape[0]
    bsz, d = c.shape
    for l in range(depth):
        mod, w_in_bf = _mod(c, w_ada[l], b_ada[l], w_in[l])
        x = _layer(x, mod.reshape(bsz, 6, d), g_norm1[l], g_norm2[l], w_in_bf, b_gate[l], g_qa[l], g_ka[l], g_qb[l],
                   g_kb[l], rpb[l], w_proj_a[l], w_proj_b[l], w_o[l], w_ffn_in[l], w_ffn_out[l])
    return x
```

```python
import functools

import jax
import jax.numpy as jnp
from jax import lax
from jax.experimental import pallas as pl
from jax.experimental.pallas import tpu as pltpu

f32 = jnp.float32
bf16 = jnp.bfloat16

D_MODEL = 1024
HEAD_DIM = 64
GRID_W = 64
NA_HEADS = 8
NA_KH = 8
NA_KW = 16
DIL_CONFIGS = ((128, 1), (512, 4), (2048, 16))
DIL_HPG = 4
DIL_HEADS = DIL_HPG * len(DIL_CONFIGS)
DIL_QBLOCK = 64
ROT_DIM = HEAD_DIM // 4
ROPE_THETA = 500000.0
D_FF = -(-8 * D_MODEL // (3 * 256)) * 256
EPS = 1e-6
NEG_INF = -1e30
LOG2E = 1.4426950408889634
WA = NA_HEADS * HEAD_DIM
WB = DIL_HEADS * HEAD_DIM
WB_OUT = DIL_HPG * HEAD_DIM
W_QKV = 3 * WA + 3 * WB

LANES = 128
BF16_ROWS = 16
COL_TILE = 256
DIL_TILE = 1024
QKV_ROW_SPLIT = 2
QKV_DOT_TILES = 2
DIL_UNROLL = 16
NA_ROWS = 16
MOD_STEPS = 8
TAIL_TM = 1024
TAIL_ROW_SPLIT = 2
FF_CHUNK = 256
VMEM_LIMIT = 56 * 1024 * 1024

SAFE_STRIDE = 4

assert all((win // 2) // dil == DIL_QBLOCK for win, dil in DIL_CONFIGS)
assert all(dil == 1 or (dil % SAFE_STRIDE == 0 and dil // SAFE_STRIDE <= SAFE_STRIDE)
           for _, dil in DIL_CONFIGS)


def _params(sem):
    return pltpu.CompilerParams(dimension_semantics=sem, vmem_limit_bytes=VMEM_LIMIT)


def _const_spec(shape):
    nd = len(shape)
    return pl.BlockSpec(shape, lambda *_: (0,) * nd, pipeline_mode=pl.Buffered(1))


def _mod_kernel(ct_ref, w_ref, b_ref, win_ref, o_ref, win_o_ref):
    ct = ct_ref[...]
    act = ct * jax.nn.sigmoid(ct)
    w = w_ref[...]
    rows = [jnp.sum(act[:, b:b + 1] * w, axis=0, keepdims=True) for b in range(ct.shape[1])]
    o_ref[...] = jnp.concatenate(rows, axis=0) + b_ref[...]
    win_o_ref[...] = win_ref[...].astype(bf16)


def _mod(c, w_ada, b_ada, w_in):
    bsz, d = c.shape
    n = w_ada.shape[1]
    steps = MOD_STEPS
    tn = n // steps
    slab = pl.BlockSpec((w_in.shape[0] // steps, w_in.shape[1]), lambda j: (j, 0))
    return pl.pallas_call(
        _mod_kernel,
        grid=(steps,),
        in_specs=[pl.BlockSpec((d, bsz), lambda j: (0, 0)),
                  pl.BlockSpec((d, tn), lambda j: (0, j)),
                  pl.BlockSpec((1, tn), lambda j: (0, j)),
                  slab],
        out_specs=[pl.BlockSpec((bsz, tn), lambda j: (0, j)), slab],
        out_shape=[jax.ShapeDtypeStruct((bsz, n), f32), jax.ShapeDtypeStruct(w_in.shape, bf16)],
        compiler_params=_params(("arbitrary",)),
        name="mod",
    )(c.T, w_ada, b_ada.reshape(1, n), w_in)


def _modulated_norm(x, g, sc, sh):
    ms = jnp.mean(x * x, axis=-1, keepdims=True)
    y = x * lax.rsqrt(ms + EPS) * g
    return y * (1.0 + sc) + sh


_J_QB = 3 * WA // COL_TILE
_J_KB = _J_QB + WB // COL_TILE
_J_VB = _J_KB + WB // COL_TILE
_N_J = W_QKV // COL_TILE


def _qkv_tile_kind(j):
    if j < WA // COL_TILE:
        return 0, False, 1
    if j < 2 * WA // COL_TILE:
        return 1, False, 1
    if j < _J_QB:
        return None, False, 1
    dil = DIL_CONFIGS[(j - _J_QB) % len(DIL_CONFIGS)][1]
    if j < _J_KB:
        return 2, True, dil
    if j < _J_VB:
        return 3, True, dil
    return None, False, dil


_QKV_OUT_WIDTHS = (WA,) * 3 + (WB_OUT,) * (3 * len(DIL_CONFIGS))
_OUT_QA, _OUT_KA, _OUT_VA, _OUT_QB, _OUT_KB, _OUT_VB = 0, 1, 2, 3, 3 + len(DIL_CONFIGS), 3 + 2 * len(DIL_CONFIGS)


def _qkv_tile_out(j, o_refs):
    if j < _J_QB:
        per = WA // COL_TILE
        return o_refs[j // per], (j % per) * COL_TILE
    return o_refs[_OUT_QB + j - _J_QB], 0


def _qkv_kernel(n_cast, x_ref, mod_ref, g1_ref, w_ref, gvec_ref, ones_ref, cos_ref, ssin_ref, *rest):
    n_out = len(_QKV_OUT_WIDTHS)
    cast_in, o_refs = rest[:n_cast], rest[n_cast:n_cast + n_out]
    cast_out = rest[n_cast + n_out:2 * n_cast + n_out]
    h_scr, y_scr, z_scr = rest[2 * n_cast + n_out:]
    for src, dst in zip(cast_in, cast_out):
        dst[...] = src[...].astype(bf16)
    tm = x_ref.shape[1]
    nc = COL_TILE // LANES
    mod = mod_ref[0]
    h_scr[...] = _modulated_norm(x_ref[0], g1_ref[...], mod[1:2], mod[0:1]).astype(bf16)
    head_dim = lax.broadcasted_iota(jnp.int32, (tm // QKV_ROW_SPLIT, COL_TILE), 1) % HEAD_DIM
    first_half = head_dim < ROT_DIM // 2

    wide = {}
    for j in range(_N_J):
        gain_row, rotary, dil = _qkv_tile_kind(j)
        o_ref, c0 = _qkv_tile_out(j, o_refs)
        slot = j % y_scr.shape[0]
        for hb in range(QKV_ROW_SPLIT):
            rows = slice(hb * tm // QKV_ROW_SPLIT, (hb + 1) * tm // QKV_ROW_SPLIT)
            if j % QKV_DOT_TILES == 0:
                wcols = slice(j * COL_TILE, min(j + QKV_DOT_TILES, _N_J) * COL_TILE)
                wide[hb] = jnp.dot(h_scr[rows, :], w_ref[:, wcols], preferred_element_type=f32)
            sub = j % QKV_DOT_TILES
            y = wide[hb][:, sub * COL_TILE:(sub + 1) * COL_TILE]
            if gain_row is not None:
                ss = jnp.dot((y * y).astype(bf16), ones_ref[...], preferred_element_type=f32)
                y = y * lax.rsqrt(ss * (1.0 / HEAD_DIM) + EPS) * gvec_ref[gain_row:gain_row + 1, :]
            if rotary:
                cos, ssin = [jnp.concatenate([t_ref[rows, :]] * nc, axis=1) for t_ref in (cos_ref, ssin_ref)]
                half = ROT_DIM // 2
                up = pltpu.roll(y, COL_TILE - half, 1)
                dn = pltpu.roll(y, half, 1)
                y = y * cos + jnp.where(first_half, up, dn) * ssin
            if dil == 1:
                o_ref[0, rows, c0:c0 + COL_TILE] = y.astype(bf16)
            else:
                for c in range(nc):
                    y_scr[slot, c, rows, :] = y[:, c * LANES:(c + 1) * LANES]
        if dil == 1:
            continue
        seg = DIL_TILE // dil
        for c in range(nc):
            ocols = slice(c0 + c * LANES, c0 + (c + 1) * LANES)
            if dil <= SAFE_STRIDE:
                for r in range(dil):
                    rows = y_scr[slot, c, pl.ds(r, seg, stride=dil), :]
                    o_ref[0, r * seg:(r + 1) * seg, ocols] = rows.astype(bf16)
                continue
            hi = dil // SAFE_STRIDE
            sub = DIL_TILE // SAFE_STRIDE
            for r_lo in range(SAFE_STRIDE):
                z_scr[c, r_lo * sub:(r_lo + 1) * sub, :] = y_scr[slot, c, pl.ds(r_lo, sub, stride=SAFE_STRIDE), :]
            for r_lo in range(SAFE_STRIDE):
                for r_hi in range(hi):
                    r = SAFE_STRIDE * r_hi + r_lo
                    rows = z_scr[c, pl.ds(r_lo * sub + r_hi, seg, stride=hi), :]
                    o_ref[0, r * seg:(r + 1) * seg, ocols] = rows.astype(bf16)


def _qkv(x, mod6, g1, w_qkv, gvec, ones_bd, rot_tables, cast_ws, tm=DIL_TILE):
    bsz, s, d = x.shape
    assert tm == DIL_TILE
    nt = s // tm
    steps = bsz * nt
    assert all(w.shape[0] % (BF16_ROWS * steps) == 0 for w in cast_ws)
    slab_specs = [pl.BlockSpec((w.shape[0] // steps, w.shape[1]), lambda b, i: (b * nt + i, 0)) for w in cast_ws]
    outs = pl.pallas_call(
        functools.partial(_qkv_kernel, len(cast_ws)),
        grid=(bsz, nt),
        in_specs=[pl.BlockSpec((1, tm, d), lambda b, i: (b, i, 0)),
                  pl.BlockSpec((1, 6, d), lambda b, i: (b, 0, 0)),
                  _const_spec(g1.shape),
                  pl.BlockSpec((d, W_QKV), lambda b, i: (0, 0), pipeline_mode=pl.Buffered(1)),
                  _const_spec(gvec.shape),
                  _const_spec(ones_bd.shape),
                  ] + [pl.BlockSpec((tm, LANES), lambda b, i: (i, 0))] * len(rot_tables) + slab_specs,
        out_specs=[pl.BlockSpec((1, tm, w), lambda b, i: (b, i, 0)) for w in _QKV_OUT_WIDTHS] + slab_specs,
        out_shape=[jax.ShapeDtypeStruct((bsz, s, w), bf16) for w in _QKV_OUT_WIDTHS]
        + [jax.ShapeDtypeStruct(w.shape, bf16) for w in cast_ws],
        scratch_shapes=[pltpu.VMEM((tm, d), bf16),
                        pltpu.VMEM((2, COL_TILE // LANES, tm, LANES), f32),
                        pltpu.VMEM((COL_TILE // LANES, tm, LANES), f32)],
        compiler_params=_params(("parallel", "parallel")),
        name="qkv",
    )(x, mod6, g1, w_qkv, gvec, ones_bd, *rot_tables, *cast_ws)
    n_out = len(_QKV_OUT_WIDTHS)
    return outs[:n_out], outs[n_out:]


def _pair_mask(nq):
    row = lax.broadcasted_iota(jnp.int32, (2 * nq, LANES), 0)
    lane = lax.broadcasted_iota(jnp.int32, (2 * nq, LANES), 1)
    return (row < nq) == (lane < HEAD_DIM)


def _stack_pair(qp, own_head):
    q2 = jnp.concatenate([qp, qp], axis=0)
    return jnp.where(own_head, q2, jnp.zeros_like(q2))


def _na_window_start(rb, n_rows):
    return jnp.clip(rb * NA_ROWS - NA_KH // 2, 0, n_rows - (NA_ROWS + NA_KH))


def _na_kernel(q_ref, kwin, vwin, tbl_ref, o_ref):
    rb = pl.program_id(1)
    n_rows = pl.num_programs(1) * NA_ROWS
    win_start = _na_window_start(rb, n_rows)
    first_head = lax.broadcasted_iota(jnp.int32, (GRID_W, LANES), 1) < HEAD_DIM
    own_head = _pair_mask(GRID_W)
    nkeys = NA_KH * GRID_W
    npair = NA_HEADS // 2
    pair_rows = 2 * GRID_W

    def row_body(a, carry):
        r = rb * NA_ROWS + a
        row_start = jnp.clip(r - NA_KH // 2, 0, n_rows - NA_KH)
        delta = r - row_start
        off = pl.multiple_of((row_start - win_start) * GRID_W, GRID_W)
        qoff = pl.multiple_of(a * GRID_W, GRID_W)
        s_parts = []
        for hp in range(npair):
            cols = slice(hp * LANES, (hp + 1) * LANES)
            q2 = _stack_pair(q_ref[0, pl.ds(qoff, GRID_W), cols], own_head)
            kk = kwin[0, pl.ds(off, nkeys), cols]
            s_parts.append(lax.dot_general(q2, kk, (((1,), (1,)), ((), ())), preferred_element_type=f32))
        n_off = 2 * NA_KH - 2
        bias = jnp.concatenate(
            [jnp.concatenate([tbl_ref[h * n_off + 2 * kp - delta + NA_KH - 1] for kp in range(NA_KH // 2)], axis=1)
             for h in range(NA_HEADS)], axis=0)
        s = jnp.concatenate(s_parts, axis=0) + bias
        m = jnp.max(s, axis=-1, keepdims=True)
        p = jnp.exp2(s - m)
        inv = 1.0 / jnp.sum(p, axis=-1, keepdims=True)
        pb = p.astype(bf16)
        for hp in range(npair):
            cols = slice(hp * LANES, (hp + 1) * LANES)
            rows = slice(hp * pair_rows, (hp + 1) * pair_rows)
            vv = vwin[0, pl.ds(off, nkeys), cols]
            pv = jnp.dot(pb[rows], vv, preferred_element_type=f32) * inv[rows]
            o_ref[0, pl.ds(qoff, GRID_W), cols] = jnp.where(first_head, pv[:GRID_W], pv[GRID_W:]).astype(bf16)
        return carry

    lax.fori_loop(0, NA_ROWS, row_body, 0, unroll=NA_ROWS)


def _na(q, k, v, tbl):
    bsz, s, _ = q.shape
    blk = NA_ROWS * GRID_W
    nb = s // blk
    qspec = pl.BlockSpec((1, blk, WA), lambda b, i: (b, i, 0))
    win_tokens = (NA_ROWS + NA_KH) * GRID_W
    n_rows = s // GRID_W
    window = pl.BlockSpec(
        (pl.Element(1), pl.Element(win_tokens), pl.Element(WA)),
        lambda b, i: (b, pl.multiple_of(_na_window_start(i, n_rows) * GRID_W, GRID_W), 0))

    return pl.pallas_call(
        _na_kernel,
        grid=(bsz, nb),
        in_specs=[qspec, window, window, _const_spec(tbl.shape)],
        out_specs=pl.BlockSpec((1, blk, WA), lambda b, i: (b, i, 0)),
        out_shape=jax.ShapeDtypeStruct((bsz, s, WA), bf16),
        compiler_params=_params(("parallel", "parallel")),
        name="na",
    )(q, k, v, tbl)


def _na_bias_table(rpb):
    col = jnp.arange(GRID_W)
    col_start = jnp.clip(col - NA_KW // 2, 0, GRID_W - NA_KW)
    cmask = (col[None, :] >= col_start[:, None]) & (col[None, :] < col_start[:, None] + NA_KW)
    col_off = jnp.clip(col[None, :] - col[:, None] + (NA_KW - 1), 0, 2 * NA_KW - 2)
    onehot = (col_off[None] == jnp.arange(2 * NA_KW - 1)[:, None, None]).astype(f32)
    t = jnp.einsum('hrc,cqk->hrqk', rpb.astype(f32), onehot, precision=lax.Precision.HIGHEST)
    t = jnp.where(cmask, t * LOG2E, NEG_INF)
    t = jnp.concatenate([t[:, :-1], t[:, 1:]], axis=-1)
    return t.reshape(NA_HEADS * (2 * NA_KH - 2), GRID_W, 2 * GRID_W)


def _dil_halo_pieces(dil):
    return 1 if DIL_TILE // dil == DIL_QBLOCK else dil


def _dil_kernel(*refs):
    ng = len(DIL_CONFIGS)
    n_in = sum(1 + 2 * (1 + 2 * _dil_halo_pieces(dil)) for _, dil in DIL_CONFIGS)
    in_refs = list(refs[:n_in])
    o_ref = refs[n_in]
    o_scr, m_scr, l_scr = refs[n_in + 1:]
    ti = pl.program_id(1)
    n_tiles = pl.num_programs(1)
    qb = DIL_QBLOCK
    span = 3 * qb
    sub = DIL_TILE // SAFE_STRIDE

    first_head = lax.broadcasted_iota(jnp.int32, (qb, LANES), 1) < HEAD_DIM
    own_head = _pair_mask(qb)
    npair = DIL_HPG // 2
    qi = lax.broadcasted_iota(jnp.int32, (qb, span), 0)
    kj = lax.broadcasted_iota(jnp.int32, (qb, span), 1)
    band = jnp.where((kj >= qi) & (kj <= qi + 2 * qb), 0.0, NEG_INF).astype(f32)
    kcol = lax.broadcasted_iota(jnp.int32, (1, span), 1)
    pen_lo = jnp.where(kcol < qb, jnp.where(ti == 0, NEG_INF, 0.0), 0.0).astype(f32)
    pen_hi = jnp.where(kcol >= 2 * qb, jnp.where(ti == n_tiles - 1, NEG_INF, 0.0), 0.0).astype(f32)

    for g, (_, dil) in enumerate(DIL_CONFIGS):
        seg = DIL_TILE // dil
        nj = seg // qb
        nh = _dil_halo_pieces(dil)
        q_ref = in_refs.pop(0)
        kc, kp, kn = in_refs.pop(0), [in_refs.pop(0) for _ in range(nh)], [in_refs.pop(0) for _ in range(nh)]
        vc, vp, vn = in_refs.pop(0), [in_refs.pop(0) for _ in range(nh)], [in_refs.pop(0) for _ in range(nh)]

        def halo(pieces, r, cols, nh=nh):
            return pieces[r][0, :, cols] if nh > 1 else pieces[0][0, r * qb:(r + 1) * qb, cols]

        def window(refs3, r, j, cols, seg=seg, nj=nj, halo=halo):
            p_refs, c_ref, n_refs = refs3
            base = r * seg
            if 0 < j < nj - 1:
                return c_ref[0, base + (j - 1) * qb:base + (j + 2) * qb, cols]
            lo = halo(p_refs, r, cols) if j == 0 else c_ref[0, base + (j - 1) * qb:base + j * qb, cols]
            mid = c_ref[0, base + j * qb:base + (j + 1) * qb, cols]
            hi = halo(n_refs, r, cols) if j == nj - 1 else c_ref[0, base + (j + 1) * qb:base + (j + 2) * qb, cols]
            return jnp.concatenate([lo, mid, hi], axis=0)

        for t0 in range(0, dil * nj, DIL_UNROLL):
            blocks = [divmod(t, nj) for t in range(t0, t0 + DIL_UNROLL)]
            s_parts = []
            for k, (r, j) in enumerate(blocks):
                mask = band
                if j == 0:
                    mask = mask + pen_lo
                if j == nj - 1:
                    mask = mask + pen_hi
                for hp in range(npair):
                    cols = slice(hp * LANES, (hp + 1) * LANES)
                    q2 = _stack_pair(q_ref[0, (t0 + k) * qb:(t0 + k + 1) * qb, cols], own_head)
                    kk = window((kp, kc, kn), r, j, cols)
                    s2 = lax.dot_general(q2, kk, (((1,), (1,)), ((), ())), preferred_element_type=f32)
                    s_parts += [s2[:qb] + mask, s2[qb:] + mask]
            s = jnp.concatenate(s_parts, axis=0)
            m = jnp.max(s, axis=-1, keepdims=True)
            p = jnp.exp2(s - m)
            l = jnp.sum(p, axis=-1, keepdims=True)
            pb = p.astype(bf16)
            for k, (r, j) in enumerate(blocks):
                if dil == 1:
                    store_rows = pl.ds(j * qb, qb)
                else:
                    hi = dil // SAFE_STRIDE
                    start = (r % SAFE_STRIDE) * sub + j * qb * hi + r // SAFE_STRIDE
                    store_rows = pl.ds(start, qb, stride=hi) if hi > 1 else pl.ds(start, qb)
                for hp in range(npair):
                    cols = slice(hp * LANES, (hp + 1) * LANES)
                    r0 = (k * npair + hp) * 2 * qb
                    vv = window((vp, vc, vn), r, j, cols)
                    pv = jnp.dot(pb[r0:r0 + 2 * qb], vv, preferred_element_type=f32)
                    slot = g * npair + hp
                    o_scr[slot, store_rows, :] = jnp.where(first_head, pv[:qb], pv[qb:])
                    m_scr[slot, store_rows, :] = jnp.where(first_head, m[r0:r0 + qb], m[r0 + qb:r0 + 2 * qb])
                    l_scr[slot, store_rows, :] = jnp.where(first_head, l[r0:r0 + qb], l[r0 + qb:r0 + 2 * qb])

    for hp in range(npair):
        for c in range(SAFE_STRIDE):
            def stream(scr, g):
                if DIL_CONFIGS[g][1] == 1:
                    return scr[g * npair + hp, pl.ds(c, sub, stride=SAFE_STRIDE), :]
                return scr[g * npair + hp, c * sub:(c + 1) * sub, :]

            ms = [stream(m_scr, g) for g in range(ng)]
            m_all = functools.reduce(jnp.maximum, ms)
            num = jnp.zeros_like(m_all)
            den = jnp.zeros_like(m_all)
            for g in range(ng):
                w = jnp.exp2(ms[g] - m_all)
                num = num + w * stream(o_scr, g)
                den = den + w * stream(l_scr, g)
            o_scr[ng * npair + hp, pl.ds(c, sub, stride=SAFE_STRIDE), :] = num / den
        o_ref[0, :, hp * LANES:(hp + 1) * LANES] = o_scr[ng * npair + hp].astype(bf16)


def _dil(qs, ks, vs):
    bsz, s, _ = qs[0].shape
    nt = s // DIL_TILE
    ng = len(DIL_CONFIGS)
    qb = DIL_QBLOCK
    per_tile = DIL_TILE // qb
    blk = (1, DIL_TILE, WB_OUT)
    cur = pl.BlockSpec(blk, lambda b, i: (b, i, 0))
    in_specs, args = [], []
    for g, (_, dil) in enumerate(DIL_CONFIGS):
        seg = DIL_TILE // dil
        if _dil_halo_pieces(dil) == 1 and seg == qb:
            prev = [pl.BlockSpec(blk, lambda b, i: (b, jnp.maximum(i - 1, 0), 0))]
            nxt = [pl.BlockSpec(blk, lambda b, i: (b, jnp.minimum(i + 1, nt - 1), 0))]
        else:
            prev = [pl.BlockSpec((1, qb, WB_OUT), lambda b, i, o=(r + 1) * seg // qb - 1:
                                 (b, jnp.maximum((i - 1) * per_tile + o, 0), 0)) for r in range(dil)]
            nxt = [pl.BlockSpec((1, qb, WB_OUT), lambda b, i, o=r * seg // qb:
                                (b, jnp.minimum((i + 1) * per_tile + o, s // qb - 1), 0)) for r in range(dil)]
        in_specs += [cur] + 2 * ([cur] + prev + nxt)
        args += [qs[g]] + [ks[g]] * (1 + len(prev) + len(nxt)) + [vs[g]] * (1 + len(prev) + len(nxt))
    return pl.pallas_call(
        _dil_kernel,
        grid=(bsz, nt),
        in_specs=in_specs,
        out_specs=pl.BlockSpec(blk, lambda b, i: (b, i, 0)),
        out_shape=jax.ShapeDtypeStruct((bsz, s, WB_OUT), bf16),
        scratch_shapes=[pltpu.VMEM(((ng + 1) * DIL_HPG // 2, DIL_TILE, LANES), f32)]
        + [pltpu.VMEM((ng * DIL_HPG // 2, DIL_TILE, LANES), f32)] * 2,
        compiler_params=_params(("parallel", "parallel")),
        name="dil",
    )(*args)


def _tail_kernel(x_ref, oa_ref, ob_ref, mod_ref, g1_ref, g2_ref, bg_ref, wpa_ref, wpb_ref,
                 wo_ref, win_ref, wout_ref, *rest):
    d = x_ref.shape[2]
    n_gate = 2 * d // COL_TILE
    wg_refs = rest[:n_gate]
    o_ref, h_scr, h2_scr, m_scr, act_scr = rest[n_gate:]
    mod = mod_ref[0]
    sh1, sc1, gt1, sh2, sc2, gt2 = [mod[k:k + 1] for k in range(6)]
    tm = x_ref.shape[1]
    for rc in range(TAIL_ROW_SPLIT):
        rows = slice(rc * tm // TAIL_ROW_SPLIT, (rc + 1) * tm // TAIL_ROW_SPLIT)
        x = x_ref[0, rows, :]
        h_scr[rows, :] = _modulated_norm(x, g1_ref[...], sc1, sh1).astype(bf16)
        oa = oa_ref[0, rows, :]
        ob = ob_ref[0, rows, :]
        for n in range(d // COL_TILE):
            ca = slice(n * COL_TILE, (n + 1) * COL_TILE)
            cb = slice(d + n * COL_TILE, d + (n + 1) * COL_TILE)
            wga, wgb = wg_refs[n], wg_refs[n_gate // 2 + n]
            ga = jax.nn.sigmoid(jnp.dot(h_scr[rows, :], wga[...], preferred_element_type=f32) + bg_ref[:, ca])
            gb = jax.nn.sigmoid(jnp.dot(h_scr[rows, :], wgb[...], preferred_element_type=f32) + bg_ref[:, cb])
            pa = jnp.dot(oa, wpa_ref[:, ca], preferred_element_type=f32)
            pb = jnp.dot(ob, wpb_ref[:, ca], preferred_element_type=f32)
            m_scr[rows, ca] = (ga * pa + gb * pb).astype(bf16)
        x1 = x + gt1 * jnp.dot(m_scr[rows, :], wo_ref[...], preferred_element_type=f32)
        h2_scr[rows, :] = _modulated_norm(x1, g2_ref[...], sc2, sh2).astype(bf16)
        for f in range(D_FF // FF_CHUNK):
            ca = slice(f * FF_CHUNK, (f + 1) * FF_CHUNK)
            cu = slice(D_FF + f * FF_CHUNK, D_FF + (f + 1) * FF_CHUNK)
            a = jnp.dot(h2_scr[rows, :], win_ref[:, ca], preferred_element_type=f32)
            u = jnp.dot(h2_scr[rows, :], win_ref[:, cu], preferred_element_type=f32)
            act_scr[rows, ca] = ((a * jax.nn.sigmoid(a)) * u).astype(bf16)
        o_ref[0, rows, :] = x1 + gt2 * jnp.dot(act_scr[rows, :], wout_ref[...], preferred_element_type=f32)


def _tail(x, o_a, o_b, mod6, g1, g2, w_in, bg, wpa, wpb, wo, w_ffn_in, w_ffn_out, tm=TAIL_TM):
    bsz, s, d = x.shape
    tok = lambda w: pl.BlockSpec((1, tm, w), lambda b, i: (b, i, 0))
    consts = [g1, g2, bg, wpa, wpb, wo, w_ffn_in, w_ffn_out]
    n_gate = 2 * d // COL_TILE
    gate_specs = [pl.BlockSpec((d, COL_TILE), lambda b, i, c=_N_J + n: (0, c), pipeline_mode=pl.Buffered(1))
                  for n in range(n_gate)]
    return pl.pallas_call(
        _tail_kernel,
        grid=(bsz, s // tm),
        in_specs=[tok(d), tok(WA), tok(WB_OUT), pl.BlockSpec((1, 6, d), lambda b, i: (b, 0, 0))]
        + [_const_spec(c.shape) for c in consts] + gate_specs,
        out_specs=tok(d),
        out_shape=jax.ShapeDtypeStruct((bsz, s, d), f32),
        scratch_shapes=[pltpu.VMEM((tm, d), bf16)] * 3 + [pltpu.VMEM((tm, D_FF), bf16)],
        compiler_params=_params(("parallel", "parallel")),
        name="tail",
    )(x, o_a, o_b, mod6, *consts, *([w_in] * n_gate))


def _rotary_tables(s):
    half = ROT_DIM // 2
    inv_freq = ROPE_THETA ** (-(jnp.arange(half, dtype=f32) * 2.0) / ROT_DIM)
    ang = jnp.arange(s).astype(f32)[:, None] * inv_freq[None, :]
    cos, sin = jnp.cos(ang), jnp.sin(ang)
    rest = HEAD_DIM - ROT_DIM
    cos_h = jnp.concatenate([cos, cos, jnp.ones((s, rest), f32)], axis=1)
    ssin_h = jnp.concatenate([-sin, sin, jnp.zeros((s, rest), f32)], axis=1)
    rep = LANES // HEAD_DIM
    return jnp.tile(cos_h, (1, rep)), jnp.tile(ssin_h, (1, rep))


def _layer(x, mod6, g_norm1, g_norm2, w_in, b_gate, g_qa, g_ka, g_qb, g_kb, rpb,
           w_proj_a, w_proj_b, w_o, w_ffn_in, w_ffn_out):
    bsz, s, d = x.shape
    assert d == D_MODEL and w_in.shape == (d, W_QKV + 2 * d) and w_ffn_in.shape == (d, 2 * D_FF)
    assert s % DIL_TILE == 0 and s % (NA_ROWS * GRID_W) == 0 and s % TAIL_TM == 0
    assert s // GRID_W >= NA_ROWS + NA_KH
    scale = HEAD_DIM ** -0.5 * LOG2E
    rep = COL_TILE // HEAD_DIM
    gvec = jnp.stack([jnp.tile(g_qa * scale, rep), jnp.tile(g_ka, rep),
                      jnp.tile(g_qb * scale, rep), jnp.tile(g_kb, rep)]).astype(f32)
    hid = jnp.arange(COL_TILE) // HEAD_DIM
    ones_bd = (hid[:, None] == hid[None, :]).astype(bf16)
    rot_tables = _rotary_tables(s)
    g1 = g_norm1.reshape(1, d)
    g2 = g_norm2.reshape(1, d)

    qkv, tail_ws = _qkv(x, mod6, g1, w_in, gvec, ones_bd, rot_tables,
                        [w_proj_a, w_proj_b, w_o, w_ffn_in, w_ffn_out])
    ng = len(DIL_CONFIGS)
    o_a = _na(qkv[_OUT_QA], qkv[_OUT_KA], qkv[_OUT_VA], _na_bias_table(rpb))
    o_b = _dil(qkv[_OUT_QB:_OUT_QB + ng], qkv[_OUT_KB:_OUT_KB + ng], qkv[_OUT_VB:_OUT_VB + ng])

    return _tail(x, o_a, o_b, mod6, g1, g2, w_in, b_gate.reshape(1, 2 * d), *tail_ws)


def kernel(x, c, w_ada, b_ada, g_norm1, g_norm2, w_in, b_gate, g_qa, g_ka, g_qb, g_kb, rpb,
           w_proj_a, w_proj_b, w_o, w_ffn_in, w_ffn_out):
    depth = w_ada.shape[0]
    bsz, d = c.shape
    for l in range(depth):
        mod, w_in_bf = _mod(c, w_ada[l], b_ada[l], w_in[l])
        x = _layer(x, mod.reshape(bsz, 6, d), g_norm1[l], g_norm2[l], w_in_bf, b_gate[l], g_qa[l], g_ka[l], g_qb[l],
                   g_kb[l], rpb[l], w_proj_a[l], w_proj_b[l], w_o[l], w_ffn_in[l], w_ffn_out[l])
    return x
```

```python
import functools

import jax
import jax.numpy as jnp
import numpy as np
from jax import lax
from jax.experimental import pallas as pl
from jax.experimental.pallas import tpu as pltpu

f32 = jnp.float32
bf16 = jnp.bfloat16

D_MODEL = 1024
HEAD_DIM = 64
GRID_W = 64
NA_HEADS = 8
NA_KH = 8
NA_KW = 16
DIL_CONFIGS = ((128, 1), (512, 4), (2048, 16))
DIL_HPG = 4
DIL_HEADS = DIL_HPG * len(DIL_CONFIGS)
DIL_QBLOCK = 64
ROT_DIM = HEAD_DIM // 4
ROPE_THETA = 500000.0
D_FF = -(-8 * D_MODEL // (3 * 256)) * 256
EPS = 1e-6
NEG_INF = -1e30
LOG2E = 1.4426950408889634
WA = NA_HEADS * HEAD_DIM
WB = DIL_HEADS * HEAD_DIM
WB_OUT = DIL_HPG * HEAD_DIM
W_QKV = 3 * WA + 3 * WB

LANES = 128
BF16_ROWS = 16
COL_TILE = 256
DIL_TILE = 1024
QKV_ROW_SPLIT = 2
QKV_DOT_TILES = 2
DIL_UNROLL = 16
NA_ROWS = 16
MOD_STEPS = 8
TAIL_TM = 1024
TAIL_ROW_SPLIT = 2
FF_CHUNK = 256
VMEM_LIMIT = 56 * 1024 * 1024

SAFE_STRIDE = 4

assert all((win // 2) // dil == DIL_QBLOCK for win, dil in DIL_CONFIGS)
assert all(dil == 1 or (dil % SAFE_STRIDE == 0 and dil // SAFE_STRIDE <= SAFE_STRIDE)
           for _, dil in DIL_CONFIGS)


def _params(sem):
    return pltpu.CompilerParams(dimension_semantics=sem, vmem_limit_bytes=VMEM_LIMIT)


def _const_spec(shape):
    nd = len(shape)
    return pl.BlockSpec(shape, lambda *_: (0,) * nd, pipeline_mode=pl.Buffered(1))


def _mod_kernel(ct_ref, w_ref, b_ref, win_ref, o_ref, win_o_ref):
    ct = ct_ref[...]
    act = ct * jax.nn.sigmoid(ct)
    w = w_ref[...]
    rows = [jnp.sum(act[:, b:b + 1] * w, axis=0, keepdims=True) for b in range(ct.shape[1])]
    o_ref[...] = jnp.concatenate(rows, axis=0) + b_ref[...]
    win_o_ref[...] = win_ref[...].astype(bf16)


def _mod(c, w_ada, b_ada, w_in):
    bsz, d = c.shape
    n = w_ada.shape[1]
    steps = MOD_STEPS
    tn = n // steps
    slab = pl.BlockSpec((w_in.shape[0] // steps, w_in.shape[1]), lambda j: (j, 0))
    return pl.pallas_call(
        _mod_kernel,
        grid=(steps,),
        in_specs=[pl.BlockSpec((d, bsz), lambda j: (0, 0)),
                  pl.BlockSpec((d, tn), lambda j: (0, j)),
                  pl.BlockSpec((1, tn), lambda j: (0, j)),
                  slab],
        out_specs=[pl.BlockSpec((bsz, tn), lambda j: (0, j)), slab],
        out_shape=[jax.ShapeDtypeStruct((bsz, n), f32), jax.ShapeDtypeStruct(w_in.shape, bf16)],
        compiler_params=_params(("arbitrary",)),
        name="mod",
    )(c.T, w_ada, b_ada.reshape(1, n), w_in)


def _modulated_norm(x, g, sc, sh):
    ms = jnp.mean(x * x, axis=-1, keepdims=True)
    y = x * lax.rsqrt(ms + EPS) * g
    return y * (1.0 + sc) + sh


_J_QB = 3 * WA // COL_TILE
_J_KB = _J_QB + WB // COL_TILE
_J_VB = _J_KB + WB // COL_TILE
_N_J = W_QKV // COL_TILE


def _qkv_tile_kind(j):
    if j < WA // COL_TILE:
        return 0, False, 1
    if j < 2 * WA // COL_TILE:
        return 1, False, 1
    if j < _J_QB:
        return None, False, 1
    dil = DIL_CONFIGS[(j - _J_QB) % len(DIL_CONFIGS)][1]
    if j < _J_KB:
        return 2, True, dil
    if j < _J_VB:
        return 3, True, dil
    return None, False, dil


_QKV_OUT_WIDTHS = (WA,) * 3 + (WB_OUT,) * (3 * len(DIL_CONFIGS))
_OUT_QA, _OUT_KA, _OUT_VA, _OUT_QB, _OUT_KB, _OUT_VB = 0, 1, 2, 3, 3 + len(DIL_CONFIGS), 3 + 2 * len(DIL_CONFIGS)


def _qkv_tile_out(j, o_refs):
    if j < _J_QB:
        per = WA // COL_TILE
        return o_refs[j // per], (j % per) * COL_TILE
    return o_refs[_OUT_QB + j - _J_QB], 0


def _qkv_kernel(n_cast, x_ref, mod_ref, g1_ref, w_ref, gvec_ref, ones_ref, cos_ref, ssin_ref, *rest):
    n_out = len(_QKV_OUT_WIDTHS)
    cast_in, o_refs = rest[:n_cast], rest[n_cast:n_cast + n_out]
    cast_out = rest[n_cast + n_out:2 * n_cast + n_out]
    h_scr, y_scr, z_scr = rest[2 * n_cast + n_out:]
    for src, dst in zip(cast_in, cast_out):
        dst[...] = src[...].astype(bf16)
    tm = x_ref.shape[1]
    nc = COL_TILE // LANES
    mod = mod_ref[0]
    h_scr[...] = _modulated_norm(x_ref[0], g1_ref[...], mod[1:2], mod[0:1]).astype(bf16)
    head_dim = lax.broadcasted_iota(jnp.int32, (tm // QKV_ROW_SPLIT, COL_TILE), 1) % HEAD_DIM
    first_half = head_dim < ROT_DIM // 2

    wide = {}
    for j in range(_N_J):
        gain_row, rotary, dil = _qkv_tile_kind(j)
        o_ref, c0 = _qkv_tile_out(j, o_refs)
        slot = j % y_scr.shape[0]
        for hb in range(QKV_ROW_SPLIT):
            rows = slice(hb * tm // QKV_ROW_SPLIT, (hb + 1) * tm // QKV_ROW_SPLIT)
            if j % QKV_DOT_TILES == 0:
                wcols = slice(j * COL_TILE, min(j + QKV_DOT_TILES, _N_J) * COL_TILE)
                wide[hb] = jnp.dot(h_scr[rows, :], w_ref[:, wcols], preferred_element_type=f32)
            sub = j % QKV_DOT_TILES
            y = wide[hb][:, sub * COL_TILE:(sub + 1) * COL_TILE]
            if gain_row is not None:
                ss = jnp.dot((y * y).astype(bf16), ones_ref[...], preferred_element_type=f32)
                y = y * lax.rsqrt(ss * (1.0 / HEAD_DIM) + EPS) * gvec_ref[gain_row:gain_row + 1, :]
            if rotary:
                cos, ssin = [jnp.concatenate([t_ref[rows, :]] * nc, axis=1) for t_ref in (cos_ref, ssin_ref)]
                half = ROT_DIM // 2
                up = pltpu.roll(y, COL_TILE - half, 1)
                dn = pltpu.roll(y, half, 1)
                y = y * cos + jnp.where(first_half, up, dn) * ssin
            if dil == 1:
                o_ref[0, rows, c0:c0 + COL_TILE] = y.astype(bf16)
            else:
                for c in range(nc):
                    y_scr[slot, c, rows, :] = y[:, c * LANES:(c + 1) * LANES]
        if dil == 1:
            continue
        seg = DIL_TILE // dil
        for c in range(nc):
            ocols = slice(c0 + c * LANES, c0 + (c + 1) * LANES)
            if dil <= SAFE_STRIDE:
                for r in range(dil):
                    rows = y_scr[slot, c, pl.ds(r, seg, stride=dil), :]
                    o_ref[0, r * seg:(r + 1) * seg, ocols] = rows.astype(bf16)
                continue
            hi = dil // SAFE_STRIDE
            sub = DIL_TILE // SAFE_STRIDE
            for r_lo in range(SAFE_STRIDE):
                z_scr[c, r_lo * sub:(r_lo + 1) * sub, :] = y_scr[slot, c, pl.ds(r_lo, sub, stride=SAFE_STRIDE), :]
            for r_lo in range(SAFE_STRIDE):
                for r_hi in range(hi):
                    r = SAFE_STRIDE * r_hi + r_lo
                    rows = z_scr[c, pl.ds(r_lo * sub + r_hi, seg, stride=hi), :]
                    o_ref[0, r * seg:(r + 1) * seg, ocols] = rows.astype(bf16)


def _qkv(x, mod6, g1, w_qkv, gvec, ones_bd, rot_tables, cast_ws, tm=DIL_TILE):
    bsz, s, d = x.shape
    assert tm == DIL_TILE
    nt = s // tm
    steps = bsz * nt
    assert all(w.shape[0] % (BF16_ROWS * steps) == 0 for w in cast_ws)
    slab_specs = [pl.BlockSpec((w.shape[0] // steps, w.shape[1]), lambda b, i: (b * nt + i, 0)) for w in cast_ws]
    outs = pl.pallas_call(
        functools.partial(_qkv_kernel, len(cast_ws)),
        grid=(bsz, nt),
        in_specs=[pl.BlockSpec((1, tm, d), lambda b, i: (b, i, 0)),
                  pl.BlockSpec((1, 6, d), lambda b, i: (b, 0, 0)),
                  _const_spec(g1.shape),
                  pl.BlockSpec((d, W_QKV), lambda b, i: (0, 0), pipeline_mode=pl.Buffered(1)),
                  _const_spec(gvec.shape),
                  _const_spec(ones_bd.shape),
                  ] + [pl.BlockSpec((tm, LANES), lambda b, i: (i, 0))] * len(rot_tables) + slab_specs,
        out_specs=[pl.BlockSpec((1, tm, w), lambda b, i: (b, i, 0)) for w in _QKV_OUT_WIDTHS] + slab_specs,
        out_shape=[jax.ShapeDtypeStruct((bsz, s, w), bf16) for w in _QKV_OUT_WIDTHS]
        + [jax.ShapeDtypeStruct(w.shape, bf16) for w in cast_ws],
        scratch_shapes=[pltpu.VMEM((tm, d), bf16),
                        pltpu.VMEM((2, COL_TILE // LANES, tm, LANES), f32),
                        pltpu.VMEM((COL_TILE // LANES, tm, LANES), f32)],
        compiler_params=_params(("parallel", "parallel")),
        name="qkv",
    )(x, mod6, g1, w_qkv, gvec, ones_bd, *rot_tables, *cast_ws)
    n_out = len(_QKV_OUT_WIDTHS)
    return outs[:n_out], outs[n_out:]


def _pair_mask(nq):
    row = lax.broadcasted_iota(jnp.int32, (2 * nq, LANES), 0)
    lane = lax.broadcasted_iota(jnp.int32, (2 * nq, LANES), 1)
    return (row < nq) == (lane < HEAD_DIM)


def _stack_pair(qp, own_head):
    q2 = jnp.concatenate([qp, qp], axis=0)
    return jnp.where(own_head, q2, jnp.zeros_like(q2))


def _na_window_start(rb, n_rows):
    return jnp.clip(rb * NA_ROWS - NA_KH // 2, 0, n_rows - (NA_ROWS + NA_KH))


def _na_kernel(q_ref, kwin, vwin, tbl_ref, o_ref):
    rb = pl.program_id(1)
    n_rows = pl.num_programs(1) * NA_ROWS
    win_start = _na_window_start(rb, n_rows)
    first_head = lax.broadcasted_iota(jnp.int32, (GRID_W, LANES), 1) < HEAD_DIM
    own_head = _pair_mask(GRID_W)
    nkeys = NA_KH * GRID_W
    npair = NA_HEADS // 2
    pair_rows = 2 * GRID_W

    def row_body(a, carry):
        r = rb * NA_ROWS + a
        row_start = jnp.clip(r - NA_KH // 2, 0, n_rows - NA_KH)
        delta = r - row_start
        off = pl.multiple_of((row_start - win_start) * GRID_W, GRID_W)
        qoff = pl.multiple_of(a * GRID_W, GRID_W)
        s_parts = []
        for hp in range(npair):
            cols = slice(hp * LANES, (hp + 1) * LANES)
            q2 = _stack_pair(q_ref[0, pl.ds(qoff, GRID_W), cols], own_head)
            kk = kwin[0, pl.ds(off, nkeys), cols]
            s_parts.append(lax.dot_general(q2, kk, (((1,), (1,)), ((), ())), preferred_element_type=f32))
        n_off = 2 * NA_KH - 2
        bias = jnp.concatenate(
            [jnp.concatenate([tbl_ref[h * n_off + 2 * kp - delta + NA_KH - 1] for kp in range(NA_KH // 2)], axis=1)
             for h in range(NA_HEADS)], axis=0)
        s = jnp.concatenate(s_parts, axis=0) + bias
        m = jnp.max(s, axis=-1, keepdims=True)
        p = jnp.exp2(s - m)
        inv = 1.0 / jnp.sum(p, axis=-1, keepdims=True)
        pb = p.astype(bf16)
        for hp in range(npair):
            cols = slice(hp * LANES, (hp + 1) * LANES)
            rows = slice(hp * pair_rows, (hp + 1) * pair_rows)
            vv = vwin[0, pl.ds(off, nkeys), cols]
            pv = jnp.dot(pb[rows], vv, preferred_element_type=f32) * inv[rows]
            o_ref[0, pl.ds(qoff, GRID_W), cols] = jnp.where(first_head, pv[:GRID_W], pv[GRID_W:]).astype(bf16)
        return carry

    lax.fori_loop(0, NA_ROWS, row_body, 0, unroll=NA_ROWS)


def _na(q, k, v, tbl):
    bsz, s, _ = q.shape
    blk = NA_ROWS * GRID_W
    nb = s // blk
    qspec = pl.BlockSpec((1, blk, WA), lambda b, i: (b, i, 0))
    win_tokens = (NA_ROWS + NA_KH) * GRID_W
    n_rows = s // GRID_W
    window = pl.BlockSpec(
        (pl.Element(1), pl.Element(win_tokens), pl.Element(WA)),
        lambda b, i: (b, pl.multiple_of(_na_window_start(i, n_rows) * GRID_W, GRID_W), 0))

    return pl.pallas_call(
        _na_kernel,
        grid=(bsz, nb),
        in_specs=[qspec, window, window, _const_spec(tbl.shape)],
        out_specs=pl.BlockSpec((1, blk, WA), lambda b, i: (b, i, 0)),
        out_shape=jax.ShapeDtypeStruct((bsz, s, WA), bf16),
        compiler_params=_params(("parallel", "parallel")),
        name="na",
    )(q, k, v, tbl)


def _na_bias_table(rpb):
    col = jnp.arange(GRID_W)
    col_start = jnp.clip(col - NA_KW // 2, 0, GRID_W - NA_KW)
    cmask = (col[None, :] >= col_start[:, None]) & (col[None, :] < col_start[:, None] + NA_KW)
    col_off = jnp.clip(col[None, :] - col[:, None] + (NA_KW - 1), 0, 2 * NA_KW - 2)
    onehot = (col_off[None] == jnp.arange(2 * NA_KW - 1)[:, None, None]).astype(f32)
    t = jnp.einsum('hrc,cqk->hrqk', rpb.astype(f32), onehot, precision=lax.Precision.HIGHEST)
    t = jnp.where(cmask, t * LOG2E, NEG_INF)
    t = jnp.concatenate([t[:, :-1], t[:, 1:]], axis=-1)
    return t.reshape(NA_HEADS * (2 * NA_KH - 2), GRID_W, 2 * GRID_W)


def _dil_halo_pieces(dil):
    return 1 if DIL_TILE // dil == DIL_QBLOCK else dil


def _dil_kernel(*refs):
    ng = len(DIL_CONFIGS)
    n_in = sum(1 + 2 * (1 + 2 * _dil_halo_pieces(dil)) for _, dil in DIL_CONFIGS)
    in_refs = list(refs[:n_in])
    o_ref = refs[n_in]
    o_scr, m_scr, l_scr = refs[n_in + 1:]
    ti = pl.program_id(1)
    n_tiles = pl.num_programs(1)
    qb = DIL_QBLOCK
    span = 3 * qb
    sub = DIL_TILE // SAFE_STRIDE

    first_head = lax.broadcasted_iota(jnp.int32, (qb, LANES), 1) < HEAD_DIM
    own_head = _pair_mask(qb)
    npair = DIL_HPG // 2
    qi = lax.broadcasted_iota(jnp.int32, (qb, span), 0)
    kj = lax.broadcasted_iota(jnp.int32, (qb, span), 1)
    band = jnp.where((kj >= qi) & (kj <= qi + 2 * qb), 0.0, NEG_INF).astype(f32)
    kcol = lax.broadcasted_iota(jnp.int32, (1, span), 1)
    pen_lo = jnp.where(kcol < qb, jnp.where(ti == 0, NEG_INF, 0.0), 0.0).astype(f32)
    pen_hi = jnp.where(kcol >= 2 * qb, jnp.where(ti == n_tiles - 1, NEG_INF, 0.0), 0.0).astype(f32)

    for g, (_, dil) in enumerate(DIL_CONFIGS):
        seg = DIL_TILE // dil
        nj = seg // qb
        nh = _dil_halo_pieces(dil)
        q_ref = in_refs.pop(0)
        kc, kp, kn = in_refs.pop(0), [in_refs.pop(0) for _ in range(nh)], [in_refs.pop(0) for _ in range(nh)]
        vc, vp, vn = in_refs.pop(0), [in_refs.pop(0) for _ in range(nh)], [in_refs.pop(0) for _ in range(nh)]

        def halo(pieces, r, cols, nh=nh):
            return pieces[r][0, :, cols] if nh > 1 else pieces[0][0, r * qb:(r + 1) * qb, cols]

        def window(refs3, r, j, cols, seg=seg, nj=nj, halo=halo):
            p_refs, c_ref, n_refs = refs3
            base = r * seg
            if 0 < j < nj - 1:
                return c_ref[0, base + (j - 1) * qb:base + (j + 2) * qb, cols]
            lo = halo(p_refs, r, cols) if j == 0 else c_ref[0, base + (j - 1) * qb:base + j * qb, cols]
            mid = c_ref[0, base + j * qb:base + (j + 1) * qb, cols]
            hi = halo(n_refs, r, cols) if j == nj - 1 else c_ref[0, base + (j + 1) * qb:base + (j + 2) * qb, cols]
            return jnp.concatenate([lo, mid, hi], axis=0)

        for t0 in range(0, dil * nj, DIL_UNROLL):
            blocks = [divmod(t, nj) for t in range(t0, t0 + DIL_UNROLL)]
            s_parts = []
            for k, (r, j) in enumerate(blocks):
                mask = band
                if j == 0:
                    mask = mask + pen_lo
                if j == nj - 1:
                    mask = mask + pen_hi
                for hp in range(npair):
                    cols = slice(hp * LANES, (hp + 1) * LANES)
                    q2 = _stack_pair(q_ref[0, (t0 + k) * qb:(t0 + k + 1) * qb, cols], own_head)
                    kk = window((kp, kc, kn), r, j, cols)
                    s2 = lax.dot_general(q2, kk, (((1,), (1,)), ((), ())), preferred_element_type=f32)
                    s_parts += [s2[:qb] + mask, s2[qb:] + mask]
            s = jnp.concatenate(s_parts, axis=0)
            m = jnp.max(s, axis=-1, keepdims=True)
            p = jnp.exp2(s - m)
            l = jnp.sum(p, axis=-1, keepdims=True)
            pb = p.astype(bf16)
            for k, (r, j) in enumerate(blocks):
                if dil == 1:
                    store_rows = pl.ds(j * qb, qb)
                else:
                    hi = dil // SAFE_STRIDE
                    start = (r % SAFE_STRIDE) * sub + j * qb * hi + r // SAFE_STRIDE
                    store_rows = pl.ds(start, qb, stride=hi) if hi > 1 else pl.ds(start, qb)
                for hp in range(npair):
                    cols = slice(hp * LANES, (hp + 1) * LANES)
                    r0 = (k * npair + hp) * 2 * qb
                    vv = window((vp, vc, vn), r, j, cols)
                    pv = jnp.dot(pb[r0:r0 + 2 * qb], vv, preferred_element_type=f32)
                    slot = g * npair + hp
                    o_scr[slot, store_rows, :] = jnp.where(first_head, pv[:qb], pv[qb:])
                    m_scr[slot, store_rows, :] = jnp.where(first_head, m[r0:r0 + qb], m[r0 + qb:r0 + 2 * qb])
                    l_scr[slot, store_rows, :] = jnp.where(first_head, l[r0:r0 + qb], l[r0 + qb:r0 + 2 * qb])

    for hp in range(npair):
        for c in range(SAFE_STRIDE):
            def stream(scr, g):
                if DIL_CONFIGS[g][1] == 1:
                    return scr[g * npair + hp, pl.ds(c, sub, stride=SAFE_STRIDE), :]
                return scr[g * npair + hp, c * sub:(c + 1) * sub, :]

            ms = [stream(m_scr, g) for g in range(ng)]
            m_all = functools.reduce(jnp.maximum, ms)
            num = jnp.zeros_like(m_all)
            den = jnp.zeros_like(m_all)
            for g in range(ng):
                w = jnp.exp2(ms[g] - m_all)
                num = num + w * stream(o_scr, g)
                den = den + w * stream(l_scr, g)
            o_scr[ng * npair + hp, pl.ds(c, sub, stride=SAFE_STRIDE), :] = num / den
        o_ref[0, :, hp * LANES:(hp + 1) * LANES] = o_scr[ng * npair + hp].astype(bf16)


def _dil(qs, ks, vs):
    bsz, s, _ = qs[0].shape
    nt = s // DIL_TILE
    ng = len(DIL_CONFIGS)
    qb = DIL_QBLOCK
    per_tile = DIL_TILE // qb
    blk = (1, DIL_TILE, WB_OUT)
    cur = pl.BlockSpec(blk, lambda b, i: (b, i, 0))
    in_specs, args = [], []
    for g, (_, dil) in enumerate(DIL_CONFIGS):
        seg = DIL_TILE // dil
        if _dil_halo_pieces(dil) == 1 and seg == qb:
            prev = [pl.BlockSpec(blk, lambda b, i: (b, jnp.maximum(i - 1, 0), 0))]
            nxt = [pl.BlockSpec(blk, lambda b, i: (b, jnp.minimum(i + 1, nt - 1), 0))]
        else:
            prev = [pl.BlockSpec((1, qb, WB_OUT), lambda b, i, o=(r + 1) * seg // qb - 1:
                                 (b, jnp.maximum((i - 1) * per_tile + o, 0), 0)) for r in range(dil)]
            nxt = [pl.BlockSpec((1, qb, WB_OUT), lambda b, i, o=r * seg // qb:
                                (b, jnp.minimum((i + 1) * per_tile + o, s // qb - 1), 0)) for r in range(dil)]
        in_specs += [cur] + 2 * ([cur] + prev + nxt)
        args += [qs[g]] + [ks[g]] * (1 + len(prev) + len(nxt)) + [vs[g]] * (1 + len(prev) + len(nxt))
    return pl.pallas_call(
        _dil_kernel,
        grid=(bsz, nt),
        in_specs=in_specs,
        out_specs=pl.BlockSpec(blk, lambda b, i: (b, i, 0)),
        out_shape=jax.ShapeDtypeStruct((bsz, s, WB_OUT), bf16),
        scratch_shapes=[pltpu.VMEM(((ng + 1) * DIL_HPG // 2, DIL_TILE, LANES), f32)]
        + [pltpu.VMEM((ng * DIL_HPG // 2, DIL_TILE, LANES), f32)] * 2,
        compiler_params=_params(("parallel", "parallel")),
        name="dil",
    )(*args)


def _tail_kernel(x_ref, oa_ref, ob_ref, mod_ref, g1_ref, g2_ref, bg_ref, wpa_ref, wpb_ref,
                 wo_ref, win_ref, wout_ref, *rest):
    d = x_ref.shape[2]
    n_gate = 2 * d // COL_TILE
    wg_refs = rest[:n_gate]
    o_ref, h_scr, h2_scr, m_scr, act_scr = rest[n_gate:]
    mod = mod_ref[0]
    sh1, sc1, gt1, sh2, sc2, gt2 = [mod[k:k + 1] for k in range(6)]
    tm = x_ref.shape[1]
    for rc in range(TAIL_ROW_SPLIT):
        rows = slice(rc * tm // TAIL_ROW_SPLIT, (rc + 1) * tm // TAIL_ROW_SPLIT)
        x = x_ref[0, rows, :]
        h_scr[rows, :] = _modulated_norm(x, g1_ref[...], sc1, sh1).astype(bf16)
        oa = oa_ref[0, rows, :]
        ob = ob_ref[0, rows, :]
        for n in range(d // COL_TILE):
            ca = slice(n * COL_TILE, (n + 1) * COL_TILE)
            cb = slice(d + n * COL_TILE, d + (n + 1) * COL_TILE)
            wga, wgb = wg_refs[n], wg_refs[n_gate // 2 + n]
            ga = jax.nn.sigmoid(jnp.dot(h_scr[rows, :], wga[...], preferred_element_type=f32) + bg_ref[:, ca])
            gb = jax.nn.sigmoid(jnp.dot(h_scr[rows, :], wgb[...], preferred_element_type=f32) + bg_ref[:, cb])
            pa = jnp.dot(oa, wpa_ref[:, ca], preferred_element_type=f32)
            pb = jnp.dot(ob, wpb_ref[:, ca], preferred_element_type=f32)
            m_scr[rows, ca] = (ga * pa + gb * pb).astype(bf16)
        x1 = x + gt1 * jnp.dot(m_scr[rows, :], wo_ref[...], preferred_element_type=f32)
        h2_scr[rows, :] = _modulated_norm(x1, g2_ref[...], sc2, sh2).astype(bf16)
        for f in range(D_FF // FF_CHUNK):
            ca = slice(f * FF_CHUNK, (f + 1) * FF_CHUNK)
            cu = slice(D_FF + f * FF_CHUNK, D_FF + (f + 1) * FF_CHUNK)
            a = jnp.dot(h2_scr[rows, :], win_ref[:, ca], preferred_element_type=f32)
            u = jnp.dot(h2_scr[rows, :], win_ref[:, cu], preferred_element_type=f32)
            act_scr[rows, ca] = ((a * jax.nn.sigmoid(a)) * u).astype(bf16)
        o_ref[0, rows, :] = x1 + gt2 * jnp.dot(act_scr[rows, :], wout_ref[...], preferred_element_type=f32)


def _tail(x, o_a, o_b, mod6, g1, g2, w_in, bg, wpa, wpb, wo, w_ffn_in, w_ffn_out, tm=TAIL_TM):
    bsz, s, d = x.shape
    tok = lambda w: pl.BlockSpec((1, tm, w), lambda b, i: (b, i, 0))
    consts = [g1, g2, bg, wpa, wpb, wo, w_ffn_in, w_ffn_out]
    n_gate = 2 * d // COL_TILE
    gate_specs = [pl.BlockSpec((d, COL_TILE), lambda b, i, c=_N_J + n: (0, c), pipeline_mode=pl.Buffered(1))
                  for n in range(n_gate)]
    return pl.pallas_call(
        _tail_kernel,
        grid=(bsz, s // tm),
        in_specs=[tok(d), tok(WA), tok(WB_OUT), pl.BlockSpec((1, 6, d), lambda b, i: (b, 0, 0))]
        + [_const_spec(c.shape) for c in consts] + gate_specs,
        out_specs=tok(d),
        out_shape=jax.ShapeDtypeStruct((bsz, s, d), f32),
        scratch_shapes=[pltpu.VMEM((tm, d), bf16)] * 3 + [pltpu.VMEM((tm, D_FF), bf16)],
        compiler_params=_params(("parallel", "parallel")),
        name="tail",
    )(x, o_a, o_b, mod6, *consts, *([w_in] * n_gate))


def _rotary_tables(s):
    half = ROT_DIM // 2
    inv_freq = ROPE_THETA ** (-(np.arange(half, dtype=np.float64) * 2.0) / ROT_DIM)
    ang = np.arange(s, dtype=np.float64)[:, None] * inv_freq[None, :]
    cos, sin = np.cos(ang), np.sin(ang)
    rest = HEAD_DIM - ROT_DIM
    cos_h = np.concatenate([cos, cos, np.ones((s, rest))], axis=1)
    ssin_h = np.concatenate([-sin, sin, np.zeros((s, rest))], axis=1)
    rep = LANES // HEAD_DIM
    return (jnp.asarray(np.tile(cos_h, (1, rep)), dtype=f32), jnp.asarray(np.tile(ssin_h, (1, rep)), dtype=f32))


def _layer(x, mod6, g_norm1, g_norm2, w_in, b_gate, g_qa, g_ka, g_qb, g_kb, rpb,
           w_proj_a, w_proj_b, w_o, w_ffn_in, w_ffn_out):
    bsz, s, d = x.shape
    assert d == D_MODEL and w_in.shape == (d, W_QKV + 2 * d) and w_ffn_in.shape == (d, 2 * D_FF)
    assert s % DIL_TILE == 0 and s % (NA_ROWS * GRID_W) == 0 and s % TAIL_TM == 0
    assert s // GRID_W >= NA_ROWS + NA_KH
    scale = HEAD_DIM ** -0.5 * LOG2E
    rep = COL_TILE // HEAD_DIM
    gvec = jnp.stack([jnp.tile(g_qa * scale, rep), jnp.tile(g_ka, rep),
                      jnp.tile(g_qb * scale, rep), jnp.tile(g_kb, rep)]).astype(f32)
    hid = jnp.arange(COL_TILE) // HEAD_DIM
    ones_bd = (hid[:, None] == hid[None, :]).astype(bf16)
    rot_tables = _rotary_tables(s)
    g1 = g_norm1.reshape(1, d)
    g2 = g_norm2.reshape(1, d)

    qkv, tail_ws = _qkv(x, mod6, g1, w_in, gvec, ones_bd, rot_tables,
                        [w_proj_a, w_proj_b, w_o, w_ffn_in, w_ffn_out])
    ng = len(DIL_CONFIGS)
    o_a = _na(qkv[_OUT_QA], qkv[_OUT_KA], qkv[_OUT_VA], _na_bias_table(rpb))
    o_b = _dil(qkv[_OUT_QB:_OUT_QB + ng], qkv[_OUT_KB:_OUT_KB + ng], qkv[_OUT_VB:_OUT_VB + ng])

    return _tail(x, o_a, o_b, mod6, g1, g2, w_in, b_gate.reshape(1, 2 * d), *tail_ws)


def kernel(x, c, w_ada, b_ada, g_norm1, g_norm2, w_in, b_gate, g_qa, g_ka, g_qb, g_kb, rpb,
           w_proj_a, w_proj_b, w_o, w_ffn_in, w_ffn_out):
    depth = w_ada.shape[0]
    bsz, d = c.shape
    for l in range(depth):
        mod, w_in_bf = _mod(c, w_ada[l], b_ada[l], w_in[l])
        x = _layer(x, mod.reshape(bsz, 6, d), g_norm1[l], g_norm2[l], w_in_bf, b_gate[l], g_qa[l], g_ka[l], g_qb[l],
                   g_kb[l], rpb[l], w_proj_a[l], w_proj_b[l], w_o[l], w_ffn_in[l], w_ffn_out[l])
    return x
```

```python
import functools

import jax
import jax.numpy as jnp
import numpy as np
from jax import lax
from jax.experimental import pallas as pl
from jax.experimental.pallas import tpu as pltpu

f32 = jnp.float32
bf16 = jnp.bfloat16

D_MODEL = 1024
HEAD_DIM = 64
GRID_W = 64
NA_HEADS = 8
NA_KH = 8
NA_KW = 16
DIL_CONFIGS = ((128, 1), (512, 4), (2048, 16))
DIL_HPG = 4
DIL_HEADS = DIL_HPG * len(DIL_CONFIGS)
DIL_QBLOCK = 64
ROT_DIM = HEAD_DIM // 4
ROPE_THETA = 500000.0
D_FF = -(-8 * D_MODEL // (3 * 256)) * 256
EPS = 1e-6
NEG_INF = -1e30
LOG2E = 1.4426950408889634
WA = NA_HEADS * HEAD_DIM
WB = DIL_HEADS * HEAD_DIM
WB_OUT = DIL_HPG * HEAD_DIM
W_QKV = 3 * WA + 3 * WB

LANES = 128
BF16_ROWS = 16
COL_TILE = 256
DIL_TILE = 1024
QKV_ROW_SPLIT = 2
QKV_DOT_TILES = 2
DIL_UNROLL = 16
NA_ROWS = 16
MOD_STEPS = 8
TAIL_TM = 1024
TAIL_ROW_SPLIT = 2
FF_CHUNK = 256
VMEM_LIMIT = 56 * 1024 * 1024

SAFE_STRIDE = 4

assert all((win // 2) // dil == DIL_QBLOCK for win, dil in DIL_CONFIGS)
assert all(dil == 1 or (dil % SAFE_STRIDE == 0 and dil // SAFE_STRIDE <= SAFE_STRIDE)
           for _, dil in DIL_CONFIGS)


def _params(sem):
    return pltpu.CompilerParams(dimension_semantics=sem, vmem_limit_bytes=VMEM_LIMIT)


def _const_spec(shape):
    nd = len(shape)
    return pl.BlockSpec(shape, lambda *_: (0,) * nd, pipeline_mode=pl.Buffered(1))


def _mod_kernel(ct_ref, w_ref, b_ref, win_ref, o_ref, win_o_ref):
    ct = ct_ref[...]
    act = ct * jax.nn.sigmoid(ct)
    w = w_ref[...]
    rows = [jnp.sum(act[:, b:b + 1] * w, axis=0, keepdims=True) for b in range(ct.shape[1])]
    o_ref[...] = jnp.concatenate(rows, axis=0) + b_ref[...]
    win_o_ref[...] = win_ref[...].astype(bf16)


def _mod(c, w_ada, b_ada, w_in):
    bsz, d = c.shape
    n = w_ada.shape[1]
    steps = MOD_STEPS
    tn = n // steps
    slab = pl.BlockSpec((w_in.shape[0] // steps, w_in.shape[1]), lambda j: (j, 0))
    return pl.pallas_call(
        _mod_kernel,
        grid=(steps,),
        in_specs=[pl.BlockSpec((d, bsz), lambda j: (0, 0)),
                  pl.BlockSpec((d, tn), lambda j: (0, j)),
                  pl.BlockSpec((1, tn), lambda j: (0, j)),
                  slab],
        out_specs=[pl.BlockSpec((bsz, tn), lambda j: (0, j)), slab],
        out_shape=[jax.ShapeDtypeStruct((bsz, n), f32), jax.ShapeDtypeStruct(w_in.shape, bf16)],
        compiler_params=_params(("arbitrary",)),
        name="mod",
    )(c.T, w_ada, b_ada.reshape(1, n), w_in)


def _modulated_norm(x, g, sc, sh):
    ms = jnp.mean(x * x, axis=-1, keepdims=True)
    y = x * lax.rsqrt(ms + EPS) * g
    return y * (1.0 + sc) + sh


_J_QB = 3 * WA // COL_TILE
_J_KB = _J_QB + WB // COL_TILE
_J_VB = _J_KB + WB // COL_TILE
_N_J = W_QKV // COL_TILE


def _qkv_tile_kind(j):
    if j < WA // COL_TILE:
        return 0, False, 1
    if j < 2 * WA // COL_TILE:
        return 1, False, 1
    if j < _J_QB:
        return None, False, 1
    dil = DIL_CONFIGS[(j - _J_QB) % len(DIL_CONFIGS)][1]
    if j < _J_KB:
        return 2, True, dil
    if j < _J_VB:
        return 3, True, dil
    return None, False, dil


_QKV_OUT_WIDTHS = (WA,) * 3 + (WB_OUT,) * (3 * len(DIL_CONFIGS))
_OUT_QA, _OUT_KA, _OUT_VA, _OUT_QB, _OUT_KB, _OUT_VB = 0, 1, 2, 3, 3 + len(DIL_CONFIGS), 3 + 2 * len(DIL_CONFIGS)


def _qkv_tile_out(j, o_refs):
    if j < _J_QB:
        per = WA // COL_TILE
        return o_refs[j // per], (j % per) * COL_TILE
    return o_refs[_OUT_QB + j - _J_QB], 0


def _qkv_kernel(n_cast, x_ref, mod_ref, g1_ref, w_ref, gvec_ref, ones_ref, cos_ref, ssin_ref, *rest):
    n_out = len(_QKV_OUT_WIDTHS)
    cast_in, o_refs = rest[:n_cast], rest[n_cast:n_cast + n_out]
    cast_out = rest[n_cast + n_out:2 * n_cast + n_out]
    h_scr, y_scr, z_scr = rest[2 * n_cast + n_out:]
    for src, dst in zip(cast_in, cast_out):
        dst[...] = src[...].astype(bf16)
    tm = x_ref.shape[1]
    nc = COL_TILE // LANES
    mod = mod_ref[0]
    h_scr[...] = _modulated_norm(x_ref[0], g1_ref[...], mod[1:2], mod[0:1]).astype(bf16)
    head_dim = lax.broadcasted_iota(jnp.int32, (tm // QKV_ROW_SPLIT, COL_TILE), 1) % HEAD_DIM
    first_half = head_dim < ROT_DIM // 2

    wide = {}
    for j in range(_N_J):
        gain_row, rotary, dil = _qkv_tile_kind(j)
        o_ref, c0 = _qkv_tile_out(j, o_refs)
        slot = j % y_scr.shape[0]
        for hb in range(QKV_ROW_SPLIT):
            rows = slice(hb * tm // QKV_ROW_SPLIT, (hb + 1) * tm // QKV_ROW_SPLIT)
            if j % QKV_DOT_TILES == 0:
                wcols = slice(j * COL_TILE, min(j + QKV_DOT_TILES, _N_J) * COL_TILE)
                wide[hb] = jnp.dot(h_scr[rows, :], w_ref[:, wcols], preferred_element_type=f32)
            sub = j % QKV_DOT_TILES
            y = wide[hb][:, sub * COL_TILE:(sub + 1) * COL_TILE]
            if gain_row is not None:
                ss = jnp.dot((y * y).astype(bf16), ones_ref[...], preferred_element_type=f32)
                y = y * lax.rsqrt(ss * (1.0 / HEAD_DIM) + EPS) * gvec_ref[gain_row:gain_row + 1, :]
            if rotary:
                cos, ssin = [jnp.concatenate([t_ref[rows, :]] * nc, axis=1) for t_ref in (cos_ref, ssin_ref)]
                half = ROT_DIM // 2
                up = pltpu.roll(y, COL_TILE - half, 1)
                dn = pltpu.roll(y, half, 1)
                y = y * cos + jnp.where(first_half, up, dn) * ssin
            if dil == 1:
                o_ref[0, rows, c0:c0 + COL_TILE] = y.astype(bf16)
            else:
                for c in range(nc):
                    y_scr[slot, c, rows, :] = y[:, c * LANES:(c + 1) * LANES]
        if dil == 1:
            continue
        seg = DIL_TILE // dil
        for c in range(nc):
            ocols = slice(c0 + c * LANES, c0 + (c + 1) * LANES)
            if dil <= SAFE_STRIDE:
                for r in range(dil):
                    rows = y_scr[slot, c, pl.ds(r, seg, stride=dil), :]
                    o_ref[0, r * seg:(r + 1) * seg, ocols] = rows.astype(bf16)
                continue
            hi = dil // SAFE_STRIDE
            sub = DIL_TILE // SAFE_STRIDE
            for r_lo in range(SAFE_STRIDE):
                z_scr[c, r_lo * sub:(r_lo + 1) * sub, :] = y_scr[slot, c, pl.ds(r_lo, sub, stride=SAFE_STRIDE), :]
            for r_lo in range(SAFE_STRIDE):
                for r_hi in range(hi):
                    r = SAFE_STRIDE * r_hi + r_lo
                    rows = z_scr[c, pl.ds(r_lo * sub + r_hi, seg, stride=hi), :]
                    o_ref[0, r * seg:(r + 1) * seg, ocols] = rows.astype(bf16)


def _qkv(x, mod6, g1, w_qkv, gvec, ones_bd, rot_tables, cast_ws, tm=DIL_TILE):
    bsz, s, d = x.shape
    assert tm == DIL_TILE
    nt = s // tm
    steps = bsz * nt
    assert all(w.shape[0] % (BF16_ROWS * steps) == 0 for w in cast_ws)
    slab_specs = [pl.BlockSpec((w.shape[0] // steps, w.shape[1]), lambda b, i: (b * nt + i, 0)) for w in cast_ws]
    outs = pl.pallas_call(
        functools.partial(_qkv_kernel, len(cast_ws)),
        grid=(bsz, nt),
        in_specs=[pl.BlockSpec((1, tm, d), lambda b, i: (b, i, 0)),
                  pl.BlockSpec((1, 6, d), lambda b, i: (b, 0, 0)),
                  _const_spec(g1.shape),
                  pl.BlockSpec((d, W_QKV), lambda b, i: (0, 0), pipeline_mode=pl.Buffered(1)),
                  _const_spec(gvec.shape),
                  _const_spec(ones_bd.shape),
                  ] + [pl.BlockSpec((tm, LANES), lambda b, i: (i, 0))] * len(rot_tables) + slab_specs,
        out_specs=[pl.BlockSpec((1, tm, w), lambda b, i: (b, i, 0)) for w in _QKV_OUT_WIDTHS] + slab_specs,
        out_shape=[jax.ShapeDtypeStruct((bsz, s, w), bf16) for w in _QKV_OUT_WIDTHS]
        + [jax.ShapeDtypeStruct(w.shape, bf16) for w in cast_ws],
        scratch_shapes=[pltpu.VMEM((tm, d), bf16),
                        pltpu.VMEM((2, COL_TILE // LANES, tm, LANES), f32),
                        pltpu.VMEM((COL_TILE // LANES, tm, LANES), f32)],
        compiler_params=_params(("parallel", "parallel")),
        name="qkv",
    )(x, mod6, g1, w_qkv, gvec, ones_bd, *rot_tables, *cast_ws)
    n_out = len(_QKV_OUT_WIDTHS)
    return outs[:n_out], outs[n_out:]


def _pair_mask(nq):
    row = lax.broadcasted_iota(jnp.int32, (2 * nq, LANES), 0)
    lane = lax.broadcasted_iota(jnp.int32, (2 * nq, LANES), 1)
    return (row < nq) == (lane < HEAD_DIM)


def _stack_pair(qp, own_head):
    q2 = jnp.concatenate([qp, qp], axis=0)
    return jnp.where(own_head, q2, jnp.zeros_like(q2))


def _na_window_start(rb, n_rows):
    return jnp.clip(rb * NA_ROWS - NA_KH // 2, 0, n_rows - (NA_ROWS + NA_KH))


def _na_kernel(q_ref, kwin, vwin, tbl_ref, o_ref):
    rb = pl.program_id(1)
    n_rows = pl.num_programs(1) * NA_ROWS
    win_start = _na_window_start(rb, n_rows)
    first_head = lax.broadcasted_iota(jnp.int32, (GRID_W, LANES), 1) < HEAD_DIM
    own_head = _pair_mask(GRID_W)
    nkeys = NA_KH * GRID_W
    npair = NA_HEADS // 2
    pair_rows = 2 * GRID_W

    def row_body(a, carry):
        r = rb * NA_ROWS + a
        row_start = jnp.clip(r - NA_KH // 2, 0, n_rows - NA_KH)
        delta = r - row_start
        off = pl.multiple_of((row_start - win_start) * GRID_W, GRID_W)
        qoff = pl.multiple_of(a * GRID_W, GRID_W)
        s_parts = []
        for hp in range(npair):
            cols = slice(hp * LANES, (hp + 1) * LANES)
            q2 = _stack_pair(q_ref[0, pl.ds(qoff, GRID_W), cols], own_head)
            kk = kwin[0, pl.ds(off, nkeys), cols]
            s_parts.append(lax.dot_general(q2, kk, (((1,), (1,)), ((), ())), preferred_element_type=f32))
        n_off = 2 * NA_KH - 2
        bias = jnp.concatenate(
            [jnp.concatenate([tbl_ref[h * n_off + 2 * kp - delta + NA_KH - 1] for kp in range(NA_KH // 2)], axis=1)
             for h in range(NA_HEADS)], axis=0)
        s = jnp.concatenate(s_parts, axis=0) + bias
        m = jnp.max(s, axis=-1, keepdims=True)
        p = jnp.exp2(s - m)
        inv = 1.0 / jnp.sum(p, axis=-1, keepdims=True)
        pb = p.astype(bf16)
        for hp in range(npair):
            cols = slice(hp * LANES, (hp + 1) * LANES)
            rows = slice(hp * pair_rows, (hp + 1) * pair_rows)
            vv = vwin[0, pl.ds(off, nkeys), cols]
            pv = jnp.dot(pb[rows], vv, preferred_element_type=f32) * inv[rows]
            o_ref[0, pl.ds(qoff, GRID_W), cols] = jnp.where(first_head, pv[:GRID_W], pv[GRID_W:]).astype(bf16)
        return carry

    lax.fori_loop(0, NA_ROWS, row_body, 0, unroll=NA_ROWS)


def _na(q, k, v, tbl):
    bsz, s, _ = q.shape
    blk = NA_ROWS * GRID_W
    nb = s // blk
    qspec = pl.BlockSpec((1, blk, WA), lambda b, i: (b, i, 0))
    win_tokens = (NA_ROWS + NA_KH) * GRID_W
    n_rows = s // GRID_W
    window = pl.BlockSpec(
        (pl.Element(1), pl.Element(win_tokens), pl.Element(WA)),
        lambda b, i: (b, pl.multiple_of(_na_window_start(i, n_rows) * GRID_W, GRID_W), 0))

    return pl.pallas_call(
        _na_kernel,
        grid=(bsz, nb),
        in_specs=[qspec, window, window, _const_spec(tbl.shape)],
        out_specs=pl.BlockSpec((1, blk, WA), lambda b, i: (b, i, 0)),
        out_shape=jax.ShapeDtypeStruct((bsz, s, WA), bf16),
        compiler_params=_params(("parallel", "parallel")),
        name="na",
    )(q, k, v, tbl)


def _na_bias_table(rpb):
    col = np.arange(GRID_W)
    col_start = np.clip(col - NA_KW // 2, 0, GRID_W - NA_KW)
    cmask = (col[None, :] >= col_start[:, None]) & (col[None, :] < col_start[:, None] + NA_KW)
    col_off = np.clip(col[None, :] - col[:, None] + (NA_KW - 1), 0, 2 * NA_KW - 2)
    n_c = 2 * NA_KW - 1
    sel = np.zeros((2, n_c, GRID_W, 2, GRID_W), np.float32)
    qq, kk = np.meshgrid(col, col, indexing="ij")
    for j in range(2):
        sel[j, col_off, qq, j, kk] = 1.0
    pen = np.broadcast_to(np.where(cmask, 0.0, NEG_INF)[:, None, :], (GRID_W, 2, GRID_W))
    sel = np.concatenate([sel.reshape(2 * n_c, -1), pen.reshape(1, -1).astype(np.float32)], axis=0)
    rows = jnp.stack([rpb[:, :-1], rpb[:, 1:]], axis=2).astype(f32) * LOG2E
    rows = rows.reshape(NA_HEADS * (2 * NA_KH - 2), 2 * n_c)
    rows = jnp.concatenate([rows, jnp.ones((rows.shape[0], 1), f32)], axis=1)
    t = jnp.dot(rows, jnp.asarray(sel), precision=lax.Precision.HIGHEST)
    return t.reshape(NA_HEADS * (2 * NA_KH - 2), GRID_W, 2 * GRID_W)


def _dil_halo_pieces(dil):
    return 1 if DIL_TILE // dil == DIL_QBLOCK else dil


def _dil_kernel(*refs):
    ng = len(DIL_CONFIGS)
    n_in = sum(1 + 2 * (1 + 2 * _dil_halo_pieces(dil)) for _, dil in DIL_CONFIGS)
    in_refs = list(refs[:n_in])
    o_ref = refs[n_in]
    o_scr, m_scr, l_scr = refs[n_in + 1:]
    ti = pl.program_id(1)
    n_tiles = pl.num_programs(1)
    qb = DIL_QBLOCK
    span = 3 * qb
    sub = DIL_TILE // SAFE_STRIDE

    first_head = lax.broadcasted_iota(jnp.int32, (qb, LANES), 1) < HEAD_DIM
    own_head = _pair_mask(qb)
    npair = DIL_HPG // 2
    qi = lax.broadcasted_iota(jnp.int32, (qb, span), 0)
    kj = lax.broadcasted_iota(jnp.int32, (qb, span), 1)
    band = jnp.where((kj >= qi) & (kj <= qi + 2 * qb), 0.0, NEG_INF).astype(f32)
    kcol = lax.broadcasted_iota(jnp.int32, (1, span), 1)
    pen_lo = jnp.where(kcol < qb, jnp.where(ti == 0, NEG_INF, 0.0), 0.0).astype(f32)
    pen_hi = jnp.where(kcol >= 2 * qb, jnp.where(ti == n_tiles - 1, NEG_INF, 0.0), 0.0).astype(f32)

    for g, (_, dil) in enumerate(DIL_CONFIGS):
        seg = DIL_TILE // dil
        nj = seg // qb
        nh = _dil_halo_pieces(dil)
        q_ref = in_refs.pop(0)
        kc, kp, kn = in_refs.pop(0), [in_refs.pop(0) for _ in range(nh)], [in_refs.pop(0) for _ in range(nh)]
        vc, vp, vn = in_refs.pop(0), [in_refs.pop(0) for _ in range(nh)], [in_refs.pop(0) for _ in range(nh)]

        def halo(pieces, r, cols, nh=nh):
            return pieces[r][0, :, cols] if nh > 1 else pieces[0][0, r * qb:(r + 1) * qb, cols]

        def window(refs3, r, j, cols, seg=seg, nj=nj, halo=halo):
            p_refs, c_ref, n_refs = refs3
            base = r * seg
            if 0 < j < nj - 1:
                return c_ref[0, base + (j - 1) * qb:base + (j + 2) * qb, cols]
            lo = halo(p_refs, r, cols) if j == 0 else c_ref[0, base + (j - 1) * qb:base + j * qb, cols]
            mid = c_ref[0, base + j * qb:base + (j + 1) * qb, cols]
            hi = halo(n_refs, r, cols) if j == nj - 1 else c_ref[0, base + (j + 1) * qb:base + (j + 2) * qb, cols]
            return jnp.concatenate([lo, mid, hi], axis=0)

        for t0 in range(0, dil * nj, DIL_UNROLL):
            blocks = [divmod(t, nj) for t in range(t0, t0 + DIL_UNROLL)]
            s_parts = []
            for k, (r, j) in enumerate(blocks):
                mask = band
                if j == 0:
                    mask = mask + pen_lo
                if j == nj - 1:
                    mask = mask + pen_hi
                for hp in range(npair):
                    cols = slice(hp * LANES, (hp + 1) * LANES)
                    q2 = _stack_pair(q_ref[0, (t0 + k) * qb:(t0 + k + 1) * qb, cols], own_head)
                    kk = window((kp, kc, kn), r, j, cols)
                    s2 = lax.dot_general(q2, kk, (((1,), (1,)), ((), ())), preferred_element_type=f32)
                    s_parts += [s2[:qb] + mask, s2[qb:] + mask]
            s = jnp.concatenate(s_parts, axis=0)
            m = jnp.max(s, axis=-1, keepdims=True)
            p = jnp.exp2(s - m)
            l = jnp.sum(p, axis=-1, keepdims=True)
            pb = p.astype(bf16)
            for k, (r, j) in enumerate(blocks):
                if dil == 1:
                    store_rows = pl.ds(j * qb, qb)
                else:
                    hi = dil // SAFE_STRIDE
                    start = (r % SAFE_STRIDE) * sub + j * qb * hi + r // SAFE_STRIDE
                    store_rows = pl.ds(start, qb, stride=hi) if hi > 1 else pl.ds(start, qb)
                for hp in range(npair):
                    cols = slice(hp * LANES, (hp + 1) * LANES)
                    r0 = (k * npair + hp) * 2 * qb
                    vv = window((vp, vc, vn), r, j, cols)
                    pv = jnp.dot(pb[r0:r0 + 2 * qb], vv, preferred_element_type=f32)
                    slot = g * npair + hp
                    o_scr[slot, store_rows, :] = jnp.where(first_head, pv[:qb], pv[qb:])
                    m_scr[slot, store_rows, :] = jnp.where(first_head, m[r0:r0 + qb], m[r0 + qb:r0 + 2 * qb])
                    l_scr[slot, store_rows, :] = jnp.where(first_head, l[r0:r0 + qb], l[r0 + qb:r0 + 2 * qb])

    for hp in range(npair):
        for c in range(SAFE_STRIDE):
            def stream(scr, g):
                if DIL_CONFIGS[g][1] == 1:
                    return scr[g * npair + hp, pl.ds(c, sub, stride=SAFE_STRIDE), :]
                return scr[g * npair + hp, c * sub:(c + 1) * sub, :]

            ms = [stream(m_scr, g) for g in range(ng)]
            m_all = functools.reduce(jnp.maximum, ms)
            num = jnp.zeros_like(m_all)
            den = jnp.zeros_like(m_all)
            for g in range(ng):
                w = jnp.exp2(ms[g] - m_all)
                num = num + w * stream(o_scr, g)
                den = den + w * stream(l_scr, g)
            o_scr[ng * npair + hp, pl.ds(c, sub, stride=SAFE_STRIDE), :] = num / den
        o_ref[0, :, hp * LANES:(hp + 1) * LANES] = o_scr[ng * npair + hp].astype(bf16)


def _dil(qs, ks, vs):
    bsz, s, _ = qs[0].shape
    nt = s // DIL_TILE
    ng = len(DIL_CONFIGS)
    qb = DIL_QBLOCK
    per_tile = DIL_TILE // qb
    blk = (1, DIL_TILE, WB_OUT)
    cur = pl.BlockSpec(blk, lambda b, i: (b, i, 0))
    in_specs, args = [], []
    for g, (_, dil) in enumerate(DIL_CONFIGS):
        seg = DIL_TILE // dil
        if _dil_halo_pieces(dil) == 1 and seg == qb:
            prev = [pl.BlockSpec(blk, lambda b, i: (b, jnp.maximum(i - 1, 0), 0))]
            nxt = [pl.BlockSpec(blk, lambda b, i: (b, jnp.minimum(i + 1, nt - 1), 0))]
        else:
            prev = [pl.BlockSpec((1, qb, WB_OUT), lambda b, i, o=(r + 1) * seg // qb - 1:
                                 (b, jnp.maximum((i - 1) * per_tile + o, 0), 0)) for r in range(dil)]
            nxt = [pl.BlockSpec((1, qb, WB_OUT), lambda b, i, o=r * seg // qb:
                                (b, jnp.minimum((i + 1) * per_tile + o, s // qb - 1), 0)) for r in range(dil)]
        in_specs += [cur] + 2 * ([cur] + prev + nxt)
        args += [qs[g]] + [ks[g]] * (1 + len(prev) + len(nxt)) + [vs[g]] * (1 + len(prev) + len(nxt))
    return pl.pallas_call(
        _dil_kernel,
        grid=(bsz, nt),
        in_specs=in_specs,
        out_specs=pl.BlockSpec(blk, lambda b, i: (b, i, 0)),
        out_shape=jax.ShapeDtypeStruct((bsz, s, WB_OUT), bf16),
        scratch_shapes=[pltpu.VMEM(((ng + 1) * DIL_HPG // 2, DIL_TILE, LANES), f32)]
        + [pltpu.VMEM((ng * DIL_HPG // 2, DIL_TILE, LANES), f32)] * 2,
        compiler_params=_params(("parallel", "parallel")),
        name="dil",
    )(*args)


def _tail_kernel(x_ref, oa_ref, ob_ref, mod_ref, g1_ref, g2_ref, bg_ref, wpa_ref, wpb_ref,
                 wo_ref, win_ref, wout_ref, *rest):
    d = x_ref.shape[2]
    n_gate = 2 * d // COL_TILE
    wg_refs = rest[:n_gate]
    o_ref, h_scr, h2_scr, m_scr, act_scr = rest[n_gate:]
    mod = mod_ref[0]
    sh1, sc1, gt1, sh2, sc2, gt2 = [mod[k:k + 1] for k in range(6)]
    tm = x_ref.shape[1]
    for rc in range(TAIL_ROW_SPLIT):
        rows = slice(rc * tm // TAIL_ROW_SPLIT, (rc + 1) * tm // TAIL_ROW_SPLIT)
        x = x_ref[0, rows, :]
        h_scr[rows, :] = _modulated_norm(x, g1_ref[...], sc1, sh1).astype(bf16)
        oa = oa_ref[0, rows, :]
        ob = ob_ref[0, rows, :]
        for n in range(d // COL_TILE):
            ca = slice(n * COL_TILE, (n + 1) * COL_TILE)
            cb = slice(d + n * COL_TILE, d + (n + 1) * COL_TILE)
            wga, wgb = wg_refs[n], wg_refs[n_gate // 2 + n]
            ga = jax.nn.sigmoid(jnp.dot(h_scr[rows, :], wga[...], preferred_element_type=f32) + bg_ref[:, ca])
            gb = jax.nn.sigmoid(jnp.dot(h_scr[rows, :], wgb[...], preferred_element_type=f32) + bg_ref[:, cb])
            pa = jnp.dot(oa, wpa_ref[:, ca], preferred_element_type=f32)
            pb = jnp.dot(ob, wpb_ref[:, ca], preferred_element_type=f32)
            m_scr[rows, ca] = (ga * pa + gb * pb).astype(bf16)
        x1 = x + gt1 * jnp.dot(m_scr[rows, :], wo_ref[...], preferred_element_type=f32)
        h2_scr[rows, :] = _modulated_norm(x1, g2_ref[...], sc2, sh2).astype(bf16)
        for f in range(D_FF // FF_CHUNK):
            ca = slice(f * FF_CHUNK, (f + 1) * FF_CHUNK)
            cu = slice(D_FF + f * FF_CHUNK, D_FF + (f + 1) * FF_CHUNK)
            a = jnp.dot(h2_scr[rows, :], win_ref[:, ca], preferred_element_type=f32)
            u = jnp.dot(h2_scr[rows, :], win_ref[:, cu], preferred_element_type=f32)
            act_scr[rows, ca] = ((a * jax.nn.sigmoid(a)) * u).astype(bf16)
        o_ref[0, rows, :] = x1 + gt2 * jnp.dot(act_scr[rows, :], wout_ref[...], preferred_element_type=f32)


def _tail(x, o_a, o_b, mod6, g1, g2, w_in, bg, wpa, wpb, wo, w_ffn_in, w_ffn_out, tm=TAIL_TM):
    bsz, s, d = x.shape
    tok = lambda w: pl.BlockSpec((1, tm, w), lambda b, i: (b, i, 0))
    consts = [g1, g2, bg, wpa, wpb, wo, w_ffn_in, w_ffn_out]
    n_gate = 2 * d // COL_TILE
    gate_specs = [pl.BlockSpec((d, COL_TILE), lambda b, i, c=_N_J + n: (0, c), pipeline_mode=pl.Buffered(1))
                  for n in range(n_gate)]
    return pl.pallas_call(
        _tail_kernel,
        grid=(bsz, s // tm),
        in_specs=[tok(d), tok(WA), tok(WB_OUT), pl.BlockSpec((1, 6, d), lambda b, i: (b, 0, 0))]
        + [_const_spec(c.shape) for c in consts] + gate_specs,
        out_specs=tok(d),
        out_shape=jax.ShapeDtypeStruct((bsz, s, d), f32),
        scratch_shapes=[pltpu.VMEM((tm, d), bf16)] * 3 + [pltpu.VMEM((tm, D_FF), bf16)],
        compiler_params=_params(("parallel", "parallel")),
        name="tail",
    )(x, o_a, o_b, mod6, *consts, *([w_in] * n_gate))


def _rotary_tables(s):
    half = ROT_DIM // 2
    inv_freq = ROPE_THETA ** (-(np.arange(half, dtype=np.float64) * 2.0) / ROT_DIM)
    ang = np.arange(s, dtype=np.float64)[:, None] * inv_freq[None, :]
    cos, sin = np.cos(ang), np.sin(ang)
    rest = HEAD_DIM - ROT_DIM
    cos_h = np.concatenate([cos, cos, np.ones((s, rest))], axis=1)
    ssin_h = np.concatenate([-sin, sin, np.zeros((s, rest))], axis=1)
    rep = LANES // HEAD_DIM
    return (jnp.asarray(np.tile(cos_h, (1, rep)), dtype=f32), jnp.asarray(np.tile(ssin_h, (1, rep)), dtype=f32))


def _layer(x, mod6, g_norm1, g_norm2, w_in, b_gate, g_qa, g_ka, g_qb, g_kb, rpb,
           w_proj_a, w_proj_b, w_o, w_ffn_in, w_ffn_out):
    bsz, s, d = x.shape
    assert d == D_MODEL and w_in.shape == (d, W_QKV + 2 * d) and w_ffn_in.shape == (d, 2 * D_FF)
    assert s % DIL_TILE == 0 and s % (NA_ROWS * GRID_W) == 0 and s % TAIL_TM == 0
    assert s // GRID_W >= NA_ROWS + NA_KH
    scale = HEAD_DIM ** -0.5 * LOG2E
    rep = COL_TILE // HEAD_DIM
    gvec = jnp.stack([jnp.tile(g_qa * scale, rep), jnp.tile(g_ka, rep),
                      jnp.tile(g_qb * scale, rep), jnp.tile(g_kb, rep)]).astype(f32)
    hid = jnp.arange(COL_TILE) // HEAD_DIM
    ones_bd = (hid[:, None] == hid[None, :]).astype(bf16)
    rot_tables = _rotary_tables(s)
    g1 = g_norm1.reshape(1, d)
    g2 = g_norm2.reshape(1, d)

    qkv, tail_ws = _qkv(x, mod6, g1, w_in, gvec, ones_bd, rot_tables,
                        [w_proj_a, w_proj_b, w_o, w_ffn_in, w_ffn_out])
    ng = len(DIL_CONFIGS)
    o_a = _na(qkv[_OUT_QA], qkv[_OUT_KA], qkv[_OUT_VA], _na_bias_table(rpb))
    o_b = _dil(qkv[_OUT_QB:_OUT_QB + ng], qkv[_OUT_KB:_OUT_KB + ng], qkv[_OUT_VB:_OUT_VB + ng])

    return _tail(x, o_a, o_b, mod6, g1, g2, w_in, b_gate.reshape(1, 2 * d), *tail_ws)


def kernel(x, c, w_ada, b_ada, g_norm1, g_norm2, w_in, b_gate, g_qa, g_ka, g_qb, g_kb, rpb,
           w_proj_a, w_proj_b, w_o, w_ffn_in, w_ffn_out):
    depth = w_ada.shape[0]
    bsz, d = c.shape
    for l in range(depth):
        mod, w_in_bf = _mod(c, w_ada[l], b_ada[l], w_in[l])
        x = _layer(x, mod.reshape(bsz, 6, d), g_norm1[l], g_norm2[l], w_in_bf, b_gate[l], g_qa[l], g_ka[l], g_qb[l],
                   g_kb[l], rpb[l], w_proj_a[l], w_proj_b[l], w_o[l], w_ffn_in[l], w_ffn_out[l])
    return x
```

```python
import functools

import jax
import jax.numpy as jnp
import numpy as np
from jax import lax
from jax.experimental import pallas as pl
from jax.experimental.pallas import tpu as pltpu

f32 = jnp.float32
bf16 = jnp.bfloat16

D_MODEL = 1024
HEAD_DIM = 64
GRID_W = 64
NA_HEADS = 8
NA_KH = 8
NA_KW = 16
DIL_CONFIGS = ((128, 1), (512, 4), (2048, 16))
DIL_HPG = 4
DIL_HEADS = DIL_HPG * len(DIL_CONFIGS)
DIL_QBLOCK = 64
ROT_DIM = HEAD_DIM // 4
ROPE_THETA = 500000.0
D_FF = -(-8 * D_MODEL // (3 * 256)) * 256
EPS = 1e-6
NEG_INF = -1e30
LOG2E = 1.4426950408889634
WA = NA_HEADS * HEAD_DIM
WB = DIL_HEADS * HEAD_DIM
WB_OUT = DIL_HPG * HEAD_DIM
W_QKV = 3 * WA + 3 * WB

LANES = 128
BF16_ROWS = 16
COL_TILE = 256
DIL_TILE = 1024
QKV_ROW_SPLIT = 2
QKV_DOT_TILES = 2
DIL_UNROLL = 16
NA_ROWS = 16
MOD_STEPS = 8
TAIL_TM = 1024
TAIL_ROW_SPLIT = 2
FF_CHUNK = 256
VMEM_LIMIT = 56 * 1024 * 1024

SAFE_STRIDE = 4

assert all((win // 2) // dil == DIL_QBLOCK for win, dil in DIL_CONFIGS)
assert all(dil == 1 or (dil % SAFE_STRIDE == 0 and dil // SAFE_STRIDE <= SAFE_STRIDE)
           for _, dil in DIL_CONFIGS)


def _params(sem):
    return pltpu.CompilerParams(dimension_semantics=sem, vmem_limit_bytes=VMEM_LIMIT)


def _const_spec(shape):
    nd = len(shape)
    return pl.BlockSpec(shape, lambda *_: (0,) * nd, pipeline_mode=pl.Buffered(1))


def _mod_kernel(ct_ref, w_ref, b_ref, win_ref, o_ref, win_o_ref):
    ct = ct_ref[...]
    act = ct * jax.nn.sigmoid(ct)
    w = w_ref[...]
    rows = [jnp.sum(act[:, b:b + 1] * w, axis=0, keepdims=True) for b in range(ct.shape[1])]
    o_ref[...] = jnp.concatenate(rows, axis=0) + b_ref[...]
    win_o_ref[...] = win_ref[...].astype(bf16)


def _mod(c, w_ada, b_ada, w_in):
    bsz, d = c.shape
    n = w_ada.shape[1]
    steps = MOD_STEPS
    tn = n // steps
    slab = pl.BlockSpec((w_in.shape[0] // steps, W_QKV), lambda j: (j, 0))
    return pl.pallas_call(
        _mod_kernel,
        grid=(steps,),
        in_specs=[pl.BlockSpec((d, bsz), lambda j: (0, 0)),
                  pl.BlockSpec((d, tn), lambda j: (0, j)),
                  pl.BlockSpec((1, tn), lambda j: (0, j)),
                  slab],
        out_specs=[pl.BlockSpec((bsz, tn), lambda j: (0, j)), slab],
        out_shape=[jax.ShapeDtypeStruct((bsz, n), f32), jax.ShapeDtypeStruct((w_in.shape[0], W_QKV), bf16)],
        compiler_params=_params(("arbitrary",)),
        name="mod",
    )(c.T, w_ada, b_ada.reshape(1, n), w_in)


def _modulated_norm(x, g, sc, sh):
    ms = jnp.mean(x * x, axis=-1, keepdims=True)
    y = x * lax.rsqrt(ms + EPS) * g
    return y * (1.0 + sc) + sh


_J_QB = 3 * WA // COL_TILE
_J_KB = _J_QB + WB // COL_TILE
_J_VB = _J_KB + WB // COL_TILE
_N_J = W_QKV // COL_TILE


def _qkv_tile_kind(j):
    if j < WA // COL_TILE:
        return 0, False, 1
    if j < 2 * WA // COL_TILE:
        return 1, False, 1
    if j < _J_QB:
        return None, False, 1
    dil = DIL_CONFIGS[(j - _J_QB) % len(DIL_CONFIGS)][1]
    if j < _J_KB:
        return 2, True, dil
    if j < _J_VB:
        return 3, True, dil
    return None, False, dil


_QKV_OUT_WIDTHS = (WA,) * 3 + (WB_OUT,) * (3 * len(DIL_CONFIGS))
_OUT_QA, _OUT_KA, _OUT_VA, _OUT_QB, _OUT_KB, _OUT_VB = 0, 1, 2, 3, 3 + len(DIL_CONFIGS), 3 + 2 * len(DIL_CONFIGS)


def _qkv_tile_out(j, o_refs):
    if j < _J_QB:
        per = WA // COL_TILE
        return o_refs[j // per], (j % per) * COL_TILE
    return o_refs[_OUT_QB + j - _J_QB], 0


def _qkv_kernel(n_cast, x_ref, mod_ref, g1_ref, w_ref, gvec_ref, ones_ref, cos_ref, ssin_ref, *rest):
    n_out = len(_QKV_OUT_WIDTHS)
    cast_in, o_refs = rest[:n_cast], rest[n_cast:n_cast + n_out]
    cast_out = rest[n_cast + n_out:2 * n_cast + n_out]
    h_scr, y_scr, z_scr = rest[2 * n_cast + n_out:]
    for src, dst in zip(cast_in, cast_out):
        dst[...] = src[...].astype(bf16)
    tm = x_ref.shape[1]
    nc = COL_TILE // LANES
    mod = mod_ref[0]
    h_scr[...] = _modulated_norm(x_ref[0], g1_ref[...], mod[1:2], mod[0:1]).astype(bf16)
    head_dim = lax.broadcasted_iota(jnp.int32, (tm // QKV_ROW_SPLIT, COL_TILE), 1) % HEAD_DIM
    first_half = head_dim < ROT_DIM // 2

    wide = {}
    for j in range(_N_J):
        gain_row, rotary, dil = _qkv_tile_kind(j)
        o_ref, c0 = _qkv_tile_out(j, o_refs)
        slot = j % y_scr.shape[0]
        for hb in range(QKV_ROW_SPLIT):
            rows = slice(hb * tm // QKV_ROW_SPLIT, (hb + 1) * tm // QKV_ROW_SPLIT)
            if j % QKV_DOT_TILES == 0:
                wcols = slice(j * COL_TILE, min(j + QKV_DOT_TILES, _N_J) * COL_TILE)
                wide[hb] = jnp.dot(h_scr[rows, :], w_ref[:, wcols], preferred_element_type=f32)
            sub = j % QKV_DOT_TILES
            y = wide[hb][:, sub * COL_TILE:(sub + 1) * COL_TILE]
            if gain_row is not None:
                ss = jnp.dot((y * y).astype(bf16), ones_ref[...], preferred_element_type=f32)
                y = y * lax.rsqrt(ss * (1.0 / HEAD_DIM) + EPS) * gvec_ref[gain_row:gain_row + 1, :]
            if rotary:
                cos, ssin = [jnp.concatenate([t_ref[rows, :]] * nc, axis=1) for t_ref in (cos_ref, ssin_ref)]
                half = ROT_DIM // 2
                up = pltpu.roll(y, COL_TILE - half, 1)
                dn = pltpu.roll(y, half, 1)
                y = y * cos + jnp.where(first_half, up, dn) * ssin
            if dil == 1:
                o_ref[0, rows, c0:c0 + COL_TILE] = y.astype(bf16)
            else:
                for c in range(nc):
                    y_scr[slot, c, rows, :] = y[:, c * LANES:(c + 1) * LANES]
        if dil == 1:
            continue
        seg = DIL_TILE // dil
        for c in range(nc):
            ocols = slice(c0 + c * LANES, c0 + (c + 1) * LANES)
            if dil <= SAFE_STRIDE:
                for r in range(dil):
                    rows = y_scr[slot, c, pl.ds(r, seg, stride=dil), :]
                    o_ref[0, r * seg:(r + 1) * seg, ocols] = rows.astype(bf16)
                continue
            hi = dil // SAFE_STRIDE
            sub = DIL_TILE // SAFE_STRIDE
            for r_lo in range(SAFE_STRIDE):
                z_scr[c, r_lo * sub:(r_lo + 1) * sub, :] = y_scr[slot, c, pl.ds(r_lo, sub, stride=SAFE_STRIDE), :]
            for r_lo in range(SAFE_STRIDE):
                for r_hi in range(hi):
                    r = SAFE_STRIDE * r_hi + r_lo
                    rows = z_scr[c, pl.ds(r_lo * sub + r_hi, seg, stride=hi), :]
                    o_ref[0, r * seg:(r + 1) * seg, ocols] = rows.astype(bf16)


def _qkv(x, mod6, g1, w_qkv, gvec, ones_bd, rot_tables, cast_ws, w_in_f32, tm=DIL_TILE):
    bsz, s, d = x.shape
    assert tm == DIL_TILE
    nt = s // tm
    steps = bsz * nt
    assert all(w.shape[0] % (BF16_ROWS * steps) == 0 for w in cast_ws)
    slab_specs = [pl.BlockSpec((w.shape[0] // steps, w.shape[1]), lambda b, i: (b * nt + i, 0)) for w in cast_ws]
    assert d % (BF16_ROWS * steps) == 0
    gate_rows, gate_w = d // steps, w_in_f32.shape[1] - W_QKV
    gate_in = pl.BlockSpec((pl.Element(gate_rows), pl.Element(gate_w)),
                           lambda b, i: (pl.multiple_of((b * nt + i) * gate_rows, BF16_ROWS), W_QKV))
    gate_out = pl.BlockSpec((gate_rows, gate_w), lambda b, i: (b * nt + i, 0))
    in_slabs, out_slabs = slab_specs + [gate_in], slab_specs + [gate_out]
    outs = pl.pallas_call(
        functools.partial(_qkv_kernel, len(cast_ws) + 1),
        grid=(bsz, nt),
        in_specs=[pl.BlockSpec((1, tm, d), lambda b, i: (b, i, 0)),
                  pl.BlockSpec((1, 6, d), lambda b, i: (b, 0, 0)),
                  _const_spec(g1.shape),
                  pl.BlockSpec((d, W_QKV), lambda b, i: (0, 0), pipeline_mode=pl.Buffered(1)),
                  _const_spec(gvec.shape),
                  _const_spec(ones_bd.shape),
                  ] + [pl.BlockSpec((tm, LANES), lambda b, i: (i, 0))] * len(rot_tables) + in_slabs,
        out_specs=[pl.BlockSpec((1, tm, w), lambda b, i: (b, i, 0)) for w in _QKV_OUT_WIDTHS] + out_slabs,
        out_shape=[jax.ShapeDtypeStruct((bsz, s, w), bf16) for w in _QKV_OUT_WIDTHS]
        + [jax.ShapeDtypeStruct(w.shape, bf16) for w in cast_ws] + [jax.ShapeDtypeStruct((d, gate_w), bf16)],
        scratch_shapes=[pltpu.VMEM((tm, d), bf16),
                        pltpu.VMEM((2, COL_TILE // LANES, tm, LANES), f32),
                        pltpu.VMEM((COL_TILE // LANES, tm, LANES), f32)],
        compiler_params=_params(("parallel", "parallel")),
        name="qkv",
    )(x, mod6, g1, w_qkv, gvec, ones_bd, *rot_tables, *cast_ws, w_in_f32)
    n_out = len(_QKV_OUT_WIDTHS)
    return outs[:n_out], outs[n_out:-1], outs[-1]


def _pair_mask(nq):
    row = lax.broadcasted_iota(jnp.int32, (2 * nq, LANES), 0)
    lane = lax.broadcasted_iota(jnp.int32, (2 * nq, LANES), 1)
    return (row < nq) == (lane < HEAD_DIM)


def _stack_pair(qp, own_head):
    q2 = jnp.concatenate([qp, qp], axis=0)
    return jnp.where(own_head, q2, jnp.zeros_like(q2))


def _na_window_start(rb, n_rows):
    return jnp.clip(rb * NA_ROWS - NA_KH // 2, 0, n_rows - (NA_ROWS + NA_KH))


def _na_kernel(q_ref, kwin, vwin, tbl_ref, o_ref):
    rb = pl.program_id(1)
    n_rows = pl.num_programs(1) * NA_ROWS
    win_start = _na_window_start(rb, n_rows)
    first_head = lax.broadcasted_iota(jnp.int32, (GRID_W, LANES), 1) < HEAD_DIM
    own_head = _pair_mask(GRID_W)
    nkeys = NA_KH * GRID_W
    npair = NA_HEADS // 2
    pair_rows = 2 * GRID_W

    def row_body(a, carry):
        r = rb * NA_ROWS + a
        row_start = jnp.clip(r - NA_KH // 2, 0, n_rows - NA_KH)
        delta = r - row_start
        off = pl.multiple_of((row_start - win_start) * GRID_W, GRID_W)
        qoff = pl.multiple_of(a * GRID_W, GRID_W)
        s_parts = []
        for hp in range(npair):
            cols = slice(hp * LANES, (hp + 1) * LANES)
            q2 = _stack_pair(q_ref[0, pl.ds(qoff, GRID_W), cols], own_head)
            kk = kwin[0, pl.ds(off, nkeys), cols]
            s_parts.append(lax.dot_general(q2, kk, (((1,), (1,)), ((), ())), preferred_element_type=f32))
        n_off = 2 * NA_KH - 2
        bias = jnp.concatenate(
            [jnp.concatenate([tbl_ref[h * n_off + 2 * kp - delta + NA_KH - 1] for kp in range(NA_KH // 2)], axis=1)
             for h in range(NA_HEADS)], axis=0)
        s = jnp.concatenate(s_parts, axis=0) + bias
        m = jnp.max(s, axis=-1, keepdims=True)
        p = jnp.exp2(s - m)
        inv = 1.0 / jnp.sum(p, axis=-1, keepdims=True)
        pb = p.astype(bf16)
        for hp in range(npair):
            cols = slice(hp * LANES, (hp + 1) * LANES)
            rows = slice(hp * pair_rows, (hp + 1) * pair_rows)
            vv = vwin[0, pl.ds(off, nkeys), cols]
            pv = jnp.dot(pb[rows], vv, preferred_element_type=f32) * inv[rows]
            o_ref[0, pl.ds(qoff, GRID_W), cols] = jnp.where(first_head, pv[:GRID_W], pv[GRID_W:]).astype(bf16)
        return carry

    lax.fori_loop(0, NA_ROWS, row_body, 0, unroll=NA_ROWS)


def _na(q, k, v, tbl):
    bsz, s, _ = q.shape
    blk = NA_ROWS * GRID_W
    nb = s // blk
    qspec = pl.BlockSpec((1, blk, WA), lambda b, i: (b, i, 0))
    win_tokens = (NA_ROWS + NA_KH) * GRID_W
    n_rows = s // GRID_W
    window = pl.BlockSpec(
        (pl.Element(1), pl.Element(win_tokens), pl.Element(WA)),
        lambda b, i: (b, pl.multiple_of(_na_window_start(i, n_rows) * GRID_W, GRID_W), 0))

    return pl.pallas_call(
        _na_kernel,
        grid=(bsz, nb),
        in_specs=[qspec, window, window, _const_spec(tbl.shape)],
        out_specs=pl.BlockSpec((1, blk, WA), lambda b, i: (b, i, 0)),
        out_shape=jax.ShapeDtypeStruct((bsz, s, WA), bf16),
        compiler_params=_params(("parallel", "parallel")),
        name="na",
    )(q, k, v, tbl)


def _na_bias_table(rpb):
    col = np.arange(GRID_W)
    col_start = np.clip(col - NA_KW // 2, 0, GRID_W - NA_KW)
    cmask = (col[None, :] >= col_start[:, None]) & (col[None, :] < col_start[:, None] + NA_KW)
    col_off = np.clip(col[None, :] - col[:, None] + (NA_KW - 1), 0, 2 * NA_KW - 2)
    n_c = 2 * NA_KW - 1
    sel = np.zeros((2, n_c, GRID_W, 2, GRID_W), np.float32)
    qq, kk = np.meshgrid(col, col, indexing="ij")
    for j in range(2):
        sel[j, col_off, qq, j, kk] = 1.0
    pen = np.broadcast_to(np.where(cmask, 0.0, NEG_INF)[:, None, :], (GRID_W, 2, GRID_W))
    sel = np.concatenate([sel.reshape(2 * n_c, -1), pen.reshape(1, -1).astype(np.float32)], axis=0)
    rows = jnp.stack([rpb[:, :-1], rpb[:, 1:]], axis=2).astype(f32) * LOG2E
    rows = rows.reshape(NA_HEADS * (2 * NA_KH - 2), 2 * n_c)
    rows = jnp.concatenate([rows, jnp.ones((rows.shape[0], 1), f32)], axis=1)
    t = jnp.dot(rows, jnp.asarray(sel), precision=lax.Precision.HIGHEST)
    return t.reshape(NA_HEADS * (2 * NA_KH - 2), GRID_W, 2 * GRID_W)


def _dil_halo_pieces(dil):
    return 1 if DIL_TILE // dil == DIL_QBLOCK else dil


def _dil_kernel(*refs):
    ng = len(DIL_CONFIGS)
    n_in = sum(1 + 2 * (1 + 2 * _dil_halo_pieces(dil)) for _, dil in DIL_CONFIGS)
    in_refs = list(refs[:n_in])
    o_ref = refs[n_in]
    o_scr, m_scr, l_scr = refs[n_in + 1:]
    ti = pl.program_id(1)
    n_tiles = pl.num_programs(1)
    qb = DIL_QBLOCK
    span = 3 * qb
    sub = DIL_TILE // SAFE_STRIDE

    first_head = lax.broadcasted_iota(jnp.int32, (qb, LANES), 1) < HEAD_DIM
    own_head = _pair_mask(qb)
    npair = DIL_HPG // 2
    qi = lax.broadcasted_iota(jnp.int32, (qb, span), 0)
    kj = lax.broadcasted_iota(jnp.int32, (qb, span), 1)
    band = jnp.where((kj >= qi) & (kj <= qi + 2 * qb), 0.0, NEG_INF).astype(f32)
    kcol = lax.broadcasted_iota(jnp.int32, (1, span), 1)
    pen_lo = jnp.where(kcol < qb, jnp.where(ti == 0, NEG_INF, 0.0), 0.0).astype(f32)
    pen_hi = jnp.where(kcol >= 2 * qb, jnp.where(ti == n_tiles - 1, NEG_INF, 0.0), 0.0).astype(f32)

    for g, (_, dil) in enumerate(DIL_CONFIGS):
        seg = DIL_TILE // dil
        nj = seg // qb
        nh = _dil_halo_pieces(dil)
        q_ref = in_refs.pop(0)
        kc, kp, kn = in_refs.pop(0), [in_refs.pop(0) for _ in range(nh)], [in_refs.pop(0) for _ in range(nh)]
        vc, vp, vn = in_refs.pop(0), [in_refs.pop(0) for _ in range(nh)], [in_refs.pop(0) for _ in range(nh)]

        def halo(pieces, r, cols, nh=nh):
            return pieces[r][0, :, cols] if nh > 1 else pieces[0][0, r * qb:(r + 1) * qb, cols]

        def window(refs3, r, j, cols, seg=seg, nj=nj, halo=halo):
            p_refs, c_ref, n_refs = refs3
            base = r * seg
            if 0 < j < nj - 1:
                return c_ref[0, base + (j - 1) * qb:base + (j + 2) * qb, cols]
            lo = halo(p_refs, r, cols) if j == 0 else c_ref[0, base + (j - 1) * qb:base + j * qb, cols]
            mid = c_ref[0, base + j * qb:base + (j + 1) * qb, cols]
            hi = halo(n_refs, r, cols) if j == nj - 1 else c_ref[0, base + (j + 1) * qb:base + (j + 2) * qb, cols]
            return jnp.concatenate([lo, mid, hi], axis=0)

        for t0 in range(0, dil * nj, DIL_UNROLL):
            blocks = [divmod(t, nj) for t in range(t0, t0 + DIL_UNROLL)]
            s_parts = []
            for k, (r, j) in enumerate(blocks):
                mask = band
                if j == 0:
                    mask = mask + pen_lo
                if j == nj - 1:
                    mask = mask + pen_hi
                for hp in range(npair):
                    cols = slice(hp * LANES, (hp + 1) * LANES)
                    q2 = _stack_pair(q_ref[0, (t0 + k) * qb:(t0 + k + 1) * qb, cols], own_head)
                    kk = window((kp, kc, kn), r, j, cols)
                    s2 = lax.dot_general(q2, kk, (((1,), (1,)), ((), ())), preferred_element_type=f32)
                    s_parts += [s2[:qb] + mask, s2[qb:] + mask]
            s = jnp.concatenate(s_parts, axis=0)
            m = jnp.max(s, axis=-1, keepdims=True)
            p = jnp.exp2(s - m)
            l = jnp.sum(p, axis=-1, keepdims=True)
            pb = p.astype(bf16)
            for k, (r, j) in enumerate(blocks):
                if dil == 1:
                    store_rows = pl.ds(j * qb, qb)
                else:
                    hi = dil // SAFE_STRIDE
                    start = (r % SAFE_STRIDE) * sub + j * qb * hi + r // SAFE_STRIDE
                    store_rows = pl.ds(start, qb, stride=hi) if hi > 1 else pl.ds(start, qb)
                for hp in range(npair):
                    cols = slice(hp * LANES, (hp + 1) * LANES)
                    r0 = (k * npair + hp) * 2 * qb
                    vv = window((vp, vc, vn), r, j, cols)
                    pv = jnp.dot(pb[r0:r0 + 2 * qb], vv, preferred_element_type=f32)
                    slot = g * npair + hp
                    o_scr[slot, store_rows, :] = jnp.where(first_head, pv[:qb], pv[qb:])
                    m_scr[slot, store_rows, :] = jnp.where(first_head, m[r0:r0 + qb], m[r0 + qb:r0 + 2 * qb])
                    l_scr[slot, store_rows, :] = jnp.where(first_head, l[r0:r0 + qb], l[r0 + qb:r0 + 2 * qb])

    for hp in range(npair):
        for c in range(SAFE_STRIDE):
            def stream(scr, g):
                if DIL_CONFIGS[g][1] == 1:
                    return scr[g * npair + hp, pl.ds(c, sub, stride=SAFE_STRIDE), :]
                return scr[g * npair + hp, c * sub:(c + 1) * sub, :]

            ms = [stream(m_scr, g) for g in range(ng)]
            m_all = functools.reduce(jnp.maximum, ms)
            num = jnp.zeros_like(m_all)
            den = jnp.zeros_like(m_all)
            for g in range(ng):
                w = jnp.exp2(ms[g] - m_all)
                num = num + w * stream(o_scr, g)
                den = den + w * stream(l_scr, g)
            o_scr[ng * npair + hp, pl.ds(c, sub, stride=SAFE_STRIDE), :] = num / den
        o_ref[0, :, hp * LANES:(hp + 1) * LANES] = o_scr[ng * npair + hp].astype(bf16)


def _dil(qs, ks, vs):
    bsz, s, _ = qs[0].shape
    nt = s // DIL_TILE
    ng = len(DIL_CONFIGS)
    qb = DIL_QBLOCK
    per_tile = DIL_TILE // qb
    blk = (1, DIL_TILE, WB_OUT)
    cur = pl.BlockSpec(blk, lambda b, i: (b, i, 0))
    in_specs, args = [], []
    for g, (_, dil) in enumerate(DIL_CONFIGS):
        seg = DIL_TILE // dil
        if _dil_halo_pieces(dil) == 1 and seg == qb:
            prev = [pl.BlockSpec(blk, lambda b, i: (b, jnp.maximum(i - 1, 0), 0))]
            nxt = [pl.BlockSpec(blk, lambda b, i: (b, jnp.minimum(i + 1, nt - 1), 0))]
        else:
            prev = [pl.BlockSpec((1, qb, WB_OUT), lambda b, i, o=(r + 1) * seg // qb - 1:
                                 (b, jnp.maximum((i - 1) * per_tile + o, 0), 0)) for r in range(dil)]
            nxt = [pl.BlockSpec((1, qb, WB_OUT), lambda b, i, o=r * seg // qb:
                                (b, jnp.minimum((i + 1) * per_tile + o, s // qb - 1), 0)) for r in range(dil)]
        in_specs += [cur] + 2 * ([cur] + prev + nxt)
        args += [qs[g]] + [ks[g]] * (1 + len(prev) + len(nxt)) + [vs[g]] * (1 + len(prev) + len(nxt))
    return pl.pallas_call(
        _dil_kernel,
        grid=(bsz, nt),
        in_specs=in_specs,
        out_specs=pl.BlockSpec(blk, lambda b, i: (b, i, 0)),
        out_shape=jax.ShapeDtypeStruct((bsz, s, WB_OUT), bf16),
        scratch_shapes=[pltpu.VMEM(((ng + 1) * DIL_HPG // 2, DIL_TILE, LANES), f32)]
        + [pltpu.VMEM((ng * DIL_HPG // 2, DIL_TILE, LANES), f32)] * 2,
        compiler_params=_params(("parallel", "parallel")),
        name="dil",
    )(*args)


def _tail_kernel(x_ref, oa_ref, ob_ref, mod_ref, g1_ref, g2_ref, wg_ref, bg_ref, wpa_ref, wpb_ref,
                 wo_ref, win_ref, wout_ref, o_ref, h_scr, h2_scr, m_scr, act_scr):
    d = x_ref.shape[2]
    mod = mod_ref[0]
    sh1, sc1, gt1, sh2, sc2, gt2 = [mod[k:k + 1] for k in range(6)]
    tm = x_ref.shape[1]
    for rc in range(TAIL_ROW_SPLIT):
        rows = slice(rc * tm // TAIL_ROW_SPLIT, (rc + 1) * tm // TAIL_ROW_SPLIT)
        x = x_ref[0, rows, :]
        h_scr[rows, :] = _modulated_norm(x, g1_ref[...], sc1, sh1).astype(bf16)
        oa = oa_ref[0, rows, :]
        ob = ob_ref[0, rows, :]
        for n in range(d // COL_TILE):
            ca = slice(n * COL_TILE, (n + 1) * COL_TILE)
            cb = slice(d + n * COL_TILE, d + (n + 1) * COL_TILE)
            ga = jax.nn.sigmoid(jnp.dot(h_scr[rows, :], wg_ref[:, ca], preferred_element_type=f32) + bg_ref[:, ca])
            gb = jax.nn.sigmoid(jnp.dot(h_scr[rows, :], wg_ref[:, cb], preferred_element_type=f32) + bg_ref[:, cb])
            pa = jnp.dot(oa, wpa_ref[:, ca], preferred_element_type=f32)
            pb = jnp.dot(ob, wpb_ref[:, ca], preferred_element_type=f32)
            m_scr[rows, ca] = (ga * pa + gb * pb).astype(bf16)
        x1 = x + gt1 * jnp.dot(m_scr[rows, :], wo_ref[...], preferred_element_type=f32)
        h2_scr[rows, :] = _modulated_norm(x1, g2_ref[...], sc2, sh2).astype(bf16)
        for f in range(D_FF // FF_CHUNK):
            ca = slice(f * FF_CHUNK, (f + 1) * FF_CHUNK)
            cu = slice(D_FF + f * FF_CHUNK, D_FF + (f + 1) * FF_CHUNK)
            a = jnp.dot(h2_scr[rows, :], win_ref[:, ca], preferred_element_type=f32)
            u = jnp.dot(h2_scr[rows, :], win_ref[:, cu], preferred_element_type=f32)
            act_scr[rows, ca] = ((a * jax.nn.sigmoid(a)) * u).astype(bf16)
        o_ref[0, rows, :] = x1 + gt2 * jnp.dot(act_scr[rows, :], wout_ref[...], preferred_element_type=f32)


def _tail(x, o_a, o_b, mod6, g1, g2, wg, bg, wpa, wpb, wo, w_ffn_in, w_ffn_out, tm=TAIL_TM):
    bsz, s, d = x.shape
    tok = lambda w: pl.BlockSpec((1, tm, w), lambda b, i: (b, i, 0))
    consts = [g1, g2, wg, bg, wpa, wpb, wo, w_ffn_in, w_ffn_out]
    return pl.pallas_call(
        _tail_kernel,
        grid=(bsz, s // tm),
        in_specs=[tok(d), tok(WA), tok(WB_OUT), pl.BlockSpec((1, 6, d), lambda b, i: (b, 0, 0))]
        + [_const_spec(c.shape) for c in consts],
        out_specs=tok(d),
        out_shape=jax.ShapeDtypeStruct((bsz, s, d), f32),
        scratch_shapes=[pltpu.VMEM((tm, d), bf16)] * 3 + [pltpu.VMEM((tm, D_FF), bf16)],
        compiler_params=_params(("parallel", "parallel")),
        name="tail",
    )(x, o_a, o_b, mod6, *consts)


def _rotary_tables(s):
    half = ROT_DIM // 2
    inv_freq = ROPE_THETA ** (-(np.arange(half, dtype=np.float64) * 2.0) / ROT_DIM)
    ang = np.arange(s, dtype=np.float64)[:, None] * inv_freq[None, :]
    cos, sin = np.cos(ang), np.sin(ang)
    rest = HEAD_DIM - ROT_DIM
    cos_h = np.concatenate([cos, cos, np.ones((s, rest))], axis=1)
    ssin_h = np.concatenate([-sin, sin, np.zeros((s, rest))], axis=1)
    rep = LANES // HEAD_DIM
    return (jnp.asarray(np.tile(cos_h, (1, rep)), dtype=f32), jnp.asarray(np.tile(ssin_h, (1, rep)), dtype=f32))


def _layer(x, mod6, g_norm1, g_norm2, w_in, w_qkv, b_gate, g_qa, g_ka, g_qb, g_kb, rpb,
           w_proj_a, w_proj_b, w_o, w_ffn_in, w_ffn_out):
    bsz, s, d = x.shape
    assert d == D_MODEL and w_in.shape == (d, W_QKV + 2 * d) and w_ffn_in.shape == (d, 2 * D_FF)
    assert s % DIL_TILE == 0 and s % (NA_ROWS * GRID_W) == 0 and s % TAIL_TM == 0
    assert s // GRID_W >= NA_ROWS + NA_KH
    scale = HEAD_DIM ** -0.5 * LOG2E
    rep = COL_TILE // HEAD_DIM
    gvec = jnp.stack([jnp.tile(g_qa * scale, rep), jnp.tile(g_ka, rep),
                      jnp.tile(g_qb * scale, rep), jnp.tile(g_kb, rep)]).astype(f32)
    hid = jnp.arange(COL_TILE) // HEAD_DIM
    ones_bd = (hid[:, None] == hid[None, :]).astype(bf16)
    rot_tables = _rotary_tables(s)
    g1 = g_norm1.reshape(1, d)
    g2 = g_norm2.reshape(1, d)

    qkv, tail_ws, w_gate = _qkv(x, mod6, g1, w_qkv, gvec, ones_bd, rot_tables,
                                [w_proj_a, w_proj_b, w_o, w_ffn_in, w_ffn_out], w_in)
    ng = len(DIL_CONFIGS)
    o_a = _na(qkv[_OUT_QA], qkv[_OUT_KA], qkv[_OUT_VA], _na_bias_table(rpb))
    o_b = _dil(qkv[_OUT_QB:_OUT_QB + ng], qkv[_OUT_KB:_OUT_KB + ng], qkv[_OUT_VB:_OUT_VB + ng])

    return _tail(x, o_a, o_b, mod6, g1, g2, w_gate, b_gate.reshape(1, 2 * d), *tail_ws)


def kernel(x, c, w_ada, b_ada, g_norm1, g_norm2, w_in, b_gate, g_qa, g_ka, g_qb, g_kb, rpb,
           w_proj_a, w_proj_b, w_o, w_ffn_in, w_ffn_out):
    depth = w_ada.shape[0]
    bsz, d = c.shape
    for l in range(depth):
        mod, w_qkv = _mod(c, w_ada[l], b_ada[l], w_in[l])
        x = _layer(x, mod.reshape(bsz, 6, d), g_norm1[l], g_norm2[l], w_in[l], w_qkv, b_gate[l], g_qa[l], g_ka[l],
                   g_qb[l], g_kb[l], rpb[l], w_proj_a[l], w_proj_b[l], w_o[l], w_ffn_in[l], w_ffn_out[l])
    return x
```

```python
import functools

import jax
import jax.numpy as jnp
import numpy as np
from jax import lax
from jax.experimental import pallas as pl
from jax.experimental.pallas import tpu as pltpu

f32 = jnp.float32
bf16 = jnp.bfloat16

D_MODEL = 1024
HEAD_DIM = 64
GRID_W = 64
NA_HEADS = 8
NA_KH = 8
NA_KW = 16
DIL_CONFIGS = ((128, 1), (512, 4), (2048, 16))
DIL_HPG = 4
DIL_HEADS = DIL_HPG * len(DIL_CONFIGS)
DIL_QBLOCK = 64
ROT_DIM = HEAD_DIM // 4
ROPE_THETA = 500000.0
D_FF = -(-8 * D_MODEL // (3 * 256)) * 256
EPS = 1e-6
NEG_INF = -1e30
LOG2E = 1.4426950408889634
WA = NA_HEADS * HEAD_DIM
WB = DIL_HEADS * HEAD_DIM
WB_OUT = DIL_HPG * HEAD_DIM
W_QKV = 3 * WA + 3 * WB

LANES = 128
BF16_ROWS = 16
COL_TILE = 256
DIL_TILE = 1024
QKV_ROW_SPLIT = 2
QKV_DOT_TILES = 2
DIL_UNROLL = 16
NA_ROWS = 16
NA_BATCH = 4
MOD_STEPS = 8
TAIL_TM = 1024
TAIL_ROW_SPLIT = 2
FF_CHUNK = 256
VMEM_LIMIT = 56 * 1024 * 1024

SAFE_STRIDE = 4

assert all((win // 2) // dil == DIL_QBLOCK for win, dil in DIL_CONFIGS)
assert all(dil == 1 or (dil % SAFE_STRIDE == 0 and dil // SAFE_STRIDE <= SAFE_STRIDE)
           for _, dil in DIL_CONFIGS)


def _params(sem):
    return pltpu.CompilerParams(dimension_semantics=sem, vmem_limit_bytes=VMEM_LIMIT)


def _const_spec(shape):
    nd = len(shape)
    return pl.BlockSpec(shape, lambda *_: (0,) * nd, pipeline_mode=pl.Buffered(1))


def _mod_kernel(ct_ref, w_ref, b_ref, win_ref, o_ref, win_o_ref):
    ct = ct_ref[...]
    act = ct * jax.nn.sigmoid(ct)
    w = w_ref[...]
    rows = [jnp.sum(act[:, b:b + 1] * w, axis=0, keepdims=True) for b in range(ct.shape[1])]
    o_ref[...] = jnp.concatenate(rows, axis=0) + b_ref[...]
    win_o_ref[...] = win_ref[...].astype(bf16)


def _mod(c, w_ada, b_ada, w_in):
    bsz, d = c.shape
    n = w_ada.shape[1]
    steps = MOD_STEPS
    tn = n // steps
    slab = pl.BlockSpec((w_in.shape[0] // steps, w_in.shape[1]), lambda j: (j, 0))
    return pl.pallas_call(
        _mod_kernel,
        grid=(steps,),
        in_specs=[pl.BlockSpec((d, bsz), lambda j: (0, 0)),
                  pl.BlockSpec((d, tn), lambda j: (0, j)),
                  pl.BlockSpec((1, tn), lambda j: (0, j)),
                  slab],
        out_specs=[pl.BlockSpec((bsz, tn), lambda j: (0, j)), slab],
        out_shape=[jax.ShapeDtypeStruct((bsz, n), f32), jax.ShapeDtypeStruct(w_in.shape, bf16)],
        compiler_params=_params(("arbitrary",)),
        name="mod",
    )(c.T, w_ada, b_ada.reshape(1, n), w_in)


def _modulated_norm(x, g, sc, sh):
    ms = jnp.mean(x * x, axis=-1, keepdims=True)
    y = x * lax.rsqrt(ms + EPS) * g
    return y * (1.0 + sc) + sh


_J_QB = 3 * WA // COL_TILE
_J_KB = _J_QB + WB // COL_TILE
_J_VB = _J_KB + WB // COL_TILE
_N_J = W_QKV // COL_TILE


def _qkv_tile_kind(j):
    if j < WA // COL_TILE:
        return 0, False, 1
    if j < 2 * WA // COL_TILE:
        return 1, False, 1
    if j < _J_QB:
        return None, False, 1
    dil = DIL_CONFIGS[(j - _J_QB) % len(DIL_CONFIGS)][1]
    if j < _J_KB:
        return 2, True, dil
    if j < _J_VB:
        return 3, True, dil
    return None, False, dil


_QKV_OUT_WIDTHS = (WA,) * 3 + (WB_OUT,) * (3 * len(DIL_CONFIGS))
_OUT_QA, _OUT_KA, _OUT_VA, _OUT_QB, _OUT_KB, _OUT_VB = 0, 1, 2, 3, 3 + len(DIL_CONFIGS), 3 + 2 * len(DIL_CONFIGS)


def _qkv_tile_out(j, o_refs):
    if j < _J_QB:
        per = WA // COL_TILE
        return o_refs[j // per], (j % per) * COL_TILE
    return o_refs[_OUT_QB + j - _J_QB], 0


def _qkv_kernel(n_cast, x_ref, mod_ref, g1_ref, w_ref, gvec_ref, ones_ref, cos_ref, ssin_ref, *rest):
    n_out = len(_QKV_OUT_WIDTHS)
    cast_in, o_refs = rest[:n_cast], rest[n_cast:n_cast + n_out]
    cast_out = rest[n_cast + n_out:2 * n_cast + n_out]
    h_scr, y_scr, z_scr = rest[2 * n_cast + n_out:]
    for src, dst in zip(cast_in, cast_out):
        dst[...] = src[...].astype(bf16)
    tm = x_ref.shape[1]
    nc = COL_TILE // LANES
    mod = mod_ref[0]
    h_scr[...] = _modulated_norm(x_ref[0], g1_ref[...], mod[1:2], mod[0:1]).astype(bf16)
    head_dim = lax.broadcasted_iota(jnp.int32, (tm // QKV_ROW_SPLIT, COL_TILE), 1) % HEAD_DIM
    first_half = head_dim < ROT_DIM // 2

    wide = {}
    for j in range(_N_J):
        gain_row, rotary, dil = _qkv_tile_kind(j)
        o_ref, c0 = _qkv_tile_out(j, o_refs)
        slot = j % y_scr.shape[0]
        for hb in range(QKV_ROW_SPLIT):
            rows = slice(hb * tm // QKV_ROW_SPLIT, (hb + 1) * tm // QKV_ROW_SPLIT)
            if j % QKV_DOT_TILES == 0:
                wcols = slice(j * COL_TILE, min(j + QKV_DOT_TILES, _N_J) * COL_TILE)
                wide[hb] = jnp.dot(h_scr[rows, :], w_ref[:, wcols], preferred_element_type=f32)
            sub = j % QKV_DOT_TILES
            y = wide[hb][:, sub * COL_TILE:(sub + 1) * COL_TILE]
            if gain_row is not None:
                ss = jnp.dot((y * y).astype(bf16), ones_ref[...], preferred_element_type=f32)
                y = y * lax.rsqrt(ss * (1.0 / HEAD_DIM) + EPS) * gvec_ref[gain_row:gain_row + 1, :]
            if rotary:
                cos, ssin = [jnp.concatenate([t_ref[rows, :]] * nc, axis=1) for t_ref in (cos_ref, ssin_ref)]
                half = ROT_DIM // 2
                up = pltpu.roll(y, COL_TILE - half, 1)
                dn = pltpu.roll(y, half, 1)
                y = y * cos + jnp.where(first_half, up, dn) * ssin
            if dil == 1:
                o_ref[0, rows, c0:c0 + COL_TILE] = y.astype(bf16)
            else:
                for c in range(nc):
                    y_scr[slot, c, rows, :] = y[:, c * LANES:(c + 1) * LANES]
        if dil == 1:
            continue
        seg = DIL_TILE // dil
        for c in range(nc):
            ocols = slice(c0 + c * LANES, c0 + (c + 1) * LANES)
            if dil <= SAFE_STRIDE:
                for r in range(dil):
                    rows = y_scr[slot, c, pl.ds(r, seg, stride=dil), :]
                    o_ref[0, r * seg:(r + 1) * seg, ocols] = rows.astype(bf16)
                continue
            hi = dil // SAFE_STRIDE
            sub = DIL_TILE // SAFE_STRIDE
            for r_lo in range(SAFE_STRIDE):
                z_scr[c, r_lo * sub:(r_lo + 1) * sub, :] = y_scr[slot, c, pl.ds(r_lo, sub, stride=SAFE_STRIDE), :]
            for r_lo in range(SAFE_STRIDE):
                for r_hi in range(hi):
                    r = SAFE_STRIDE * r_hi + r_lo
                    rows = z_scr[c, pl.ds(r_lo * sub + r_hi, seg, stride=hi), :]
                    o_ref[0, r * seg:(r + 1) * seg, ocols] = rows.astype(bf16)


def _qkv(x, mod6, g1, w_qkv, gvec, ones_bd, rot_tables, cast_ws, tm=DIL_TILE):
    bsz, s, d = x.shape
    assert tm == DIL_TILE
    nt = s // tm
    steps = bsz * nt
    assert all(w.shape[0] % (BF16_ROWS * steps) == 0 for w in cast_ws)
    slab_specs = [pl.BlockSpec((w.shape[0] // steps, w.shape[1]), lambda b, i: (b * nt + i, 0)) for w in cast_ws]
    outs = pl.pallas_call(
        functools.partial(_qkv_kernel, len(cast_ws)),
        grid=(bsz, nt),
        in_specs=[pl.BlockSpec((1, tm, d), lambda b, i: (b, i, 0)),
                  pl.BlockSpec((1, 6, d), lambda b, i: (b, 0, 0)),
                  _const_spec(g1.shape),
                  pl.BlockSpec((d, W_QKV), lambda b, i: (0, 0), pipeline_mode=pl.Buffered(1)),
                  _const_spec(gvec.shape),
                  _const_spec(ones_bd.shape),
                  ] + [pl.BlockSpec((tm, LANES), lambda b, i: (i, 0))] * len(rot_tables) + slab_specs,
        out_specs=[pl.BlockSpec((1, tm, w), lambda b, i: (b, i, 0)) for w in _QKV_OUT_WIDTHS] + slab_specs,
        out_shape=[jax.ShapeDtypeStruct((bsz, s, w), bf16) for w in _QKV_OUT_WIDTHS]
        + [jax.ShapeDtypeStruct(w.shape, bf16) for w in cast_ws],
        scratch_shapes=[pltpu.VMEM((tm, d), bf16),
                        pltpu.VMEM((2, COL_TILE // LANES, tm, LANES), f32),
                        pltpu.VMEM((COL_TILE // LANES, tm, LANES), f32)],
        compiler_params=_params(("parallel", "parallel")),
        name="qkv",
    )(x, mod6, g1, w_qkv, gvec, ones_bd, *rot_tables, *cast_ws)
    n_out = len(_QKV_OUT_WIDTHS)
    return outs[:n_out], outs[n_out:]


def _pair_mask(nq):
    row = lax.broadcasted_iota(jnp.int32, (2 * nq, LANES), 0)
    lane = lax.broadcasted_iota(jnp.int32, (2 * nq, LANES), 1)
    return (row < nq) == (lane < HEAD_DIM)


def _stack_pair(qp, own_head):
    q2 = jnp.concatenate([qp, qp], axis=0)
    return jnp.where(own_head, q2, jnp.zeros_like(q2))


def _na_window_start(rb, n_rows):
    return jnp.clip(rb * NA_ROWS - NA_KH // 2, 0, n_rows - (NA_ROWS + NA_KH))


def _na_kernel(q_ref, kwin, vwin, tbl_ref, o_ref):
    rb = pl.program_id(1)
    n_rows = pl.num_programs(1) * NA_ROWS
    win_start = _na_window_start(rb, n_rows)
    first_head = lax.broadcasted_iota(jnp.int32, (GRID_W, LANES), 1) < HEAD_DIM
    own_head = _pair_mask(GRID_W)
    nkeys = NA_KH * GRID_W
    npair = NA_HEADS // 2
    pair_rows = 2 * GRID_W

    n_off = 2 * NA_KH - 2
    for a0 in range(0, NA_ROWS, NA_BATCH):
        offs, s_parts = [], []
        for a in range(a0, a0 + NA_BATCH):
            r = rb * NA_ROWS + a
            row_start = jnp.clip(r - NA_KH // 2, 0, n_rows - NA_KH)
            delta = r - row_start
            off = pl.multiple_of((row_start - win_start) * GRID_W, GRID_W)
            offs.append(off)
            row_parts = []
            for hp in range(npair):
                cols = slice(hp * LANES, (hp + 1) * LANES)
                q2 = _stack_pair(q_ref[0, a * GRID_W:(a + 1) * GRID_W, cols], own_head)
                kk = kwin[0, pl.ds(off, nkeys), cols]
                row_parts.append(lax.dot_general(q2, kk, (((1,), (1,)), ((), ())), preferred_element_type=f32))
            bias = jnp.concatenate(
                [jnp.concatenate([tbl_ref[h * n_off + 2 * kp - delta + NA_KH - 1] for kp in range(NA_KH // 2)],
                                 axis=1) for h in range(NA_HEADS)], axis=0)
            s_parts.append(jnp.concatenate(row_parts, axis=0) + bias)
        s = jnp.concatenate(s_parts, axis=0)
        m = jnp.max(s, axis=-1, keepdims=True)
        p = jnp.exp2(s - m)
        inv = 1.0 / jnp.sum(p, axis=-1, keepdims=True)
        pb = p.astype(bf16)
        for k, a in enumerate(range(a0, a0 + NA_BATCH)):
            for hp in range(npair):
                cols = slice(hp * LANES, (hp + 1) * LANES)
                r0 = (k * npair + hp) * pair_rows
                vv = vwin[0, pl.ds(offs[k], nkeys), cols]
                pv = jnp.dot(pb[r0:r0 + pair_rows], vv, preferred_element_type=f32) * inv[r0:r0 + pair_rows]
                o_ref[0, a * GRID_W:(a + 1) * GRID_W, cols] = (
                    jnp.where(first_head, pv[:GRID_W], pv[GRID_W:]).astype(bf16))


def _na(q, k, v, tbl):
    bsz, s, _ = q.shape
    blk = NA_ROWS * GRID_W
    nb = s // blk
    qspec = pl.BlockSpec((1, blk, WA), lambda b, i: (b, i, 0))
    win_tokens = (NA_ROWS + NA_KH) * GRID_W
    n_rows = s // GRID_W
    window = pl.BlockSpec(
        (pl.Element(1), pl.Element(win_tokens), pl.Element(WA)),
        lambda b, i: (b, pl.multiple_of(_na_window_start(i, n_rows) * GRID_W, GRID_W), 0))

    return pl.pallas_call(
        _na_kernel,
        grid=(bsz, nb),
        in_specs=[qspec, window, window, _const_spec(tbl.shape)],
        out_specs=pl.BlockSpec((1, blk, WA), lambda b, i: (b, i, 0)),
        out_shape=jax.ShapeDtypeStruct((bsz, s, WA), bf16),
        compiler_params=_params(("parallel", "parallel")),
        name="na",
    )(q, k, v, tbl)


def _na_bias_table(rpb):
    col = np.arange(GRID_W)
    col_start = np.clip(col - NA_KW // 2, 0, GRID_W - NA_KW)
    cmask = (col[None, :] >= col_start[:, None]) & (col[None, :] < col_start[:, None] + NA_KW)
    col_off = np.clip(col[None, :] - col[:, None] + (NA_KW - 1), 0, 2 * NA_KW - 2)
    n_c = 2 * NA_KW - 1
    sel = np.zeros((2, n_c, GRID_W, 2, GRID_W), np.float32)
    qq, kk = np.meshgrid(col, col, indexing="ij")
    for j in range(2):
        sel[j, col_off, qq, j, kk] = 1.0
    pen = np.broadcast_to(np.where(cmask, 0.0, NEG_INF)[:, None, :], (GRID_W, 2, GRID_W))
    sel = np.concatenate([sel.reshape(2 * n_c, -1), pen.reshape(1, -1).astype(np.float32)], axis=0)
    rows = jnp.stack([rpb[:, :-1], rpb[:, 1:]], axis=2).astype(f32) * LOG2E
    rows = rows.reshape(NA_HEADS * (2 * NA_KH - 2), 2 * n_c)
    rows = jnp.concatenate([rows, jnp.ones((rows.shape[0], 1), f32)], axis=1)
    t = jnp.dot(rows, jnp.asarray(sel), precision=lax.Precision.HIGHEST)
    return t.reshape(NA_HEADS * (2 * NA_KH - 2), GRID_W, 2 * GRID_W)


def _dil_halo_pieces(dil):
    return 1 if DIL_TILE // dil == DIL_QBLOCK else dil


def _dil_kernel(*refs):
    ng = len(DIL_CONFIGS)
    n_in = sum(1 + 2 * (1 + 2 * _dil_halo_pieces(dil)) for _, dil in DIL_CONFIGS)
    in_refs = list(refs[:n_in])
    o_ref = refs[n_in]
    o_scr, m_scr, l_scr = refs[n_in + 1:]
    ti = pl.program_id(1)
    n_tiles = pl.num_programs(1)
    qb = DIL_QBLOCK
    span = 3 * qb
    sub = DIL_TILE // SAFE_STRIDE

    first_head = lax.broadcasted_iota(jnp.int32, (qb, LANES), 1) < HEAD_DIM
    own_head = _pair_mask(qb)
    npair = DIL_HPG // 2
    qi = lax.broadcasted_iota(jnp.int32, (qb, span), 0)
    kj = lax.broadcasted_iota(jnp.int32, (qb, span), 1)
    band = jnp.where((kj >= qi) & (kj <= qi + 2 * qb), 0.0, NEG_INF).astype(f32)
    kcol = lax.broadcasted_iota(jnp.int32, (1, span), 1)
    pen_lo = jnp.where(kcol < qb, jnp.where(ti == 0, NEG_INF, 0.0), 0.0).astype(f32)
    pen_hi = jnp.where(kcol >= 2 * qb, jnp.where(ti == n_tiles - 1, NEG_INF, 0.0), 0.0).astype(f32)

    for g, (_, dil) in enumerate(DIL_CONFIGS):
        seg = DIL_TILE // dil
        nj = seg // qb
        nh = _dil_halo_pieces(dil)
        q_ref = in_refs.pop(0)
        kc, kp, kn = in_refs.pop(0), [in_refs.pop(0) for _ in range(nh)], [in_refs.pop(0) for _ in range(nh)]
        vc, vp, vn = in_refs.pop(0), [in_refs.pop(0) for _ in range(nh)], [in_refs.pop(0) for _ in range(nh)]

        def halo(pieces, r, cols, nh=nh):
            return pieces[r][0, :, cols] if nh > 1 else pieces[0][0, r * qb:(r + 1) * qb, cols]

        def window(refs3, r, j, cols, seg=seg, nj=nj, halo=halo):
            p_refs, c_ref, n_refs = refs3
            base = r * seg
            if 0 < j < nj - 1:
                return c_ref[0, base + (j - 1) * qb:base + (j + 2) * qb, cols]
            lo = halo(p_refs, r, cols) if j == 0 else c_ref[0, base + (j - 1) * qb:base + j * qb, cols]
            mid = c_ref[0, base + j * qb:base + (j + 1) * qb, cols]
            hi = halo(n_refs, r, cols) if j == nj - 1 else c_ref[0, base + (j + 1) * qb:base + (j + 2) * qb, cols]
            return jnp.concatenate([lo, mid, hi], axis=0)

        for t0 in range(0, dil * nj, DIL_UNROLL):
            blocks = [divmod(t, nj) for t in range(t0, t0 + DIL_UNROLL)]
            s_parts = []
            for k, (r, j) in enumerate(blocks):
                mask = band
                if j == 0:
                    mask = mask + pen_lo
                if j == nj - 1:
                    mask = mask + pen_hi
                for hp in range(npair):
                    cols = slice(hp * LANES, (hp + 1) * LANES)
                    q2 = _stack_pair(q_ref[0, (t0 + k) * qb:(t0 + k + 1) * qb, cols], own_head)
                    kk = window((kp, kc, kn), r, j, cols)
                    s2 = lax.dot_general(q2, kk, (((1,), (1,)), ((), ())), preferred_element_type=f32)
                    s_parts += [s2[:qb] + mask, s2[qb:] + mask]
            s = jnp.concatenate(s_parts, axis=0)
            m = jnp.max(s, axis=-1, keepdims=True)
            p = jnp.exp2(s - m)
            l = jnp.sum(p, axis=-1, keepdims=True)
            pb = p.astype(bf16)
            for k, (r, j) in enumerate(blocks):
                if dil == 1:
                    store_rows = pl.ds(j * qb, qb)
                else:
                    hi = dil // SAFE_STRIDE
                    start = (r % SAFE_STRIDE) * sub + j * qb * hi + r // SAFE_STRIDE
                    store_rows = pl.ds(start, qb, stride=hi) if hi > 1 else pl.ds(start, qb)
                for hp in range(npair):
                    cols = slice(hp * LANES, (hp + 1) * LANES)
                    r0 = (k * npair + hp) * 2 * qb
                    vv = window((vp, vc, vn), r, j, cols)
                    pv = jnp.dot(pb[r0:r0 + 2 * qb], vv, preferred_element_type=f32)
                    slot = g * npair + hp
                    o_scr[slot, store_rows, :] = jnp.where(first_head, pv[:qb], pv[qb:])
                    m_scr[slot, store_rows, :] = jnp.where(first_head, m[r0:r0 + qb], m[r0 + qb:r0 + 2 * qb])
                    l_scr[slot, store_rows, :] = jnp.where(first_head, l[r0:r0 + qb], l[r0 + qb:r0 + 2 * qb])

    for hp in range(npair):
        for c in range(SAFE_STRIDE):
            def stream(scr, g):
                if DIL_CONFIGS[g][1] == 1:
                    return scr[g * npair + hp, pl.ds(c, sub, stride=SAFE_STRIDE), :]
                return scr[g * npair + hp, c * sub:(c + 1) * sub, :]

            ms = [stream(m_scr, g) for g in range(ng)]
            m_all = functools.reduce(jnp.maximum, ms)
            num = jnp.zeros_like(m_all)
            den = jnp.zeros_like(m_all)
            for g in range(ng):
                w = jnp.exp2(ms[g] - m_all)
                num = num + w * stream(o_scr, g)
                den = den + w * stream(l_scr, g)
            o_scr[ng * npair + hp, pl.ds(c, sub, stride=SAFE_STRIDE), :] = num / den
        o_ref[0, :, hp * LANES:(hp + 1) * LANES] = o_scr[ng * npair + hp].astype(bf16)


def _dil(qs, ks, vs):
    bsz, s, _ = qs[0].shape
    nt = s // DIL_TILE
    ng = len(DIL_CONFIGS)
    qb = DIL_QBLOCK
    per_tile = DIL_TILE // qb
    blk = (1, DIL_TILE, WB_OUT)
    cur = pl.BlockSpec(blk, lambda b, i: (b, i, 0))
    in_specs, args = [], []
    for g, (_, dil) in enumerate(DIL_CONFIGS):
        seg = DIL_TILE // dil
        if _dil_halo_pieces(dil) == 1 and seg == qb:
            prev = [pl.BlockSpec(blk, lambda b, i: (b, jnp.maximum(i - 1, 0), 0))]
            nxt = [pl.BlockSpec(blk, lambda b, i: (b, jnp.minimum(i + 1, nt - 1), 0))]
        else:
            prev = [pl.BlockSpec((1, qb, WB_OUT), lambda b, i, o=(r + 1) * seg // qb - 1:
                                 (b, jnp.maximum((i - 1) * per_tile + o, 0), 0)) for r in range(dil)]
            nxt = [pl.BlockSpec((1, qb, WB_OUT), lambda b, i, o=r * seg // qb:
                                (b, jnp.minimum((i + 1) * per_tile + o, s // qb - 1), 0)) for r in range(dil)]
        in_specs += [cur] + 2 * ([cur] + prev + nxt)
        args += [qs[g]] + [ks[g]] * (1 + len(prev) + len(nxt)) + [vs[g]] * (1 + len(prev) + len(nxt))
    return pl.pallas_call(
        _dil_kernel,
        grid=(bsz, nt),
        in_specs=in_specs,
        out_specs=pl.BlockSpec(blk, lambda b, i: (b, i, 0)),
        out_shape=jax.ShapeDtypeStruct((bsz, s, WB_OUT), bf16),
        scratch_shapes=[pltpu.VMEM(((ng + 1) * DIL_HPG // 2, DIL_TILE, LANES), f32)]
        + [pltpu.VMEM((ng * DIL_HPG // 2, DIL_TILE, LANES), f32)] * 2,
        compiler_params=_params(("parallel", "parallel")),
        name="dil",
    )(*args)


def _tail_kernel(x_ref, oa_ref, ob_ref, mod_ref, g1_ref, g2_ref, bg_ref, wpa_ref, wpb_ref,
                 wo_ref, win_ref, wout_ref, *rest):
    d = x_ref.shape[2]
    n_gate = 2 * d // COL_TILE
    wg_refs = rest[:n_gate]
    o_ref, h_scr, h2_scr, m_scr, act_scr = rest[n_gate:]
    mod = mod_ref[0]
    sh1, sc1, gt1, sh2, sc2, gt2 = [mod[k:k + 1] for k in range(6)]
    tm = x_ref.shape[1]
    for rc in range(TAIL_ROW_SPLIT):
        rows = slice(rc * tm // TAIL_ROW_SPLIT, (rc + 1) * tm // TAIL_ROW_SPLIT)
        x = x_ref[0, rows, :]
        h_scr[rows, :] = _modulated_norm(x, g1_ref[...], sc1, sh1).astype(bf16)
        oa = oa_ref[0, rows, :]
        ob = ob_ref[0, rows, :]
        for n in range(d // COL_TILE):
            ca = slice(n * COL_TILE, (n + 1) * COL_TILE)
            cb = slice(d + n * COL_TILE, d + (n + 1) * COL_TILE)
            wga, wgb = wg_refs[n], wg_refs[n_gate // 2 + n]
            ga = jax.nn.sigmoid(jnp.dot(h_scr[rows, :], wga[...], preferred_element_type=f32) + bg_ref[:, ca])
            gb = jax.nn.sigmoid(jnp.dot(h_scr[rows, :], wgb[...], preferred_element_type=f32) + bg_ref[:, cb])
            pa = jnp.dot(oa, wpa_ref[:, ca], preferred_element_type=f32)
            pb = jnp.dot(ob, wpb_ref[:, ca], preferred_element_type=f32)
            m_scr[rows, ca] = (ga * pa + gb * pb).astype(bf16)
        x1 = x + gt1 * jnp.dot(m_scr[rows, :], wo_ref[...], preferred_element_type=f32)
        h2_scr[rows, :] = _modulated_norm(x1, g2_ref[...], sc2, sh2).astype(bf16)
        for f in range(D_FF // FF_CHUNK):
            ca = slice(f * FF_CHUNK, (f + 1) * FF_CHUNK)
            cu = slice(D_FF + f * FF_CHUNK, D_FF + (f + 1) * FF_CHUNK)
            a = jnp.dot(h2_scr[rows, :], win_ref[:, ca], preferred_element_type=f32)
            u = jnp.dot(h2_scr[rows, :], win_ref[:, cu], preferred_element_type=f32)
            act_scr[rows, ca] = ((a * jax.nn.sigmoid(a)) * u).astype(bf16)
        o_ref[0, rows, :] = x1 + gt2 * jnp.dot(act_scr[rows, :], wout_ref[...], preferred_element_type=f32)


def _tail(x, o_a, o_b, mod6, g1, g2, w_in, bg, wpa, wpb, wo, w_ffn_in, w_ffn_out, tm=TAIL_TM):
    bsz, s, d = x.shape
    tok = lambda w: pl.BlockSpec((1, tm, w), lambda b, i: (b, i, 0))
    consts = [g1, g2, bg, wpa, wpb, wo, w_ffn_in, w_ffn_out]
    n_gate = 2 * d // COL_TILE
    gate_specs = [pl.BlockSpec((d, COL_TILE), lambda b, i, c=_N_J + n: (0, c), pipeline_mode=pl.Buffered(1))
                  for n in range(n_gate)]
    return pl.pallas_call(
        _tail_kernel,
        grid=(bsz, s // tm),
        in_specs=[tok(d), tok(WA), tok(WB_OUT), pl.BlockSpec((1, 6, d), lambda b, i: (b, 0, 0))]
        + [_const_spec(c.shape) for c in consts] + gate_specs,
        out_specs=tok(d),
        out_shape=jax.ShapeDtypeStruct((bsz, s, d), f32),
        scratch_shapes=[pltpu.VMEM((tm, d), bf16)] * 3 + [pltpu.VMEM((tm, D_FF), bf16)],
        compiler_params=_params(("parallel", "parallel")),
        name="tail",
    )(x, o_a, o_b, mod6, *consts, *([w_in] * n_gate))


def _rotary_tables(s):
    half = ROT_DIM // 2
    inv_freq = ROPE_THETA ** (-(np.arange(half, dtype=np.float64) * 2.0) / ROT_DIM)
    ang = np.arange(s, dtype=np.float64)[:, None] * inv_freq[None, :]
    cos, sin = np.cos(ang), np.sin(ang)
    rest = HEAD_DIM - ROT_DIM
    cos_h = np.concatenate([cos, cos, np.ones((s, rest))], axis=1)
    ssin_h = np.concatenate([-sin, sin, np.zeros((s, rest))], axis=1)
    rep = LANES // HEAD_DIM
    return (jnp.asarray(np.tile(cos_h, (1, rep)), dtype=f32), jnp.asarray(np.tile(ssin_h, (1, rep)), dtype=f32))


def _layer(x, mod6, g_norm1, g_norm2, w_in, b_gate, g_qa, g_ka, g_qb, g_kb, rpb,
           w_proj_a, w_proj_b, w_o, w_ffn_in, w_ffn_out):
    bsz, s, d = x.shape
    assert d == D_MODEL and w_in.shape == (d, W_QKV + 2 * d) and w_ffn_in.shape == (d, 2 * D_FF)
    assert s % DIL_TILE == 0 and s % (NA_ROWS * GRID_W) == 0 and s % TAIL_TM == 0
    assert s // GRID_W >= NA_ROWS + NA_KH
    scale = HEAD_DIM ** -0.5 * LOG2E
    rep = COL_TILE // HEAD_DIM
    gvec = jnp.stack([jnp.tile(g_qa * scale, rep), jnp.tile(g_ka, rep),
                      jnp.tile(g_qb * scale, rep), jnp.tile(g_kb, rep)]).astype(f32)
    hid = jnp.arange(COL_TILE) // HEAD_DIM
    ones_bd = (hid[:, None] == hid[None, :]).astype(bf16)
    rot_tables = _rotary_tables(s)
    g1 = g_norm1.reshape(1, d)
    g2 = g_norm2.reshape(1, d)

    qkv, tail_ws = _qkv(x, mod6, g1, w_in, gvec, ones_bd, rot_tables,
                        [w_proj_a, w_proj_b, w_o, w_ffn_in, w_ffn_out])
    ng = len(DIL_CONFIGS)
    o_a = _na(qkv[_OUT_QA], qkv[_OUT_KA], qkv[_OUT_VA], _na_bias_table(rpb))
    o_b = _dil(qkv[_OUT_QB:_OUT_QB + ng], qkv[_OUT_KB:_OUT_KB + ng], qkv[_OUT_VB:_OUT_VB + ng])

    return _tail(x, o_a, o_b, mod6, g1, g2, w_in, b_gate.reshape(1, 2 * d), *tail_ws)


def kernel(x, c, w_ada, b_ada, g_norm1, g_norm2, w_in, b_gate, g_qa, g_ka, g_qb, g_kb, rpb,
           w_proj_a, w_proj_b, w_o, w_ffn_in, w_ffn_out):
    depth = w_ada.shape[0]
    bsz, d = c.shape
    for l in range(depth):
        mod, w_in_bf = _mod(c, w_ada[l], b_ada[l], w_in[l])
        x = _layer(x, mod.reshape(bsz, 6, d), g_norm1[l], g_norm2[l], w_in_bf, b_gate[l], g_qa[l], g_ka[l], g_qb[l],
                   g_kb[l], rpb[l], w_proj_a[l], w_proj_b[l], w_o[l], w_ffn_in[l], w_ffn_out[l])
    return x
```

```python
import functools

import jax
import jax.numpy as jnp
import numpy as np
from jax import lax
from jax.experimental import pallas as pl
from jax.experimental.pallas import tpu as pltpu

f32 = jnp.float32
bf16 = jnp.bfloat16

D_MODEL = 1024
HEAD_DIM = 64
GRID_W = 64
NA_HEADS = 8
NA_KH = 8
NA_KW = 16
DIL_CONFIGS = ((128, 1), (512, 4), (2048, 16))
DIL_HPG = 4
DIL_HEADS = DIL_HPG * len(DIL_CONFIGS)
DIL_QBLOCK = 64
ROT_DIM = HEAD_DIM // 4
ROPE_THETA = 500000.0
D_FF = -(-8 * D_MODEL // (3 * 256)) * 256
EPS = 1e-6
NEG_INF = -1e30
LOG2E = 1.4426950408889634
WA = NA_HEADS * HEAD_DIM
WB = DIL_HEADS * HEAD_DIM
WB_OUT = DIL_HPG * HEAD_DIM
W_QKV = 3 * WA + 3 * WB

LANES = 128
BF16_ROWS = 16
COL_TILE = 256
DIL_TILE = 1024
QKV_ROW_SPLIT = 2
QKV_DOT_TILES = 2
DIL_UNROLL = 16
NA_ROWS = 16
NA_BATCH = 4
MOD_STEPS = 8
TAIL_TM = 1024
TAIL_ROW_SPLIT = 2
FF_CHUNK = 256
VMEM_LIMIT = 56 * 1024 * 1024

SAFE_STRIDE = 4

assert all((win // 2) // dil == DIL_QBLOCK for win, dil in DIL_CONFIGS)
assert all(dil == 1 or (dil % SAFE_STRIDE == 0 and dil // SAFE_STRIDE <= SAFE_STRIDE)
           for _, dil in DIL_CONFIGS)


def _params(sem):
    return pltpu.CompilerParams(dimension_semantics=sem, vmem_limit_bytes=VMEM_LIMIT)


def _const_spec(shape):
    nd = len(shape)
    return pl.BlockSpec(shape, lambda *_: (0,) * nd, pipeline_mode=pl.Buffered(1))


def _mod_kernel(ct_ref, w_ref, b_ref, win_ref, o_ref, win_o_ref):
    ct = ct_ref[...]
    act = ct * jax.nn.sigmoid(ct)
    w = w_ref[...]
    rows = [jnp.sum(act[:, b:b + 1] * w, axis=0, keepdims=True) for b in range(ct.shape[1])]
    o_ref[...] = jnp.concatenate(rows, axis=0) + b_ref[...]
    win_o_ref[...] = win_ref[...].astype(bf16)


def _mod(c, w_ada, b_ada, w_in):
    bsz, d = c.shape
    n = w_ada.shape[1]
    steps = MOD_STEPS
    tn = n // steps
    slab = pl.BlockSpec((w_in.shape[0] // steps, w_in.shape[1]), lambda j: (j, 0))
    return pl.pallas_call(
        _mod_kernel,
        grid=(steps,),
        in_specs=[pl.BlockSpec((d, bsz), lambda j: (0, 0)),
                  pl.BlockSpec((d, tn), lambda j: (0, j)),
                  pl.BlockSpec((1, tn), lambda j: (0, j)),
                  slab],
        out_specs=[pl.BlockSpec((bsz, tn), lambda j: (0, j)), slab],
        out_shape=[jax.ShapeDtypeStruct((bsz, n), f32), jax.ShapeDtypeStruct(w_in.shape, bf16)],
        compiler_params=_params(("arbitrary",)),
        name="mod",
    )(c.T, w_ada, b_ada.reshape(1, n), w_in)


def _modulated_norm(x, g, sc, sh):
    ms = jnp.mean(x * x, axis=-1, keepdims=True)
    y = x * lax.rsqrt(ms + EPS) * g
    return y * (1.0 + sc) + sh


_J_QB = 3 * WA // COL_TILE
_J_KB = _J_QB + WB // COL_TILE
_J_VB = _J_KB + WB // COL_TILE
_N_J = W_QKV // COL_TILE


def _qkv_tile_kind(j):
    if j < WA // COL_TILE:
        return 0, False, 1
    if j < 2 * WA // COL_TILE:
        return 1, False, 1
    if j < _J_QB:
        return None, False, 1
    dil = DIL_CONFIGS[(j - _J_QB) % len(DIL_CONFIGS)][1]
    if j < _J_KB:
        return 2, True, dil
    if j < _J_VB:
        return 3, True, dil
    return None, False, dil


_QKV_OUT_WIDTHS = (WA,) * 3 + (WB_OUT,) * (3 * len(DIL_CONFIGS))
_OUT_QA, _OUT_KA, _OUT_VA, _OUT_QB, _OUT_KB, _OUT_VB = 0, 1, 2, 3, 3 + len(DIL_CONFIGS), 3 + 2 * len(DIL_CONFIGS)


def _qkv_tile_out(j, o_refs):
    if j < _J_QB:
        per = WA // COL_TILE
        return o_refs[j // per], (j % per) * COL_TILE
    return o_refs[_OUT_QB + j - _J_QB], 0


def _qkv_kernel(n_cast, x_ref, mod_ref, g1_ref, w_ref, gvec_ref, ones_ref, cos_ref, ssin_ref, *rest):
    n_out = len(_QKV_OUT_WIDTHS)
    cast_in, o_refs = rest[:n_cast], rest[n_cast:n_cast + n_out]
    cast_out = rest[n_cast + n_out:2 * n_cast + n_out]
    h_scr, y_scr, z_scr = rest[2 * n_cast + n_out:]
    for src, dst in zip(cast_in, cast_out):
        dst[...] = src[...].astype(bf16)
    tm = x_ref.shape[1]
    nc = COL_TILE // LANES
    mod = mod_ref[0]
    h_scr[...] = _modulated_norm(x_ref[0], g1_ref[...], mod[1:2], mod[0:1]).astype(bf16)
    head_dim = lax.broadcasted_iota(jnp.int32, (tm // QKV_ROW_SPLIT, COL_TILE), 1) % HEAD_DIM
    first_half = head_dim < ROT_DIM // 2

    wide = {}
    for j in range(_N_J):
        gain_row, rotary, dil = _qkv_tile_kind(j)
        o_ref, c0 = _qkv_tile_out(j, o_refs)
        slot = j % y_scr.shape[0]
        for hb in range(QKV_ROW_SPLIT):
            rows = slice(hb * tm // QKV_ROW_SPLIT, (hb + 1) * tm // QKV_ROW_SPLIT)
            if j % QKV_DOT_TILES == 0:
                wcols = slice(j * COL_TILE, min(j + QKV_DOT_TILES, _N_J) * COL_TILE)
                wide[hb] = jnp.dot(h_scr[rows, :], w_ref[:, wcols], preferred_element_type=f32)
            sub = j % QKV_DOT_TILES
            y = wide[hb][:, sub * COL_TILE:(sub + 1) * COL_TILE]
            if gain_row is not None:
                ms = jnp.dot((y * y).astype(bf16), ones_ref[...], preferred_element_type=f32)
                y = y * lax.rsqrt(ms + EPS) * gvec_ref[gain_row:gain_row + 1, :]
            if rotary:
                cos, ssin = [jnp.concatenate([t_ref[rows, :]] * nc, axis=1) for t_ref in (cos_ref, ssin_ref)]
                half = ROT_DIM // 2
                up = pltpu.roll(y, COL_TILE - half, 1)
                dn = pltpu.roll(y, half, 1)
                y = y * cos + jnp.where(first_half, up, dn) * ssin
            if dil == 1:
                o_ref[0, rows, c0:c0 + COL_TILE] = y.astype(bf16)
            else:
                for c in range(nc):
                    y_scr[slot, c, rows, :] = y[:, c * LANES:(c + 1) * LANES]
        if dil == 1:
            continue
        seg = DIL_TILE // dil
        for c in range(nc):
            ocols = slice(c0 + c * LANES, c0 + (c + 1) * LANES)
            if dil <= SAFE_STRIDE:
                for r in range(dil):
                    rows = y_scr[slot, c, pl.ds(r, seg, stride=dil), :]
                    o_ref[0, r * seg:(r + 1) * seg, ocols] = rows.astype(bf16)
                continue
            hi = dil // SAFE_STRIDE
            sub = DIL_TILE // SAFE_STRIDE
            for r_lo in range(SAFE_STRIDE):
                z_scr[c, r_lo * sub:(r_lo + 1) * sub, :] = y_scr[slot, c, pl.ds(r_lo, sub, stride=SAFE_STRIDE), :]
            for r_lo in range(SAFE_STRIDE):
                for r_hi in range(hi):
                    r = SAFE_STRIDE * r_hi + r_lo
                    rows = z_scr[c, pl.ds(r_lo * sub + r_hi, seg, stride=hi), :]
                    o_ref[0, r * seg:(r + 1) * seg, ocols] = rows.astype(bf16)


def _qkv(x, mod6, g1, w_qkv, gvec, ones_bd, rot_tables, cast_ws, tm=DIL_TILE):
    bsz, s, d = x.shape
    assert tm == DIL_TILE
    nt = s // tm
    steps = bsz * nt
    assert all(w.shape[0] % (BF16_ROWS * steps) == 0 for w in cast_ws)
    slab_specs = [pl.BlockSpec((w.shape[0] // steps, w.shape[1]), lambda b, i: (b * nt + i, 0)) for w in cast_ws]
    outs = pl.pallas_call(
        functools.partial(_qkv_kernel, len(cast_ws)),
        grid=(bsz, nt),
        in_specs=[pl.BlockSpec((1, tm, d), lambda b, i: (b, i, 0)),
                  pl.BlockSpec((1, 6, d), lambda b, i: (b, 0, 0)),
                  _const_spec(g1.shape),
                  pl.BlockSpec((d, W_QKV), lambda b, i: (0, 0), pipeline_mode=pl.Buffered(1)),
                  _const_spec(gvec.shape),
                  _const_spec(ones_bd.shape),
                  ] + [pl.BlockSpec((tm, LANES), lambda b, i: (i, 0))] * len(rot_tables) + slab_specs,
        out_specs=[pl.BlockSpec((1, tm, w), lambda b, i: (b, i, 0)) for w in _QKV_OUT_WIDTHS] + slab_specs,
        out_shape=[jax.ShapeDtypeStruct((bsz, s, w), bf16) for w in _QKV_OUT_WIDTHS]
        + [jax.ShapeDtypeStruct(w.shape, bf16) for w in cast_ws],
        scratch_shapes=[pltpu.VMEM((tm, d), bf16),
                        pltpu.VMEM((2, COL_TILE // LANES, tm, LANES), f32),
                        pltpu.VMEM((COL_TILE // LANES, tm, LANES), f32)],
        compiler_params=_params(("parallel", "parallel")),
        name="qkv",
    )(x, mod6, g1, w_qkv, gvec, ones_bd, *rot_tables, *cast_ws)
    n_out = len(_QKV_OUT_WIDTHS)
    return outs[:n_out], outs[n_out:]


def _pair_mask(nq):
    row = lax.broadcasted_iota(jnp.int32, (2 * nq, LANES), 0)
    lane = lax.broadcasted_iota(jnp.int32, (2 * nq, LANES), 1)
    return (row < nq) == (lane < HEAD_DIM)


def _stack_pair(qp, own_head):
    q2 = jnp.concatenate([qp, qp], axis=0)
    return jnp.where(own_head, q2, jnp.zeros_like(q2))


def _na_window_start(rb, n_rows):
    return jnp.clip(rb * NA_ROWS - NA_KH // 2, 0, n_rows - (NA_ROWS + NA_KH))


def _na_kernel(q_ref, kwin, vwin, tbl_ref, o_ref):
    rb = pl.program_id(1)
    n_rows = pl.num_programs(1) * NA_ROWS
    win_start = _na_window_start(rb, n_rows)
    first_head = lax.broadcasted_iota(jnp.int32, (GRID_W, LANES), 1) < HEAD_DIM
    own_head = _pair_mask(GRID_W)
    nkeys = NA_KH * GRID_W
    npair = NA_HEADS // 2
    pair_rows = 2 * GRID_W

    n_off = 2 * NA_KH - 2
    for a0 in range(0, NA_ROWS, NA_BATCH):
        offs, s_parts = [], []
        for a in range(a0, a0 + NA_BATCH):
            r = rb * NA_ROWS + a
            row_start = jnp.clip(r - NA_KH // 2, 0, n_rows - NA_KH)
            delta = r - row_start
            off = pl.multiple_of((row_start - win_start) * GRID_W, GRID_W)
            offs.append(off)
            row_parts = []
            for hp in range(npair):
                cols = slice(hp * LANES, (hp + 1) * LANES)
                q2 = _stack_pair(q_ref[0, a * GRID_W:(a + 1) * GRID_W, cols], own_head)
                kk = kwin[0, pl.ds(off, nkeys), cols]
                row_parts.append(lax.dot_general(q2, kk, (((1,), (1,)), ((), ())), preferred_element_type=f32))
            bias = jnp.concatenate(
                [jnp.concatenate([tbl_ref[h * n_off + 2 * kp - delta + NA_KH - 1] for kp in range(NA_KH // 2)],
                                 axis=1) for h in range(NA_HEADS)], axis=0)
            s_parts.append(jnp.concatenate(row_parts, axis=0) + bias)
        s = jnp.concatenate(s_parts, axis=0)
        m = jnp.max(s, axis=-1, keepdims=True)
        p = jnp.exp2(s - m)
        inv = 1.0 / jnp.sum(p, axis=-1, keepdims=True)
        pb = p.astype(bf16)
        for k, a in enumerate(range(a0, a0 + NA_BATCH)):
            for hp in range(npair):
                cols = slice(hp * LANES, (hp + 1) * LANES)
                r0 = (k * npair + hp) * pair_rows
                vv = vwin[0, pl.ds(offs[k], nkeys), cols]
                pv = jnp.dot(pb[r0:r0 + pair_rows], vv, preferred_element_type=f32) * inv[r0:r0 + pair_rows]
                o_ref[0, a * GRID_W:(a + 1) * GRID_W, cols] = (
                    jnp.where(first_head, pv[:GRID_W], pv[GRID_W:]).astype(bf16))


def _na(q, k, v, tbl):
    bsz, s, _ = q.shape
    blk = NA_ROWS * GRID_W
    nb = s // blk
    qspec = pl.BlockSpec((1, blk, WA), lambda b, i: (b, i, 0))
    win_tokens = (NA_ROWS + NA_KH) * GRID_W
    n_rows = s // GRID_W
    window = pl.BlockSpec(
        (pl.Element(1), pl.Element(win_tokens), pl.Element(WA)),
        lambda b, i: (b, pl.multiple_of(_na_window_start(i, n_rows) * GRID_W, GRID_W), 0))

    return pl.pallas_call(
        _na_kernel,
        grid=(bsz, nb),
        in_specs=[qspec, window, window, _const_spec(tbl.shape)],
        out_specs=pl.BlockSpec((1, blk, WA), lambda b, i: (b, i, 0)),
        out_shape=jax.ShapeDtypeStruct((bsz, s, WA), bf16),
        compiler_params=_params(("parallel", "parallel")),
        name="na",
    )(q, k, v, tbl)


def _na_bias_table(rpb):
    col = np.arange(GRID_W)
    col_start = np.clip(col - NA_KW // 2, 0, GRID_W - NA_KW)
    cmask = (col[None, :] >= col_start[:, None]) & (col[None, :] < col_start[:, None] + NA_KW)
    col_off = np.clip(col[None, :] - col[:, None] + (NA_KW - 1), 0, 2 * NA_KW - 2)
    n_c = 2 * NA_KW - 1
    sel = np.zeros((2, n_c, GRID_W, 2, GRID_W), np.float32)
    qq, kk = np.meshgrid(col, col, indexing="ij")
    for j in range(2):
        sel[j, col_off, qq, j, kk] = 1.0
    pen = np.broadcast_to(np.where(cmask, 0.0, NEG_INF)[:, None, :], (GRID_W, 2, GRID_W))
    sel = np.concatenate([sel.reshape(2 * n_c, -1), pen.reshape(1, -1).astype(np.float32)], axis=0)
    rows = jnp.stack([rpb[:, :-1], rpb[:, 1:]], axis=2).astype(f32) * LOG2E
    rows = rows.reshape(NA_HEADS * (2 * NA_KH - 2), 2 * n_c)
    rows = jnp.concatenate([rows, jnp.ones((rows.shape[0], 1), f32)], axis=1)
    t = jnp.dot(rows, jnp.asarray(sel), precision=lax.Precision.HIGHEST)
    return t.reshape(NA_HEADS * (2 * NA_KH - 2), GRID_W, 2 * GRID_W)


def _dil_halo_pieces(dil):
    return 1 if DIL_TILE // dil == DIL_QBLOCK else dil


def _dil_kernel(*refs):
    ng = len(DIL_CONFIGS)
    n_in = sum(1 + 2 * (1 + 2 * _dil_halo_pieces(dil)) for _, dil in DIL_CONFIGS)
    in_refs = list(refs[:n_in])
    o_ref = refs[n_in]
    o_scr, m_scr, l_scr = refs[n_in + 1:]
    ti = pl.program_id(1)
    n_tiles = pl.num_programs(1)
    qb = DIL_QBLOCK
    span = 3 * qb
    sub = DIL_TILE // SAFE_STRIDE

    first_head = lax.broadcasted_iota(jnp.int32, (qb, LANES), 1) < HEAD_DIM
    own_head = _pair_mask(qb)
    npair = DIL_HPG // 2
    qi = lax.broadcasted_iota(jnp.int32, (qb, span), 0)
    kj = lax.broadcasted_iota(jnp.int32, (qb, span), 1)
    band = jnp.where((kj >= qi) & (kj <= qi + 2 * qb), 0.0, NEG_INF).astype(f32)
    kcol = lax.broadcasted_iota(jnp.int32, (1, span), 1)
    pen_lo = jnp.where(kcol < qb, jnp.where(ti == 0, NEG_INF, 0.0), 0.0).astype(f32)
    pen_hi = jnp.where(kcol >= 2 * qb, jnp.where(ti == n_tiles - 1, NEG_INF, 0.0), 0.0).astype(f32)

    for g, (_, dil) in enumerate(DIL_CONFIGS):
        seg = DIL_TILE // dil
        nj = seg // qb
        nh = _dil_halo_pieces(dil)
        q_ref = in_refs.pop(0)
        kc, kp, kn = in_refs.pop(0), [in_refs.pop(0) for _ in range(nh)], [in_refs.pop(0) for _ in range(nh)]
        vc, vp, vn = in_refs.pop(0), [in_refs.pop(0) for _ in range(nh)], [in_refs.pop(0) for _ in range(nh)]

        def halo(pieces, r, cols, nh=nh):
            return pieces[r][0, :, cols] if nh > 1 else pieces[0][0, r * qb:(r + 1) * qb, cols]

        def window(refs3, r, j, cols, seg=seg, nj=nj, halo=halo):
            p_refs, c_ref, n_refs = refs3
            base = r * seg
            if 0 < j < nj - 1:
                return c_ref[0, base + (j - 1) * qb:base + (j + 2) * qb, cols]
            lo = halo(p_refs, r, cols) if j == 0 else c_ref[0, base + (j - 1) * qb:base + j * qb, cols]
            mid = c_ref[0, base + j * qb:base + (j + 1) * qb, cols]
            hi = halo(n_refs, r, cols) if j == nj - 1 else c_ref[0, base + (j + 1) * qb:base + (j + 2) * qb, cols]
            return jnp.concatenate([lo, mid, hi], axis=0)

        for t0 in range(0, dil * nj, DIL_UNROLL):
            blocks = [divmod(t, nj) for t in range(t0, t0 + DIL_UNROLL)]
            s_parts = []
            for k, (r, j) in enumerate(blocks):
                mask = band
                if j == 0:
                    mask = mask + pen_lo
                if j == nj - 1:
                    mask = mask + pen_hi
                for hp in range(npair):
                    cols = slice(hp * LANES, (hp + 1) * LANES)
                    q2 = _stack_pair(q_ref[0, (t0 + k) * qb:(t0 + k + 1) * qb, cols], own_head)
                    kk = window((kp, kc, kn), r, j, cols)
                    s2 = lax.dot_general(q2, kk, (((1,), (1,)), ((), ())), preferred_element_type=f32)
                    s_parts += [s2[:qb] + mask, s2[qb:] + mask]
            s = jnp.concatenate(s_parts, axis=0)
            m = jnp.max(s, axis=-1, keepdims=True)
            p = jnp.exp2(s - m)
            l = jnp.sum(p, axis=-1, keepdims=True)
            pb = p.astype(bf16)
            for k, (r, j) in enumerate(blocks):
                if dil == 1:
                    store_rows = pl.ds(j * qb, qb)
                else:
                    hi = dil // SAFE_STRIDE
                    start = (r % SAFE_STRIDE) * sub + j * qb * hi + r // SAFE_STRIDE
                    store_rows = pl.ds(start, qb, stride=hi) if hi > 1 else pl.ds(start, qb)
                for hp in range(npair):
                    cols = slice(hp * LANES, (hp + 1) * LANES)
                    r0 = (k * npair + hp) * 2 * qb
                    vv = window((vp, vc, vn), r, j, cols)
                    pv = jnp.dot(pb[r0:r0 + 2 * qb], vv, preferred_element_type=f32)
                    slot = g * npair + hp
                    o_scr[slot, store_rows, :] = jnp.where(first_head, pv[:qb], pv[qb:])
                    m_scr[slot, store_rows, :] = jnp.where(first_head, m[r0:r0 + qb], m[r0 + qb:r0 + 2 * qb])
                    l_scr[slot, store_rows, :] = jnp.where(first_head, l[r0:r0 + qb], l[r0 + qb:r0 + 2 * qb])

    for hp in range(npair):
        for c in range(SAFE_STRIDE):
            def stream(scr, g):
                if DIL_CONFIGS[g][1] == 1:
                    return scr[g * npair + hp, pl.ds(c, sub, stride=SAFE_STRIDE), :]
                return scr[g * npair + hp, c * sub:(c + 1) * sub, :]

            ms = [stream(m_scr, g) for g in range(ng)]
            m_all = functools.reduce(jnp.maximum, ms)
            num = jnp.zeros_like(m_all)
            den = jnp.zeros_like(m_all)
            for g in range(ng):
                w = jnp.exp2(ms[g] - m_all)
                num = num + w * stream(o_scr, g)
                den = den + w * stream(l_scr, g)
            o_scr[ng * npair + hp, pl.ds(c, sub, stride=SAFE_STRIDE), :] = num / den
        o_ref[0, :, hp * LANES:(hp + 1) * LANES] = o_scr[ng * npair + hp].astype(bf16)


def _dil(qs, ks, vs):
    bsz, s, _ = qs[0].shape
    nt = s // DIL_TILE
    ng = len(DIL_CONFIGS)
    qb = DIL_QBLOCK
    per_tile = DIL_TILE // qb
    blk = (1, DIL_TILE, WB_OUT)
    cur = pl.BlockSpec(blk, lambda b, i: (b, i, 0))
    in_specs, args = [], []
    for g, (_, dil) in enumerate(DIL_CONFIGS):
        seg = DIL_TILE // dil
        if _dil_halo_pieces(dil) == 1 and seg == qb:
            prev = [pl.BlockSpec(blk, lambda b, i: (b, jnp.maximum(i - 1, 0), 0))]
            nxt = [pl.BlockSpec(blk, lambda b, i: (b, jnp.minimum(i + 1, nt - 1), 0))]
        else:
            prev = [pl.BlockSpec((1, qb, WB_OUT), lambda b, i, o=(r + 1) * seg // qb - 1:
                                 (b, jnp.maximum((i - 1) * per_tile + o, 0), 0)) for r in range(dil)]
            nxt = [pl.BlockSpec((1, qb, WB_OUT), lambda b, i, o=r * seg // qb:
                                (b, jnp.minimum((i + 1) * per_tile + o, s // qb - 1), 0)) for r in range(dil)]
        in_specs += [cur] + 2 * ([cur] + prev + nxt)
        args += [qs[g]] + [ks[g]] * (1 + len(prev) + len(nxt)) + [vs[g]] * (1 + len(prev) + len(nxt))
    return pl.pallas_call(
        _dil_kernel,
        grid=(bsz, nt),
        in_specs=in_specs,
        out_specs=pl.BlockSpec(blk, lambda b, i: (b, i, 0)),
        out_shape=jax.ShapeDtypeStruct((bsz, s, WB_OUT), bf16),
        scratch_shapes=[pltpu.VMEM(((ng + 1) * DIL_HPG // 2, DIL_TILE, LANES), f32)]
        + [pltpu.VMEM((ng * DIL_HPG // 2, DIL_TILE, LANES), f32)] * 2,
        compiler_params=_params(("parallel", "parallel")),
        name="dil",
    )(*args)


def _tail_kernel(x_ref, oa_ref, ob_ref, mod_ref, g1_ref, g2_ref, bg_ref, wpa_ref, wpb_ref,
                 wo_ref, win_ref, wout_ref, *rest):
    d = x_ref.shape[2]
    n_gate = 2 * d // COL_TILE
    wg_refs = rest[:n_gate]
    o_ref, h_scr, h2_scr, m_scr, act_scr = rest[n_gate:]
    mod = mod_ref[0]
    sh1, sc1, gt1, sh2, sc2, gt2 = [mod[k:k + 1] for k in range(6)]
    tm = x_ref.shape[1]
    for rc in range(TAIL_ROW_SPLIT):
        rows = slice(rc * tm // TAIL_ROW_SPLIT, (rc + 1) * tm // TAIL_ROW_SPLIT)
        x = x_ref[0, rows, :]
        h_scr[rows, :] = _modulated_norm(x, g1_ref[...], sc1, sh1).astype(bf16)
        oa = oa_ref[0, rows, :]
        ob = ob_ref[0, rows, :]
        for n in range(d // COL_TILE):
            ca = slice(n * COL_TILE, (n + 1) * COL_TILE)
            cb = slice(d + n * COL_TILE, d + (n + 1) * COL_TILE)
            wga, wgb = wg_refs[n], wg_refs[n_gate // 2 + n]
            ga = jax.nn.sigmoid(jnp.dot(h_scr[rows, :], wga[...], preferred_element_type=f32) + bg_ref[:, ca])
            gb = jax.nn.sigmoid(jnp.dot(h_scr[rows, :], wgb[...], preferred_element_type=f32) + bg_ref[:, cb])
            pa = jnp.dot(oa, wpa_ref[:, ca], preferred_element_type=f32)
            pb = jnp.dot(ob, wpb_ref[:, ca], preferred_element_type=f32)
            m_scr[rows, ca] = (ga * pa + gb * pb).astype(bf16)
        x1 = x + gt1 * jnp.dot(m_scr[rows, :], wo_ref[...], preferred_element_type=f32)
        h2_scr[rows, :] = _modulated_norm(x1, g2_ref[...], sc2, sh2).astype(bf16)
        for f in range(D_FF // FF_CHUNK):
            ca = slice(f * FF_CHUNK, (f + 1) * FF_CHUNK)
            cu = slice(D_FF + f * FF_CHUNK, D_FF + (f + 1) * FF_CHUNK)
            a = jnp.dot(h2_scr[rows, :], win_ref[:, ca], preferred_element_type=f32)
            u = jnp.dot(h2_scr[rows, :], win_ref[:, cu], preferred_element_type=f32)
            act_scr[rows, ca] = ((a * jax.nn.sigmoid(a)) * u).astype(bf16)
        o_ref[0, rows, :] = x1 + gt2 * jnp.dot(act_scr[rows, :], wout_ref[...], preferred_element_type=f32)


def _tail(x, o_a, o_b, mod6, g1, g2, w_in, bg, wpa, wpb, wo, w_ffn_in, w_ffn_out, tm=TAIL_TM):
    bsz, s, d = x.shape
    tok = lambda w: pl.BlockSpec((1, tm, w), lambda b, i: (b, i, 0))
    consts = [g1, g2, bg, wpa, wpb, wo, w_ffn_in, w_ffn_out]
    n_gate = 2 * d // COL_TILE
    gate_specs = [pl.BlockSpec((d, COL_TILE), lambda b, i, c=_N_J + n: (0, c), pipeline_mode=pl.Buffered(1))
                  for n in range(n_gate)]
    return pl.pallas_call(
        _tail_kernel,
        grid=(bsz, s // tm),
        in_specs=[tok(d), tok(WA), tok(WB_OUT), pl.BlockSpec((1, 6, d), lambda b, i: (b, 0, 0))]
        + [_const_spec(c.shape) for c in consts] + gate_specs,
        out_specs=tok(d),
        out_shape=jax.ShapeDtypeStruct((bsz, s, d), f32),
        scratch_shapes=[pltpu.VMEM((tm, d), bf16)] * 3 + [pltpu.VMEM((tm, D_FF), bf16)],
        compiler_params=_params(("parallel", "parallel")),
        name="tail",
    )(x, o_a, o_b, mod6, *consts, *([w_in] * n_gate))


def _rotary_tables(s):
    half = ROT_DIM // 2
    inv_freq = ROPE_THETA ** (-(np.arange(half, dtype=np.float64) * 2.0) / ROT_DIM)
    ang = np.arange(s, dtype=np.float64)[:, None] * inv_freq[None, :]
    cos, sin = np.cos(ang), np.sin(ang)
    rest = HEAD_DIM - ROT_DIM
    cos_h = np.concatenate([cos, cos, np.ones((s, rest))], axis=1)
    ssin_h = np.concatenate([-sin, sin, np.zeros((s, rest))], axis=1)
    rep = LANES // HEAD_DIM
    return (jnp.asarray(np.tile(cos_h, (1, rep)), dtype=f32), jnp.asarray(np.tile(ssin_h, (1, rep)), dtype=f32))


def _layer(x, mod6, g_norm1, g_norm2, w_in, b_gate, g_qa, g_ka, g_qb, g_kb, rpb,
           w_proj_a, w_proj_b, w_o, w_ffn_in, w_ffn_out):
    bsz, s, d = x.shape
    assert d == D_MODEL and w_in.shape == (d, W_QKV + 2 * d) and w_ffn_in.shape == (d, 2 * D_FF)
    assert s % DIL_TILE == 0 and s % (NA_ROWS * GRID_W) == 0 and s % TAIL_TM == 0
    assert s // GRID_W >= NA_ROWS + NA_KH
    scale = HEAD_DIM ** -0.5 * LOG2E
    rep = COL_TILE // HEAD_DIM
    gvec = jnp.stack([jnp.tile(g_qa * scale, rep), jnp.tile(g_ka, rep),
                      jnp.tile(g_qb * scale, rep), jnp.tile(g_kb, rep)]).astype(f32)
    hid = jnp.arange(COL_TILE) // HEAD_DIM
    ones_bd = ((hid[:, None] == hid[None, :]).astype(f32) * (1.0 / HEAD_DIM)).astype(bf16)
    rot_tables = _rotary_tables(s)
    g1 = g_norm1.reshape(1, d)
    g2 = g_norm2.reshape(1, d)

    qkv, tail_ws = _qkv(x, mod6, g1, w_in, gvec, ones_bd, rot_tables,
                        [w_proj_a, w_proj_b, w_o, w_ffn_in, w_ffn_out])
    ng = len(DIL_CONFIGS)
    o_a = _na(qkv[_OUT_QA], qkv[_OUT_KA], qkv[_OUT_VA], _na_bias_table(rpb))
    o_b = _dil(qkv[_OUT_QB:_OUT_QB + ng], qkv[_OUT_KB:_OUT_KB + ng], qkv[_OUT_VB:_OUT_VB + ng])

    return _tail(x, o_a, o_b, mod6, g1, g2, w_in, b_gate.reshape(1, 2 * d), *tail_ws)


def kernel(x, c, w_ada, b_ada, g_norm1, g_norm2, w_in, b_gate, g_qa, g_ka, g_qb, g_kb, rpb,
           w_proj_a, w_proj_b, w_o, w_ffn_in, w_ffn_out):
    depth = w_ada.shape[0]
    bsz, d = c.shape
    for l in range(depth):
        mod, w_in_bf = _mod(c, w_ada[l], b_ada[l], w_in[l])
        x = _layer(x, mod.reshape(bsz, 6, d), g_norm1[l], g_norm2[l], w_in_bf, b_gate[l], g_qa[l], g_ka[l], g_qb[l],
                   g_kb[l], rpb[l], w_proj_a[l], w_proj_b[l], w_o[l], w_ffn_in[l], w_ffn_out[l])
    return x
```

```python
import functools

import jax
import jax.numpy as jnp
import numpy as np
from jax import lax
from jax.experimental import pallas as pl
from jax.experimental.pallas import tpu as pltpu

f32 = jnp.float32
bf16 = jnp.bfloat16

D_MODEL = 1024
HEAD_DIM = 64
GRID_W = 64
NA_HEADS = 8
NA_KH = 8
NA_KW = 16
DIL_CONFIGS = ((128, 1), (512, 4), (2048, 16))
DIL_HPG = 4
DIL_HEADS = DIL_HPG * len(DIL_CONFIGS)
DIL_QBLOCK = 64
ROT_DIM = HEAD_DIM // 4
ROPE_THETA = 500000.0
D_FF = -(-8 * D_MODEL // (3 * 256)) * 256
EPS = 1e-6
NEG_INF = -1e30
LOG2E = 1.4426950408889634
WA = NA_HEADS * HEAD_DIM
WB = DIL_HEADS * HEAD_DIM
WB_OUT = DIL_HPG * HEAD_DIM
W_QKV = 3 * WA + 3 * WB

LANES = 128
BF16_ROWS = 16
COL_TILE = 256
DIL_TILE = 1024
QKV_ROW_SPLIT = 2
QKV_DOT_TILES = 2
DIL_QMAX = 128
DIL_UNROLL = 16
NA_ROWS = 16
NA_BATCH = 4
MOD_STEPS = 8
TAIL_TM = 1024
TAIL_ROW_SPLIT = 2
FF_CHUNK = 256
VMEM_LIMIT = 56 * 1024 * 1024

SAFE_STRIDE = 4

assert all((win // 2) // dil == DIL_QBLOCK for win, dil in DIL_CONFIGS)
assert all(dil == 1 or (dil % SAFE_STRIDE == 0 and dil // SAFE_STRIDE <= SAFE_STRIDE)
           for _, dil in DIL_CONFIGS)


def _params(sem):
    return pltpu.CompilerParams(dimension_semantics=sem, vmem_limit_bytes=VMEM_LIMIT)


def _const_spec(shape):
    nd = len(shape)
    return pl.BlockSpec(shape, lambda *_: (0,) * nd, pipeline_mode=pl.Buffered(1))


def _mod_kernel(ct_ref, w_ref, b_ref, win_ref, o_ref, win_o_ref):
    ct = ct_ref[...]
    act = ct * jax.nn.sigmoid(ct)
    w = w_ref[...]
    rows = [jnp.sum(act[:, b:b + 1] * w, axis=0, keepdims=True) for b in range(ct.shape[1])]
    o_ref[...] = jnp.concatenate(rows, axis=0) + b_ref[...]
    win_o_ref[...] = win_ref[...].astype(bf16)


def _mod(c, w_ada, b_ada, w_in):
    bsz, d = c.shape
    n = w_ada.shape[1]
    steps = MOD_STEPS
    tn = n // steps
    slab = pl.BlockSpec((w_in.shape[0] // steps, w_in.shape[1]), lambda j: (j, 0))
    return pl.pallas_call(
        _mod_kernel,
        grid=(steps,),
        in_specs=[pl.BlockSpec((d, bsz), lambda j: (0, 0)),
                  pl.BlockSpec((d, tn), lambda j: (0, j)),
                  pl.BlockSpec((1, tn), lambda j: (0, j)),
                  slab],
        out_specs=[pl.BlockSpec((bsz, tn), lambda j: (0, j)), slab],
        out_shape=[jax.ShapeDtypeStruct((bsz, n), f32), jax.ShapeDtypeStruct(w_in.shape, bf16)],
        compiler_params=_params(("arbitrary",)),
        name="mod",
    )(c.T, w_ada, b_ada.reshape(1, n), w_in)


def _modulated_norm(x, g, sc, sh):
    ms = jnp.mean(x * x, axis=-1, keepdims=True)
    y = x * lax.rsqrt(ms + EPS) * g
    return y * (1.0 + sc) + sh


_J_QB = 3 * WA // COL_TILE
_J_KB = _J_QB + WB // COL_TILE
_J_VB = _J_KB + WB // COL_TILE
_N_J = W_QKV // COL_TILE


def _qkv_tile_kind(j):
    if j < WA // COL_TILE:
        return 0, False, 1
    if j < 2 * WA // COL_TILE:
        return 1, False, 1
    if j < _J_QB:
        return None, False, 1
    dil = DIL_CONFIGS[(j - _J_QB) % len(DIL_CONFIGS)][1]
    if j < _J_KB:
        return 2, True, dil
    if j < _J_VB:
        return 3, True, dil
    return None, False, dil


_QKV_OUT_WIDTHS = (WA,) * 3 + (WB_OUT,) * (3 * len(DIL_CONFIGS))
_OUT_QA, _OUT_KA, _OUT_VA, _OUT_QB, _OUT_KB, _OUT_VB = 0, 1, 2, 3, 3 + len(DIL_CONFIGS), 3 + 2 * len(DIL_CONFIGS)


def _qkv_tile_out(j, o_refs):
    if j < _J_QB:
        per = WA // COL_TILE
        return o_refs[j // per], (j % per) * COL_TILE
    return o_refs[_OUT_QB + j - _J_QB], 0


def _qkv_kernel(n_cast, x_ref, mod_ref, g1_ref, w_ref, gvec_ref, ones_ref, cos_ref, ssin_ref, *rest):
    n_out = len(_QKV_OUT_WIDTHS)
    cast_in, o_refs = rest[:n_cast], rest[n_cast:n_cast + n_out]
    cast_out = rest[n_cast + n_out:2 * n_cast + n_out]
    h_scr, y_scr, z_scr = rest[2 * n_cast + n_out:]
    for src, dst in zip(cast_in, cast_out):
        dst[...] = src[...].astype(bf16)
    tm = x_ref.shape[1]
    nc = COL_TILE // LANES
    mod = mod_ref[0]
    h_scr[...] = _modulated_norm(x_ref[0], g1_ref[...], mod[1:2], mod[0:1]).astype(bf16)
    head_dim = lax.broadcasted_iota(jnp.int32, (tm // QKV_ROW_SPLIT, COL_TILE), 1) % HEAD_DIM
    first_half = head_dim < ROT_DIM // 2

    wide = {}
    for j in range(_N_J):
        gain_row, rotary, dil = _qkv_tile_kind(j)
        o_ref, c0 = _qkv_tile_out(j, o_refs)
        slot = j % y_scr.shape[0]
        for hb in range(QKV_ROW_SPLIT):
            rows = slice(hb * tm // QKV_ROW_SPLIT, (hb + 1) * tm // QKV_ROW_SPLIT)
            if j % QKV_DOT_TILES == 0:
                wcols = slice(j * COL_TILE, min(j + QKV_DOT_TILES, _N_J) * COL_TILE)
                wide[hb] = jnp.dot(h_scr[rows, :], w_ref[:, wcols], preferred_element_type=f32)
            sub = j % QKV_DOT_TILES
            y = wide[hb][:, sub * COL_TILE:(sub + 1) * COL_TILE]
            if gain_row is not None:
                ms = jnp.dot((y * y).astype(bf16), ones_ref[...], preferred_element_type=f32)
                y = y * lax.rsqrt(ms + EPS) * gvec_ref[gain_row:gain_row + 1, :]
            if rotary:
                cos, ssin = [jnp.concatenate([t_ref[rows, :]] * nc, axis=1) for t_ref in (cos_ref, ssin_ref)]
                half = ROT_DIM // 2
                up = pltpu.roll(y, COL_TILE - half, 1)
                dn = pltpu.roll(y, half, 1)
                y = y * cos + jnp.where(first_half, up, dn) * ssin
            if dil == 1:
                o_ref[0, rows, c0:c0 + COL_TILE] = y.astype(bf16)
            else:
                for c in range(nc):
                    y_scr[slot, c, rows, :] = y[:, c * LANES:(c + 1) * LANES]
        if dil == 1:
            continue
        seg = DIL_TILE // dil
        for c in range(nc):
            ocols = slice(c0 + c * LANES, c0 + (c + 1) * LANES)
            if dil <= SAFE_STRIDE:
                for r in range(dil):
                    rows = y_scr[slot, c, pl.ds(r, seg, stride=dil), :]
                    o_ref[0, r * seg:(r + 1) * seg, ocols] = rows.astype(bf16)
                continue
            hi = dil // SAFE_STRIDE
            sub = DIL_TILE // SAFE_STRIDE
            for r_lo in range(SAFE_STRIDE):
                z_scr[c, r_lo * sub:(r_lo + 1) * sub, :] = y_scr[slot, c, pl.ds(r_lo, sub, stride=SAFE_STRIDE), :]
            for r_lo in range(SAFE_STRIDE):
                for r_hi in range(hi):
                    r = SAFE_STRIDE * r_hi + r_lo
                    rows = z_scr[c, pl.ds(r_lo * sub + r_hi, seg, stride=hi), :]
                    o_ref[0, r * seg:(r + 1) * seg, ocols] = rows.astype(bf16)


def _qkv(x, mod6, g1, w_qkv, gvec, ones_bd, rot_tables, cast_ws, tm=DIL_TILE):
    bsz, s, d = x.shape
    assert tm == DIL_TILE
    nt = s // tm
    steps = bsz * nt
    assert all(w.shape[0] % (BF16_ROWS * steps) == 0 for w in cast_ws)
    slab_specs = [pl.BlockSpec((w.shape[0] // steps, w.shape[1]), lambda b, i: (b * nt + i, 0)) for w in cast_ws]
    outs = pl.pallas_call(
        functools.partial(_qkv_kernel, len(cast_ws)),
        grid=(bsz, nt),
        in_specs=[pl.BlockSpec((1, tm, d), lambda b, i: (b, i, 0)),
                  pl.BlockSpec((1, 6, d), lambda b, i: (b, 0, 0)),
                  _const_spec(g1.shape),
                  pl.BlockSpec((d, W_QKV), lambda b, i: (0, 0), pipeline_mode=pl.Buffered(1)),
                  _const_spec(gvec.shape),
                  _const_spec(ones_bd.shape),
                  ] + [pl.BlockSpec((tm, LANES), lambda b, i: (i, 0))] * len(rot_tables) + slab_specs,
        out_specs=[pl.BlockSpec((1, tm, w), lambda b, i: (b, i, 0)) for w in _QKV_OUT_WIDTHS] + slab_specs,
        out_shape=[jax.ShapeDtypeStruct((bsz, s, w), bf16) for w in _QKV_OUT_WIDTHS]
        + [jax.ShapeDtypeStruct(w.shape, bf16) for w in cast_ws],
        scratch_shapes=[pltpu.VMEM((tm, d), bf16),
                        pltpu.VMEM((2, COL_TILE // LANES, tm, LANES), f32),
                        pltpu.VMEM((COL_TILE // LANES, tm, LANES), f32)],
        compiler_params=_params(("parallel", "parallel")),
        name="qkv",
    )(x, mod6, g1, w_qkv, gvec, ones_bd, *rot_tables, *cast_ws)
    n_out = len(_QKV_OUT_WIDTHS)
    return outs[:n_out], outs[n_out:]


def _pair_mask(nq):
    row = lax.broadcasted_iota(jnp.int32, (2 * nq, LANES), 0)
    lane = lax.broadcasted_iota(jnp.int32, (2 * nq, LANES), 1)
    return (row < nq) == (lane < HEAD_DIM)


def _stack_pair(qp, own_head):
    q2 = jnp.concatenate([qp, qp], axis=0)
    return jnp.where(own_head, q2, jnp.zeros_like(q2))


def _na_window_start(rb, n_rows):
    return jnp.clip(rb * NA_ROWS - NA_KH // 2, 0, n_rows - (NA_ROWS + NA_KH))


def _na_kernel(q_ref, kwin, vwin, tbl_ref, o_ref):
    rb = pl.program_id(1)
    n_rows = pl.num_programs(1) * NA_ROWS
    win_start = _na_window_start(rb, n_rows)
    first_head = lax.broadcasted_iota(jnp.int32, (GRID_W, LANES), 1) < HEAD_DIM
    own_head = _pair_mask(GRID_W)
    nkeys = NA_KH * GRID_W
    npair = NA_HEADS // 2
    pair_rows = 2 * GRID_W

    n_off = 2 * NA_KH - 2
    for a0 in range(0, NA_ROWS, NA_BATCH):
        offs, s_parts = [], []
        for a in range(a0, a0 + NA_BATCH):
            r = rb * NA_ROWS + a
            row_start = jnp.clip(r - NA_KH // 2, 0, n_rows - NA_KH)
            delta = r - row_start
            off = pl.multiple_of((row_start - win_start) * GRID_W, GRID_W)
            offs.append(off)
            row_parts = []
            for hp in range(npair):
                cols = slice(hp * LANES, (hp + 1) * LANES)
                q2 = _stack_pair(q_ref[0, a * GRID_W:(a + 1) * GRID_W, cols], own_head)
                kk = kwin[0, pl.ds(off, nkeys), cols]
                row_parts.append(lax.dot_general(q2, kk, (((1,), (1,)), ((), ())), preferred_element_type=f32))
            bias = jnp.concatenate(
                [jnp.concatenate([tbl_ref[h * n_off + 2 * kp - delta + NA_KH - 1] for kp in range(NA_KH // 2)],
                                 axis=1) for h in range(NA_HEADS)], axis=0)
            s_parts.append(jnp.concatenate(row_parts, axis=0) + bias)
        s = jnp.concatenate(s_parts, axis=0)
        m = jnp.max(s, axis=-1, keepdims=True)
        p = jnp.exp2(s - m)
        inv = 1.0 / jnp.sum(p, axis=-1, keepdims=True)
        pb = p.astype(bf16)
        for k, a in enumerate(range(a0, a0 + NA_BATCH)):
            for hp in range(npair):
                cols = slice(hp * LANES, (hp + 1) * LANES)
                r0 = (k * npair + hp) * pair_rows
                vv = vwin[0, pl.ds(offs[k], nkeys), cols]
                pv = jnp.dot(pb[r0:r0 + pair_rows], vv, preferred_element_type=f32) * inv[r0:r0 + pair_rows]
                o_ref[0, a * GRID_W:(a + 1) * GRID_W, cols] = (
                    jnp.where(first_head, pv[:GRID_W], pv[GRID_W:]).astype(bf16))


def _na(q, k, v, tbl):
    bsz, s, _ = q.shape
    blk = NA_ROWS * GRID_W
    nb = s // blk
    qspec = pl.BlockSpec((1, blk, WA), lambda b, i: (b, i, 0))
    win_tokens = (NA_ROWS + NA_KH) * GRID_W
    n_rows = s // GRID_W
    window = pl.BlockSpec(
        (pl.Element(1), pl.Element(win_tokens), pl.Element(WA)),
        lambda b, i: (b, pl.multiple_of(_na_window_start(i, n_rows) * GRID_W, GRID_W), 0))

    return pl.pallas_call(
        _na_kernel,
        grid=(bsz, nb),
        in_specs=[qspec, window, window, _const_spec(tbl.shape)],
        out_specs=pl.BlockSpec((1, blk, WA), lambda b, i: (b, i, 0)),
        out_shape=jax.ShapeDtypeStruct((bsz, s, WA), bf16),
        compiler_params=_params(("parallel", "parallel")),
        name="na",
    )(q, k, v, tbl)


def _na_bias_table(rpb):
    col = np.arange(GRID_W)
    col_start = np.clip(col - NA_KW // 2, 0, GRID_W - NA_KW)
    cmask = (col[None, :] >= col_start[:, None]) & (col[None, :] < col_start[:, None] + NA_KW)
    col_off = np.clip(col[None, :] - col[:, None] + (NA_KW - 1), 0, 2 * NA_KW - 2)
    n_c = 2 * NA_KW - 1
    sel = np.zeros((2, n_c, GRID_W, 2, GRID_W), np.float32)
    qq, kk = np.meshgrid(col, col, indexing="ij")
    for j in range(2):
        sel[j, col_off, qq, j, kk] = 1.0
    pen = np.broadcast_to(np.where(cmask, 0.0, NEG_INF)[:, None, :], (GRID_W, 2, GRID_W))
    sel = np.concatenate([sel.reshape(2 * n_c, -1), pen.reshape(1, -1).astype(np.float32)], axis=0)
    rows = jnp.stack([rpb[:, :-1], rpb[:, 1:]], axis=2).astype(f32) * LOG2E
    rows = rows.reshape(NA_HEADS * (2 * NA_KH - 2), 2 * n_c)
    rows = jnp.concatenate([rows, jnp.ones((rows.shape[0], 1), f32)], axis=1)
    t = jnp.dot(rows, jnp.asarray(sel), precision=lax.Precision.HIGHEST)
    return t.reshape(NA_HEADS * (2 * NA_KH - 2), GRID_W, 2 * GRID_W)


def _dil_halo_pieces(dil):
    return 1 if DIL_TILE // dil == DIL_QBLOCK else dil


def _dil_kernel(*refs):
    ng = len(DIL_CONFIGS)
    n_in = sum(1 + 2 * (1 + 2 * _dil_halo_pieces(dil)) for _, dil in DIL_CONFIGS)
    in_refs = list(refs[:n_in])
    o_ref = refs[n_in]
    o_scr, m_scr, l_scr = refs[n_in + 1:]
    ti = pl.program_id(1)
    n_tiles = pl.num_programs(1)
    hw = DIL_QBLOCK
    sub = DIL_TILE // SAFE_STRIDE
    npair = DIL_HPG // 2

    def block_masks(qb):
        span = qb + 2 * hw
        qi = lax.broadcasted_iota(jnp.int32, (qb, span), 0)
        kj = lax.broadcasted_iota(jnp.int32, (qb, span), 1)
        band = jnp.where((kj >= qi) & (kj <= qi + 2 * hw), 0.0, NEG_INF).astype(f32)
        kcol = lax.broadcasted_iota(jnp.int32, (1, span), 1)
        pen_lo = jnp.where(kcol < hw, jnp.where(ti == 0, NEG_INF, 0.0), 0.0).astype(f32)
        pen_hi = jnp.where(kcol >= hw + qb, jnp.where(ti == n_tiles - 1, NEG_INF, 0.0), 0.0).astype(f32)
        first_head = lax.broadcasted_iota(jnp.int32, (qb, LANES), 1) < HEAD_DIM
        return band, pen_lo, pen_hi, first_head, _pair_mask(qb)

    masks = {}
    for g, (_, dil) in enumerate(DIL_CONFIGS):
        seg = DIL_TILE // dil
        qb = min(DIL_QMAX, seg)
        if qb not in masks:
            masks[qb] = block_masks(qb)
        band, pen_lo, pen_hi, first_head, own_head = masks[qb]
        nj = seg // qb
        nh = _dil_halo_pieces(dil)
        q_ref = in_refs.pop(0)
        kc, kp, kn = in_refs.pop(0), [in_refs.pop(0) for _ in range(nh)], [in_refs.pop(0) for _ in range(nh)]
        vc, vp, vn = in_refs.pop(0), [in_refs.pop(0) for _ in range(nh)], [in_refs.pop(0) for _ in range(nh)]

        def halo(pieces, r, cols, nh=nh):
            return pieces[r][0, :, cols] if nh > 1 else pieces[0][0, r * hw:(r + 1) * hw, cols]

        def window(refs3, r, j, cols, seg=seg, nj=nj, qb=qb, halo=halo):
            p_refs, c_ref, n_refs = refs3
            base = r * seg
            if 0 < j < nj - 1:
                return c_ref[0, base + j * qb - hw:base + (j + 1) * qb + hw, cols]
            lo = halo(p_refs, r, cols) if j == 0 else c_ref[0, base + j * qb - hw:base + j * qb, cols]
            mid = c_ref[0, base + j * qb:base + (j + 1) * qb, cols]
            hi = (halo(n_refs, r, cols) if j == nj - 1
                  else c_ref[0, base + (j + 1) * qb:base + (j + 1) * qb + hw, cols])
            return jnp.concatenate([lo, mid, hi], axis=0)

        n_blocks = dil * nj
        for t0 in range(0, n_blocks, min(DIL_UNROLL, n_blocks)):
            blocks = [divmod(t, nj) for t in range(t0, t0 + min(DIL_UNROLL, n_blocks))]
            s_parts = []
            for k, (r, j) in enumerate(blocks):
                mask = band
                if j == 0:
                    mask = mask + pen_lo
                if j == nj - 1:
                    mask = mask + pen_hi
                for hp in range(npair):
                    cols = slice(hp * LANES, (hp + 1) * LANES)
                    q2 = _stack_pair(q_ref[0, (t0 + k) * qb:(t0 + k + 1) * qb, cols], own_head)
                    kk = window((kp, kc, kn), r, j, cols)
                    s2 = lax.dot_general(q2, kk, (((1,), (1,)), ((), ())), preferred_element_type=f32)
                    s_parts += [s2[:qb] + mask, s2[qb:] + mask]
            s = jnp.concatenate(s_parts, axis=0)
            m = jnp.max(s, axis=-1, keepdims=True)
            p = jnp.exp2(s - m)
            l = jnp.sum(p, axis=-1, keepdims=True)
            pb = p.astype(bf16)
            for k, (r, j) in enumerate(blocks):
                if dil == 1:
                    store_rows = pl.ds(j * qb, qb)
                else:
                    hi = dil // SAFE_STRIDE
                    start = (r % SAFE_STRIDE) * sub + j * qb * hi + r // SAFE_STRIDE
                    store_rows = pl.ds(start, qb, stride=hi) if hi > 1 else pl.ds(start, qb)
                for hp in range(npair):
                    cols = slice(hp * LANES, (hp + 1) * LANES)
                    r0 = (k * npair + hp) * 2 * qb
                    vv = window((vp, vc, vn), r, j, cols)
                    pv = jnp.dot(pb[r0:r0 + 2 * qb], vv, preferred_element_type=f32)
                    slot = g * npair + hp
                    o_scr[slot, store_rows, :] = jnp.where(first_head, pv[:qb], pv[qb:])
                    m_scr[slot, store_rows, :] = jnp.where(first_head, m[r0:r0 + qb], m[r0 + qb:r0 + 2 * qb])
                    l_scr[slot, store_rows, :] = jnp.where(first_head, l[r0:r0 + qb], l[r0 + qb:r0 + 2 * qb])

    for hp in range(npair):
        for c in range(SAFE_STRIDE):
            def stream(scr, g):
                if DIL_CONFIGS[g][1] == 1:
                    return scr[g * npair + hp, pl.ds(c, sub, stride=SAFE_STRIDE), :]
                return scr[g * npair + hp, c * sub:(c + 1) * sub, :]

            ms = [stream(m_scr, g) for g in range(ng)]
            m_all = functools.reduce(jnp.maximum, ms)
            num = jnp.zeros_like(m_all)
            den = jnp.zeros_like(m_all)
            for g in range(ng):
                w = jnp.exp2(ms[g] - m_all)
                num = num + w * stream(o_scr, g)
                den = den + w * stream(l_scr, g)
            o_scr[ng * npair + hp, pl.ds(c, sub, stride=SAFE_STRIDE), :] = num / den
        o_ref[0, :, hp * LANES:(hp + 1) * LANES] = o_scr[ng * npair + hp].astype(bf16)


def _dil(qs, ks, vs):
    bsz, s, _ = qs[0].shape
    nt = s // DIL_TILE
    ng = len(DIL_CONFIGS)
    qb = DIL_QBLOCK
    per_tile = DIL_TILE // qb
    blk = (1, DIL_TILE, WB_OUT)
    cur = pl.BlockSpec(blk, lambda b, i: (b, i, 0))
    in_specs, args = [], []
    for g, (_, dil) in enumerate(DIL_CONFIGS):
        seg = DIL_TILE // dil
        if _dil_halo_pieces(dil) == 1 and seg == qb:
            prev = [pl.BlockSpec(blk, lambda b, i: (b, jnp.maximum(i - 1, 0), 0))]
            nxt = [pl.BlockSpec(blk, lambda b, i: (b, jnp.minimum(i + 1, nt - 1), 0))]
        else:
            prev = [pl.BlockSpec((1, qb, WB_OUT), lambda b, i, o=(r + 1) * seg // qb - 1:
                                 (b, jnp.maximum((i - 1) * per_tile + o, 0), 0)) for r in range(dil)]
            nxt = [pl.BlockSpec((1, qb, WB_OUT), lambda b, i, o=r * seg // qb:
                                (b, jnp.minimum((i + 1) * per_tile + o, s // qb - 1), 0)) for r in range(dil)]
        in_specs += [cur] + 2 * ([cur] + prev + nxt)
        args += [qs[g]] + [ks[g]] * (1 + len(prev) + len(nxt)) + [vs[g]] * (1 + len(prev) + len(nxt))
    return pl.pallas_call(
        _dil_kernel,
        grid=(bsz, nt),
        in_specs=in_specs,
        out_specs=pl.BlockSpec(blk, lambda b, i: (b, i, 0)),
        out_shape=jax.ShapeDtypeStruct((bsz, s, WB_OUT), bf16),
        scratch_shapes=[pltpu.VMEM(((ng + 1) * DIL_HPG // 2, DIL_TILE, LANES), f32)]
        + [pltpu.VMEM((ng * DIL_HPG // 2, DIL_TILE, LANES), f32)] * 2,
        compiler_params=_params(("parallel", "parallel")),
        name="dil",
    )(*args)


def _tail_kernel(x_ref, oa_ref, ob_ref, mod_ref, g1_ref, g2_ref, bg_ref, wpa_ref, wpb_ref,
                 wo_ref, win_ref, wout_ref, *rest):
    d = x_ref.shape[2]
    n_gate = 2 * d // COL_TILE
    wg_refs = rest[:n_gate]
    o_ref, h_scr, h2_scr, m_scr, act_scr = rest[n_gate:]
    mod = mod_ref[0]
    sh1, sc1, gt1, sh2, sc2, gt2 = [mod[k:k + 1] for k in range(6)]
    tm = x_ref.shape[1]
    for rc in range(TAIL_ROW_SPLIT):
        rows = slice(rc * tm // TAIL_ROW_SPLIT, (rc + 1) * tm // TAIL_ROW_SPLIT)
        x = x_ref[0, rows, :]
        h_scr[rows, :] = _modulated_norm(x, g1_ref[...], sc1, sh1).astype(bf16)
        oa = oa_ref[0, rows, :]
        ob = ob_ref[0, rows, :]
        for n in range(d // COL_TILE):
            ca = slice(n * COL_TILE, (n + 1) * COL_TILE)
            cb = slice(d + n * COL_TILE, d + (n + 1) * COL_TILE)
            wga, wgb = wg_refs[n], wg_refs[n_gate // 2 + n]
            ga = jax.nn.sigmoid(jnp.dot(h_scr[rows, :], wga[...], preferred_element_type=f32) + bg_ref[:, ca])
            gb = jax.nn.sigmoid(jnp.dot(h_scr[rows, :], wgb[...], preferred_element_type=f32) + bg_ref[:, cb])
            pa = jnp.dot(oa, wpa_ref[:, ca], preferred_element_type=f32)
            pb = jnp.dot(ob, wpb_ref[:, ca], preferred_element_type=f32)
            m_scr[rows, ca] = (ga * pa + gb * pb).astype(bf16)
        x1 = x + gt1 * jnp.dot(m_scr[rows, :], wo_ref[...], preferred_element_type=f32)
        h2_scr[rows, :] = _modulated_norm(x1, g2_ref[...], sc2, sh2).astype(bf16)
        for f in range(D_FF // FF_CHUNK):
            ca = slice(f * FF_CHUNK, (f + 1) * FF_CHUNK)
            cu = slice(D_FF + f * FF_CHUNK, D_FF + (f + 1) * FF_CHUNK)
            a = jnp.dot(h2_scr[rows, :], win_ref[:, ca], preferred_element_type=f32)
            u = jnp.dot(h2_scr[rows, :], win_ref[:, cu], preferred_element_type=f32)
            act_scr[rows, ca] = ((a * jax.nn.sigmoid(a)) * u).astype(bf16)
        o_ref[0, rows, :] = x1 + gt2 * jnp.dot(act_scr[rows, :], wout_ref[...], preferred_element_type=f32)


def _tail(x, o_a, o_b, mod6, g1, g2, w_in, bg, wpa, wpb, wo, w_ffn_in, w_ffn_out, tm=TAIL_TM):
    bsz, s, d = x.shape
    tok = lambda w: pl.BlockSpec((1, tm, w), lambda b, i: (b, i, 0))
    consts = [g1, g2, bg, wpa, wpb, wo, w_ffn_in, w_ffn_out]
    n_gate = 2 * d // COL_TILE
    gate_specs = [pl.BlockSpec((d, COL_TILE), lambda b, i, c=_N_J + n: (0, c), pipeline_mode=pl.Buffered(1))
                  for n in range(n_gate)]
    return pl.pallas_call(
        _tail_kernel,
        grid=(bsz, s // tm),
        in_specs=[tok(d), tok(WA), tok(WB_OUT), pl.BlockSpec((1, 6, d), lambda b, i: (b, 0, 0))]
        + [_const_spec(c.shape) for c in consts] + gate_specs,
        out_specs=tok(d),
        out_shape=jax.ShapeDtypeStruct((bsz, s, d), f32),
        scratch_shapes=[pltpu.VMEM((tm, d), bf16)] * 3 + [pltpu.VMEM((tm, D_FF), bf16)],
        compiler_params=_params(("parallel", "parallel")),
        name="tail",
    )(x, o_a, o_b, mod6, *consts, *([w_in] * n_gate))


def _rotary_tables(s):
    half = ROT_DIM // 2
    inv_freq = ROPE_THETA ** (-(np.arange(half, dtype=np.float64) * 2.0) / ROT_DIM)
    ang = np.arange(s, dtype=np.float64)[:, None] * inv_freq[None, :]
    cos, sin = np.cos(ang), np.sin(ang)
    rest = HEAD_DIM - ROT_DIM
    cos_h = np.concatenate([cos, cos, np.ones((s, rest))], axis=1)
    ssin_h = np.concatenate([-sin, sin, np.zeros((s, rest))], axis=1)
    rep = LANES // HEAD_DIM
    return (jnp.asarray(np.tile(cos_h, (1, rep)), dtype=f32), jnp.asarray(np.tile(ssin_h, (1, rep)), dtype=f32))


def _layer(x, mod6, g_norm1, g_norm2, w_in, b_gate, g_qa, g_ka, g_qb, g_kb, rpb,
           w_proj_a, w_proj_b, w_o, w_ffn_in, w_ffn_out):
    bsz, s, d = x.shape
    assert d == D_MODEL and w_in.shape == (d, W_QKV + 2 * d) and w_ffn_in.shape == (d, 2 * D_FF)
    assert s % DIL_TILE == 0 and s % (NA_ROWS * GRID_W) == 0 and s % TAIL_TM == 0
    assert s // GRID_W >= NA_ROWS + NA_KH
    scale = HEAD_DIM ** -0.5 * LOG2E
    rep = COL_TILE // HEAD_DIM
    gvec = jnp.stack([jnp.tile(g_qa * scale, rep), jnp.tile(g_ka, rep),
                      jnp.tile(g_qb * scale, rep), jnp.tile(g_kb, rep)]).astype(f32)
    hid = jnp.arange(COL_TILE) // HEAD_DIM
    ones_bd = ((hid[:, None] == hid[None, :]).astype(f32) * (1.0 / HEAD_DIM)).astype(bf16)
    rot_tables = _rotary_tables(s)
    g1 = g_norm1.reshape(1, d)
    g2 = g_norm2.reshape(1, d)

    qkv, tail_ws = _qkv(x, mod6, g1, w_in, gvec, ones_bd, rot_tables,
                        [w_proj_a, w_proj_b, w_o, w_ffn_in, w_ffn_out])
    ng = len(DIL_CONFIGS)
    o_a = _na(qkv[_OUT_QA], qkv[_OUT_KA], qkv[_OUT_VA], _na_bias_table(rpb))
    o_b = _dil(qkv[_OUT_QB:_OUT_QB + ng], qkv[_OUT_KB:_OUT_KB + ng], qkv[_OUT_VB:_OUT_VB + ng])

    return _tail(x, o_a, o_b, mod6, g1, g2, w_in, b_gate.reshape(1, 2 * d), *tail_ws)


def kernel(x, c, w_ada, b_ada, g_norm1, g_norm2, w_in, b_gate, g_qa, g_ka, g_qb, g_kb, rpb,
           w_proj_a, w_proj_b, w_o, w_ffn_in, w_ffn_out):
    depth = w_ada.shape[0]
    bsz, d = c.shape
    for l in range(depth):
        mod, w_in_bf = _mod(c, w_ada[l], b_ada[l], w_in[l])
        x = _layer(x, mod.reshape(bsz, 6, d), g_norm1[l], g_norm2[l], w_in_bf, b_gate[l], g_qa[l], g_ka[l], g_qb[l],
                   g_kb[l], rpb[l], w_proj_a[l], w_proj_b[l], w_o[l], w_ffn_in[l], w_ffn_out[l])
    return x
```

```python
import functools

import jax
import jax.numpy as jnp
import numpy as np
from jax import lax
from jax.experimental import pallas as pl
from jax.experimental.pallas import tpu as pltpu

f32 = jnp.float32
bf16 = jnp.bfloat16

D_MODEL = 1024
HEAD_DIM = 64
GRID_W = 64
NA_HEADS = 8
NA_KH = 8
NA_KW = 16
DIL_CONFIGS = ((128, 1), (512, 4), (2048, 16))
DIL_HPG = 4
DIL_HEADS = DIL_HPG * len(DIL_CONFIGS)
DIL_QBLOCK = 64
ROT_DIM = HEAD_DIM // 4
ROPE_THETA = 500000.0
D_FF = -(-8 * D_MODEL // (3 * 256)) * 256
EPS = 1e-6
NEG_INF = -1e30
LOG2E = 1.4426950408889634
WA = NA_HEADS * HEAD_DIM
WB = DIL_HEADS * HEAD_DIM
WB_OUT = DIL_HPG * HEAD_DIM
W_QKV = 3 * WA + 3 * WB

LANES = 128
BF16_ROWS = 16
COL_TILE = 256
DIL_TILE = 1024
QKV_ROW_SPLIT = 2
QKV_DOT_TILES = 2
DIL_QMAX = 128
DIL_UNROLL = 16
NA_ROWS = 16
NA_BATCH = 4
MOD_STEPS = 8
TAIL_TM = 1024
TAIL_ROW_SPLIT = 2
FF_CHUNK = 256
VMEM_LIMIT = 56 * 1024 * 1024

SAFE_STRIDE = 4

assert all((win // 2) // dil == DIL_QBLOCK for win, dil in DIL_CONFIGS)
assert all(dil == 1 or (dil % SAFE_STRIDE == 0 and dil // SAFE_STRIDE <= SAFE_STRIDE)
           for _, dil in DIL_CONFIGS)


def _params(sem):
    return pltpu.CompilerParams(dimension_semantics=sem, vmem_limit_bytes=VMEM_LIMIT)


def _const_spec(shape):
    nd = len(shape)
    return pl.BlockSpec(shape, lambda *_: (0,) * nd, pipeline_mode=pl.Buffered(1))


def _mod_kernel(ct_ref, w_ref, b_ref, win_ref, o_ref, win_o_ref):
    ct = ct_ref[...]
    act = ct * jax.nn.sigmoid(ct)
    w = w_ref[...]
    rows = [jnp.sum(act[:, b:b + 1] * w, axis=0, keepdims=True) for b in range(ct.shape[1])]
    o_ref[...] = jnp.concatenate(rows, axis=0) + b_ref[...]
    win_o_ref[...] = win_ref[...].astype(bf16)


def _mod(c, w_ada, b_ada, w_in):
    bsz, d = c.shape
    n = w_ada.shape[1]
    steps = MOD_STEPS
    tn = n // steps
    slab = pl.BlockSpec((w_in.shape[0] // steps, w_in.shape[1]), lambda j: (j, 0))
    return pl.pallas_call(
        _mod_kernel,
        grid=(steps,),
        in_specs=[pl.BlockSpec((d, bsz), lambda j: (0, 0)),
                  pl.BlockSpec((d, tn), lambda j: (0, j)),
                  pl.BlockSpec((1, tn), lambda j: (0, j)),
                  slab],
        out_specs=[pl.BlockSpec((bsz, tn), lambda j: (0, j)), slab],
        out_shape=[jax.ShapeDtypeStruct((bsz, n), f32), jax.ShapeDtypeStruct(w_in.shape, bf16)],
        compiler_params=_params(("arbitrary",)),
        name="mod",
    )(c.T, w_ada, b_ada.reshape(1, n), w_in)


def _modulated_norm(x, g, sc, sh):
    ms = jnp.mean(x * x, axis=-1, keepdims=True)
    y = x * lax.rsqrt(ms + EPS) * g
    return y * (1.0 + sc) + sh


_J_QB = 3 * WA // COL_TILE
_J_KB = _J_QB + WB // COL_TILE
_J_VB = _J_KB + WB // COL_TILE
_N_J = W_QKV // COL_TILE


def _qkv_tile_kind(j):
    if j < WA // COL_TILE:
        return 0, False, 1
    if j < 2 * WA // COL_TILE:
        return 1, False, 1
    if j < _J_QB:
        return None, False, 1
    dil = DIL_CONFIGS[(j - _J_QB) % len(DIL_CONFIGS)][1]
    if j < _J_KB:
        return 2, True, dil
    if j < _J_VB:
        return 3, True, dil
    return None, False, dil


_QKV_OUT_WIDTHS = (WA,) * 3 + (WB_OUT,) * (3 * len(DIL_CONFIGS))
_OUT_QA, _OUT_KA, _OUT_VA, _OUT_QB, _OUT_KB, _OUT_VB = 0, 1, 2, 3, 3 + len(DIL_CONFIGS), 3 + 2 * len(DIL_CONFIGS)


def _qkv_tile_out(j, o_refs):
    if j < _J_QB:
        per = WA // COL_TILE
        return o_refs[j // per], (j % per) * COL_TILE
    return o_refs[_OUT_QB + j - _J_QB], 0


def _qkv_kernel(n_cast, x_ref, mod_ref, g1_ref, w_ref, gvec_ref, ones_ref, cos_ref, ssin_ref, *rest):
    n_out = len(_QKV_OUT_WIDTHS)
    cast_in, o_refs = rest[:n_cast], rest[n_cast:n_cast + n_out]
    cast_out = rest[n_cast + n_out:2 * n_cast + n_out]
    h_scr, y_scr, z_scr = rest[2 * n_cast + n_out:]
    for src, dst in zip(cast_in, cast_out):
        dst[...] = src[...].astype(bf16)
    tm = x_ref.shape[1]
    nc = COL_TILE // LANES
    mod = mod_ref[0]
    h_scr[...] = _modulated_norm(x_ref[0], g1_ref[...], mod[1:2], mod[0:1]).astype(bf16)
    head_dim = lax.broadcasted_iota(jnp.int32, (tm // QKV_ROW_SPLIT, COL_TILE), 1) % HEAD_DIM
    first_half = head_dim < ROT_DIM // 2

    wide = {}
    for j in range(_N_J):
        gain_row, rotary, dil = _qkv_tile_kind(j)
        o_ref, c0 = _qkv_tile_out(j, o_refs)
        slot = j % y_scr.shape[0]
        for hb in range(QKV_ROW_SPLIT):
            rows = slice(hb * tm // QKV_ROW_SPLIT, (hb + 1) * tm // QKV_ROW_SPLIT)
            if j % QKV_DOT_TILES == 0:
                wcols = slice(j * COL_TILE, min(j + QKV_DOT_TILES, _N_J) * COL_TILE)
                wide[hb] = jnp.dot(h_scr[rows, :], w_ref[:, wcols], preferred_element_type=f32)
            sub = j % QKV_DOT_TILES
            y = wide[hb][:, sub * COL_TILE:(sub + 1) * COL_TILE]
            if gain_row is not None:
                ms = jnp.dot((y * y).astype(bf16), ones_ref[...], preferred_element_type=f32)
                y = y * lax.rsqrt(ms + EPS) * gvec_ref[gain_row:gain_row + 1, :]
            if rotary:
                cos, ssin = [jnp.concatenate([t_ref[rows, :]] * nc, axis=1) for t_ref in (cos_ref, ssin_ref)]
                half = ROT_DIM // 2
                up = pltpu.roll(y, COL_TILE - half, 1)
                dn = pltpu.roll(y, half, 1)
                y = y * cos + jnp.where(first_half, up, dn) * ssin
            if dil == 1:
                o_ref[0, rows, c0:c0 + COL_TILE] = y.astype(bf16)
            else:
                for c in range(nc):
                    y_scr[slot, c, rows, :] = y[:, c * LANES:(c + 1) * LANES]
        if dil == 1:
            continue
        seg = DIL_TILE // dil
        for c in range(nc):
            ocols = slice(c0 + c * LANES, c0 + (c + 1) * LANES)
            if dil <= SAFE_STRIDE:
                for r in range(dil):
                    rows = y_scr[slot, c, pl.ds(r, seg, stride=dil), :]
                    o_ref[0, r * seg:(r + 1) * seg, ocols] = rows.astype(bf16)
                continue
            hi = dil // SAFE_STRIDE
            sub = DIL_TILE // SAFE_STRIDE
            for r_lo in range(SAFE_STRIDE):
                z_scr[c, r_lo * sub:(r_lo + 1) * sub, :] = y_scr[slot, c, pl.ds(r_lo, sub, stride=SAFE_STRIDE), :]
            for r_lo in range(SAFE_STRIDE):
                for r_hi in range(hi):
                    r = SAFE_STRIDE * r_hi + r_lo
                    rows = z_scr[c, pl.ds(r_lo * sub + r_hi, seg, stride=hi), :]
                    o_ref[0, r * seg:(r + 1) * seg, ocols] = rows.astype(bf16)


def _qkv(x, mod6, g1, w_qkv, gvec, ones_bd, rot_tables, cast_ws, tm=DIL_TILE):
    bsz, s, d = x.shape
    assert tm == DIL_TILE
    nt = s // tm
    steps = bsz * nt
    assert all(w.shape[0] % (BF16_ROWS * steps) == 0 for w in cast_ws)
    slab_specs = [pl.BlockSpec((w.shape[0] // steps, w.shape[1]), lambda b, i: (b * nt + i, 0)) for w in cast_ws]
    outs = pl.pallas_call(
        functools.partial(_qkv_kernel, len(cast_ws)),
        grid=(bsz, nt),
        in_specs=[pl.BlockSpec((1, tm, d), lambda b, i: (b, i, 0)),
                  pl.BlockSpec((1, 6, d), lambda b, i: (b, 0, 0)),
                  _const_spec(g1.shape),
                  pl.BlockSpec((d, W_QKV), lambda b, i: (0, 0), pipeline_mode=pl.Buffered(1)),
                  _const_spec(gvec.shape),
                  _const_spec(ones_bd.shape),
                  ] + [pl.BlockSpec((tm, LANES), lambda b, i: (i, 0))] * len(rot_tables) + slab_specs,
        out_specs=[pl.BlockSpec((1, tm, w), lambda b, i: (b, i, 0)) for w in _QKV_OUT_WIDTHS] + slab_specs,
        out_shape=[jax.ShapeDtypeStruct((bsz, s, w), bf16) for w in _QKV_OUT_WIDTHS]
        + [jax.ShapeDtypeStruct(w.shape, bf16) for w in cast_ws],
        scratch_shapes=[pltpu.VMEM((tm, d), bf16),
                        pltpu.VMEM((2, COL_TILE // LANES, tm, LANES), f32),
                        pltpu.VMEM((COL_TILE // LANES, tm, LANES), f32)],
        compiler_params=_params(("parallel", "parallel")),
        name="qkv",
    )(x, mod6, g1, w_qkv, gvec, ones_bd, *rot_tables, *cast_ws)
    n_out = len(_QKV_OUT_WIDTHS)
    return outs[:n_out], outs[n_out:]


def _pair_mask(nq):
    row = lax.broadcasted_iota(jnp.int32, (2 * nq, LANES), 0)
    lane = lax.broadcasted_iota(jnp.int32, (2 * nq, LANES), 1)
    return (row < nq) == (lane < HEAD_DIM)


def _stack_pair(qp, own_head):
    q2 = jnp.concatenate([qp, qp], axis=0)
    return jnp.where(own_head, q2, jnp.zeros_like(q2))


def _na_window_start(rb, n_rows):
    return jnp.clip(rb * NA_ROWS - NA_KH // 2, 0, n_rows - (NA_ROWS + NA_KH))


def _na_kernel(q_ref, kwin, vwin, tbl_ref, o_ref):
    rb = pl.program_id(1)
    n_rows = pl.num_programs(1) * NA_ROWS
    win_start = _na_window_start(rb, n_rows)
    first_head = lax.broadcasted_iota(jnp.int32, (GRID_W, LANES), 1) < HEAD_DIM
    own_head = _pair_mask(GRID_W)
    nkeys = NA_KH * GRID_W
    npair = NA_HEADS // 2
    pair_rows = 2 * GRID_W

    n_off = 2 * NA_KH - 2
    for a0 in range(0, NA_ROWS, NA_BATCH):
        offs, s_parts = [], []
        for a in range(a0, a0 + NA_BATCH):
            r = rb * NA_ROWS + a
            row_start = jnp.clip(r - NA_KH // 2, 0, n_rows - NA_KH)
            delta = r - row_start
            off = pl.multiple_of((row_start - win_start) * GRID_W, GRID_W)
            offs.append(off)
            row_parts = []
            for hp in range(npair):
                cols = slice(hp * LANES, (hp + 1) * LANES)
                q2 = _stack_pair(q_ref[0, a * GRID_W:(a + 1) * GRID_W, cols], own_head)
                kk = kwin[0, pl.ds(off, nkeys), cols]
                row_parts.append(lax.dot_general(q2, kk, (((1,), (1,)), ((), ())), preferred_element_type=f32))
            bias = jnp.concatenate(
                [jnp.concatenate([tbl_ref[h * n_off + 2 * kp - delta + NA_KH - 1] for kp in range(NA_KH // 2)],
                                 axis=1) for h in range(NA_HEADS)], axis=0)
            s_parts.append(jnp.concatenate(row_parts, axis=0) + bias)
        s = jnp.concatenate(s_parts, axis=0)
        m = jnp.max(s, axis=-1, keepdims=True)
        p = jnp.exp2(s - m)
        inv = 1.0 / jnp.sum(p, axis=-1, keepdims=True)
        pb = p.astype(bf16)
        for k, a in enumerate(range(a0, a0 + NA_BATCH)):
            for hp in range(npair):
                cols = slice(hp * LANES, (hp + 1) * LANES)
                r0 = (k * npair + hp) * pair_rows
                vv = vwin[0, pl.ds(offs[k], nkeys), cols]
                pv = jnp.dot(pb[r0:r0 + pair_rows], vv, preferred_element_type=f32) * inv[r0:r0 + pair_rows]
                o_ref[0, a * GRID_W:(a + 1) * GRID_W, cols] = (
                    jnp.where(first_head, pv[:GRID_W], pv[GRID_W:]).astype(bf16))


def _na(q, k, v, tbl):
    bsz, s, _ = q.shape
    blk = NA_ROWS * GRID_W
    nb = s // blk
    qspec = pl.BlockSpec((1, blk, WA), lambda b, i: (b, i, 0))
    win_tokens = (NA_ROWS + NA_KH) * GRID_W
    n_rows = s // GRID_W
    window = pl.BlockSpec(
        (pl.Element(1), pl.Element(win_tokens), pl.Element(WA)),
        lambda b, i: (b, pl.multiple_of(_na_window_start(i, n_rows) * GRID_W, GRID_W), 0))

    return pl.pallas_call(
        _na_kernel,
        grid=(bsz, nb),
        in_specs=[qspec, window, window, _const_spec(tbl.shape)],
        out_specs=pl.BlockSpec((1, blk, WA), lambda b, i: (b, i, 0)),
        out_shape=jax.ShapeDtypeStruct((bsz, s, WA), bf16),
        compiler_params=_params(("parallel", "parallel")),
        name="na",
    )(q, k, v, tbl)


def _na_bias_table(rpb):
    col = np.arange(GRID_W)
    col_start = np.clip(col - NA_KW // 2, 0, GRID_W - NA_KW)
    cmask = (col[None, :] >= col_start[:, None]) & (col[None, :] < col_start[:, None] + NA_KW)
    col_off = np.clip(col[None, :] - col[:, None] + (NA_KW - 1), 0, 2 * NA_KW - 2)
    n_c = 2 * NA_KW - 1
    sel = np.zeros((2, n_c, GRID_W, 2, GRID_W), np.float32)
    qq, kk = np.meshgrid(col, col, indexing="ij")
    for j in range(2):
        sel[j, col_off, qq, j, kk] = 1.0
    pen = np.broadcast_to(np.where(cmask, 0.0, NEG_INF)[:, None, :], (GRID_W, 2, GRID_W))
    sel = np.concatenate([sel.reshape(2 * n_c, -1), pen.reshape(1, -1).astype(np.float32)], axis=0)
    rows = jnp.stack([rpb[:, :-1], rpb[:, 1:]], axis=2).astype(f32) * LOG2E
    rows = rows.reshape(NA_HEADS * (2 * NA_KH - 2), 2 * n_c)
    rows = jnp.concatenate([rows, jnp.ones((rows.shape[0], 1), f32)], axis=1)
    t = jnp.dot(rows, jnp.asarray(sel), precision=lax.Precision.HIGHEST)
    return t.reshape(NA_HEADS * (2 * NA_KH - 2), GRID_W, 2 * GRID_W)


def _dil_halo_pieces(dil):
    return 1 if DIL_TILE // dil == DIL_QBLOCK else dil


def _dil_kernel(*refs):
    ng = len(DIL_CONFIGS)
    n_in = sum(1 + 2 * (1 + 2 * _dil_halo_pieces(dil)) for _, dil in DIL_CONFIGS)
    in_refs = list(refs[:n_in])
    o_ref = refs[n_in]
    o_scr, m_scr, l_scr = refs[n_in + 1:]
    ti = pl.program_id(1)
    n_tiles = pl.num_programs(1)
    hw = DIL_QBLOCK
    sub = DIL_TILE // SAFE_STRIDE
    npair = DIL_HPG // 2

    def block_masks(qb):
        span = qb + 2 * hw
        qi = lax.broadcasted_iota(jnp.int32, (qb, span), 0)
        kj = lax.broadcasted_iota(jnp.int32, (qb, span), 1)
        band = jnp.where((kj >= qi) & (kj <= qi + 2 * hw), 0.0, NEG_INF).astype(f32)
        kcol = lax.broadcasted_iota(jnp.int32, (1, span), 1)
        pen_lo = jnp.where(kcol < hw, jnp.where(ti == 0, NEG_INF, 0.0), 0.0).astype(f32)
        pen_hi = jnp.where(kcol >= hw + qb, jnp.where(ti == n_tiles - 1, NEG_INF, 0.0), 0.0).astype(f32)
        first_head = lax.broadcasted_iota(jnp.int32, (qb, LANES), 1) < HEAD_DIM
        return band, pen_lo, pen_hi, first_head, _pair_mask(qb)

    masks = {}
    for g, (_, dil) in enumerate(DIL_CONFIGS):
        seg = DIL_TILE // dil
        qb = min(DIL_QMAX, seg)
        if qb not in masks:
            masks[qb] = block_masks(qb)
        band, pen_lo, pen_hi, first_head, own_head = masks[qb]
        nj = seg // qb
        nh = _dil_halo_pieces(dil)
        q_ref = in_refs.pop(0)
        kc, kp, kn = in_refs.pop(0), [in_refs.pop(0) for _ in range(nh)], [in_refs.pop(0) for _ in range(nh)]
        vc, vp, vn = in_refs.pop(0), [in_refs.pop(0) for _ in range(nh)], [in_refs.pop(0) for _ in range(nh)]

        def halo(pieces, r, cols, nh=nh):
            return pieces[r][0, :, cols] if nh > 1 else pieces[0][0, r * hw:(r + 1) * hw, cols]

        def window(refs3, r, j, cols, seg=seg, nj=nj, qb=qb, halo=halo):
            p_refs, c_ref, n_refs = refs3
            base = r * seg
            if 0 < j < nj - 1:
                return c_ref[0, base + j * qb - hw:base + (j + 1) * qb + hw, cols]
            lo = halo(p_refs, r, cols) if j == 0 else c_ref[0, base + j * qb - hw:base + j * qb, cols]
            mid = c_ref[0, base + j * qb:base + (j + 1) * qb, cols]
            hi = (halo(n_refs, r, cols) if j == nj - 1
                  else c_ref[0, base + (j + 1) * qb:base + (j + 1) * qb + hw, cols])
            return jnp.concatenate([lo, mid, hi], axis=0)

        n_blocks = dil * nj
        for t0 in range(0, n_blocks, min(DIL_UNROLL, n_blocks)):
            blocks = [divmod(t, nj) for t in range(t0, t0 + min(DIL_UNROLL, n_blocks))]
            s_parts = []
            for k, (r, j) in enumerate(blocks):
                mask = band
                if j == 0:
                    mask = mask + pen_lo
                if j == nj - 1:
                    mask = mask + pen_hi
                for hp in range(npair):
                    cols = slice(hp * LANES, (hp + 1) * LANES)
                    q2 = _stack_pair(q_ref[0, (t0 + k) * qb:(t0 + k + 1) * qb, cols], own_head)
                    kk = window((kp, kc, kn), r, j, cols)
                    s2 = lax.dot_general(q2, kk, (((1,), (1,)), ((), ())), preferred_element_type=f32)
                    s_parts += [s2[:qb] + mask, s2[qb:] + mask]
            s = jnp.concatenate(s_parts, axis=0)
            m = jnp.max(s, axis=-1, keepdims=True)
            p = jnp.exp2(s - m)
            l = jnp.sum(p, axis=-1, keepdims=True)
            pb = p.astype(bf16)
            for k, (r, j) in enumerate(blocks):
                if dil == 1:
                    store_rows = pl.ds(j * qb, qb)
                else:
                    hi = dil // SAFE_STRIDE
                    start = (r % SAFE_STRIDE) * sub + j * qb * hi + r // SAFE_STRIDE
                    store_rows = pl.ds(start, qb, stride=hi) if hi > 1 else pl.ds(start, qb)
                for hp in range(npair):
                    cols = slice(hp * LANES, (hp + 1) * LANES)
                    r0 = (k * npair + hp) * 2 * qb
                    vv = window((vp, vc, vn), r, j, cols)
                    pv = jnp.dot(pb[r0:r0 + 2 * qb], vv, preferred_element_type=f32)
                    slot = g * npair + hp
                    o_scr[slot, store_rows, :] = jnp.where(first_head, pv[:qb], pv[qb:])
                    m_scr[slot, store_rows, :] = jnp.where(first_head, m[r0:r0 + qb], m[r0 + qb:r0 + 2 * qb])
                    l_scr[slot, store_rows, :] = jnp.where(first_head, l[r0:r0 + qb], l[r0 + qb:r0 + 2 * qb])

    for hp in range(npair):
        for c in range(SAFE_STRIDE):
            def stream(scr, g):
                if DIL_CONFIGS[g][1] == 1:
                    return scr[g * npair + hp, pl.ds(c, sub, stride=SAFE_STRIDE), :]
                return scr[g * npair + hp, c * sub:(c + 1) * sub, :]

            ms = [stream(m_scr, g) for g in range(ng)]
            m_all = functools.reduce(jnp.maximum, ms)
            num = jnp.zeros_like(m_all)
            den = jnp.zeros_like(m_all)
            for g in range(ng):
                w = jnp.exp2(ms[g] - m_all)
                num = num + w * stream(o_scr, g)
                den = den + w * stream(l_scr, g)
            o_scr[ng * npair + hp, pl.ds(c, sub, stride=SAFE_STRIDE), :] = num / den
        o_ref[0, :, hp * LANES:(hp + 1) * LANES] = o_scr[ng * npair + hp].astype(bf16)


def _dil(qs, ks, vs):
    bsz, s, _ = qs[0].shape
    nt = s // DIL_TILE
    ng = len(DIL_CONFIGS)
    qb = DIL_QBLOCK
    per_tile = DIL_TILE // qb
    blk = (1, DIL_TILE, WB_OUT)
    cur = pl.BlockSpec(blk, lambda b, i: (b, i, 0))
    in_specs, args = [], []
    for g, (_, dil) in enumerate(DIL_CONFIGS):
        seg = DIL_TILE // dil
        if _dil_halo_pieces(dil) == 1 and seg == qb:
            prev = [pl.BlockSpec(blk, lambda b, i: (b, jnp.maximum(i - 1, 0), 0))]
            nxt = [pl.BlockSpec(blk, lambda b, i: (b, jnp.minimum(i + 1, nt - 1), 0))]
        else:
            prev = [pl.BlockSpec((1, qb, WB_OUT), lambda b, i, o=(r + 1) * seg // qb - 1:
                                 (b, jnp.maximum((i - 1) * per_tile + o, 0), 0)) for r in range(dil)]
            nxt = [pl.BlockSpec((1, qb, WB_OUT), lambda b, i, o=r * seg // qb:
                                (b, jnp.minimum((i + 1) * per_tile + o, s // qb - 1), 0)) for r in range(dil)]
        in_specs += [cur] + 2 * ([cur] + prev + nxt)
        args += [qs[g]] + [ks[g]] * (1 + len(prev) + len(nxt)) + [vs[g]] * (1 + len(prev) + len(nxt))
    return pl.pallas_call(
        _dil_kernel,
        grid=(bsz, nt),
        in_specs=in_specs,
        out_specs=pl.BlockSpec(blk, lambda b, i: (b, i, 0)),
        out_shape=jax.ShapeDtypeStruct((bsz, s, WB_OUT), bf16),
        scratch_shapes=[pltpu.VMEM(((ng + 1) * DIL_HPG // 2, DIL_TILE, LANES), f32)]
        + [pltpu.VMEM((ng * DIL_HPG // 2, DIL_TILE, LANES), f32)] * 2,
        compiler_params=_params(("parallel", "parallel")),
        name="dil",
    )(*args)


def _tail_kernel(x_ref, oa_ref, ob_ref, mod_ref, g1_ref, g2_ref, bg_ref, wpa_ref, wpb_ref,
                 wo_ref, win_hbm, wout_hbm, *rest):
    d = x_ref.shape[2]
    n_gate = 2 * d // COL_TILE
    wg_refs = rest[:n_gate]
    o_ref, h_scr, h2_scr, m_scr, act_scr, win_ref, wout_ref, ffn_sem = rest[n_gate:]
    first_step = (pl.program_id(0) == 0) & (pl.program_id(1) == 0)
    ffn_copies = (pltpu.make_async_copy(win_hbm, win_ref, ffn_sem.at[0]),
                  pltpu.make_async_copy(wout_hbm, wout_ref, ffn_sem.at[1]))

    @pl.when(first_step)
    def _():
        for cp in ffn_copies:
            cp.start()

    mod = mod_ref[0]
    sh1, sc1, gt1, sh2, sc2, gt2 = [mod[k:k + 1] for k in range(6)]
    tm = x_ref.shape[1]
    for rc in range(TAIL_ROW_SPLIT):
        rows = slice(rc * tm // TAIL_ROW_SPLIT, (rc + 1) * tm // TAIL_ROW_SPLIT)
        x = x_ref[0, rows, :]
        h_scr[rows, :] = _modulated_norm(x, g1_ref[...], sc1, sh1).astype(bf16)
        oa = oa_ref[0, rows, :]
        ob = ob_ref[0, rows, :]
        for n in range(d // COL_TILE):
            ca = slice(n * COL_TILE, (n + 1) * COL_TILE)
            cb = slice(d + n * COL_TILE, d + (n + 1) * COL_TILE)
            wga, wgb = wg_refs[n], wg_refs[n_gate // 2 + n]
            ga = jax.nn.sigmoid(jnp.dot(h_scr[rows, :], wga[...], preferred_element_type=f32) + bg_ref[:, ca])
            gb = jax.nn.sigmoid(jnp.dot(h_scr[rows, :], wgb[...], preferred_element_type=f32) + bg_ref[:, cb])
            pa = jnp.dot(oa, wpa_ref[:, ca], preferred_element_type=f32)
            pb = jnp.dot(ob, wpb_ref[:, ca], preferred_element_type=f32)
            m_scr[rows, ca] = (ga * pa + gb * pb).astype(bf16)
        x1 = x + gt1 * jnp.dot(m_scr[rows, :], wo_ref[...], preferred_element_type=f32)
        h2_scr[rows, :] = _modulated_norm(x1, g2_ref[...], sc2, sh2).astype(bf16)
        if rc == 0:
            @pl.when(first_step)
            def _():
                for cp in ffn_copies:
                    cp.wait()
        for f in range(D_FF // FF_CHUNK):
            ca = slice(f * FF_CHUNK, (f + 1) * FF_CHUNK)
            cu = slice(D_FF + f * FF_CHUNK, D_FF + (f + 1) * FF_CHUNK)
            a = jnp.dot(h2_scr[rows, :], win_ref[:, ca], preferred_element_type=f32)
            u = jnp.dot(h2_scr[rows, :], win_ref[:, cu], preferred_element_type=f32)
            act_scr[rows, ca] = ((a * jax.nn.sigmoid(a)) * u).astype(bf16)
        o_ref[0, rows, :] = x1 + gt2 * jnp.dot(act_scr[rows, :], wout_ref[...], preferred_element_type=f32)


def _tail(x, o_a, o_b, mod6, g1, g2, w_in, bg, wpa, wpb, wo, w_ffn_in, w_ffn_out, tm=TAIL_TM):
    bsz, s, d = x.shape
    tok = lambda w: pl.BlockSpec((1, tm, w), lambda b, i: (b, i, 0))
    consts = [g1, g2, bg, wpa, wpb, wo]
    ffn_ws = [w_ffn_in, w_ffn_out]
    n_gate = 2 * d // COL_TILE
    gate_specs = [pl.BlockSpec((d, COL_TILE), lambda b, i, c=_N_J + n: (0, c), pipeline_mode=pl.Buffered(1))
                  for n in range(n_gate)]
    return pl.pallas_call(
        _tail_kernel,
        grid=(bsz, s // tm),
        in_specs=[tok(d), tok(WA), tok(WB_OUT), pl.BlockSpec((1, 6, d), lambda b, i: (b, 0, 0))]
        + [_const_spec(c.shape) for c in consts] + [pl.BlockSpec(memory_space=pl.ANY)] * len(ffn_ws)
        + gate_specs,
        out_specs=tok(d),
        out_shape=jax.ShapeDtypeStruct((bsz, s, d), f32),
        scratch_shapes=[pltpu.VMEM((tm, d), bf16)] * 3 + [pltpu.VMEM((tm, D_FF), bf16)]
        + [pltpu.VMEM(w.shape, bf16) for w in ffn_ws] + [pltpu.SemaphoreType.DMA((len(ffn_ws),))],
        compiler_params=_params(("arbitrary", "arbitrary")),
        name="tail",
    )(x, o_a, o_b, mod6, *consts, *ffn_ws, *([w_in] * n_gate))


def _rotary_tables(s):
    half = ROT_DIM // 2
    inv_freq = ROPE_THETA ** (-(np.arange(half, dtype=np.float64) * 2.0) / ROT_DIM)
    ang = np.arange(s, dtype=np.float64)[:, None] * inv_freq[None, :]
    cos, sin = np.cos(ang), np.sin(ang)
    rest = HEAD_DIM - ROT_DIM
    cos_h = np.concatenate([cos, cos, np.ones((s, rest))], axis=1)
    ssin_h = np.concatenate([-sin, sin, np.zeros((s, rest))], axis=1)
    rep = LANES // HEAD_DIM
    return (jnp.asarray(np.tile(cos_h, (1, rep)), dtype=f32), jnp.asarray(np.tile(ssin_h, (1, rep)), dtype=f32))


def _layer(x, mod6, g_norm1, g_norm2, w_in, b_gate, g_qa, g_ka, g_qb, g_kb, rpb,
           w_proj_a, w_proj_b, w_o, w_ffn_in, w_ffn_out):
    bsz, s, d = x.shape
    assert d == D_MODEL and w_in.shape == (d, W_QKV + 2 * d) and w_ffn_in.shape == (d, 2 * D_FF)
    assert s % DIL_TILE == 0 and s % (NA_ROWS * GRID_W) == 0 and s % TAIL_TM == 0
    assert s // GRID_W >= NA_ROWS + NA_KH
    scale = HEAD_DIM ** -0.5 * LOG2E
    rep = COL_TILE // HEAD_DIM
    gvec = jnp.stack([jnp.tile(g_qa * scale, rep), jnp.tile(g_ka, rep),
                      jnp.tile(g_qb * scale, rep), jnp.tile(g_kb, rep)]).astype(f32)
    hid = jnp.arange(COL_TILE) // HEAD_DIM
    ones_bd = ((hid[:, None] == hid[None, :]).astype(f32) * (1.0 / HEAD_DIM)).astype(bf16)
    rot_tables = _rotary_tables(s)
    g1 = g_norm1.reshape(1, d)
    g2 = g_norm2.reshape(1, d)

    qkv, tail_ws = _qkv(x, mod6, g1, w_in, gvec, ones_bd, rot_tables,
                        [w_proj_a, w_proj_b, w_o, w_ffn_in, w_ffn_out])
    ng = len(DIL_CONFIGS)
    o_a = _na(qkv[_OUT_QA], qkv[_OUT_KA], qkv[_OUT_VA], _na_bias_table(rpb))
    o_b = _dil(qkv[_OUT_QB:_OUT_QB + ng], qkv[_OUT_KB:_OUT_KB + ng], qkv[_OUT_VB:_OUT_VB + ng])

    return _tail(x, o_a, o_b, mod6, g1, g2, w_in, b_gate.reshape(1, 2 * d), *tail_ws)


def kernel(x, c, w_ada, b_ada, g_norm1, g_norm2, w_in, b_gate, g_qa, g_ka, g_qb, g_kb, rpb,
           w_proj_a, w_proj_b, w_o, w_ffn_in, w_ffn_out):
    depth = w_ada.shape[0]
    bsz, d = c.shape
    for l in range(depth):
        mod, w_in_bf = _mod(c, w_ada[l], b_ada[l], w_in[l])
        x = _layer(x, mod.reshape(bsz, 6, d), g_norm1[l], g_norm2[l], w_in_bf, b_gate[l], g_qa[l], g_ka[l], g_qb[l],
                   g_kb[l], rpb[l], w_proj_a[l], w_proj_b[l], w_o[l], w_ffn_in[l], w_ffn_out[l])
    return x
```

```python
import functools

import jax
import jax.numpy as jnp
import numpy as np
from jax import lax
from jax.experimental import pallas as pl
from jax.experimental.pallas import tpu as pltpu

f32 = jnp.float32
bf16 = jnp.bfloat16

D_MODEL = 1024
HEAD_DIM = 64
GRID_W = 64
NA_HEADS = 8
NA_KH = 8
NA_KW = 16
DIL_CONFIGS = ((128, 1), (512, 4), (2048, 16))
DIL_HPG = 4
DIL_HEADS = DIL_HPG * len(DIL_CONFIGS)
DIL_QBLOCK = 64
ROT_DIM = HEAD_DIM // 4
ROPE_THETA = 500000.0
D_FF = -(-8 * D_MODEL // (3 * 256)) * 256
EPS = 1e-6
NEG_INF = -1e30
LOG2E = 1.4426950408889634
WA = NA_HEADS * HEAD_DIM
WB = DIL_HEADS * HEAD_DIM
WB_OUT = DIL_HPG * HEAD_DIM
W_QKV = 3 * WA + 3 * WB

LANES = 128
BF16_ROWS = 16
COL_TILE = 256
DIL_TILE = 1024
QKV_ROW_SPLIT = 2
QKV_DOT_TILES = 2
DIL_QMAX = 128
DIL_UNROLL = 16
NA_ROWS = 16
NA_BATCH = 4
MOD_STEPS = 8
TAIL_TM = 1024
TAIL_ROW_SPLIT = 2
FF_CHUNK = 256
VMEM_LIMIT = 56 * 1024 * 1024

SAFE_STRIDE = 4

assert all((win // 2) // dil == DIL_QBLOCK for win, dil in DIL_CONFIGS)
assert all(dil == 1 or (dil % SAFE_STRIDE == 0 and dil // SAFE_STRIDE <= SAFE_STRIDE)
           for _, dil in DIL_CONFIGS)


def _params(sem):
    return pltpu.CompilerParams(dimension_semantics=sem, vmem_limit_bytes=VMEM_LIMIT)


def _const_spec(shape):
    nd = len(shape)
    return pl.BlockSpec(shape, lambda *_: (0,) * nd, pipeline_mode=pl.Buffered(1))


def _mod_kernel(ct_ref, w_ref, b_ref, win_ref, o_ref, win_o_ref):
    ct = ct_ref[...]
    act = ct * jax.nn.sigmoid(ct)
    w = w_ref[...]
    rows = [jnp.sum(act[:, b:b + 1] * w, axis=0, keepdims=True) for b in range(ct.shape[1])]
    o_ref[...] = jnp.concatenate(rows, axis=0) + b_ref[...]
    win_o_ref[...] = win_ref[...].astype(bf16)


def _mod(c, w_ada, b_ada, w_in):
    bsz, d = c.shape
    n = w_ada.shape[1]
    steps = MOD_STEPS
    tn = n // steps
    slab = pl.BlockSpec((w_in.shape[0] // steps, w_in.shape[1]), lambda j: (j, 0))
    return pl.pallas_call(
        _mod_kernel,
        grid=(steps,),
        in_specs=[pl.BlockSpec((d, bsz), lambda j: (0, 0)),
                  pl.BlockSpec((d, tn), lambda j: (0, j)),
                  pl.BlockSpec((1, tn), lambda j: (0, j)),
                  slab],
        out_specs=[pl.BlockSpec((bsz, tn), lambda j: (0, j)), slab],
        out_shape=[jax.ShapeDtypeStruct((bsz, n), f32), jax.ShapeDtypeStruct(w_in.shape, bf16)],
        compiler_params=_params(("arbitrary",)),
        name="mod",
    )(c.T, w_ada, b_ada.reshape(1, n), w_in)


def _modulated_norm(x, g, sc, sh):
    ms = jnp.mean(x * x, axis=-1, keepdims=True)
    y = x * lax.rsqrt(ms + EPS) * g
    return y * (1.0 + sc) + sh


_J_QB = 3 * WA // COL_TILE
_J_KB = _J_QB + WB // COL_TILE
_J_VB = _J_KB + WB // COL_TILE
_N_J = W_QKV // COL_TILE


def _qkv_tile_kind(j):
    if j < WA // COL_TILE:
        return 0, False, 1
    if j < 2 * WA // COL_TILE:
        return 1, False, 1
    if j < _J_QB:
        return None, False, 1
    dil = DIL_CONFIGS[(j - _J_QB) % len(DIL_CONFIGS)][1]
    if j < _J_KB:
        return 2, True, dil
    if j < _J_VB:
        return 3, True, dil
    return None, False, dil


_QKV_OUT_WIDTHS = (WA,) * 3 + (WB_OUT,) * (3 * len(DIL_CONFIGS))
_OUT_QA, _OUT_KA, _OUT_VA, _OUT_QB, _OUT_KB, _OUT_VB = 0, 1, 2, 3, 3 + len(DIL_CONFIGS), 3 + 2 * len(DIL_CONFIGS)


def _qkv_tile_out(j, o_refs):
    if j < _J_QB:
        per = WA // COL_TILE
        return o_refs[j // per], (j % per) * COL_TILE
    return o_refs[_OUT_QB + j - _J_QB], 0


def _qkv_kernel(n_cast, x_ref, mod_ref, g1_ref, w_ref, gvec_ref, ones_ref, cos_ref, ssin_ref, *rest):
    n_out = len(_QKV_OUT_WIDTHS)
    cast_in, o_refs = rest[:n_cast], rest[n_cast:n_cast + n_out]
    cast_out = rest[n_cast + n_out:2 * n_cast + n_out]
    h_scr, y_scr, z_scr = rest[2 * n_cast + n_out:]
    for src, dst in zip(cast_in, cast_out):
        dst[...] = src[...].astype(bf16)
    tm = x_ref.shape[1]
    nc = COL_TILE // LANES
    mod = mod_ref[0]
    h_scr[...] = _modulated_norm(x_ref[0], g1_ref[...], mod[1:2], mod[0:1]).astype(bf16)
    head_dim = lax.broadcasted_iota(jnp.int32, (tm // QKV_ROW_SPLIT, COL_TILE), 1) % HEAD_DIM
    first_half = head_dim < ROT_DIM // 2

    wide = {}
    for j in range(_N_J):
        gain_row, rotary, dil = _qkv_tile_kind(j)
        o_ref, c0 = _qkv_tile_out(j, o_refs)
        slot = j % y_scr.shape[0]
        for hb in range(QKV_ROW_SPLIT):
            rows = slice(hb * tm // QKV_ROW_SPLIT, (hb + 1) * tm // QKV_ROW_SPLIT)
            if j % QKV_DOT_TILES == 0:
                wcols = slice(j * COL_TILE, min(j + QKV_DOT_TILES, _N_J) * COL_TILE)
                wide[hb] = jnp.dot(h_scr[rows, :], w_ref[:, wcols], preferred_element_type=f32)
            sub = j % QKV_DOT_TILES
            y = wide[hb][:, sub * COL_TILE:(sub + 1) * COL_TILE]
            if gain_row is not None:
                ms = jnp.dot((y * y).astype(bf16), ones_ref[...], preferred_element_type=f32)
                y = y * lax.rsqrt(ms + EPS) * gvec_ref[gain_row:gain_row + 1, :]
            if rotary:
                cos, ssin = [jnp.concatenate([t_ref[rows, :]] * nc, axis=1) for t_ref in (cos_ref, ssin_ref)]
                half = ROT_DIM // 2
                up = pltpu.roll(y, COL_TILE - half, 1)
                dn = pltpu.roll(y, half, 1)
                y = y * cos + jnp.where(first_half, up, dn) * ssin
            if dil == 1:
                o_ref[0, rows, c0:c0 + COL_TILE] = y.astype(bf16)
            else:
                for c in range(nc):
                    y_scr[slot, c, rows, :] = y[:, c * LANES:(c + 1) * LANES]
        if dil == 1:
            continue
        seg = DIL_TILE // dil
        for c in range(nc):
            ocols = slice(c0 + c * LANES, c0 + (c + 1) * LANES)
            if dil <= SAFE_STRIDE:
                for r in range(dil):
                    rows = y_scr[slot, c, pl.ds(r, seg, stride=dil), :]
                    o_ref[0, r * seg:(r + 1) * seg, ocols] = rows.astype(bf16)
                continue
            hi = dil // SAFE_STRIDE
            sub = DIL_TILE // SAFE_STRIDE
            for r_lo in range(SAFE_STRIDE):
                z_scr[c, r_lo * sub:(r_lo + 1) * sub, :] = y_scr[slot, c, pl.ds(r_lo, sub, stride=SAFE_STRIDE), :]
            for r_lo in range(SAFE_STRIDE):
                for r_hi in range(hi):
                    r = SAFE_STRIDE * r_hi + r_lo
                    rows = z_scr[c, pl.ds(r_lo * sub + r_hi, seg, stride=hi), :]
                    o_ref[0, r * seg:(r + 1) * seg, ocols] = rows.astype(bf16)


def _qkv(x, mod6, g1, w_qkv, gvec, ones_bd, rot_tables, cast_ws, tm=DIL_TILE):
    bsz, s, d = x.shape
    assert tm == DIL_TILE
    nt = s // tm
    steps = bsz * nt
    assert all(w.shape[0] % (BF16_ROWS * steps) == 0 for w in cast_ws)
    slab_specs = [pl.BlockSpec((w.shape[0] // steps, w.shape[1]), lambda b, i: (b * nt + i, 0)) for w in cast_ws]
    outs = pl.pallas_call(
        functools.partial(_qkv_kernel, len(cast_ws)),
        grid=(bsz, nt),
        in_specs=[pl.BlockSpec((1, tm, d), lambda b, i: (b, i, 0)),
                  pl.BlockSpec((1, 6, d), lambda b, i: (b, 0, 0)),
                  _const_spec(g1.shape),
                  pl.BlockSpec((d, W_QKV), lambda b, i: (0, 0), pipeline_mode=pl.Buffered(1)),
                  _const_spec(gvec.shape),
                  _const_spec(ones_bd.shape),
                  ] + [pl.BlockSpec((tm, LANES), lambda b, i: (i, 0))] * len(rot_tables) + slab_specs,
        out_specs=[pl.BlockSpec((1, tm, w), lambda b, i: (b, i, 0)) for w in _QKV_OUT_WIDTHS] + slab_specs,
        out_shape=[jax.ShapeDtypeStruct((bsz, s, w), bf16) for w in _QKV_OUT_WIDTHS]
        + [jax.ShapeDtypeStruct(w.shape, bf16) for w in cast_ws],
        scratch_shapes=[pltpu.VMEM((tm, d), bf16),
                        pltpu.VMEM((2, COL_TILE // LANES, tm, LANES), f32),
                        pltpu.VMEM((COL_TILE // LANES, tm, LANES), f32)],
        compiler_params=_params(("parallel", "parallel")),
        name="qkv",
    )(x, mod6, g1, w_qkv, gvec, ones_bd, *rot_tables, *cast_ws)
    n_out = len(_QKV_OUT_WIDTHS)
    return outs[:n_out], outs[n_out:]


def _pair_mask(nq):
    row = lax.broadcasted_iota(jnp.int32, (2 * nq, LANES), 0)
    lane = lax.broadcasted_iota(jnp.int32, (2 * nq, LANES), 1)
    return (row < nq) == (lane < HEAD_DIM)


def _stack_pair(qp, own_head):
    q2 = jnp.concatenate([qp, qp], axis=0)
    return jnp.where(own_head, q2, jnp.zeros_like(q2))


def _na_window_start(rb, n_rows):
    return jnp.clip(rb * NA_ROWS - NA_KH // 2, 0, n_rows - (NA_ROWS + NA_KH))


def _na_kernel(q_ref, kwin, vwin, tbl_ref, o_ref):
    rb = pl.program_id(1)
    n_rows = pl.num_programs(1) * NA_ROWS
    win_start = _na_window_start(rb, n_rows)
    first_head = lax.broadcasted_iota(jnp.int32, (GRID_W, LANES), 1) < HEAD_DIM
    own_head = _pair_mask(GRID_W)
    nkeys = NA_KH * GRID_W
    npair = NA_HEADS // 2
    pair_rows = 2 * GRID_W

    n_off = 2 * NA_KH - 2
    for a0 in range(0, NA_ROWS, NA_BATCH):
        offs, s_parts = [], []
        for a in range(a0, a0 + NA_BATCH):
            r = rb * NA_ROWS + a
            row_start = jnp.clip(r - NA_KH // 2, 0, n_rows - NA_KH)
            delta = r - row_start
            off = pl.multiple_of((row_start - win_start) * GRID_W, GRID_W)
            offs.append(off)
            row_parts = []
            for hp in range(npair):
                cols = slice(hp * LANES, (hp + 1) * LANES)
                q2 = _stack_pair(q_ref[0, a * GRID_W:(a + 1) * GRID_W, cols], own_head)
                kk = kwin[0, pl.ds(off, nkeys), cols]
                row_parts.append(lax.dot_general(q2, kk, (((1,), (1,)), ((), ())), preferred_element_type=f32))
            bias = jnp.concatenate(
                [jnp.concatenate([tbl_ref[h * n_off + 2 * kp - delta + NA_KH - 1] for kp in range(NA_KH // 2)],
                                 axis=1) for h in range(NA_HEADS)], axis=0)
            s_parts.append(jnp.concatenate(row_parts, axis=0) + bias)
        s = jnp.concatenate(s_parts, axis=0)
        m = jnp.max(s, axis=-1, keepdims=True)
        p = jnp.exp2(s - m)
        inv = 1.0 / jnp.sum(p, axis=-1, keepdims=True)
        pb = p.astype(bf16)
        for k, a in enumerate(range(a0, a0 + NA_BATCH)):
            for hp in range(npair):
                cols = slice(hp * LANES, (hp + 1) * LANES)
                r0 = (k * npair + hp) * pair_rows
                vv = vwin[0, pl.ds(offs[k], nkeys), cols]
                pv = jnp.dot(pb[r0:r0 + pair_rows], vv, preferred_element_type=f32) * inv[r0:r0 + pair_rows]
                o_ref[0, a * GRID_W:(a + 1) * GRID_W, cols] = (
                    jnp.where(first_head, pv[:GRID_W], pv[GRID_W:]).astype(bf16))


def _na(q, k, v, tbl):
    bsz, s, _ = q.shape
    blk = NA_ROWS * GRID_W
    nb = s // blk
    qspec = pl.BlockSpec((1, blk, WA), lambda b, i: (b, i, 0))
    win_tokens = (NA_ROWS + NA_KH) * GRID_W
    n_rows = s // GRID_W
    window = pl.BlockSpec(
        (pl.Element(1), pl.Element(win_tokens), pl.Element(WA)),
        lambda b, i: (b, pl.multiple_of(_na_window_start(i, n_rows) * GRID_W, GRID_W), 0))

    return pl.pallas_call(
        _na_kernel,
        grid=(bsz, nb),
        in_specs=[qspec, window, window, _const_spec(tbl.shape)],
        out_specs=pl.BlockSpec((1, blk, WA), lambda b, i: (b, i, 0)),
        out_shape=jax.ShapeDtypeStruct((bsz, s, WA), bf16),
        compiler_params=_params(("parallel", "parallel")),
        name="na",
    )(q, k, v, tbl)


def _na_bias_table(rpb):
    col = np.arange(GRID_W)
    col_start = np.clip(col - NA_KW // 2, 0, GRID_W - NA_KW)
    cmask = (col[None, :] >= col_start[:, None]) & (col[None, :] < col_start[:, None] + NA_KW)
    col_off = np.clip(col[None, :] - col[:, None] + (NA_KW - 1), 0, 2 * NA_KW - 2)
    n_c = 2 * NA_KW - 1
    sel = np.zeros((2, n_c, GRID_W, 2, GRID_W), np.float32)
    qq, kk = np.meshgrid(col, col, indexing="ij")
    for j in range(2):
        sel[j, col_off, qq, j, kk] = 1.0
    pen = np.broadcast_to(np.where(cmask, 0.0, NEG_INF)[:, None, :], (GRID_W, 2, GRID_W))
    sel = np.concatenate([sel.reshape(2 * n_c, -1), pen.reshape(1, -1).astype(np.float32)], axis=0)
    rows = jnp.stack([rpb[:, :-1], rpb[:, 1:]], axis=2).astype(f32) * LOG2E
    rows = rows.reshape(NA_HEADS * (2 * NA_KH - 2), 2 * n_c)
    rows = jnp.concatenate([rows, jnp.ones((rows.shape[0], 1), f32)], axis=1)
    t = jnp.dot(rows, jnp.asarray(sel), precision=lax.Precision.HIGHEST)
    return t.reshape(NA_HEADS * (2 * NA_KH - 2), GRID_W, 2 * GRID_W)


def _dil_halo_pieces(dil):
    return 1 if DIL_TILE // dil == DIL_QBLOCK else dil


def _dil_kernel(*refs):
    ng = len(DIL_CONFIGS)
    n_in = sum(1 + 2 * (1 + 2 * _dil_halo_pieces(dil)) for _, dil in DIL_CONFIGS)
    in_refs = list(refs[:n_in])
    o_ref = refs[n_in]
    o_scr, m_scr, l_scr = refs[n_in + 1:]
    ti = pl.program_id(1)
    n_tiles = pl.num_programs(1)
    hw = DIL_QBLOCK
    sub = DIL_TILE // SAFE_STRIDE
    npair = DIL_HPG // 2

    def block_masks(qb):
        span = qb + 2 * hw
        qi = lax.broadcasted_iota(jnp.int32, (qb, span), 0)
        kj = lax.broadcasted_iota(jnp.int32, (qb, span), 1)
        band = jnp.where((kj >= qi) & (kj <= qi + 2 * hw), 0.0, NEG_INF).astype(f32)
        kcol = lax.broadcasted_iota(jnp.int32, (1, span), 1)
        pen_lo = jnp.where(kcol < hw, jnp.where(ti == 0, NEG_INF, 0.0), 0.0).astype(f32)
        pen_hi = jnp.where(kcol >= hw + qb, jnp.where(ti == n_tiles - 1, NEG_INF, 0.0), 0.0).astype(f32)
        first_head = lax.broadcasted_iota(jnp.int32, (qb, LANES), 1) < HEAD_DIM
        return band, pen_lo, pen_hi, first_head, _pair_mask(qb)

    masks = {}
    for g, (_, dil) in enumerate(DIL_CONFIGS):
        seg = DIL_TILE // dil
        qb = min(DIL_QMAX, seg)
        if qb not in masks:
            masks[qb] = block_masks(qb)
        band, pen_lo, pen_hi, first_head, own_head = masks[qb]
        nj = seg // qb
        nh = _dil_halo_pieces(dil)
        q_ref = in_refs.pop(0)
        kc, kp, kn = in_refs.pop(0), [in_refs.pop(0) for _ in range(nh)], [in_refs.pop(0) for _ in range(nh)]
        vc, vp, vn = in_refs.pop(0), [in_refs.pop(0) for _ in range(nh)], [in_refs.pop(0) for _ in range(nh)]

        def halo(pieces, r, cols, nh=nh):
            return pieces[r][0, :, cols] if nh > 1 else pieces[0][0, r * hw:(r + 1) * hw, cols]

        def window(refs3, r, j, cols, seg=seg, nj=nj, qb=qb, halo=halo):
            p_refs, c_ref, n_refs = refs3
            base = r * seg
            if 0 < j < nj - 1:
                return c_ref[0, base + j * qb - hw:base + (j + 1) * qb + hw, cols]
            lo = halo(p_refs, r, cols) if j == 0 else c_ref[0, base + j * qb - hw:base + j * qb, cols]
            mid = c_ref[0, base + j * qb:base + (j + 1) * qb, cols]
            hi = (halo(n_refs, r, cols) if j == nj - 1
                  else c_ref[0, base + (j + 1) * qb:base + (j + 1) * qb + hw, cols])
            return jnp.concatenate([lo, mid, hi], axis=0)

        n_blocks = dil * nj
        for t0 in range(0, n_blocks, min(DIL_UNROLL, n_blocks)):
            blocks = [divmod(t, nj) for t in range(t0, t0 + min(DIL_UNROLL, n_blocks))]
            s_parts = []
            for k, (r, j) in enumerate(blocks):
                mask = band
                if j == 0:
                    mask = mask + pen_lo
                if j == nj - 1:
                    mask = mask + pen_hi
                for hp in range(npair):
                    cols = slice(hp * LANES, (hp + 1) * LANES)
                    q2 = _stack_pair(q_ref[0, (t0 + k) * qb:(t0 + k + 1) * qb, cols], own_head)
                    kk = window((kp, kc, kn), r, j, cols)
                    s2 = lax.dot_general(q2, kk, (((1,), (1,)), ((), ())), preferred_element_type=f32)
                    s_parts += [s2[:qb] + mask, s2[qb:] + mask]
            s = jnp.concatenate(s_parts, axis=0)
            m = jnp.max(s, axis=-1, keepdims=True)
            p = jnp.exp2(s - m)
            l = jnp.sum(p, axis=-1, keepdims=True)
            pb = p.astype(bf16)
            for k, (r, j) in enumerate(blocks):
                if dil == 1:
                    store_rows = pl.ds(j * qb, qb)
                else:
                    hi = dil // SAFE_STRIDE
                    start = (r % SAFE_STRIDE) * sub + j * qb * hi + r // SAFE_STRIDE
                    store_rows = pl.ds(start, qb, stride=hi) if hi > 1 else pl.ds(start, qb)
                for hp in range(npair):
                    cols = slice(hp * LANES, (hp + 1) * LANES)
                    r0 = (k * npair + hp) * 2 * qb
                    vv = window((vp, vc, vn), r, j, cols)
                    pv = jnp.dot(pb[r0:r0 + 2 * qb], vv, preferred_element_type=f32)
                    slot = g * npair + hp
                    o_scr[slot, store_rows, :] = jnp.where(first_head, pv[:qb], pv[qb:])
                    m_scr[slot, store_rows, :] = jnp.where(first_head, m[r0:r0 + qb], m[r0 + qb:r0 + 2 * qb])
                    l_scr[slot, store_rows, :] = jnp.where(first_head, l[r0:r0 + qb], l[r0 + qb:r0 + 2 * qb])

    for hp in range(npair):
        for c in range(SAFE_STRIDE):
            def stream(scr, g):
                if DIL_CONFIGS[g][1] == 1:
                    return scr[g * npair + hp, pl.ds(c, sub, stride=SAFE_STRIDE), :]
                return scr[g * npair + hp, c * sub:(c + 1) * sub, :]

            ms = [stream(m_scr, g) for g in range(ng)]
            m_all = functools.reduce(jnp.maximum, ms)
            num = jnp.zeros_like(m_all)
            den = jnp.zeros_like(m_all)
            for g in range(ng):
                w = jnp.exp2(ms[g] - m_all)
                num = num + w * stream(o_scr, g)
                den = den + w * stream(l_scr, g)
            o_scr[ng * npair + hp, pl.ds(c, sub, stride=SAFE_STRIDE), :] = num / den
        o_ref[0, :, hp * LANES:(hp + 1) * LANES] = o_scr[ng * npair + hp].astype(bf16)


def _dil(qs, ks, vs):
    bsz, s, _ = qs[0].shape
    nt = s // DIL_TILE
    ng = len(DIL_CONFIGS)
    qb = DIL_QBLOCK
    per_tile = DIL_TILE // qb
    blk = (1, DIL_TILE, WB_OUT)
    cur = pl.BlockSpec(blk, lambda b, i: (b, i, 0))
    in_specs, args = [], []
    for g, (_, dil) in enumerate(DIL_CONFIGS):
        seg = DIL_TILE // dil
        if _dil_halo_pieces(dil) == 1 and seg == qb:
            prev = [pl.BlockSpec(blk, lambda b, i: (b, jnp.maximum(i - 1, 0), 0))]
            nxt = [pl.BlockSpec(blk, lambda b, i: (b, jnp.minimum(i + 1, nt - 1), 0))]
        else:
            prev = [pl.BlockSpec((1, qb, WB_OUT), lambda b, i, o=(r + 1) * seg // qb - 1:
                                 (b, jnp.maximum((i - 1) * per_tile + o, 0), 0)) for r in range(dil)]
            nxt = [pl.BlockSpec((1, qb, WB_OUT), lambda b, i, o=r * seg // qb:
                                (b, jnp.minimum((i + 1) * per_tile + o, s // qb - 1), 0)) for r in range(dil)]
        in_specs += [cur] + 2 * ([cur] + prev + nxt)
        args += [qs[g]] + [ks[g]] * (1 + len(prev) + len(nxt)) + [vs[g]] * (1 + len(prev) + len(nxt))
    return pl.pallas_call(
        _dil_kernel,
        grid=(bsz, nt),
        in_specs=in_specs,
        out_specs=pl.BlockSpec(blk, lambda b, i: (b, i, 0)),
        out_shape=jax.ShapeDtypeStruct((bsz, s, WB_OUT), bf16),
        scratch_shapes=[pltpu.VMEM(((ng + 1) * DIL_HPG // 2, DIL_TILE, LANES), f32)]
        + [pltpu.VMEM((ng * DIL_HPG // 2, DIL_TILE, LANES), f32)] * 2,
        compiler_params=_params(("parallel", "parallel")),
        name="dil",
    )(*args)


def _tail_kernel(x_ref, oa_ref, ob_ref, mod_ref, g1_ref, g2_ref, bg_ref, wpa_ref, wpb_ref,
                 wo_ref, win_hbm, wout_hbm, *rest):
    d = x_ref.shape[2]
    n_gate = 2 * d // COL_TILE
    wg_refs = rest[:n_gate]
    o_ref, h_scr, h2_scr, m_scr, act_scr, win_ref, wout_ref, ffn_sem = rest[n_gate:]
    first_step = (pl.program_id(0) == 0) & (pl.program_id(1) == 0)
    ffn_copies = (pltpu.make_async_copy(win_hbm, win_ref, ffn_sem.at[0]),
                  pltpu.make_async_copy(wout_hbm, wout_ref, ffn_sem.at[1]))

    @pl.when(first_step)
    def _():
        for k, cp in enumerate(ffn_copies):
            cp.start(priority=k)

    mod = mod_ref[0]
    sh1, sc1, gt1, sh2, sc2, gt2 = [mod[k:k + 1] for k in range(6)]
    tm = x_ref.shape[1]
    for rc in range(TAIL_ROW_SPLIT):
        rows = slice(rc * tm // TAIL_ROW_SPLIT, (rc + 1) * tm // TAIL_ROW_SPLIT)
        x = x_ref[0, rows, :]
        h_scr[rows, :] = _modulated_norm(x, g1_ref[...], sc1, sh1).astype(bf16)
        oa = oa_ref[0, rows, :]
        ob = ob_ref[0, rows, :]
        for n in range(d // COL_TILE):
            ca = slice(n * COL_TILE, (n + 1) * COL_TILE)
            cb = slice(d + n * COL_TILE, d + (n + 1) * COL_TILE)
            wga, wgb = wg_refs[n], wg_refs[n_gate // 2 + n]
            ga = jax.nn.sigmoid(jnp.dot(h_scr[rows, :], wga[...], preferred_element_type=f32) + bg_ref[:, ca])
            gb = jax.nn.sigmoid(jnp.dot(h_scr[rows, :], wgb[...], preferred_element_type=f32) + bg_ref[:, cb])
            pa = jnp.dot(oa, wpa_ref[:, ca], preferred_element_type=f32)
            pb = jnp.dot(ob, wpb_ref[:, ca], preferred_element_type=f32)
            m_scr[rows, ca] = (ga * pa + gb * pb).astype(bf16)
        x1 = x + gt1 * jnp.dot(m_scr[rows, :], wo_ref[...], preferred_element_type=f32)
        h2_scr[rows, :] = _modulated_norm(x1, g2_ref[...], sc2, sh2).astype(bf16)
        if rc == 0:
            @pl.when(first_step)
            def _():
                for cp in ffn_copies:
                    cp.wait()
        for f in range(D_FF // FF_CHUNK):
            ca = slice(f * FF_CHUNK, (f + 1) * FF_CHUNK)
            cu = slice(D_FF + f * FF_CHUNK, D_FF + (f + 1) * FF_CHUNK)
            a = jnp.dot(h2_scr[rows, :], win_ref[:, ca], preferred_element_type=f32)
            u = jnp.dot(h2_scr[rows, :], win_ref[:, cu], preferred_element_type=f32)
            act_scr[rows, ca] = ((a * jax.nn.sigmoid(a)) * u).astype(bf16)
        o_ref[0, rows, :] = x1 + gt2 * jnp.dot(act_scr[rows, :], wout_ref[...], preferred_element_type=f32)


def _tail(x, o_a, o_b, mod6, g1, g2, w_in, bg, wpa, wpb, wo, w_ffn_in, w_ffn_out, tm=TAIL_TM):
    bsz, s, d = x.shape
    tok = lambda w: pl.BlockSpec((1, tm, w), lambda b, i: (b, i, 0))
    consts = [g1, g2, bg, wpa, wpb, wo]
    ffn_ws = [w_ffn_in, w_ffn_out]
    n_gate = 2 * d // COL_TILE
    gate_specs = [pl.BlockSpec((d, COL_TILE), lambda b, i, c=_N_J + n: (0, c), pipeline_mode=pl.Buffered(1))
                  for n in range(n_gate)]
    return pl.pallas_call(
        _tail_kernel,
        grid=(bsz, s // tm),
        in_specs=[tok(d), tok(WA), tok(WB_OUT), pl.BlockSpec((1, 6, d), lambda b, i: (b, 0, 0))]
        + [_const_spec(c.shape) for c in consts] + [pl.BlockSpec(memory_space=pl.ANY)] * len(ffn_ws)
        + gate_specs,
        out_specs=tok(d),
        out_shape=jax.ShapeDtypeStruct((bsz, s, d), f32),
        scratch_shapes=[pltpu.VMEM((tm, d), bf16)] * 3 + [pltpu.VMEM((tm, D_FF), bf16)]
        + [pltpu.VMEM(w.shape, bf16) for w in ffn_ws] + [pltpu.SemaphoreType.DMA((len(ffn_ws),))],
        compiler_params=_params(("arbitrary", "arbitrary")),
        name="tail",
    )(x, o_a, o_b, mod6, *consts, *ffn_ws, *([w_in] * n_gate))


def _rotary_tables(s):
    half = ROT_DIM // 2
    inv_freq = ROPE_THETA ** (-(np.arange(half, dtype=np.float64) * 2.0) / ROT_DIM)
    ang = np.arange(s, dtype=np.float64)[:, None] * inv_freq[None, :]
    cos, sin = np.cos(ang), np.sin(ang)
    rest = HEAD_DIM - ROT_DIM
    cos_h = np.concatenate([cos, cos, np.ones((s, rest))], axis=1)
    ssin_h = np.concatenate([-sin, sin, np.zeros((s, rest))], axis=1)
    rep = LANES // HEAD_DIM
    return (jnp.asarray(np.tile(cos_h, (1, rep)), dtype=f32), jnp.asarray(np.tile(ssin_h, (1, rep)), dtype=f32))


def _layer(x, mod6, g_norm1, g_norm2, w_in, b_gate, g_qa, g_ka, g_qb, g_kb, rpb,
           w_proj_a, w_proj_b, w_o, w_ffn_in, w_ffn_out):
    bsz, s, d = x.shape
    assert d == D_MODEL and w_in.shape == (d, W_QKV + 2 * d) and w_ffn_in.shape == (d, 2 * D_FF)
    assert s % DIL_TILE == 0 and s % (NA_ROWS * GRID_W) == 0 and s % TAIL_TM == 0
    assert s // GRID_W >= NA_ROWS + NA_KH
    scale = HEAD_DIM ** -0.5 * LOG2E
    rep = COL_TILE // HEAD_DIM
    gvec = jnp.stack([jnp.tile(g_qa * scale, rep), jnp.tile(g_ka, rep),
                      jnp.tile(g_qb * scale, rep), jnp.tile(g_kb, rep)]).astype(f32)
    hid = jnp.arange(COL_TILE) // HEAD_DIM
    ones_bd = ((hid[:, None] == hid[None, :]).astype(f32) * (1.0 / HEAD_DIM)).astype(bf16)
    rot_tables = _rotary_tables(s)
    g1 = g_norm1.reshape(1, d)
    g2 = g_norm2.reshape(1, d)

    qkv, tail_ws = _qkv(x, mod6, g1, w_in, gvec, ones_bd, rot_tables,
                        [w_proj_a, w_proj_b, w_o, w_ffn_in, w_ffn_out])
    ng = len(DIL_CONFIGS)
    o_a = _na(qkv[_OUT_QA], qkv[_OUT_KA], qkv[_OUT_VA], _na_bias_table(rpb))
    o_b = _dil(qkv[_OUT_QB:_OUT_QB + ng], qkv[_OUT_KB:_OUT_KB + ng], qkv[_OUT_VB:_OUT_VB + ng])

    return _tail(x, o_a, o_b, mod6, g1, g2, w_in, b_gate.reshape(1, 2 * d), *tail_ws)


def kernel(x, c, w_ada, b_ada, g_norm1, g_norm2, w_in, b_gate, g_qa, g_ka, g_qb, g_kb, rpb,
           w_proj_a, w_proj_b, w_o, w_ffn_in, w_ffn_out):
    depth = w_ada.shape[0]
    bsz, d = c.shape
    for l in range(depth):
        mod, w_in_bf = _mod(c, w_ada[l], b_ada[l], w_in[l])
        x = _layer(x, mod.reshape(bsz, 6, d), g_norm1[l], g_norm2[l], w_in_bf, b_gate[l], g_qa[l], g_ka[l], g_qb[l],
                   g_kb[l], rpb[l], w_proj_a[l], w_proj_b[l], w_o[l], w_ffn_in[l], w_ffn_out[l])
    return x
```
